```python
import jax, jax.numpy as jnp
from jax import lax
import numpy as np

D_MODEL = 4096
BATCH = 32
SEQ = 256
DEPTH = 2
DEC_BATCH = 8
DEC_SEQ = 1024
PAST_LEN = 512

GRID_W = 64
HEAD_DIM = 128
A_Q_HEADS = D_MODEL // (2 * HEAD_DIM)
A_KV_HEADS = A_Q_HEADS // 4
A_GROUPS = A_Q_HEADS // A_KV_HEADS
A_HALF_WIN = 128
A_BLOCK = 128
B_HEADS = D_MODEL // (2 * HEAD_DIM)
B_WIN_ROWS = 8
B_WIN_COLS = 16
A_WIDTH = A_Q_HEADS * HEAD_DIM
A_KV_WIDTH = A_KV_HEADS * HEAD_DIM
B_WIDTH = B_HEADS * HEAD_DIM
MIX_WIDTH = A_WIDTH + B_WIDTH
ATTN_SPLITS = (A_WIDTH, A_KV_WIDTH, A_KV_WIDTH, B_WIDTH, B_WIDTH, B_WIDTH, MIX_WIDTH)
ATTN_IN_WIDTH = sum(ATTN_SPLITS)
POOL_WINDOWS = (2, 4, 8, 16)
N_POOL_GROUPS = 4
POOL_WIDTH = D_MODEL
POOL_GROUP = POOL_WIDTH // N_POOL_GROUPS
ROPE_BASE = 10000.0
NORM_EPS = 1e-6
NEG_INF = -1e30
N_ATTN_LAYERS = (DEPTH + 1) // 2
N_POOL_LAYERS = DEPTH // 2

kernel_name = 'hybrid_diffusion_prefix_ctx_step'

F32 = jnp.float32


def rmsnorm(x, g):
    xf = x.astype(F32)
    y = xf * lax.rsqrt(jnp.mean(xf * xf, axis=-1, keepdims=True) + NORM_EPS) * g.astype(F32)
    return y.astype(x.dtype)


def split_cols(a, widths):
    idx = [int(i) for i in np.cumsum(widths)[:-1]]
    return jnp.split(a, idx, axis=-1)


def axial_rope(n):
    t = jnp.arange(n)
    quarter = HEAD_DIM // 4
    inv_freq = ROPE_BASE ** (-jnp.arange(quarter, dtype=F32) / quarter)
    ang_r = (t // GRID_W).astype(F32)[:, None] * inv_freq
    ang_c = (t % GRID_W).astype(F32)[:, None] * inv_freq
    ang = jnp.concatenate([ang_r, ang_r, ang_c, ang_c], axis=-1)
    return jnp.cos(ang), jnp.sin(ang)


def apply_rope(x, cos, sin):
    quarter = HEAD_DIM // 4
    xr = x.reshape(x.shape[:-1] + (2, 2, quarter))
    rot = jnp.stack([-xr[..., 1, :], xr[..., 0, :]], axis=-2).reshape(x.shape)
    out = x.astype(F32) * cos[None, :, None, :] + rot.astype(F32) * sin[None, :, None, :]
    return out.astype(x.dtype)


def dense_attention(q, k, v, sink):
    b, n, hq, _ = q.shape
    kv = k.shape[2]
    g = hq // kv
    qg = q.reshape(b, n, kv, g, HEAD_DIM)
    s = jnp.einsum('bqkgd,bjkd->bkgqj', qg, k).astype(F32) * (HEAD_DIM ** -0.5)
    if sink is not None:
        s_sink = jnp.broadcast_to(sink.astype(F32).reshape(kv, g, 1, 1), s.shape[:-1] + (1,))
        s = jnp.concatenate([s, s_sink], axis=-1)
    p = jax.nn.softmax(s, axis=-1)[..., :n].astype(v.dtype)
    o = jnp.einsum('bkgqj,bjkd->bqkgd', p, v)
    return o.reshape(b, n, hq * HEAD_DIM)


def window_attention(q, k, v, ck, cv, sink):
    b, n = q.shape[:2]
    nb = n // A_BLOCK
    ctx_len = ck.shape[1]
    span = 3 * A_BLOCK
    scale = HEAD_DIM ** -0.5

    def band(t):
        tp = jnp.pad(t, ((0, 0), (A_BLOCK, A_BLOCK), (0, 0), (0, 0)))
        tp = tp.reshape(b, nb + 2, A_BLOCK, A_KV_HEADS, HEAD_DIM)
        return jnp.concatenate([tp[:, :nb], tp[:, 1:nb + 1], tp[:, 2:]], axis=2)

    kb, vb = band(k), band(v)
    qb = q.reshape(b, nb, A_BLOCK, A_KV_HEADS, A_GROUPS, HEAD_DIM)
    s_win = jnp.einsum('bnqkgd,bnjkd->bkgnqj', qb, kb).astype(F32) * scale
    qpos = jnp.arange(n).reshape(nb, A_BLOCK)
    kpos = jnp.arange(nb)[:, None] * A_BLOCK - A_BLOCK + jnp.arange(span)[None, :]
    valid = ((jnp.abs(qpos[:, :, None] - kpos[:, None, :]) <= A_HALF_WIN)
             & (kpos[:, None, :] >= 0) & (kpos[:, None, :] < n))
    s_win = jnp.where(valid, s_win, NEG_INF)
    s_ctx = jnp.einsum('bnqkgd,bjkd->bkgnqj', qb, ck).astype(F32) * scale
    s_sink = jnp.broadcast_to(sink.astype(F32).reshape(A_KV_HEADS, A_GROUPS, 1, 1, 1),
                              s_win.shape[:-1] + (1,))
    p = jax.nn.softmax(jnp.concatenate([s_win, s_ctx, s_sink], axis=-1), axis=-1)
    p_win = p[..., :span].astype(v.dtype)
    p_ctx = p[..., span:span + ctx_len].astype(v.dtype)
    o = (jnp.einsum('bkgnqj,bnjkd->bnqkgd', p_win, vb)
         + jnp.einsum('bkgnqj,bjkd->bnqkgd', p_ctx, cv))
    return o.reshape(b, n, A_Q_HEADS * HEAD_DIM)


def neighbourhood_attention(q, k, v, ck, cv, rpb):
    b, n = q.shape[:2]
    rows = n // GRID_W
    wr = min(B_WIN_ROWS, rows)
    kw = wr * GRID_W
    scale = HEAD_DIM ** -0.5
    r = jnp.arange(rows)
    row_start = jnp.clip(r - wr // 2, 0, rows - wr)
    row_idx = row_start[:, None] + jnp.arange(wr)[None, :]

    def gather_rows(t):
        tg = t.reshape(b, rows, GRID_W, B_HEADS, HEAD_DIM)[:, row_idx]
        return tg.reshape(b, rows, kw, B_HEADS, HEAD_DIM)

    kg, vg = gather_rows(k), gather_rows(v)
    qg = q.reshape(b, rows, GRID_W, B_HEADS, HEAD_DIM)
    s_nb = jnp.einsum('brchd,brjhd->bhrcj', qg, kg).astype(F32) * scale
    col = jnp.arange(GRID_W)
    col_start = jnp.clip(col - B_WIN_COLS // 2, 0, GRID_W - B_WIN_COLS)
    key_col = jnp.broadcast_to(col, (wr, GRID_W)).reshape(kw)
    key_row = jnp.repeat(row_idx, GRID_W, axis=1)
    col_ok = ((key_col[None, :] >= col_start[:, None])
              & (key_col[None, :] < col_start[:, None] + B_WIN_COLS))
    dr = jnp.clip(key_row - r[:, None] + B_WIN_ROWS - 1, 0, 2 * B_WIN_ROWS - 2)
    dc = jnp.clip(key_col[None, :] - col[:, None] + B_WIN_COLS - 1, 0, 2 * B_WIN_COLS - 2)
    bias = rpb[:, dr[:, None, :], dc[None, :, :]].astype(F32)
    s_nb = jnp.where(col_ok[None, None, None], s_nb + bias[None], NEG_INF)
    s_ctx = jnp.einsum('brchd,bjhd->bhrcj', qg, ck).astype(F32) * scale
    p = jax.nn.softmax(jnp.concatenate([s_nb, s_ctx], axis=-1), axis=-1)
    p_nb = p[..., :kw].astype(v.dtype)
    p_ctx = p[..., kw:].astype(v.dtype)
    o = (jnp.einsum('bhrcj,brjhd->brchd', p_nb, vg)
         + jnp.einsum('bhrcj,bjhd->brchd', p_ctx, cv))
    return o.reshape(b, n, B_HEADS * HEAD_DIM)


def attn_project(h, w_in):
    b, n = h.shape[:2]
    qa, ka, va, qb, kb, vb, gate = split_cols(h @ w_in, ATTN_SPLITS)
    heads = lambda t, nh: t.reshape(b, n, nh, HEAD_DIM)
    return (heads(qa, A_Q_HEADS), heads(ka, A_KV_HEADS), heads(va, A_KV_HEADS),
            heads(qb, B_HEADS), heads(kb, B_HEADS), heads(vb, B_HEADS), gate)


def attn_layer_context(h, w_in, sink, w_out):
    qa, ka, va, qb, kb, vb, gate = attn_project(h, w_in)
    oa = dense_attention(qa, ka, va, sink)
    ob = dense_attention(qb, kb, vb, None)
    o = jnp.concatenate([oa, ob], axis=-1) * jax.nn.silu(gate)
    return o @ w_out, ka, va, kb, vb


def attn_layer_latent(h, w_in, sink, rpb, w_out, ck_a, cv_a, ck_b, cv_b):
    qa, ka, va, qb, kb, vb, gate = attn_project(h, w_in)
    cos, sin = axial_rope(h.shape[1])
    qa = apply_rope(qa, cos, sin)
    ka = apply_rope(ka, cos, sin)
    oa = window_attention(qa, ka, va, ck_a, cv_a, sink)
    ob = neighbourhood_attention(qb, kb, vb, ck_b, cv_b, rpb)
    o = jnp.concatenate([oa, ob], axis=-1) * jax.nn.silu(gate)
    return o @ w_out


def pool_mixer(h, w_in, w_grp, scale, w_out):
    b, n = h.shape[:2]
    u, gate = jnp.split(h @ w_in, 2, axis=-1)
    ug = u.reshape(b, n, N_POOL_GROUPS, POOL_GROUP)
    cs = jnp.concatenate([jnp.zeros((b, 1, N_POOL_GROUPS, POOL_GROUP), F32),
                          jnp.cumsum(ug.astype(F32), axis=1)], axis=1)
    half = jnp.array(POOL_WINDOWS, dtype=jnp.int32) // 2
    t = jnp.arange(n)[:, None]
    lo = jnp.clip(t - half[None, :], 0, n)
    hi = jnp.clip(t + half[None, :], 0, n)
    gi = jnp.arange(N_POOL_GROUPS)[None, :]
    mean = (cs[:, hi, gi] - cs[:, lo, gi]) / (hi - lo).astype(F32)[None, :, :, None]
    pooled = mean.astype(u.dtype) - ug
    y = jnp.einsum('bngc,gcd->bngd', pooled, w_grp).reshape(b, n, POOL_WIDTH) * scale
    return (y * jax.nn.silu(gate)) @ w_out


def setup_inputs(seed: int = 0) -> dict:
    key = jax.random.key(seed)
    ks = jax.random.split(key, 20)
    nrm = lambda k, shape, s: jax.random.normal(k, shape, F32) * s
    d = D_MODEL
    return {
        'x_prompt': nrm(ks[0], (BATCH, SEQ, d), 1.0),
        'x_sample': nrm(ks[1], (DEC_BATCH, DEC_SEQ, d), 1.0),
        'c': nrm(ks[2], (DEC_BATCH, d), 1.0),
        'cache_a_k': nrm(ks[3], (DEC_BATCH, N_ATTN_LAYERS, PAST_LEN, A_KV_HEADS, HEAD_DIM), 1.0),
        'cache_a_v': nrm(ks[4], (DEC_BATCH, N_ATTN_LAYERS, PAST_LEN, A_KV_HEADS, HEAD_DIM), 1.0),
        'cache_b_k': nrm(ks[5], (DEC_BATCH, N_ATTN_LAYERS, PAST_LEN, B_HEADS, HEAD_DIM), 1.0),
        'cache_b_v': nrm(ks[6], (DEC_BATCH, N_ATTN_LAYERS, PAST_LEN, B_HEADS, HEAD_DIM), 1.0),
        'c_ctx': nrm(ks[7], (d,), 1.0),
        'w_ada': nrm(ks[8], (DEPTH, d, 3 * d), 0.5 * d ** -0.5),
        'b_ada': nrm(ks[9], (DEPTH, 3 * d), 0.02),
        'norm_g': 1.0 + nrm(ks[10], (DEPTH, d), 0.02),
        'w_in_attn': nrm(ks[11], (N_ATTN_LAYERS, d, ATTN_IN_WIDTH), d ** -0.5),
        'a_sink': nrm(ks[12], (N_ATTN_LAYERS, A_Q_HEADS), 0.5),
        'b_rpb': nrm(ks[13], (N_ATTN_LAYERS, B_HEADS, 2 * B_WIN_ROWS - 1, 2 * B_WIN_COLS - 1), 0.1),
        'w_out_attn': nrm(ks[14], (N_ATTN_LAYERS, MIX_WIDTH, d), MIX_WIDTH ** -0.5),
        'w_in_pool': nrm(ks[15], (N_POOL_LAYERS, d, 2 * POOL_WIDTH), d ** -0.5),
        'w_grp_pool': nrm(ks[16], (N_POOL_LAYERS, N_POOL_GROUPS, POOL_GROUP, POOL_GROUP), POOL_GROUP ** -0.5),
        'pool_scale': 1.0 + nrm(ks[17], (N_POOL_LAYERS, POOL_WIDTH), 0.02),
        'w_out_pool': nrm(ks[18], (N_POOL_LAYERS, POOL_WIDTH, d), POOL_WIDTH ** -0.5),
        'final_g': 1.0 + nrm(ks[19], (d,), 0.02),
    }


def reference(x_prompt, x_sample, c, cache_a_k, cache_a_v, cache_b_k, cache_b_v, c_ctx,
              w_ada, b_ada, norm_g, w_in_attn, a_sink, b_rpb, w_out_attn,
              w_in_pool, w_grp_pool, pool_scale, w_out_pool, final_g):
    xp, xs = x_prompt, x_sample
    cond_ctx = jax.nn.silu(c_ctx)
    cond_lat = jax.nn.silu(c)
    new_ak, new_av, new_bk, new_bv = [], [], [], []
    for layer in range(DEPTH):
        m_ctx = cond_ctx @ w_ada[layer] + b_ada[layer]
        m_lat = cond_lat @ w_ada[layer] + b_ada[layer]
        sh_c, sc_c, g_c = jnp.split(m_ctx, 3, axis=-1)
        sh_l, sc_l, g_l = jnp.split(m_lat[:, None, :], 3, axis=-1)
        hp = rmsnorm(xp, norm_g[layer]) * (1.0 + sc_c) + sh_c
        hs = rmsnorm(xs, norm_g[layer]) * (1.0 + sc_l) + sh_l
        i = layer // 2
        if layer % 2 == 0:
            out_p, ka, va, kb, vb = attn_layer_context(hp, w_in_attn[i], a_sink[i], w_out_attn[i])
            new_ak.append(ka)
            new_av.append(va)
            new_bk.append(kb)
            new_bv.append(vb)
            out_s = attn_layer_latent(hs, w_in_attn[i], a_sink[i], b_rpb[i], w_out_attn[i],
                                      cache_a_k[:, i], cache_a_v[:, i],
                                      cache_b_k[:, i], cache_b_v[:, i])
        else:
            out_p = pool_mixer(hp, w_in_pool[i], w_grp_pool[i], pool_scale[i], w_out_pool[i])
            out_s = pool_mixer(hs, w_in_pool[i], w_grp_pool[i], pool_scale[i], w_out_pool[i])
        xp = xp + g_c * out_p
        xs = xs + g_l * out_s
    y_prompt = rmsnorm(xp, final_g)
    y_sample = rmsnorm(xs, final_g)
    new_a_k = jnp.stack(new_ak, axis=1)
    new_a_v = jnp.stack(new_av, axis=1)
    new_b_k = jnp.stack(new_bk, axis=1)
    new_b_v = jnp.stack(new_bv, axis=1)
    return (y_prompt, y_sample, new_a_k, new_a_v, new_b_k, new_b_v)
```

```python
import functools

import jax
import jax.numpy as jnp
from jax import lax
from jax.experimental import pallas as pl
from jax.experimental.pallas import tpu as pltpu

F32 = jnp.float32
BF16 = jnp.bfloat16

D_MODEL = 4096
BATCH = 32
SEQ = 256
DEC_BATCH = 8
DEC_SEQ = 1024
PAST_LEN = 512
GRID_W = 64
HEAD_DIM = 128
A_Q_HEADS = 16
A_KV_HEADS = 4
A_GROUPS = 4
A_HALF_WIN = 128
A_BLOCK = 128
B_HEADS = 16
B_WIN_ROWS = 8
B_WIN_COLS = 16
A_WIDTH = A_Q_HEADS * HEAD_DIM
A_KV_WIDTH = A_KV_HEADS * HEAD_DIM
B_WIDTH = B_HEADS * HEAD_DIM
MIX_WIDTH = A_WIDTH + B_WIDTH
POOL_WINDOWS = (2, 4, 8, 16)
POOL_GROUP = 1024
ROPE_BASE = 10000.0
NORM_EPS = 1e-6
NEG_INF = -1e30
ATTN_SCALE = HEAD_DIM ** -0.5

N_PROMPT = BATCH * SEQ
N_SAMPLE = DEC_BATCH * DEC_SEQ
MOD_ROWS = 16
CTX_MOD_ROW = DEC_BATCH
GRID_ROWS = DEC_SEQ // GRID_W
N_BIAS_PAIRS = 2 * B_WIN_ROWS - 2
RPB_ROWS = 2 * B_WIN_ROWS - 1
RPB_COLS = 2 * B_WIN_COLS - 1

COL_KA = 0
COL_VA = COL_KA + A_KV_WIDTH
COL_KB = COL_VA + A_KV_WIDTH
COL_VB = COL_KB + B_WIDTH
COL_QA = COL_VB + B_WIDTH
COL_QB = COL_QA + A_WIDTH
COL_GATE = COL_QB + B_WIDTH
ATTN_IN_WIDTH = COL_GATE + MIX_WIDTH

VMEM_LIMIT = 48 * 1024 * 1024


def _params(n_grid_dims):
    return pltpu.CompilerParams(
        dimension_semantics=("arbitrary",) * n_grid_dims,
        vmem_limit_bytes=VMEM_LIMIT,
    )


def _silu(x):
    return x / (1.0 + jnp.exp(-x))


def _dot_nt(a, b):
    return lax.dot_general(a, b, (((1,), (1,)), ((), ())), preferred_element_type=F32)


def _dot(a, b):
    return jnp.dot(a, b, preferred_element_type=F32)


def _ada_kernel(cond_ref, w_ref, b_ref, o_ref):
    a = _silu(cond_ref[...]).astype(BF16)
    o_ref[...] = _dot(a, w_ref[...].astype(BF16)) + b_ref[...]


def _ada(cond, w_ada, b_ada, tn=512):
    depth, d, n = w_ada.shape
    return pl.pallas_call(
        _ada_kernel,
        grid=(depth, n // tn),
        in_specs=[
            pl.BlockSpec((MOD_ROWS, d), lambda l, j: (0, 0)),
            pl.BlockSpec((None, d, tn), lambda l, j: (l, 0, j)),
            pl.BlockSpec((None, 1, tn), lambda l, j: (l, 0, j)),
        ],
        out_specs=pl.BlockSpec((None, MOD_ROWS, tn), lambda l, j: (l, 0, j)),
        out_shape=jax.ShapeDtypeStruct((depth, MOD_ROWS, n), F32),
        compiler_params=_params(2),
        name="ada",
    )(cond, w_ada, b_ada.reshape(depth, 1, n))


def _mod_row_fn(is_prompt, tm):
    if is_prompt:
        return lambda i: CTX_MOD_ROW
    return lambda i: (i * tm) // DEC_SEQ


def _norm_mod_kernel(x_ref, g_ref, sh_ref, sc_ref, o_ref):
    x = x_ref[...]
    ms = jnp.mean(x * x, axis=-1, keepdims=True)
    y = x * lax.rsqrt(ms + NORM_EPS) * g_ref[...]
    o_ref[...] = (y * (1.0 + sc_ref[...]) + sh_ref[...]).astype(o_ref.dtype)


def _norm_mod(x, gain, mod, is_prompt, tm=256):
    m, d = x.shape
    row = _mod_row_fn(is_prompt, tm)
    return pl.pallas_call(
        _norm_mod_kernel,
        grid=(m // tm,),
        in_specs=[
            pl.BlockSpec((tm, d), lambda i: (i, 0)),
            pl.BlockSpec((1, d), lambda i: (0, 0)),
            pl.BlockSpec((None, 1, d), lambda i: (row(i), 0, 0)),
            pl.BlockSpec((None, 1, d), lambda i: (row(i), 0, 1)),
        ],
        out_specs=pl.BlockSpec((tm, d), lambda i: (i, 0)),
        out_shape=jax.ShapeDtypeStruct((m, d), BF16),
        compiler_params=_params(1),
        name="norm_mod",
    )(x, gain.reshape(1, d), mod, mod)


def _final_norm_kernel(x_ref, g_ref, o_ref):
    x = x_ref[...]
    ms = jnp.mean(x * x, axis=-1, keepdims=True)
    o_ref[...] = x * lax.rsqrt(ms + NORM_EPS) * g_ref[...]


def _final_norm(x, gain, tm=256):
    m, d = x.shape
    return pl.pallas_call(
        _final_norm_kernel,
        grid=(m // tm,),
        in_specs=[pl.BlockSpec((tm, d), lambda i: (i, 0)),
                  pl.BlockSpec((1, d), lambda i: (0, 0))],
        out_specs=pl.BlockSpec((tm, d), lambda i: (i, 0)),
        out_shape=jax.ShapeDtypeStruct((m, d), F32),
        compiler_params=_params(1),
        name="final_norm",
    )(x, gain.reshape(1, d))


def _mm_kernel(a_ref, w_ref, o_ref):
    o_ref[...] = _dot(a_ref[...], w_ref[...]).astype(o_ref.dtype)


def _matmul(a, w, col0, ncols, out_dtype, tn=1024, tm=512):
    m, k = a.shape
    assert col0 % tn == 0 and ncols % tn == 0 and m % tm == 0
    j0 = col0 // tn
    return pl.pallas_call(
        _mm_kernel,
        grid=(ncols // tn, m // tm),
        in_specs=[pl.BlockSpec((tm, k), lambda j, i: (i, 0)),
                  pl.BlockSpec((k, tn), lambda j, i: (0, j + j0))],
        out_specs=pl.BlockSpec((tm, tn), lambda j, i: (i, j)),
        out_shape=jax.ShapeDtypeStruct((m, ncols), out_dtype),
        compiler_params=_params(2),
        name="proj",
    )(a, w)


def _mm_residual_kernel(a_ref, w_ref, x_ref, g_ref, o_ref):
    o_ref[...] = x_ref[...] + g_ref[...] * _dot(a_ref[...], w_ref[...])


def _matmul_residual(a, w, x, mod, is_prompt, tn=1024, tm=512):
    m, k = a.shape
    n = w.shape[1]
    row = _mod_row_fn(is_prompt, tm)
    gate_block0 = 2 * n // tn
    return pl.pallas_call(
        _mm_residual_kernel,
        grid=(n // tn, m // tm),
        in_specs=[pl.BlockSpec((tm, k), lambda j, i: (i, 0)),
                  pl.BlockSpec((k, tn), lambda j, i: (0, j)),
                  pl.BlockSpec((tm, tn), lambda j, i: (i, j)),
                  pl.BlockSpec((None, 1, tn), lambda j, i: (row(i), 0, gate_block0 + j))],
        out_specs=pl.BlockSpec((tm, tn), lambda j, i: (i, j)),
        out_shape=jax.ShapeDtypeStruct((m, n), F32),
        compiler_params=_params(2),
        name="proj_residual",
    )(a, w, x, mod)


def _ctx_attn_kernel(sink_ref, qg_ref, ka_ref, va_ref, kb_ref, vb_ref, o_ref):
    n = SEQ
    gate0 = MIX_WIDTH
    for kv in range(A_KV_HEADS):
        cols = slice(kv * HEAD_DIM, (kv + 1) * HEAD_DIM)
        k = ka_ref[:, cols].astype(BF16)
        v = va_ref[:, cols].astype(BF16)
        heads = [kv * A_GROUPS + g for g in range(A_GROUPS)]
        q = jnp.concatenate(
            [qg_ref[:, h * HEAD_DIM:(h + 1) * HEAD_DIM] for h in heads], axis=0)
        sink = jnp.concatenate(
            [jnp.full((n, 1), sink_ref[h], F32) for h in heads], axis=0)
        s = _dot_nt(q, k) * ATTN_SCALE
        m = jnp.maximum(jnp.max(s, axis=-1, keepdims=True), sink)
        e = jnp.exp(s - m)
        l = jnp.sum(e, axis=-1, keepdims=True) + jnp.exp(sink - m)
        o = _dot(e.astype(BF16), v) / l
        for g, h in enumerate(heads):
            hc = slice(h * HEAD_DIM, (h + 1) * HEAD_DIM)
            gate = qg_ref[:, gate0 + h * HEAD_DIM:gate0 + (h + 1) * HEAD_DIM].astype(F32)
            o_ref[:, hc] = (o[g * n:(g + 1) * n] * _silu(gate)).astype(o_ref.dtype)
    for h in range(B_HEADS):
        cols = slice(h * HEAD_DIM, (h + 1) * HEAD_DIM)
        q = qg_ref[:, A_WIDTH + h * HEAD_DIM:A_WIDTH + (h + 1) * HEAD_DIM]
        k = kb_ref[:, cols].astype(BF16)
        v = vb_ref[:, cols].astype(BF16)
        s = _dot_nt(q, k) * ATTN_SCALE
        m = jnp.max(s, axis=-1, keepdims=True)
        e = jnp.exp(s - m)
        l = jnp.sum(e, axis=-1, keepdims=True)
        o = _dot(e.astype(BF16), v) / l
        oc = slice(A_WIDTH + h * HEAD_DIM, A_WIDTH + (h + 1) * HEAD_DIM)
        gate = qg_ref[:, gate0 + oc.start:gate0 + oc.stop].astype(F32)
        o_ref[:, oc] = (o * _silu(gate)).astype(o_ref.dtype)


def _ctx_attention(sink, qg, ka, va, kb, vb):
    return pl.pallas_call(
        _ctx_attn_kernel,
        grid=(BATCH,),
        in_specs=[
            pl.BlockSpec(memory_space=pltpu.SMEM),
            pl.BlockSpec((SEQ, 2 * MIX_WIDTH), lambda b: (b, 0)),
            pl.BlockSpec((SEQ, A_KV_WIDTH), lambda b: (b, 0)),
            pl.BlockSpec((SEQ, A_KV_WIDTH), lambda b: (b, 0)),
            pl.BlockSpec((SEQ, B_WIDTH), lambda b: (b, 0)),
            pl.BlockSpec((SEQ, B_WIDTH), lambda b: (b, 0)),
        ],
        out_specs=pl.BlockSpec((SEQ, MIX_WIDTH), lambda b: (b, 0)),
        out_shape=jax.ShapeDtypeStruct((N_PROMPT, MIX_WIDTH), BF16),
        compiler_params=_params(1),
        name="ctx_attention",
    )(sink, qg, ka, va, kb, vb)


def _win_attn_kernel(sink_ref, q_ref, k_ref, v_ref, ck_ref, cv_ref, gate_ref,
                     cos_ref, sin_ref, o_ref, qs_ref, ks_ref):
    kv = pl.program_id(1)
    n = DEC_SEQ
    cos = cos_ref[...]
    sin = sin_ref[...]
    lane = lax.broadcasted_iota(jnp.int32, (n, HEAD_DIM), 1)
    first_quarter = (lane % (HEAD_DIM // 2)) < (HEAD_DIM // 4)

    def rope(x):
        rot = jnp.where(first_quarter,
                        pltpu.roll(x, HEAD_DIM - HEAD_DIM // 4, 1),
                        pltpu.roll(x, HEAD_DIM // 4, 1))
        return (x * cos + rot * sin).astype(BF16)

    ks_ref[...] = rope(k_ref[...].astype(F32))
    for g in range(A_GROUPS):
        qs_ref[g] = rope(q_ref[:, g * HEAD_DIM:(g + 1) * HEAD_DIM].astype(F32))

    ck = ck_ref[...].astype(BF16)
    cv = cv_ref[...].astype(BF16)
    rows = A_GROUPS * A_BLOCK
    sink = jnp.concatenate(
        [jnp.full((A_BLOCK, 1), sink_ref[kv * A_GROUPS + g], F32) for g in range(A_GROUPS)],
        axis=0)
    for blk in range(n // A_BLOCK):
        r0 = blk * A_BLOCK
        lo = max(0, r0 - A_BLOCK)
        hi = min(n, r0 + 2 * A_BLOCK)
        span = hi - lo
        q = jnp.concatenate([qs_ref[g, r0:r0 + A_BLOCK, :] for g in range(A_GROUPS)], axis=0)
        s_w = _dot_nt(q, ks_ref[lo:hi, :]) * ATTN_SCALE
        qpos = r0 + lax.broadcasted_iota(jnp.int32, (rows, span), 0) % A_BLOCK
        kpos = lo + lax.broadcasted_iota(jnp.int32, (rows, span), 1)
        s_w = jnp.where(jnp.abs(qpos - kpos) <= A_HALF_WIN, s_w, NEG_INF)
        s_c = _dot_nt(q, ck) * ATTN_SCALE
        m = jnp.maximum(jnp.max(s_w, axis=-1, keepdims=True),
                        jnp.max(s_c, axis=-1, keepdims=True))
        m = jnp.maximum(m, sink)
        e_w = jnp.exp(s_w - m)
        e_c = jnp.exp(s_c - m)
        l = (jnp.sum(e_w, axis=-1, keepdims=True) + jnp.sum(e_c, axis=-1, keepdims=True)
             + jnp.exp(sink - m))
        o = (_dot(e_w.astype(BF16), v_ref[lo:hi, :]) + _dot(e_c.astype(BF16), cv)) / l
        for g in range(A_GROUPS):
            hc = slice(g * HEAD_DIM, (g + 1) * HEAD_DIM)
            gate = gate_ref[r0:r0 + A_BLOCK, hc].astype(F32)
            o_ref[r0:r0 + A_BLOCK, hc] = (
                o[g * A_BLOCK:(g + 1) * A_BLOCK] * _silu(gate)).astype(o_ref.dtype)


def _win_attention(sink, proj, cache_k, cache_v, cos, sin_signed):
    gw = A_GROUPS * HEAD_DIM
    n = DEC_SEQ
    return pl.pallas_call(
        _win_attn_kernel,
        grid=(DEC_BATCH, A_KV_HEADS),
        in_specs=[
            pl.BlockSpec(memory_space=pltpu.SMEM),
            pl.BlockSpec((n, gw), lambda b, k: (b, COL_QA // gw + k)),
            pl.BlockSpec((n, HEAD_DIM), lambda b, k: (b, COL_KA // HEAD_DIM + k)),
            pl.BlockSpec((n, HEAD_DIM), lambda b, k: (b, COL_VA // HEAD_DIM + k)),
            pl.BlockSpec((None, PAST_LEN, HEAD_DIM), lambda b, k: (b, 0, k)),
            pl.BlockSpec((None, PAST_LEN, HEAD_DIM), lambda b, k: (b, 0, k)),
            pl.BlockSpec((n, gw), lambda b, k: (b, COL_GATE // gw + k)),
            pl.BlockSpec((n, HEAD_DIM), lambda b, k: (0, 0)),
            pl.BlockSpec((n, HEAD_DIM), lambda b, k: (0, 0)),
        ],
        out_specs=pl.BlockSpec((n, gw), lambda b, k: (b, k)),
        out_shape=jax.ShapeDtypeStruct((N_SAMPLE, A_WIDTH), BF16),
        scratch_shapes=[pltpu.VMEM((A_GROUPS, n, HEAD_DIM), BF16),
                        pltpu.VMEM((n, HEAD_DIM), BF16)],
        compiler_params=_params(2),
        name="win_attention",
    )(sink, proj, proj, proj, cache_k, cache_v, proj, cos, sin_signed)


def _bias_kernel(rpb_ref, o_ref):
    h = pl.program_id(0)
    shape = (GRID_W, 2 * GRID_W)
    c = lax.broadcasted_iota(jnp.int32, shape, 0)
    j2 = lax.broadcasted_iota(jnp.int32, shape, 1)
    kc = j2 % GRID_W
    second = j2 >= GRID_W
    col_start = jnp.clip(c - B_WIN_COLS // 2, 0, GRID_W - B_WIN_COLS)
    ok = (kc >= col_start) & (kc < col_start + B_WIN_COLS)
    dc = kc - c + B_WIN_COLS - 1
    base = h * (RPB_ROWS * RPB_COLS)
    for i in range(N_BIAS_PAIRS):
        acc = jnp.full(shape, NEG_INF, F32)
        for d in range(RPB_COLS):
            val = jnp.where(second, rpb_ref[base + (i + 1) * RPB_COLS + d],
                            rpb_ref[base + i * RPB_COLS + d])
            acc = jnp.where(ok & (dc == d), val, acc)
        o_ref[i] = acc


def _expand_bias(rpb):
    return pl.pallas_call(
        _bias_kernel,
        grid=(B_HEADS,),
        in_specs=[pl.BlockSpec(memory_space=pltpu.SMEM)],
        out_specs=pl.BlockSpec((None, N_BIAS_PAIRS, GRID_W, 2 * GRID_W), lambda h: (h, 0, 0, 0)),
        out_shape=jax.ShapeDtypeStruct((B_HEADS, N_BIAS_PAIRS, GRID_W, 2 * GRID_W), F32),
        compiler_params=_params(1),
        name="expand_bias",
    )(rpb.reshape(-1))


def _na_attn_kernel(q_ref, k_ref, v_ref, ck_ref, cv_ref, gate_ref, bias_ref, o_ref):
    q = q_ref[...]
    ck = ck_ref[...].astype(BF16)
    cv = cv_ref[...].astype(BF16)
    s_c = _dot_nt(q, ck) * ATTN_SCALE
    m_c = jnp.max(s_c, axis=-1, keepdims=True)
    e_c = jnp.exp(s_c - m_c)
    l_c = jnp.sum(e_c, axis=-1, keepdims=True)
    o_c = _dot(e_c.astype(BF16), cv)
    kw = B_WIN_ROWS * GRID_W
    for r in range(GRID_ROWS):
        rs = min(max(r - B_WIN_ROWS // 2, 0), GRID_ROWS - B_WIN_ROWS)
        dr0 = rs - r + B_WIN_ROWS - 1
        rq = slice(r * GRID_W, (r + 1) * GRID_W)
        rk = slice(rs * GRID_W, rs * GRID_W + kw)
        bias = jnp.concatenate(
            [bias_ref[dr0 + 2 * i] for i in range(B_WIN_ROWS // 2)], axis=1)
        s_n = _dot_nt(q_ref[rq, :], k_ref[rk, :]) * ATTN_SCALE + bias
        m = jnp.maximum(jnp.max(s_n, axis=-1, keepdims=True), m_c[rq])
        e_n = jnp.exp(s_n - m)
        a = jnp.exp(m_c[rq] - m)
        l = jnp.sum(e_n, axis=-1, keepdims=True) + a * l_c[rq]
        o = (_dot(e_n.astype(BF16), v_ref[rk, :]) + a * o_c[rq]) / l
        gate = gate_ref[rq, :].astype(F32)
        o_ref[rq, :] = (o * _silu(gate)).astype(o_ref.dtype)


def _na_attention(proj, cache_k, cache_v, bias):
    n = DEC_SEQ
    hd = HEAD_DIM
    return pl.pallas_call(
        _na_attn_kernel,
        grid=(DEC_BATCH, B_HEADS),
        in_specs=[
            pl.BlockSpec((n, hd), lambda b, h: (b, COL_QB // hd + h)),
            pl.BlockSpec((n, hd), lambda b, h: (b, COL_KB // hd + h)),
            pl.BlockSpec((n, hd), lambda b, h: (b, COL_VB // hd + h)),
            pl.BlockSpec((None, PAST_LEN, hd), lambda b, h: (b, 0, h)),
            pl.BlockSpec((None, PAST_LEN, hd), lambda b, h: (b, 0, h)),
            pl.BlockSpec((n, hd), lambda b, h: (b, (COL_GATE + A_WIDTH) // hd + h)),
            pl.BlockSpec((None, N_BIAS_PAIRS, GRID_W, 2 * GRID_W), lambda b, h: (h, 0, 0, 0)),
        ],
        out_specs=pl.BlockSpec((n, hd), lambda b, h: (b, h)),
        out_shape=jax.ShapeDtypeStruct((N_SAMPLE, B_WIDTH), BF16),
        compiler_params=_params(2),
        name="na_attention",
    )(proj, proj, proj, cache_k, cache_v, proj, bias)


def _pool_kernel(u_ref, gate_ref, w_ref, scale_ref, o_ref, *, seq, rows):
    grp = pl.program_id(1)
    t = lax.broadcasted_iota(jnp.int32, (rows, rows), 0)
    j = lax.broadcasted_iota(jnp.int32, (rows, rows), 1)
    delta = j - t
    same_seq = (t // seq) == (j // seq)
    pos = lax.broadcasted_iota(jnp.int32, (rows, 1), 0) % seq
    for gi, window in enumerate(POOL_WINDOWS):
        half = window // 2

        @pl.when(grp == gi)
        def _(half=half):
            band = jnp.where((delta >= -half) & (delta < half) & same_seq, 1.0, 0.0).astype(BF16)
            u = u_ref[...]
            wsum = _dot(band, u)
            count = (jnp.minimum(pos + half, seq) - jnp.maximum(pos - half, 0)).astype(F32)
            pooled = wsum / count - u.astype(F32)
            y = _dot(pooled.astype(BF16), w_ref[...]) * scale_ref[...]
            o_ref[...] = (y * _silu(gate_ref[...].astype(F32))).astype(o_ref.dtype)


def _pool_mixer(ug, w_grp, scale, seq, rows=1024):
    m = ug.shape[0]
    n_groups = len(POOL_WINDOWS)
    pg = POOL_GROUP
    return pl.pallas_call(
        functools.partial(_pool_kernel, seq=seq, rows=rows),
        grid=(m // rows, n_groups),
        in_specs=[
            pl.BlockSpec((rows, pg), lambda i, g: (i, g)),
            pl.BlockSpec((rows, pg), lambda i, g: (i, n_groups + g)),
            pl.BlockSpec((None, pg, pg), lambda i, g: (g, 0, 0)),
            pl.BlockSpec((1, pg), lambda i, g: (0, g)),
        ],
        out_specs=pl.BlockSpec((rows, pg), lambda i, g: (i, g)),
        out_shape=jax.ShapeDtypeStruct((m, n_groups * pg), BF16),
        compiler_params=_params(2),
        name="pool_mixer",
    )(ug, ug, w_grp, scale.reshape(1, -1))


def _rope_tables():
    t = jnp.arange(DEC_SEQ)
    quarter = HEAD_DIM // 4
    inv_freq = ROPE_BASE ** (-jnp.arange(quarter, dtype=F32) / quarter)
    ang_r = (t // GRID_W).astype(F32)[:, None] * inv_freq
    ang_c = (t % GRID_W).astype(F32)[:, None] * inv_freq
    ang = jnp.concatenate([ang_r, ang_r, ang_c, ang_c], axis=-1)
    sign = jnp.tile(jnp.concatenate([-jnp.ones((quarter,), F32), jnp.ones((quarter,), F32)]), 2)
    return jnp.cos(ang), jnp.sin(ang) * sign


def kernel(x_prompt, x_sample, c, cache_a_k, cache_a_v, cache_b_k, cache_b_v, c_ctx,
           w_ada, b_ada, norm_g, w_in_attn, a_sink, b_rpb, w_out_attn,
           w_in_pool, w_grp_pool, pool_scale, w_out_pool, final_g):
    d = D_MODEL
    xp = x_prompt.reshape(N_PROMPT, d)
    xs = x_sample.reshape(N_SAMPLE, d)

    w = w_in_attn[0]
    o_qa = 0
    o_ka = o_qa + A_WIDTH
    o_va = o_ka + A_KV_WIDTH
    o_qb = o_va + A_KV_WIDTH
    o_kb = o_qb + B_WIDTH
    o_vb = o_kb + B_WIDTH
    o_gate = o_vb + B_WIDTH
    w_in0 = jnp.concatenate([
        w[:, o_ka:o_va], w[:, o_va:o_qb], w[:, o_kb:o_vb], w[:, o_vb:o_gate],
        w[:, o_qa:o_ka], w[:, o_qb:o_kb], w[:, o_gate:]], axis=1).astype(BF16)
    w_out0 = w_out_attn[0].astype(BF16)
    w_in1 = w_in_pool[0].astype(BF16)
    w_grp = w_grp_pool[0].astype(BF16)
    w_out1 = w_out_pool[0].astype(BF16)

    cond = jnp.zeros((MOD_ROWS, d), F32).at[:DEC_BATCH].set(c).at[CTX_MOD_ROW].set(c_ctx)
    mod = _ada(cond, w_ada, b_ada)
    mod0 = mod[0].reshape(MOD_ROWS, 1, 3 * d)
    mod1 = mod[1].reshape(MOD_ROWS, 1, 3 * d)

    hp = _norm_mod(xp, norm_g[0], mod0, True)
    hs = _norm_mod(xs, norm_g[0], mod0, False)

    ka = _matmul(hp, w_in0, COL_KA, A_KV_WIDTH, F32, tn=512)
    va = _matmul(hp, w_in0, COL_VA, A_KV_WIDTH, F32, tn=512)
    kb = _matmul(hp, w_in0, COL_KB, B_WIDTH, F32)
    vb = _matmul(hp, w_in0, COL_VB, B_WIDTH, F32)
    qg_p = _matmul(hp, w_in0, COL_QA, 2 * MIX_WIDTH, BF16)
    proj_s = _matmul(hs, w_in0, 0, ATTN_IN_WIDTH, BF16)

    sink = a_sink[0]
    og_p = _ctx_attention(sink, qg_p, ka, va, kb, vb)

    cos, sin_signed = _rope_tables()
    oa_s = _win_attention(sink, proj_s,
                          cache_a_k[:, 0].reshape(DEC_BATCH, PAST_LEN, A_KV_WIDTH),
                          cache_a_v[:, 0].reshape(DEC_BATCH, PAST_LEN, A_KV_WIDTH),
                          cos, sin_signed)
    bias = _expand_bias(b_rpb[0])
    ob_s = _na_attention(proj_s,
                         cache_b_k[:, 0].reshape(DEC_BATCH, PAST_LEN, B_WIDTH),
                         cache_b_v[:, 0].reshape(DEC_BATCH, PAST_LEN, B_WIDTH),
                         bias)
    og_s = jnp.concatenate([oa_s, ob_s], axis=1)

    xp1 = _matmul_residual(og_p, w_out0, xp, mod0, True)
    xs1 = _matmul_residual(og_s, w_out0, xs, mod0, False)

    hp1 = _norm_mod(xp1, norm_g[1], mod1, True)
    hs1 = _norm_mod(xs1, norm_g[1], mod1, False)
    ug_p = _matmul(hp1, w_in1, 0, 2 * d, BF16)
    ug_s = _matmul(hs1, w_in1, 0, 2 * d, BF16)
    y_p = _pool_mixer(ug_p, w_grp, pool_scale[0], SEQ)
    y_s = _pool_mixer(ug_s, w_grp, pool_scale[0], DEC_SEQ)
    xp2 = _matmul_residual(y_p, w_out1, xp1, mod1, True)
    xs2 = _matmul_residual(y_s, w_out1, xs1, mod1, False)

    y_prompt = _final_norm(xp2, final_g).reshape(BATCH, SEQ, d)
    y_sample = _final_norm(xs2, final_g).reshape(DEC_BATCH, DEC_SEQ, d)

    kv_shape_a = (BATCH, 1, SEQ, A_KV_HEADS, HEAD_DIM)
    kv_shape_b = (BATCH, 1, SEQ, B_HEADS, HEAD_DIM)
    return (y_prompt, y_sample, ka.reshape(kv_shape_a), va.reshape(kv_shape_a),
            kb.reshape(kv_shape_b), vb.reshape(kv_shape_b))
```

```python
import functools

import jax
import jax.numpy as jnp
from jax import lax
from jax.experimental import pallas as pl
from jax.experimental.pallas import tpu as pltpu

F32 = jnp.float32
BF16 = jnp.bfloat16

D_MODEL = 4096
BATCH = 32
SEQ = 256
DEC_BATCH = 8
DEC_SEQ = 1024
PAST_LEN = 512
GRID_W = 64
HEAD_DIM = 128
A_Q_HEADS = 16
A_KV_HEADS = 4
A_GROUPS = 4
A_HALF_WIN = 128
A_BLOCK = 128
B_HEADS = 16
B_WIN_ROWS = 8
B_WIN_COLS = 16
A_WIDTH = A_Q_HEADS * HEAD_DIM
A_KV_WIDTH = A_KV_HEADS * HEAD_DIM
B_WIDTH = B_HEADS * HEAD_DIM
MIX_WIDTH = A_WIDTH + B_WIDTH
POOL_WINDOWS = (2, 4, 8, 16)
POOL_GROUP = 1024
ROPE_BASE = 10000.0
NORM_EPS = 1e-6
NEG_INF = -1e30
ATTN_SCALE = HEAD_DIM ** -0.5

N_PROMPT = BATCH * SEQ
N_SAMPLE = DEC_BATCH * DEC_SEQ
MOD_ROWS = 16
CTX_MOD_ROW = DEC_BATCH
GRID_ROWS = DEC_SEQ // GRID_W
N_BIAS_PAIRS = 2 * B_WIN_ROWS - 2
RPB_ROWS = 2 * B_WIN_ROWS - 1
RPB_COLS = 2 * B_WIN_COLS - 1

COL_KA = 0
COL_VA = COL_KA + A_KV_WIDTH
COL_KB = COL_VA + A_KV_WIDTH
COL_VB = COL_KB + B_WIDTH
COL_QA = COL_VB + B_WIDTH
COL_QB = COL_QA + A_WIDTH
COL_GATE = COL_QB + B_WIDTH
ATTN_IN_WIDTH = COL_GATE + MIX_WIDTH

VMEM_LIMIT = 48 * 1024 * 1024
VMEM_LIMIT_ROWS = 56 * 1024 * 1024


def _params(n_grid_dims):
    return pltpu.CompilerParams(
        dimension_semantics=("arbitrary",) * n_grid_dims,
        vmem_limit_bytes=VMEM_LIMIT,
    )


def _silu(x):
    return x / (1.0 + jnp.exp(-x))


def _dot_nt(a, b):
    return lax.dot_general(a, b, (((1,), (1,)), ((), ())), preferred_element_type=F32)


def _dot(a, b):
    return jnp.dot(a, b, preferred_element_type=F32)


def _ada_kernel(cond_ref, w_ref, b_ref, o_ref):
    a = _silu(cond_ref[...]).astype(BF16)
    o_ref[...] = _dot(a, w_ref[...].astype(BF16)) + b_ref[...]


def _ada(cond, w_ada, b_ada, tn=512):
    depth, d, n = w_ada.shape
    return pl.pallas_call(
        _ada_kernel,
        grid=(depth, n // tn),
        in_specs=[
            pl.BlockSpec((MOD_ROWS, d), lambda l, j: (0, 0)),
            pl.BlockSpec((None, d, tn), lambda l, j: (l, 0, j)),
            pl.BlockSpec((None, 1, tn), lambda l, j: (l, 0, j)),
        ],
        out_specs=pl.BlockSpec((None, MOD_ROWS, tn), lambda l, j: (l, 0, j)),
        out_shape=jax.ShapeDtypeStruct((depth, MOD_ROWS, n), F32),
        compiler_params=_params(2),
        name="ada",
    )(cond, w_ada, b_ada.reshape(depth, 1, n))


def _mod_row_fn(is_prompt, tm):
    if is_prompt:
        return lambda i: CTX_MOD_ROW
    return lambda i: (i * tm) // DEC_SEQ


def _norm_mod_kernel(x_ref, g_ref, sh_ref, sc_ref, o_ref):
    x = x_ref[...]
    ms = jnp.mean(x * x, axis=-1, keepdims=True)
    y = x * lax.rsqrt(ms + NORM_EPS) * g_ref[...]
    o_ref[...] = (y * (1.0 + sc_ref[...]) + sh_ref[...]).astype(o_ref.dtype)


def _norm_mod(x, gain, mod, is_prompt, tm=256):
    m, d = x.shape
    row = _mod_row_fn(is_prompt, tm)
    return pl.pallas_call(
        _norm_mod_kernel,
        grid=(m // tm,),
        in_specs=[
            pl.BlockSpec((tm, d), lambda i: (i, 0)),
            pl.BlockSpec((1, d), lambda i: (0, 0)),
            pl.BlockSpec((None, 1, d), lambda i: (row(i), 0, 0)),
            pl.BlockSpec((None, 1, d), lambda i: (row(i), 0, 1)),
        ],
        out_specs=pl.BlockSpec((tm, d), lambda i: (i, 0)),
        out_shape=jax.ShapeDtypeStruct((m, d), BF16),
        compiler_params=_params(1),
        name="norm_mod",
    )(x, gain.reshape(1, d), mod, mod)


def _mm_kernel(a_ref, w_ref, o_ref):
    o_ref[...] = _dot(a_ref[...], w_ref[...]).astype(o_ref.dtype)


def _matmul(a, w, col0, ncols, out_dtype, tn=1024, tm=512):
    m, k = a.shape
    assert col0 % tn == 0 and ncols % tn == 0 and m % tm == 0
    j0 = col0 // tn
    return pl.pallas_call(
        _mm_kernel,
        grid=(ncols // tn, m // tm),
        in_specs=[pl.BlockSpec((tm, k), lambda j, i: (i, 0)),
                  pl.BlockSpec((k, tn), lambda j, i: (0, j + j0))],
        out_specs=pl.BlockSpec((tm, tn), lambda j, i: (i, j)),
        out_shape=jax.ShapeDtypeStruct((m, ncols), out_dtype),
        compiler_params=_params(2),
        name="proj",
    )(a, w)


def _row_rsqrt(row_scr, n_tiles):
    ss = None
    for jj in range(n_tiles):
        t = row_scr[jj]
        part = jnp.sum(t * t, axis=-1, keepdims=True)
        ss = part if ss is None else ss + part
    width = n_tiles * row_scr.shape[2]
    return lax.rsqrt(ss / width + NORM_EPS)


def _dot_halves(a1_ref, a2_ref, w_ref):
    kh = a1_ref.shape[1]
    return _dot(a1_ref[...], w_ref[:kh, :]) + _dot(a2_ref[...], w_ref[kh:, :])


def _mm_res_normmod_kernel(a1_ref, a2_ref, w_ref, x_ref, g_ref, gain_ref, sh_ref, sc_ref,
                           x1_ref, h_ref, row_scr, *, n_tiles, tn):
    j = pl.program_id(1)
    x1 = x_ref[...] + g_ref[...] * _dot_halves(a1_ref, a2_ref, w_ref)
    x1_ref[...] = x1
    row_scr[j] = x1

    @pl.when(j == n_tiles - 1)
    def _():
        r = _row_rsqrt(row_scr, n_tiles)
        for jj in range(n_tiles):
            cols = slice(jj * tn, (jj + 1) * tn)
            y = row_scr[jj] * r * gain_ref[:, cols]
            h_ref[:, cols] = (y * (1.0 + sc_ref[:, cols]) + sh_ref[:, cols]).astype(h_ref.dtype)


def _mm_res_norm_kernel(a1_ref, a2_ref, w_ref, x_ref, g_ref, gain_ref, y_ref, *, n_tiles, tn):
    j = pl.program_id(1)
    x2 = x_ref[...] + g_ref[...] * _dot_halves(a1_ref, a2_ref, w_ref)
    for jj in range(n_tiles):
        @pl.when(j == jj)
        def _(jj=jj):
            y_ref[:, jj * tn:(jj + 1) * tn] = x2

    @pl.when(j == n_tiles - 1)
    def _():
        ss = None
        for jj in range(n_tiles):
            t = y_ref[:, jj * tn:(jj + 1) * tn]
            part = jnp.sum(t * t, axis=-1, keepdims=True)
            ss = part if ss is None else ss + part
        r = lax.rsqrt(ss / (n_tiles * tn) + NORM_EPS)
        for jj in range(n_tiles):
            cols = slice(jj * tn, (jj + 1) * tn)
            y_ref[:, cols] = y_ref[:, cols] * r * gain_ref[:, cols]


def _matmul_residual_norm(a_halves, w, x, mod, is_prompt, gain, next_mod=None, tn=1024, tm=512):
    (a1, c1), (a2, c2) = a_halves
    m = a1.shape[0]
    k, n = w.shape
    n_tiles = n // tn
    row = _mod_row_fn(is_prompt, tm)
    gate_block0 = 2 * n // tn
    in_specs = [pl.BlockSpec((tm, k // 2), lambda i, j: (i, c1)),
                pl.BlockSpec((tm, k // 2), lambda i, j: (i, c2)),
                pl.BlockSpec((k, tn), lambda i, j: (0, j)),
                pl.BlockSpec((tm, tn), lambda i, j: (i, j)),
                pl.BlockSpec((None, 1, tn), lambda i, j: (row(i), 0, gate_block0 + j)),
                pl.BlockSpec((1, n), lambda i, j: (0, 0))]
    args = [a1, a2, w, x, mod, gain.reshape(1, n)]
    scratch = [pltpu.VMEM((n_tiles, tm, tn), F32)]
    params = pltpu.CompilerParams(dimension_semantics=("arbitrary", "arbitrary"),
                                  vmem_limit_bytes=VMEM_LIMIT_ROWS)
    if next_mod is None:
        return pl.pallas_call(
            functools.partial(_mm_res_norm_kernel, n_tiles=n_tiles, tn=tn),
            grid=(m // tm, n_tiles),
            in_specs=in_specs,
            out_specs=pl.BlockSpec((tm, n), lambda i, j: (i, 0)),
            out_shape=jax.ShapeDtypeStruct((m, n), F32),
            compiler_params=params,
            name="proj_residual_norm",
        )(*args)
    in_specs += [pl.BlockSpec((None, 1, n), lambda i, j: (row(i), 0, 0)),
                 pl.BlockSpec((None, 1, n), lambda i, j: (row(i), 0, 1))]
    args += [next_mod, next_mod]
    return pl.pallas_call(
        functools.partial(_mm_res_normmod_kernel, n_tiles=n_tiles, tn=tn),
        grid=(m // tm, n_tiles),
        in_specs=in_specs,
        out_specs=[pl.BlockSpec((tm, tn), lambda i, j: (i, j)),
                   pl.BlockSpec((tm, n), lambda i, j: (i, 0))],
        out_shape=[jax.ShapeDtypeStruct((m, n), F32), jax.ShapeDtypeStruct((m, n), BF16)],
        scratch_shapes=scratch,
        compiler_params=params,
        name="proj_residual_normmod",
    )(*args)


def _ctx_attn_kernel(sink_ref, qg_ref, ka_ref, va_ref, kb_ref, vb_ref, o_ref):
    n = SEQ
    gate0 = MIX_WIDTH
    for kv in range(A_KV_HEADS):
        cols = slice(kv * HEAD_DIM, (kv + 1) * HEAD_DIM)
        k = ka_ref[:, cols].astype(BF16)
        v = va_ref[:, cols].astype(BF16)
        heads = [kv * A_GROUPS + g for g in range(A_GROUPS)]
        q = jnp.concatenate(
            [qg_ref[:, h * HEAD_DIM:(h + 1) * HEAD_DIM] for h in heads], axis=0)
        sink = jnp.concatenate(
            [jnp.full((n, 1), sink_ref[h], F32) for h in heads], axis=0)
        s = _dot_nt(q, k) * ATTN_SCALE
        m = jnp.maximum(jnp.max(s, axis=-1, keepdims=True), sink)
        e = jnp.exp(s - m)
        l = jnp.sum(e, axis=-1, keepdims=True) + jnp.exp(sink - m)
        o = _dot(e.astype(BF16), v) / l
        for g, h in enumerate(heads):
            hc = slice(h * HEAD_DIM, (h + 1) * HEAD_DIM)
            gate = qg_ref[:, gate0 + h * HEAD_DIM:gate0 + (h + 1) * HEAD_DIM].astype(F32)
            o_ref[:, hc] = (o[g * n:(g + 1) * n] * _silu(gate)).astype(o_ref.dtype)
    for h in range(B_HEADS):
        cols = slice(h * HEAD_DIM, (h + 1) * HEAD_DIM)
        q = qg_ref[:, A_WIDTH + h * HEAD_DIM:A_WIDTH + (h + 1) * HEAD_DIM]
        k = kb_ref[:, cols].astype(BF16)
        v = vb_ref[:, cols].astype(BF16)
        s = _dot_nt(q, k) * ATTN_SCALE
        m = jnp.max(s, axis=-1, keepdims=True)
        e = jnp.exp(s - m)
        l = jnp.sum(e, axis=-1, keepdims=True)
        o = _dot(e.astype(BF16), v) / l
        oc = slice(A_WIDTH + h * HEAD_DIM, A_WIDTH + (h + 1) * HEAD_DIM)
        gate = qg_ref[:, gate0 + oc.start:gate0 + oc.stop].astype(F32)
        o_ref[:, oc] = (o * _silu(gate)).astype(o_ref.dtype)


def _ctx_attention(sink, qg, ka, va, kb, vb):
    return pl.pallas_call(
        _ctx_attn_kernel,
        grid=(BATCH,),
        in_specs=[
            pl.BlockSpec(memory_space=pltpu.SMEM),
            pl.BlockSpec((SEQ, 2 * MIX_WIDTH), lambda b: (b, 0)),
            pl.BlockSpec((SEQ, A_KV_WIDTH), lambda b: (b, 0)),
            pl.BlockSpec((SEQ, A_KV_WIDTH), lambda b: (b, 0)),
            pl.BlockSpec((SEQ, B_WIDTH), lambda b: (b, 0)),
            pl.BlockSpec((SEQ, B_WIDTH), lambda b: (b, 0)),
        ],
        out_specs=pl.BlockSpec((SEQ, MIX_WIDTH), lambda b: (b, 0)),
        out_shape=jax.ShapeDtypeStruct((N_PROMPT, MIX_WIDTH), BF16),
        compiler_params=_params(1),
        name="ctx_attention",
    )(sink, qg, ka, va, kb, vb)


def _win_attn_kernel(sink_ref, q_ref, k_ref, v_ref, ck_ref, cv_ref, gate_ref,
                     cos_ref, sin_ref, o_ref, qs_ref, ks_ref):
    kv = pl.program_id(1)
    n = DEC_SEQ
    cos = cos_ref[...]
    sin = sin_ref[...]
    lane = lax.broadcasted_iota(jnp.int32, (n, HEAD_DIM), 1)
    first_quarter = (lane % (HEAD_DIM // 2)) < (HEAD_DIM // 4)

    def rope(x):
        rot = jnp.where(first_quarter,
                        pltpu.roll(x, HEAD_DIM - HEAD_DIM // 4, 1),
                        pltpu.roll(x, HEAD_DIM // 4, 1))
        return x * cos + rot * sin

    ks_ref[...] = rope(k_ref[...].astype(F32)).astype(BF16)
    for g in range(A_GROUPS):
        qs_ref[g] = (rope(q_ref[:, g * HEAD_DIM:(g + 1) * HEAD_DIM].astype(F32))
                     * ATTN_SCALE).astype(BF16)

    ck = ck_ref[:, kv, :].astype(BF16)
    cv = cv_ref[:, kv, :].astype(BF16)
    rows = A_GROUPS * A_BLOCK
    sink = jnp.concatenate(
        [jnp.full((A_BLOCK, 1), sink_ref[kv * A_GROUPS + g], F32) for g in range(A_GROUPS)],
        axis=0)
    qi = lax.broadcasted_iota(jnp.int32, (rows, 3 * A_BLOCK), 0) % A_BLOCK
    rel = lax.broadcasted_iota(jnp.int32, (rows, 3 * A_BLOCK), 1) - A_BLOCK - qi
    band = jnp.where(jnp.abs(rel) <= A_HALF_WIN, 0.0, NEG_INF)
    for blk in range(n // A_BLOCK):
        r0 = blk * A_BLOCK
        lo = max(0, r0 - A_BLOCK)
        hi = min(n, r0 + 2 * A_BLOCK)
        q = jnp.concatenate([qs_ref[g, r0:r0 + A_BLOCK, :] for g in range(A_GROUPS)], axis=0)
        s_w = _dot_nt(q, ks_ref[lo:hi, :]) + band[:, lo - (r0 - A_BLOCK):hi - (r0 - A_BLOCK)]
        s_c = _dot_nt(q, ck)
        m = jnp.maximum(jnp.max(s_w, axis=-1, keepdims=True),
                        jnp.max(s_c, axis=-1, keepdims=True))
        m = jnp.maximum(m, sink)
        e_w = jnp.exp(s_w - m)
        e_c = jnp.exp(s_c - m)
        l = (jnp.sum(e_w, axis=-1, keepdims=True) + jnp.sum(e_c, axis=-1, keepdims=True)
             + jnp.exp(sink - m))
        o = (_dot(e_w.astype(BF16), v_ref[lo:hi, :]) + _dot(e_c.astype(BF16), cv)) / l
        for g in range(A_GROUPS):
            hc = slice(g * HEAD_DIM, (g + 1) * HEAD_DIM)
            gate = gate_ref[r0:r0 + A_BLOCK, hc].astype(F32)
            o_ref[r0:r0 + A_BLOCK, hc] = (
                o[g * A_BLOCK:(g + 1) * A_BLOCK] * _silu(gate)).astype(o_ref.dtype)


def _win_attention(sink, proj, cache_k, cache_v, cos, sin_signed):
    gw = A_GROUPS * HEAD_DIM
    n = DEC_SEQ
    cache_spec = pl.BlockSpec((None, None, PAST_LEN, A_KV_HEADS, HEAD_DIM),
                              lambda b, k: (b, 0, 0, 0, 0))
    return pl.pallas_call(
        _win_attn_kernel,
        grid=(DEC_BATCH, A_KV_HEADS),
        in_specs=[
            pl.BlockSpec(memory_space=pltpu.SMEM),
            pl.BlockSpec((n, gw), lambda b, k: (b, COL_QA // gw + k)),
            pl.BlockSpec((n, HEAD_DIM), lambda b, k: (b, COL_KA // HEAD_DIM + k)),
            pl.BlockSpec((n, HEAD_DIM), lambda b, k: (b, COL_VA // HEAD_DIM + k)),
            cache_spec,
            cache_spec,
            pl.BlockSpec((n, gw), lambda b, k: (b, COL_GATE // gw + k)),
            pl.BlockSpec((n, HEAD_DIM), lambda b, k: (0, 0)),
            pl.BlockSpec((n, HEAD_DIM), lambda b, k: (0, 0)),
        ],
        out_specs=pl.BlockSpec((n, gw), lambda b, k: (b, k)),
        out_shape=jax.ShapeDtypeStruct((N_SAMPLE, A_WIDTH), BF16),
        scratch_shapes=[pltpu.VMEM((A_GROUPS, n, HEAD_DIM), BF16),
                        pltpu.VMEM((n, HEAD_DIM), BF16)],
        compiler_params=_params(2),
        name="win_attention",
    )(sink, proj, proj, proj, cache_k, cache_v, proj, cos, sin_signed)


def _bias_kernel(rpb_ref, o_ref):
    h = pl.program_id(0)
    shape = (GRID_W, 2 * GRID_W)
    c = lax.broadcasted_iota(jnp.int32, shape, 0)
    j2 = lax.broadcasted_iota(jnp.int32, shape, 1)
    kc = j2 % GRID_W
    second = j2 >= GRID_W
    col_start = jnp.clip(c - B_WIN_COLS // 2, 0, GRID_W - B_WIN_COLS)
    ok = (kc >= col_start) & (kc < col_start + B_WIN_COLS)
    dc = kc - c + B_WIN_COLS - 1
    base = h * (RPB_ROWS * RPB_COLS)
    pair_scr = []
    for i in range(N_BIAS_PAIRS):
        acc = jnp.full(shape, NEG_INF, F32)
        for d in range(RPB_COLS):
            val = jnp.where(second, rpb_ref[base + (i + 1) * RPB_COLS + d],
                            rpb_ref[base + i * RPB_COLS + d])
            acc = jnp.where(ok & (dc == d), val, acc)
        pair_scr.append(acc)
    for r in range(GRID_ROWS):
        dr0 = _na_key_row0(r) - r + B_WIN_ROWS - 1
        for i in range(B_WIN_ROWS // 2):
            o_ref[r * GRID_W:(r + 1) * GRID_W, i * 2 * GRID_W:(i + 1) * 2 * GRID_W] = (
                pair_scr[dr0 + 2 * i])


def _expand_bias(rpb):
    kw = B_WIN_ROWS * GRID_W
    return pl.pallas_call(
        _bias_kernel,
        grid=(B_HEADS,),
        in_specs=[pl.BlockSpec(memory_space=pltpu.SMEM)],
        out_specs=pl.BlockSpec((None, DEC_SEQ, kw), lambda h: (h, 0, 0)),
        out_shape=jax.ShapeDtypeStruct((B_HEADS, DEC_SEQ, kw), F32),
        compiler_params=_params(1),
        name="expand_bias",
    )(rpb.reshape(-1))


def _na_key_row0(r):
    return min(max(r - B_WIN_ROWS // 2, 0), GRID_ROWS - B_WIN_ROWS)


def _na_row_groups():
    groups = []
    for r in range(GRID_ROWS):
        rs = _na_key_row0(r)
        if groups and groups[-1][2] == rs:
            groups[-1] = (groups[-1][0], r + 1, rs)
        else:
            groups.append((r, r + 1, rs))
    return groups


def _na_attn_kernel(q_ref, k_ref, v_ref, ck_ref, cv_ref, gate_ref, bias_ref, o_ref):
    kw = B_WIN_ROWS * GRID_W
    h = pl.program_id(1)
    groups = _na_row_groups()
    s_c = _dot_nt(q_ref[...], ck_ref[:, h, :].astype(BF16)) * ATTN_SCALE
    s_n = jnp.concatenate(
        [_dot_nt(q_ref[r0 * GRID_W:r1 * GRID_W, :], k_ref[rs * GRID_W:rs * GRID_W + kw, :])
         for r0, r1, rs in groups], axis=0) * ATTN_SCALE + bias_ref[...]
    m = jnp.maximum(jnp.max(s_n, axis=-1, keepdims=True), jnp.max(s_c, axis=-1, keepdims=True))
    e_n = jnp.exp(s_n - m)
    e_c = jnp.exp(s_c - m)
    l = jnp.sum(e_n, axis=-1, keepdims=True) + jnp.sum(e_c, axis=-1, keepdims=True)
    p_n = e_n.astype(BF16)
    o_n = jnp.concatenate(
        [_dot(p_n[r0 * GRID_W:r1 * GRID_W], v_ref[rs * GRID_W:rs * GRID_W + kw, :])
         for r0, r1, rs in groups], axis=0)
    o = (o_n + _dot(e_c.astype(BF16), cv_ref[:, h, :].astype(BF16))) / l
    o_ref[...] = (o * _silu(gate_ref[...].astype(F32))).astype(o_ref.dtype)


def _na_attention(proj, cache_k, cache_v, bias):
    n = DEC_SEQ
    hd = HEAD_DIM
    cache_spec = pl.BlockSpec((None, None, PAST_LEN, B_HEADS, hd), lambda b, h: (b, 0, 0, 0, 0))
    return pl.pallas_call(
        _na_attn_kernel,
        grid=(DEC_BATCH, B_HEADS),
        in_specs=[
            pl.BlockSpec((n, hd), lambda b, h: (b, COL_QB // hd + h)),
            pl.BlockSpec((n, hd), lambda b, h: (b, COL_KB // hd + h)),
            pl.BlockSpec((n, hd), lambda b, h: (b, COL_VB // hd + h)),
            cache_spec,
            cache_spec,
            pl.BlockSpec((n, hd), lambda b, h: (b, (COL_GATE + A_WIDTH) // hd + h)),
            pl.BlockSpec((None, n, B_WIN_ROWS * GRID_W), lambda b, h: (h, 0, 0)),
        ],
        out_specs=pl.BlockSpec((n, hd), lambda b, h: (b, h)),
        out_shape=jax.ShapeDtypeStruct((N_SAMPLE, B_WIDTH), BF16),
        compiler_params=_params(2),
        name="na_attention",
    )(proj, proj, proj, cache_k, cache_v, proj, bias)


def _pool_kernel(u_ref, gate_ref, w_ref, scale_ref, o_ref, *, seq, rows):
    grp = pl.program_id(1)
    t = lax.broadcasted_iota(jnp.int32, (rows, rows), 0)
    j = lax.broadcasted_iota(jnp.int32, (rows, rows), 1)
    delta = j - t
    same_seq = (t // seq) == (j // seq)
    pos = lax.broadcasted_iota(jnp.int32, (rows, 1), 0) % seq
    for gi, window in enumerate(POOL_WINDOWS):
        half = window // 2

        @pl.when(grp == gi)
        def _(half=half):
            band = jnp.where((delta >= -half) & (delta < half) & same_seq, 1.0, 0.0).astype(BF16)
            u = u_ref[...]
            wsum = _dot(band, u)
            count = (jnp.minimum(pos + half, seq) - jnp.maximum(pos - half, 0)).astype(F32)
            pooled = wsum / count - u.astype(F32)
            y = _dot(pooled.astype(BF16), w_ref[...]) * scale_ref[...]
            o_ref[...] = (y * _silu(gate_ref[...].astype(F32))).astype(o_ref.dtype)


def _pool_mixer(ug, w_grp, scale, seq, rows=1024):
    m = ug.shape[0]
    n_groups = len(POOL_WINDOWS)
    pg = POOL_GROUP
    return pl.pallas_call(
        functools.partial(_pool_kernel, seq=seq, rows=rows),
        grid=(m // rows, n_groups),
        in_specs=[
            pl.BlockSpec((rows, pg), lambda i, g: (i, g)),
            pl.BlockSpec((rows, pg), lambda i, g: (i, n_groups + g)),
            pl.BlockSpec((None, pg, pg), lambda i, g: (g, 0, 0)),
            pl.BlockSpec((1, pg), lambda i, g: (0, g)),
        ],
        out_specs=pl.BlockSpec((rows, pg), lambda i, g: (i, g)),
        out_shape=jax.ShapeDtypeStruct((m, n_groups * pg), BF16),
        compiler_params=_params(2),
        name="pool_mixer",
    )(ug, ug, w_grp, scale.reshape(1, -1))


def _rope_tables():
    t = jnp.arange(DEC_SEQ)
    quarter = HEAD_DIM // 4
    inv_freq = ROPE_BASE ** (-jnp.arange(quarter, dtype=F32) / quarter)
    ang_r = (t // GRID_W).astype(F32)[:, None] * inv_freq
    ang_c = (t % GRID_W).astype(F32)[:, None] * inv_freq
    ang = jnp.concatenate([ang_r, ang_r, ang_c, ang_c], axis=-1)
    sign = jnp.tile(jnp.concatenate([-jnp.ones((quarter,), F32), jnp.ones((quarter,), F32)]), 2)
    return jnp.cos(ang), jnp.sin(ang) * sign


def kernel(x_prompt, x_sample, c, cache_a_k, cache_a_v, cache_b_k, cache_b_v, c_ctx,
           w_ada, b_ada, norm_g, w_in_attn, a_sink, b_rpb, w_out_attn,
           w_in_pool, w_grp_pool, pool_scale, w_out_pool, final_g):
    d = D_MODEL
    xp = x_prompt.reshape(N_PROMPT, d)
    xs = x_sample.reshape(N_SAMPLE, d)

    w = w_in_attn[0]
    o_qa = 0
    o_ka = o_qa + A_WIDTH
    o_va = o_ka + A_KV_WIDTH
    o_qb = o_va + A_KV_WIDTH
    o_kb = o_qb + B_WIDTH
    o_vb = o_kb + B_WIDTH
    o_gate = o_vb + B_WIDTH
    w_in0 = jnp.concatenate([
        w[:, o_ka:o_va], w[:, o_va:o_qb], w[:, o_kb:o_vb], w[:, o_vb:o_gate],
        w[:, o_qa:o_ka], w[:, o_qb:o_kb], w[:, o_gate:]], axis=1).astype(BF16)
    w_out0 = w_out_attn[0].astype(BF16)
    w_in1 = w_in_pool[0].astype(BF16)
    w_grp = w_grp_pool[0].astype(BF16)
    w_out1 = w_out_pool[0].astype(BF16)

    cond = jnp.zeros((MOD_ROWS, d), F32).at[:DEC_BATCH].set(c).at[CTX_MOD_ROW].set(c_ctx)
    mod = _ada(cond, w_ada, b_ada)
    mod0 = mod[0].reshape(MOD_ROWS, 1, 3 * d)
    mod1 = mod[1].reshape(MOD_ROWS, 1, 3 * d)

    hp = _norm_mod(xp, norm_g[0], mod0, True)
    hs = _norm_mod(xs, norm_g[0], mod0, False)

    ka = _matmul(hp, w_in0, COL_KA, A_KV_WIDTH, F32, tn=512)
    va = _matmul(hp, w_in0, COL_VA, A_KV_WIDTH, F32, tn=512)
    kb = _matmul(hp, w_in0, COL_KB, B_WIDTH, F32)
    vb = _matmul(hp, w_in0, COL_VB, B_WIDTH, F32)
    qg_p = _matmul(hp, w_in0, COL_QA, 2 * MIX_WIDTH, BF16)
    proj_s = _matmul(hs, w_in0, 0, ATTN_IN_WIDTH, BF16)

    sink = a_sink[0]
    og_p = _ctx_attention(sink, qg_p, ka, va, kb, vb)

    cos, sin_signed = _rope_tables()
    oa_s = _win_attention(sink, proj_s, cache_a_k, cache_a_v, cos, sin_signed)
    ob_s = _na_attention(proj_s, cache_b_k, cache_b_v, _expand_bias(b_rpb[0]))

    xp1, hp1 = _matmul_residual_norm([(og_p, 0), (og_p, 1)], w_out0, xp, mod0, True,
                                     norm_g[1], mod1)
    xs1, hs1 = _matmul_residual_norm([(oa_s, 0), (ob_s, 0)], w_out0, xs, mod0, False,
                                     norm_g[1], mod1)

    ug_p = _matmul(hp1, w_in1, 0, 2 * d, BF16)
    ug_s = _matmul(hs1, w_in1, 0, 2 * d, BF16)
    y_p = _pool_mixer(ug_p, w_grp, pool_scale[0], SEQ)
    y_s = _pool_mixer(ug_s, w_grp, pool_scale[0], DEC_SEQ)
    y_prompt = _matmul_residual_norm([(y_p, 0), (y_p, 1)], w_out1, xp1, mod1, True,
                                     final_g).reshape(BATCH, SEQ, d)
    y_sample = _matmul_residual_norm([(y_s, 0), (y_s, 1)], w_out1, xs1, mod1, False,
                                     final_g).reshape(DEC_BATCH, DEC_SEQ, d)

    kv_shape_a = (BATCH, 1, SEQ, A_KV_HEADS, HEAD_DIM)
    kv_shape_b = (BATCH, 1, SEQ, B_HEADS, HEAD_DIM)
    return (y_prompt, y_sample, ka.reshape(kv_shape_a), va.reshape(kv_shape_a),
            kb.reshape(kv_shape_b), vb.reshape(kv_shape_b))
```

```python
import functools

import jax
import jax.numpy as jnp
from jax import lax
from jax.experimental import pallas as pl
from jax.experimental.pallas import tpu as pltpu

F32 = jnp.float32
BF16 = jnp.bfloat16

D_MODEL = 4096
BATCH = 32
SEQ = 256
DEC_BATCH = 8
DEC_SEQ = 1024
PAST_LEN = 512
GRID_W = 64
HEAD_DIM = 128
A_Q_HEADS = 16
A_KV_HEADS = 4
A_GROUPS = 4
A_HALF_WIN = 128
A_BLOCK = 128
B_HEADS = 16
B_WIN_ROWS = 8
B_WIN_COLS = 16
A_WIDTH = A_Q_HEADS * HEAD_DIM
A_KV_WIDTH = A_KV_HEADS * HEAD_DIM
B_WIDTH = B_HEADS * HEAD_DIM
MIX_WIDTH = A_WIDTH + B_WIDTH
POOL_WINDOWS = (2, 4, 8, 16)
POOL_GROUP = 1024
ROPE_BASE = 10000.0
NORM_EPS = 1e-6
NEG_INF = -1e30
ATTN_SCALE = HEAD_DIM ** -0.5
LOG2E = 1.4426950408889634
LOGIT_SCALE = ATTN_SCALE * LOG2E
CTX_CHUNK = 1024
WIN_CHUNK = 512
NA_CHUNK = 1024

N_PROMPT = BATCH * SEQ
N_SAMPLE = DEC_BATCH * DEC_SEQ
MOD_ROWS = 16
CTX_MOD_ROW = DEC_BATCH
GRID_ROWS = DEC_SEQ // GRID_W
N_BIAS_PAIRS = 2 * B_WIN_ROWS - 2
RPB_ROWS = 2 * B_WIN_ROWS - 1
RPB_COLS = 2 * B_WIN_COLS - 1

COL_KA = 0
COL_VA = COL_KA + A_KV_WIDTH
COL_KB = COL_VA + A_KV_WIDTH
COL_VB = COL_KB + B_WIDTH
COL_QA = COL_VB + B_WIDTH
COL_QB = COL_QA + A_WIDTH
COL_GATE = COL_QB + B_WIDTH
ATTN_IN_WIDTH = COL_GATE + MIX_WIDTH

VMEM_LIMIT = 48 * 1024 * 1024
VMEM_LIMIT_ROWS = 56 * 1024 * 1024


def _params(n_grid_dims):
    return pltpu.CompilerParams(
        dimension_semantics=("arbitrary",) * n_grid_dims,
        vmem_limit_bytes=VMEM_LIMIT,
    )


def _silu(x):
    return x / (1.0 + jnp.exp(-x))


def _dot_nt(a, b):
    return lax.dot_general(a, b, (((1,), (1,)), ((), ())), preferred_element_type=F32)


def _dot(a, b):
    return jnp.dot(a, b, preferred_element_type=F32)


def _ada_kernel(cond_ref, w_ref, b_ref, o_ref):
    a = _silu(cond_ref[...]).astype(BF16)
    o_ref[...] = _dot(a, w_ref[...].astype(BF16)) + b_ref[...]


def _ada(cond, w_ada, b_ada, tn=512):
    depth, d, n = w_ada.shape
    return pl.pallas_call(
        _ada_kernel,
        grid=(depth, n // tn),
        in_specs=[
            pl.BlockSpec((MOD_ROWS, d), lambda l, j: (0, 0)),
            pl.BlockSpec((None, d, tn), lambda l, j: (l, 0, j)),
            pl.BlockSpec((None, 1, tn), lambda l, j: (l, 0, j)),
        ],
        out_specs=pl.BlockSpec((None, MOD_ROWS, tn), lambda l, j: (l, 0, j)),
        out_shape=jax.ShapeDtypeStruct((depth, MOD_ROWS, n), F32),
        compiler_params=_params(2),
        name="ada",
    )(cond, w_ada, b_ada.reshape(depth, 1, n))


def _mod_row_fn(is_prompt, tm):
    if is_prompt:
        return lambda i: CTX_MOD_ROW
    return lambda i: (i * tm) // DEC_SEQ


def _norm_mod_kernel(x_ref, g_ref, sh_ref, sc_ref, o_ref):
    x = x_ref[...]
    ms = jnp.mean(x * x, axis=-1, keepdims=True)
    y = x * lax.rsqrt(ms + NORM_EPS) * g_ref[...]
    o_ref[...] = (y * (1.0 + sc_ref[...]) + sh_ref[...]).astype(o_ref.dtype)


def _norm_mod(x, gain, mod, is_prompt, tm=256):
    m, d = x.shape
    row = _mod_row_fn(is_prompt, tm)
    return pl.pallas_call(
        _norm_mod_kernel,
        grid=(m // tm,),
        in_specs=[
            pl.BlockSpec((tm, d), lambda i: (i, 0)),
            pl.BlockSpec((1, d), lambda i: (0, 0)),
            pl.BlockSpec((None, 1, d), lambda i: (row(i), 0, 0)),
            pl.BlockSpec((None, 1, d), lambda i: (row(i), 0, 1)),
        ],
        out_specs=pl.BlockSpec((tm, d), lambda i: (i, 0)),
        out_shape=jax.ShapeDtypeStruct((m, d), BF16),
        compiler_params=_params(1),
        name="norm_mod",
    )(x, gain.reshape(1, d), mod, mod)


def _mm_kernel(a_ref, w_ref, o_ref):
    o_ref[...] = _dot(a_ref[...], w_ref[...]).astype(o_ref.dtype)


def _matmul(a, w, col0, ncols, out_dtype, tn=1024):
    m, k = a.shape
    tm = 1024 if out_dtype == BF16 else 512
    assert col0 % tn == 0 and ncols % tn == 0 and m % tm == 0
    j0 = col0 // tn
    return pl.pallas_call(
        _mm_kernel,
        grid=(ncols // tn, m // tm),
        in_specs=[pl.BlockSpec((tm, k), lambda j, i: (i, 0)),
                  pl.BlockSpec((k, tn), lambda j, i: (0, j + j0))],
        out_specs=pl.BlockSpec((tm, tn), lambda j, i: (i, j)),
        out_shape=jax.ShapeDtypeStruct((m, ncols), out_dtype),
        compiler_params=_params(2),
        name="proj",
    )(a, w)


def _row_rsqrt(row_scr, n_tiles):
    ss = None
    for jj in range(n_tiles):
        t = row_scr[jj]
        part = jnp.sum(t * t, axis=-1, keepdims=True)
        ss = part if ss is None else ss + part
    width = n_tiles * row_scr.shape[2]
    return lax.rsqrt(ss / width + NORM_EPS)


def _dot_halves(a1_ref, a2_ref, w_ref):
    kh = a1_ref.shape[1]
    return _dot(a1_ref[...], w_ref[:kh, :]) + _dot(a2_ref[...], w_ref[kh:, :])


def _mm_res_normmod_kernel(a1_ref, a2_ref, w_ref, x_ref, g_ref, gain_ref, sh_ref, sc_ref,
                           x1_ref, h_ref, row_scr, *, n_tiles, tn):
    j = pl.program_id(1)
    x1 = x_ref[...] + g_ref[...] * _dot_halves(a1_ref, a2_ref, w_ref)
    x1_ref[...] = x1
    row_scr[j] = x1

    @pl.when(j == n_tiles - 1)
    def _():
        r = _row_rsqrt(row_scr, n_tiles)
        for jj in range(n_tiles):
            cols = slice(jj * tn, (jj + 1) * tn)
            y = row_scr[jj] * r * gain_ref[:, cols]
            h_ref[:, cols] = (y * (1.0 + sc_ref[:, cols]) + sh_ref[:, cols]).astype(h_ref.dtype)


def _mm_res_norm_kernel(a1_ref, a2_ref, w_ref, x_ref, g_ref, gain_ref, y_ref, *, n_tiles, tn):
    j = pl.program_id(1)
    x2 = x_ref[...] + g_ref[...] * _dot_halves(a1_ref, a2_ref, w_ref)
    for jj in range(n_tiles):
        @pl.when(j == jj)
        def _(jj=jj):
            y_ref[:, jj * tn:(jj + 1) * tn] = x2

    @pl.when(j == n_tiles - 1)
    def _():
        ss = None
        for jj in range(n_tiles):
            t = y_ref[:, jj * tn:(jj + 1) * tn]
            part = jnp.sum(t * t, axis=-1, keepdims=True)
            ss = part if ss is None else ss + part
        r = lax.rsqrt(ss / (n_tiles * tn) + NORM_EPS)
        for jj in range(n_tiles):
            cols = slice(jj * tn, (jj + 1) * tn)
            y_ref[:, cols] = y_ref[:, cols] * r * gain_ref[:, cols]


def _matmul_residual_norm(a_halves, w, x, mod, is_prompt, gain, next_mod=None, tn=1024, tm=512):
    (a1, c1), (a2, c2) = a_halves
    m = a1.shape[0]
    k, n = w.shape
    n_tiles = n // tn
    row = _mod_row_fn(is_prompt, tm)
    gate_block0 = 2 * n // tn
    in_specs = [pl.BlockSpec((tm, k // 2), lambda i, j: (i, c1)),
                pl.BlockSpec((tm, k // 2), lambda i, j: (i, c2)),
                pl.BlockSpec((k, tn), lambda i, j: (0, j)),
                pl.BlockSpec((tm, tn), lambda i, j: (i, j)),
                pl.BlockSpec((None, 1, tn), lambda i, j: (row(i), 0, gate_block0 + j)),
                pl.BlockSpec((1, n), lambda i, j: (0, 0))]
    args = [a1, a2, w, x, mod, gain.reshape(1, n)]
    scratch = [pltpu.VMEM((n_tiles, tm, tn), F32)]
    params = pltpu.CompilerParams(dimension_semantics=("arbitrary", "arbitrary"),
                                  vmem_limit_bytes=VMEM_LIMIT_ROWS)
    if next_mod is None:
        return pl.pallas_call(
            functools.partial(_mm_res_norm_kernel, n_tiles=n_tiles, tn=tn),
            grid=(m // tm, n_tiles),
            in_specs=in_specs,
            out_specs=pl.BlockSpec((tm, n), lambda i, j: (i, 0)),
            out_shape=jax.ShapeDtypeStruct((m, n), F32),
            compiler_params=params,
            name="proj_residual_norm",
        )(*args)
    in_specs += [pl.BlockSpec((None, 1, n), lambda i, j: (row(i), 0, 0)),
                 pl.BlockSpec((None, 1, n), lambda i, j: (row(i), 0, 1))]
    args += [next_mod, next_mod]
    return pl.pallas_call(
        functools.partial(_mm_res_normmod_kernel, n_tiles=n_tiles, tn=tn),
        grid=(m // tm, n_tiles),
        in_specs=in_specs,
        out_specs=[pl.BlockSpec((tm, tn), lambda i, j: (i, j)),
                   pl.BlockSpec((tm, n), lambda i, j: (i, 0))],
        out_shape=[jax.ShapeDtypeStruct((m, n), F32), jax.ShapeDtypeStruct((m, n), BF16)],
        scratch_shapes=scratch,
        compiler_params=params,
        name="proj_residual_normmod",
    )(*args)


def _chunked_softmax(n_rows, chunk, loads, stores, sink2=None):
    inv = []
    for c in range(n_rows // chunk):
        rows = slice(c * chunk, (c + 1) * chunk)
        ts = [load(rows) for load in loads]
        m = functools.reduce(jnp.maximum, [jnp.max(t, axis=-1, keepdims=True) for t in ts])
        if sink2 is not None:
            m = jnp.maximum(m, sink2[rows])
        es = [jnp.exp2(t - m) for t in ts]
        l = functools.reduce(jnp.add, [jnp.sum(e, axis=-1, keepdims=True) for e in es])
        if sink2 is not None:
            l = l + jnp.exp2(sink2[rows] - m)
        for store, e in zip(stores, es):
            store(rows, e.astype(BF16))
        inv.append(1.0 / l)
    return jnp.concatenate(inv, axis=0)


def _ctx_attn_kernel(sink_ref, qg_ref, ka_ref, va_ref, kb_ref, vb_ref, o_ref, s_scr, e_scr):
    n = SEQ
    gate0 = MIX_WIDTH
    n_stack = A_GROUPS

    def softmax(sink2):
        def store(rows, e):
            e_scr[rows, :] = e
        return _chunked_softmax(n_stack * n, CTX_CHUNK, [lambda rows: s_scr[rows, :] * LOGIT_SCALE],
                                [store], sink2)

    def emit(o, g, out_col):
        cols = slice(out_col, out_col + HEAD_DIM)
        gate = qg_ref[:, gate0 + out_col:gate0 + out_col + HEAD_DIM].astype(F32)
        o_ref[:, cols] = (o[g * n:(g + 1) * n] * _silu(gate)).astype(o_ref.dtype)

    for kv in range(A_KV_HEADS):
        cols = slice(kv * HEAD_DIM, (kv + 1) * HEAD_DIM)
        heads = [kv * A_GROUPS + g for g in range(A_GROUPS)]
        q = jnp.concatenate(
            [qg_ref[:, h * HEAD_DIM:(h + 1) * HEAD_DIM] for h in heads], axis=0)
        sink2 = jnp.concatenate(
            [jnp.full((n, 1), sink_ref[h] * LOG2E, F32) for h in heads], axis=0)
        s_scr[...] = _dot_nt(q, ka_ref[:, cols].astype(BF16))
        inv = softmax(sink2)
        o = _dot(e_scr[...], va_ref[:, cols].astype(BF16)) * inv
        for g, h in enumerate(heads):
            emit(o, g, h * HEAD_DIM)
    for h0 in range(0, B_HEADS, n_stack):
        heads = range(h0, h0 + n_stack)
        for g, h in enumerate(heads):
            cols = slice(h * HEAD_DIM, (h + 1) * HEAD_DIM)
            q = qg_ref[:, A_WIDTH + h * HEAD_DIM:A_WIDTH + (h + 1) * HEAD_DIM]
            s_scr[g * n:(g + 1) * n, :] = _dot_nt(q, kb_ref[:, cols].astype(BF16))
        inv = softmax(None)
        o = jnp.concatenate(
            [_dot(e_scr[g * n:(g + 1) * n, :],
                  vb_ref[:, h * HEAD_DIM:(h + 1) * HEAD_DIM].astype(BF16))
             for g, h in enumerate(heads)], axis=0) * inv
        for g, h in enumerate(heads):
            emit(o, g, A_WIDTH + h * HEAD_DIM)


def _ctx_attention(sink, qg, ka, va, kb, vb):
    stack_rows = A_GROUPS * SEQ
    return pl.pallas_call(
        _ctx_attn_kernel,
        grid=(BATCH,),
        in_specs=[
            pl.BlockSpec(memory_space=pltpu.SMEM),
            pl.BlockSpec((SEQ, 2 * MIX_WIDTH), lambda b: (b, 0)),
            pl.BlockSpec((SEQ, A_KV_WIDTH), lambda b: (b, 0)),
            pl.BlockSpec((SEQ, A_KV_WIDTH), lambda b: (b, 0)),
            pl.BlockSpec((SEQ, B_WIDTH), lambda b: (b, 0)),
            pl.BlockSpec((SEQ, B_WIDTH), lambda b: (b, 0)),
        ],
        out_specs=pl.BlockSpec((SEQ, MIX_WIDTH), lambda b: (b, 0)),
        out_shape=jax.ShapeDtypeStruct((N_PROMPT, MIX_WIDTH), BF16),
        scratch_shapes=[pltpu.VMEM((stack_rows, SEQ), F32),
                        pltpu.VMEM((stack_rows, SEQ), BF16)],
        compiler_params=_params(1),
        name="ctx_attention",
    )(sink, qg, ka, va, kb, vb)


def _split_cache_heads(c_ref, scr, n_heads):
    for h in range(n_heads):
        scr[h] = c_ref[pl.ds(h, PAST_LEN, stride=n_heads), :].astype(scr.dtype)


def _win_attn_kernel(sink_ref, q_ref, k_ref, v_ref, ck_ref, cv_ref, gate_ref,
                     cos_ref, sin_ref, o_ref, qs_ref, ks_ref, ck_scr, cv_scr,
                     band_scr, sw_scr, sc_scr, ew_scr, ec_scr):
    kv = pl.program_id(1)
    n = DEC_SEQ

    @pl.when(kv == 0)
    def _():
        _split_cache_heads(ck_ref, ck_scr, A_KV_HEADS)
        _split_cache_heads(cv_ref, cv_scr, A_KV_HEADS)

    cos = cos_ref[...]
    sin = sin_ref[...]
    lane = lax.broadcasted_iota(jnp.int32, (n, HEAD_DIM), 1)
    first_quarter = (lane % (HEAD_DIM // 2)) < (HEAD_DIM // 4)

    def rope(x):
        rot = jnp.where(first_quarter,
                        pltpu.roll(x, HEAD_DIM - HEAD_DIM // 4, 1),
                        pltpu.roll(x, HEAD_DIM // 4, 1))
        return x * cos + rot * sin

    ks_ref[...] = rope(k_ref[...].astype(F32)).astype(BF16)
    for g in range(A_GROUPS):
        qs_ref[g] = (rope(q_ref[:, g * HEAD_DIM:(g + 1) * HEAD_DIM].astype(F32))
                     * LOGIT_SCALE).astype(BF16)

    ck = ck_scr[kv]
    cv = cv_scr[kv]
    rows = A_GROUPS * A_BLOCK
    sink2 = jnp.concatenate(
        [jnp.full((A_BLOCK, 1), sink_ref[kv * A_GROUPS + g] * LOG2E, F32)
         for g in range(A_GROUPS)], axis=0)
    qi = lax.broadcasted_iota(jnp.int32, (rows, 3 * A_BLOCK), 0) % A_BLOCK
    rel = lax.broadcasted_iota(jnp.int32, (rows, 3 * A_BLOCK), 1) - A_BLOCK - qi
    band_scr[...] = jnp.where(jnp.abs(rel) <= A_HALF_WIN, 0.0, NEG_INF)
    for blk in range(n // A_BLOCK):
        r0 = blk * A_BLOCK
        lo = max(0, r0 - A_BLOCK)
        hi = min(n, r0 + 2 * A_BLOCK)
        span = hi - lo
        b0 = lo - (r0 - A_BLOCK)
        q = jnp.concatenate([qs_ref[g, r0:r0 + A_BLOCK, :] for g in range(A_GROUPS)], axis=0)
        sw_scr[:, :span] = _dot_nt(q, ks_ref[lo:hi, :])
        sc_scr[...] = _dot_nt(q, ck)

        def load_w(r, span=span, b0=b0):
            return sw_scr[r, :span] + band_scr[r, b0:b0 + span]

        def store_w(r, e, span=span):
            ew_scr[r, :span] = e

        def store_c(r, e):
            ec_scr[r, :] = e

        inv = _chunked_softmax(rows, WIN_CHUNK, [load_w, lambda r: sc_scr[r, :]],
                               [store_w, store_c], sink2)
        o = (_dot(ew_scr[:, :span], v_ref[lo:hi, :]) + _dot(ec_scr[...], cv)) * inv
        for g in range(A_GROUPS):
            hc = slice(g * HEAD_DIM, (g + 1) * HEAD_DIM)
            gate = gate_ref[r0:r0 + A_BLOCK, hc].astype(F32)
            o_ref[r0:r0 + A_BLOCK, hc] = (
                o[g * A_BLOCK:(g + 1) * A_BLOCK] * _silu(gate)).astype(o_ref.dtype)


def _win_attention(sink, proj, cache_k, cache_v, cos, sin_signed):
    gw = A_GROUPS * HEAD_DIM
    stack_rows = A_GROUPS * A_BLOCK
    n = DEC_SEQ
    cache_spec = pl.BlockSpec((None, PAST_LEN * A_KV_HEADS, HEAD_DIM), lambda b, k: (b, 0, 0))
    return pl.pallas_call(
        _win_attn_kernel,
        grid=(DEC_BATCH, A_KV_HEADS),
        in_specs=[
            pl.BlockSpec(memory_space=pltpu.SMEM),
            pl.BlockSpec((n, gw), lambda b, k: (b, COL_QA // gw + k)),
            pl.BlockSpec((n, HEAD_DIM), lambda b, k: (b, COL_KA // HEAD_DIM + k)),
            pl.BlockSpec((n, HEAD_DIM), lambda b, k: (b, COL_VA // HEAD_DIM + k)),
            cache_spec,
            cache_spec,
            pl.BlockSpec((n, gw), lambda b, k: (b, COL_GATE // gw + k)),
            pl.BlockSpec((n, HEAD_DIM), lambda b, k: (0, 0)),
            pl.BlockSpec((n, HEAD_DIM), lambda b, k: (0, 0)),
        ],
        out_specs=pl.BlockSpec((n, gw), lambda b, k: (b, k)),
        out_shape=jax.ShapeDtypeStruct((N_SAMPLE, A_WIDTH), BF16),
        scratch_shapes=[pltpu.VMEM((A_GROUPS, n, HEAD_DIM), BF16),
                        pltpu.VMEM((n, HEAD_DIM), BF16),
                        pltpu.VMEM((A_KV_HEADS, PAST_LEN, HEAD_DIM), BF16),
                        pltpu.VMEM((A_KV_HEADS, PAST_LEN, HEAD_DIM), BF16),
                        pltpu.VMEM((stack_rows, 3 * A_BLOCK), F32),
                        pltpu.VMEM((stack_rows, 3 * A_BLOCK), F32),
                        pltpu.VMEM((stack_rows, PAST_LEN), F32),
                        pltpu.VMEM((stack_rows, 3 * A_BLOCK), BF16),
                        pltpu.VMEM((stack_rows, PAST_LEN), BF16)],
        compiler_params=_params(2),
        name="win_attention",
    )(sink, proj, proj, proj, cache_k, cache_v, proj, cos, sin_signed)


def _bias_kernel(rpb_ref, o_ref):
    h = pl.program_id(0)
    shape = (GRID_W, 2 * GRID_W)
    c = lax.broadcasted_iota(jnp.int32, shape, 0)
    j2 = lax.broadcasted_iota(jnp.int32, shape, 1)
    kc = j2 % GRID_W
    second = j2 >= GRID_W
    col_start = jnp.clip(c - B_WIN_COLS // 2, 0, GRID_W - B_WIN_COLS)
    ok = (kc >= col_start) & (kc < col_start + B_WIN_COLS)
    dc = kc - c + B_WIN_COLS - 1
    base = h * (RPB_ROWS * RPB_COLS)
    pair_scr = []
    for i in range(N_BIAS_PAIRS):
        acc = jnp.full(shape, NEG_INF, F32)
        for d in range(RPB_COLS):
            val = jnp.where(second, rpb_ref[base + (i + 1) * RPB_COLS + d],
                            rpb_ref[base + i * RPB_COLS + d]) * LOG2E
            acc = jnp.where(ok & (dc == d), val, acc)
        pair_scr.append(acc)
    for r in range(GRID_ROWS):
        dr0 = _na_key_row0(r) - r + B_WIN_ROWS - 1
        for i in range(B_WIN_ROWS // 2):
            o_ref[r * GRID_W:(r + 1) * GRID_W, i * 2 * GRID_W:(i + 1) * 2 * GRID_W] = (
                pair_scr[dr0 + 2 * i])


def _expand_bias(rpb):
    kw = B_WIN_ROWS * GRID_W
    return pl.pallas_call(
        _bias_kernel,
        grid=(B_HEADS,),
        in_specs=[pl.BlockSpec(memory_space=pltpu.SMEM)],
        out_specs=pl.BlockSpec((None, DEC_SEQ, kw), lambda h: (h, 0, 0)),
        out_shape=jax.ShapeDtypeStruct((B_HEADS, DEC_SEQ, kw), F32),
        compiler_params=_params(1),
        name="expand_bias",
    )(rpb.reshape(-1))


def _na_key_row0(r):
    return min(max(r - B_WIN_ROWS // 2, 0), GRID_ROWS - B_WIN_ROWS)


def _na_row_groups():
    groups = []
    for r in range(GRID_ROWS):
        rs = _na_key_row0(r)
        if groups and groups[-1][2] == rs:
            groups[-1] = (groups[-1][0], r + 1, rs)
        else:
            groups.append((r, r + 1, rs))
    return groups


def _na_attn_kernel(q_ref, k_ref, v_ref, ck_ref, cv_ref, gate_ref, bias_ref, o_ref,
                    ck_scr, cv_scr, sn_scr, sc_scr, en_scr, ec_scr):
    kw = B_WIN_ROWS * GRID_W
    h = pl.program_id(1)

    @pl.when(h == 0)
    def _():
        _split_cache_heads(ck_ref, ck_scr, B_HEADS)
        _split_cache_heads(cv_ref, cv_scr, B_HEADS)

    groups = _na_row_groups()
    sc_scr[...] = _dot_nt(q_ref[...], ck_scr[h])
    for r0, r1, rs in groups:
        sn_scr[r0 * GRID_W:r1 * GRID_W, :] = _dot_nt(
            q_ref[r0 * GRID_W:r1 * GRID_W, :], k_ref[rs * GRID_W:rs * GRID_W + kw, :])

    def store_n(r, e):
        en_scr[r, :] = e

    def store_c(r, e):
        ec_scr[r, :] = e

    inv = _chunked_softmax(
        DEC_SEQ, NA_CHUNK,
        [lambda r: sn_scr[r, :] * LOGIT_SCALE + bias_ref[r, :], lambda r: sc_scr[r, :] * LOGIT_SCALE],
        [store_n, store_c])
    o_n = jnp.concatenate(
        [_dot(en_scr[r0 * GRID_W:r1 * GRID_W, :], v_ref[rs * GRID_W:rs * GRID_W + kw, :])
         for r0, r1, rs in groups], axis=0)
    o = (o_n + _dot(ec_scr[...], cv_scr[h])) * inv
    o_ref[...] = (o * _silu(gate_ref[...].astype(F32))).astype(o_ref.dtype)


def _na_attention(proj, cache_k, cache_v, bias):
    n = DEC_SEQ
    hd = HEAD_DIM
    cache_spec = pl.BlockSpec((None, PAST_LEN * B_HEADS, hd), lambda b, h: (b, 0, 0))
    return pl.pallas_call(
        _na_attn_kernel,
        grid=(DEC_BATCH, B_HEADS),
        in_specs=[
            pl.BlockSpec((n, hd), lambda b, h: (b, COL_QB // hd + h)),
            pl.BlockSpec((n, hd), lambda b, h: (b, COL_KB // hd + h)),
            pl.BlockSpec((n, hd), lambda b, h: (b, COL_VB // hd + h)),
            cache_spec,
            cache_spec,
            pl.BlockSpec((n, hd), lambda b, h: (b, (COL_GATE + A_WIDTH) // hd + h)),
            pl.BlockSpec((None, n, B_WIN_ROWS * GRID_W), lambda b, h: (h, 0, 0)),
        ],
        out_specs=pl.BlockSpec((n, hd), lambda b, h: (b, h)),
        out_shape=jax.ShapeDtypeStruct((N_SAMPLE, B_WIDTH), BF16),
        scratch_shapes=[pltpu.VMEM((B_HEADS, PAST_LEN, hd), BF16),
                        pltpu.VMEM((B_HEADS, PAST_LEN, hd), BF16),
                        pltpu.VMEM((n, B_WIN_ROWS * GRID_W), F32),
                        pltpu.VMEM((n, PAST_LEN), F32),
                        pltpu.VMEM((n, B_WIN_ROWS * GRID_W), BF16),
                        pltpu.VMEM((n, PAST_LEN), BF16)],
        compiler_params=_params(2),
        name="na_attention",
    )(proj, proj, proj, cache_k, cache_v, proj, bias)


def _pool_kernel(u_ref, gate_ref, w_ref, scale_ref, o_ref, *, seq, rows):
    grp = pl.program_id(1)
    t = lax.broadcasted_iota(jnp.int32, (rows, rows), 0)
    j = lax.broadcasted_iota(jnp.int32, (rows, rows), 1)
    delta = j - t
    same_seq = (t // seq) == (j // seq)
    pos = lax.broadcasted_iota(jnp.int32, (rows, 1), 0) % seq
    for gi, window in enumerate(POOL_WINDOWS):
        half = window // 2

        @pl.when(grp == gi)
        def _(half=half):
            band = jnp.where((delta >= -half) & (delta < half) & same_seq, 1.0, 0.0).astype(BF16)
            u = u_ref[...]
            wsum = _dot(band, u)
            count = (jnp.minimum(pos + half, seq) - jnp.maximum(pos - half, 0)).astype(F32)
            pooled = wsum / count - u.astype(F32)
            y = _dot(pooled.astype(BF16), w_ref[...]) * scale_ref[...]
            o_ref[...] = (y * _silu(gate_ref[...].astype(F32))).astype(o_ref.dtype)


def _pool_mixer(ug, w_grp, scale, seq, rows=1024):
    m = ug.shape[0]
    n_groups = len(POOL_WINDOWS)
    pg = POOL_GROUP
    return pl.pallas_call(
        functools.partial(_pool_kernel, seq=seq, rows=rows),
        grid=(m // rows, n_groups),
        in_specs=[
            pl.BlockSpec((rows, pg), lambda i, g: (i, g)),
            pl.BlockSpec((rows, pg), lambda i, g: (i, n_groups + g)),
            pl.BlockSpec((None, pg, pg), lambda i, g: (g, 0, 0)),
            pl.BlockSpec((1, pg), lambda i, g: (0, g)),
        ],
        out_specs=pl.BlockSpec((rows, pg), lambda i, g: (i, g)),
        out_shape=jax.ShapeDtypeStruct((m, n_groups * pg), BF16),
        compiler_params=_params(2),
        name="pool_mixer",
    )(ug, ug, w_grp, scale.reshape(1, -1))


def _rope_tables():
    t = jnp.arange(DEC_SEQ)
    quarter = HEAD_DIM // 4
    inv_freq = ROPE_BASE ** (-jnp.arange(quarter, dtype=F32) / quarter)
    ang_r = (t // GRID_W).astype(F32)[:, None] * inv_freq
    ang_c = (t % GRID_W).astype(F32)[:, None] * inv_freq
    ang = jnp.concatenate([ang_r, ang_r, ang_c, ang_c], axis=-1)
    sign = jnp.tile(jnp.concatenate([-jnp.ones((quarter,), F32), jnp.ones((quarter,), F32)]), 2)
    return jnp.cos(ang), jnp.sin(ang) * sign


def kernel(x_prompt, x_sample, c, cache_a_k, cache_a_v, cache_b_k, cache_b_v, c_ctx,
           w_ada, b_ada, norm_g, w_in_attn, a_sink, b_rpb, w_out_attn,
           w_in_pool, w_grp_pool, pool_scale, w_out_pool, final_g):
    d = D_MODEL
    xp = x_prompt.reshape(N_PROMPT, d)
    xs = x_sample.reshape(N_SAMPLE, d)

    w = w_in_attn[0]
    o_qa = 0
    o_ka = o_qa + A_WIDTH
    o_va = o_ka + A_KV_WIDTH
    o_qb = o_va + A_KV_WIDTH
    o_kb = o_qb + B_WIDTH
    o_vb = o_kb + B_WIDTH
    o_gate = o_vb + B_WIDTH
    w_in0 = jnp.concatenate([
        w[:, o_ka:o_va], w[:, o_va:o_qb], w[:, o_kb:o_vb], w[:, o_vb:o_gate],
        w[:, o_qa:o_ka], w[:, o_qb:o_kb], w[:, o_gate:]], axis=1).astype(BF16)
    w_out0 = w_out_attn[0].astype(BF16)
    w_in1 = w_in_pool[0].astype(BF16)
    w_grp = w_grp_pool[0].astype(BF16)
    w_out1 = w_out_pool[0].astype(BF16)

    cond = jnp.zeros((MOD_ROWS, d), F32).at[:DEC_BATCH].set(c).at[CTX_MOD_ROW].set(c_ctx)
    mod = _ada(cond, w_ada, b_ada)
    mod0 = mod[0].reshape(MOD_ROWS, 1, 3 * d)
    mod1 = mod[1].reshape(MOD_ROWS, 1, 3 * d)

    hp = _norm_mod(xp, norm_g[0], mod0, True)
    hs = _norm_mod(xs, norm_g[0], mod0, False)

    ka = _matmul(hp, w_in0, COL_KA, A_KV_WIDTH, F32, tn=512)
    va = _matmul(hp, w_in0, COL_VA, A_KV_WIDTH, F32, tn=512)
    kb = _matmul(hp, w_in0, COL_KB, B_WIDTH, F32)
    vb = _matmul(hp, w_in0, COL_VB, B_WIDTH, F32)
    qg_p = _matmul(hp, w_in0, COL_QA, 2 * MIX_WIDTH, BF16)
    proj_s = _matmul(hs, w_in0, 0, ATTN_IN_WIDTH, BF16)

    sink = a_sink[0]
    og_p = _ctx_attention(sink, qg_p, ka, va, kb, vb)

    cos, sin_signed = _rope_tables()
    flat = lambda cache: cache.reshape(DEC_BATCH, -1, HEAD_DIM)
    oa_s = _win_attention(sink, proj_s, flat(cache_a_k), flat(cache_a_v), cos, sin_signed)
    ob_s = _na_attention(proj_s, flat(cache_b_k), flat(cache_b_v), _expand_bias(b_rpb[0]))

    xp1, hp1 = _matmul_residual_norm([(og_p, 0), (og_p, 1)], w_out0, xp, mod0, True,
                                     norm_g[1], mod1)
    xs1, hs1 = _matmul_residual_norm([(oa_s, 0), (ob_s, 0)], w_out0, xs, mod0, False,
                                     norm_g[1], mod1)

    ug_p = _matmul(hp1, w_in1, 0, 2 * d, BF16)
    ug_s = _matmul(hs1, w_in1, 0, 2 * d, BF16)
    y_p = _pool_mixer(ug_p, w_grp, pool_scale[0], SEQ)
    y_s = _pool_mixer(ug_s, w_grp, pool_scale[0], DEC_SEQ)
    y_prompt = _matmul_residual_norm([(y_p, 0), (y_p, 1)], w_out1, xp1, mod1, True,
                                     final_g).reshape(BATCH, SEQ, d)
    y_sample = _matmul_residual_norm([(y_s, 0), (y_s, 1)], w_out1, xs1, mod1, False,
                                     final_g).reshape(DEC_BATCH, DEC_SEQ, d)

    kv_shape_a = (BATCH, 1, SEQ, A_KV_HEADS, HEAD_DIM)
    kv_shape_b = (BATCH, 1, SEQ, B_HEADS, HEAD_DIM)
    return (y_prompt, y_sample, ka.reshape(kv_shape_a), va.reshape(kv_shape_a),
            kb.reshape(kv_shape_b), vb.reshape(kv_shape_b))
```

```python
import functools

import jax
import jax.numpy as jnp
from jax import lax
from jax.experimental import pallas as pl
from jax.experimental.pallas import tpu as pltpu

F32 = jnp.float32
BF16 = jnp.bfloat16

D_MODEL = 4096
BATCH = 32
SEQ = 256
DEC_BATCH = 8
DEC_SEQ = 1024
PAST_LEN = 512
GRID_W = 64
HEAD_DIM = 128
A_Q_HEADS = 16
A_KV_HEADS = 4
A_GROUPS = 4
A_HALF_WIN = 128
A_BLOCK = 128
B_HEADS = 16
B_WIN_ROWS = 8
B_WIN_COLS = 16
A_WIDTH = A_Q_HEADS * HEAD_DIM
A_KV_WIDTH = A_KV_HEADS * HEAD_DIM
B_WIDTH = B_HEADS * HEAD_DIM
MIX_WIDTH = A_WIDTH + B_WIDTH
POOL_WINDOWS = (2, 4, 8, 16)
POOL_GROUP = 1024
POOL_BAND_BLOCK = 256
ROPE_BASE = 10000.0
NORM_EPS = 1e-6
NEG_INF = -1e30
ATTN_SCALE = HEAD_DIM ** -0.5
LOG2E = 1.4426950408889634
LOGIT_SCALE = ATTN_SCALE * LOG2E
CTX_CHUNK = 1024
WIN_CHUNK = 512
NA_CHUNK = 1024

N_PROMPT = BATCH * SEQ
N_SAMPLE = DEC_BATCH * DEC_SEQ
MOD_ROWS = 16
CTX_MOD_ROW = DEC_BATCH
GRID_ROWS = DEC_SEQ // GRID_W
N_BIAS_PAIRS = 2 * B_WIN_ROWS - 2
RPB_ROWS = 2 * B_WIN_ROWS - 1
RPB_COLS = 2 * B_WIN_COLS - 1

COL_KA = 0
COL_VA = COL_KA + A_KV_WIDTH
COL_KB = COL_VA + A_KV_WIDTH
COL_VB = COL_KB + B_WIDTH
COL_QA = COL_VB + B_WIDTH
COL_QB = COL_QA + A_WIDTH
COL_GATE = COL_QB + B_WIDTH
ATTN_IN_WIDTH = COL_GATE + MIX_WIDTH

VMEM_LIMIT = 48 * 1024 * 1024
VMEM_LIMIT_ROWS = 56 * 1024 * 1024


def _params(n_grid_dims):
    return pltpu.CompilerParams(
        dimension_semantics=("arbitrary",) * n_grid_dims,
        vmem_limit_bytes=VMEM_LIMIT,
    )


def _silu(x):
    return x / (1.0 + jnp.exp(-x))


def _dot_nt(a, b):
    return lax.dot_general(a, b, (((1,), (1,)), ((), ())), preferred_element_type=F32)


def _dot(a, b):
    return jnp.dot(a, b, preferred_element_type=F32)


def _ada_kernel(cond_ref, w_ref, b_ref, o_ref):
    a = _silu(cond_ref[...]).astype(BF16)
    o_ref[...] = _dot(a, w_ref[...].astype(BF16)) + b_ref[...]


def _ada(cond, w_ada, b_ada, tn=512):
    depth, d, n = w_ada.shape
    return pl.pallas_call(
        _ada_kernel,
        grid=(depth, n // tn),
        in_specs=[
            pl.BlockSpec((MOD_ROWS, d), lambda l, j: (0, 0)),
            pl.BlockSpec((None, d, tn), lambda l, j: (l, 0, j)),
            pl.BlockSpec((None, 1, tn), lambda l, j: (l, 0, j)),
        ],
        out_specs=pl.BlockSpec((None, MOD_ROWS, tn), lambda l, j: (l, 0, j)),
        out_shape=jax.ShapeDtypeStruct((depth, MOD_ROWS, n), F32),
        compiler_params=_params(2),
        name="ada",
    )(cond, w_ada, b_ada.reshape(depth, 1, n))


def _mod_row_fn(is_prompt, tm):
    if is_prompt:
        return lambda i: CTX_MOD_ROW
    return lambda i: (i * tm) // DEC_SEQ


def _norm_mod_kernel(x_ref, g_ref, sh_ref, sc_ref, o_ref):
    x = x_ref[...]
    ms = jnp.mean(x * x, axis=-1, keepdims=True)
    y = x * lax.rsqrt(ms + NORM_EPS) * g_ref[...]
    o_ref[...] = (y * (1.0 + sc_ref[...]) + sh_ref[...]).astype(o_ref.dtype)


def _norm_mod(x, gain, mod, is_prompt, tm=256):
    m, d = x.shape
    row = _mod_row_fn(is_prompt, tm)
    return pl.pallas_call(
        _norm_mod_kernel,
        grid=(m // tm,),
        in_specs=[
            pl.BlockSpec((tm, d), lambda i: (i, 0)),
            pl.BlockSpec((1, d), lambda i: (0, 0)),
            pl.BlockSpec((None, 1, d), lambda i: (row(i), 0, 0)),
            pl.BlockSpec((None, 1, d), lambda i: (row(i), 0, 1)),
        ],
        out_specs=pl.BlockSpec((tm, d), lambda i: (i, 0)),
        out_shape=jax.ShapeDtypeStruct((m, d), BF16),
        compiler_params=_params(1),
        name="norm_mod",
    )(x, gain.reshape(1, d), mod, mod)


def _mm_kernel(a_ref, w_ref, o_ref):
    o_ref[...] = _dot(a_ref[...], w_ref[...]).astype(o_ref.dtype)


def _matmul(a, w, col0, ncols, out_dtype, tn=1024):
    m, k = a.shape
    tm = 1024 if out_dtype == BF16 else 512
    assert col0 % tn == 0 and ncols % tn == 0 and m % tm == 0
    j0 = col0 // tn
    return pl.pallas_call(
        _mm_kernel,
        grid=(ncols // tn, m // tm),
        in_specs=[pl.BlockSpec((tm, k), lambda j, i: (i, 0)),
                  pl.BlockSpec((k, tn), lambda j, i: (0, j + j0))],
        out_specs=pl.BlockSpec((tm, tn), lambda j, i: (i, j)),
        out_shape=jax.ShapeDtypeStruct((m, ncols), out_dtype),
        compiler_params=_params(2),
        name="proj",
    )(a, w)


def _row_rsqrt(row_scr, n_tiles):
    ss = None
    for jj in range(n_tiles):
        t = row_scr[jj]
        part = jnp.sum(t * t, axis=-1, keepdims=True)
        ss = part if ss is None else ss + part
    width = n_tiles * row_scr.shape[2]
    return lax.rsqrt(ss / width + NORM_EPS)


def _dot_halves(a1_ref, a2_ref, w_ref):
    kh = a1_ref.shape[1]
    return _dot(a1_ref[...], w_ref[:kh, :]) + _dot(a2_ref[...], w_ref[kh:, :])


def _mm_res_normmod_kernel(a1_ref, a2_ref, w_ref, x_ref, g_ref, gain_ref, sh_ref, sc_ref,
                           x1_ref, h_ref, row_scr, *, n_tiles, tn):
    j = pl.program_id(1)
    x1 = x_ref[...] + g_ref[...] * _dot_halves(a1_ref, a2_ref, w_ref)
    x1_ref[...] = x1
    row_scr[j] = x1

    @pl.when(j == n_tiles - 1)
    def _():
        r = _row_rsqrt(row_scr, n_tiles)
        for jj in range(n_tiles):
            cols = slice(jj * tn, (jj + 1) * tn)
            y = row_scr[jj] * r * gain_ref[:, cols]
            h_ref[:, cols] = (y * (1.0 + sc_ref[:, cols]) + sh_ref[:, cols]).astype(h_ref.dtype)


def _mm_res_norm_kernel(a1_ref, a2_ref, w_ref, x_ref, g_ref, gain_ref, y_ref, *, n_tiles, tn):
    j = pl.program_id(1)
    x2 = x_ref[...] + g_ref[...] * _dot_halves(a1_ref, a2_ref, w_ref)
    for jj in range(n_tiles):
        @pl.when(j == jj)
        def _(jj=jj):
            y_ref[:, jj * tn:(jj + 1) * tn] = x2

    @pl.when(j == n_tiles - 1)
    def _():
        ss = None
        for jj in range(n_tiles):
            t = y_ref[:, jj * tn:(jj + 1) * tn]
            part = jnp.sum(t * t, axis=-1, keepdims=True)
            ss = part if ss is None else ss + part
        r = lax.rsqrt(ss / (n_tiles * tn) + NORM_EPS)
        for jj in range(n_tiles):
            cols = slice(jj * tn, (jj + 1) * tn)
            y_ref[:, cols] = y_ref[:, cols] * r * gain_ref[:, cols]


def _matmul_residual_norm(a_halves, w, x, mod, is_prompt, gain, next_mod=None, tn=1024, tm=512):
    (a1, c1), (a2, c2) = a_halves
    m = a1.shape[0]
    k, n = w.shape
    n_tiles = n // tn
    row = _mod_row_fn(is_prompt, tm)
    gate_block0 = 2 * n // tn
    in_specs = [pl.BlockSpec((tm, k // 2), lambda i, j: (i, c1)),
                pl.BlockSpec((tm, k // 2), lambda i, j: (i, c2)),
                pl.BlockSpec((k, tn), lambda i, j: (0, j)),
                pl.BlockSpec((tm, tn), lambda i, j: (i, j)),
                pl.BlockSpec((None, 1, tn), lambda i, j: (row(i), 0, gate_block0 + j)),
                pl.BlockSpec((1, n), lambda i, j: (0, 0))]
    args = [a1, a2, w, x, mod, gain.reshape(1, n)]
    scratch = [pltpu.VMEM((n_tiles, tm, tn), F32)]
    params = pltpu.CompilerParams(dimension_semantics=("arbitrary", "arbitrary"),
                                  vmem_limit_bytes=VMEM_LIMIT_ROWS)
    if next_mod is None:
        return pl.pallas_call(
            functools.partial(_mm_res_norm_kernel, n_tiles=n_tiles, tn=tn),
            grid=(m // tm, n_tiles),
            in_specs=in_specs,
            out_specs=pl.BlockSpec((tm, n), lambda i, j: (i, 0)),
            out_shape=jax.ShapeDtypeStruct((m, n), F32),
            compiler_params=params,
            name="proj_residual_norm",
        )(*args)
    in_specs += [pl.BlockSpec((None, 1, n), lambda i, j: (row(i), 0, 0)),
                 pl.BlockSpec((None, 1, n), lambda i, j: (row(i), 0, 1))]
    args += [next_mod, next_mod]
    return pl.pallas_call(
        functools.partial(_mm_res_normmod_kernel, n_tiles=n_tiles, tn=tn),
        grid=(m // tm, n_tiles),
        in_specs=in_specs,
        out_specs=[pl.BlockSpec((tm, tn), lambda i, j: (i, j)),
                   pl.BlockSpec((tm, n), lambda i, j: (i, 0))],
        out_shape=[jax.ShapeDtypeStruct((m, n), F32), jax.ShapeDtypeStruct((m, n), BF16)],
        scratch_shapes=scratch,
        compiler_params=params,
        name="proj_residual_normmod",
    )(*args)


def _chunked_softmax(n_rows, chunk, loads, stores, sink2=None):
    inv = []
    for c in range(n_rows // chunk):
        rows = slice(c * chunk, (c + 1) * chunk)
        ts = [load(rows) for load in loads]
        m = functools.reduce(jnp.maximum, [jnp.max(t, axis=-1, keepdims=True) for t in ts])
        if sink2 is not None:
            m = jnp.maximum(m, sink2[rows])
        es = [jnp.exp2(t - m) for t in ts]
        l = functools.reduce(jnp.add, [jnp.sum(e, axis=-1, keepdims=True) for e in es])
        if sink2 is not None:
            l = l + jnp.exp2(sink2[rows] - m)
        for store, e in zip(stores, es):
            store(rows, e.astype(BF16))
        inv.append(1.0 / l)
    return jnp.concatenate(inv, axis=0)


def _ctx_attn_kernel(sink_ref, qg_ref, ka_ref, va_ref, kb_ref, vb_ref, o_ref, s_scr, e_scr):
    n = SEQ
    gate0 = MIX_WIDTH
    n_stack = A_GROUPS

    def softmax(sink2):
        def store(rows, e):
            e_scr[rows, :] = e
        return _chunked_softmax(n_stack * n, CTX_CHUNK, [lambda rows: s_scr[rows, :] * LOGIT_SCALE],
                                [store], sink2)

    def emit(o, g, out_col):
        cols = slice(out_col, out_col + HEAD_DIM)
        gate = qg_ref[:, gate0 + out_col:gate0 + out_col + HEAD_DIM].astype(F32)
        o_ref[:, cols] = (o[g * n:(g + 1) * n] * _silu(gate)).astype(o_ref.dtype)

    for kv in range(A_KV_HEADS):
        cols = slice(kv * HEAD_DIM, (kv + 1) * HEAD_DIM)
        heads = [kv * A_GROUPS + g for g in range(A_GROUPS)]
        q = jnp.concatenate(
            [qg_ref[:, h * HEAD_DIM:(h + 1) * HEAD_DIM] for h in heads], axis=0)
        sink2 = jnp.concatenate(
            [jnp.full((n, 1), sink_ref[h] * LOG2E, F32) for h in heads], axis=0)
        s_scr[...] = _dot_nt(q, ka_ref[:, cols].astype(BF16))
        inv = softmax(sink2)
        o = _dot(e_scr[...], va_ref[:, cols].astype(BF16)) * inv
        for g, h in enumerate(heads):
            emit(o, g, h * HEAD_DIM)
    for h0 in range(0, B_HEADS, n_stack):
        heads = range(h0, h0 + n_stack)
        for g, h in enumerate(heads):
            cols = slice(h * HEAD_DIM, (h + 1) * HEAD_DIM)
            q = qg_ref[:, A_WIDTH + h * HEAD_DIM:A_WIDTH + (h + 1) * HEAD_DIM]
            s_scr[g * n:(g + 1) * n, :] = _dot_nt(q, kb_ref[:, cols].astype(BF16))
        inv = softmax(None)
        o = jnp.concatenate(
            [_dot(e_scr[g * n:(g + 1) * n, :],
                  vb_ref[:, h * HEAD_DIM:(h + 1) * HEAD_DIM].astype(BF16))
             for g, h in enumerate(heads)], axis=0) * inv
        for g, h in enumerate(heads):
            emit(o, g, A_WIDTH + h * HEAD_DIM)


def _ctx_attention(sink, qg, ka, va, kb, vb):
    stack_rows = A_GROUPS * SEQ
    return pl.pallas_call(
        _ctx_attn_kernel,
        grid=(BATCH,),
        in_specs=[
            pl.BlockSpec(memory_space=pltpu.SMEM),
            pl.BlockSpec((SEQ, 2 * MIX_WIDTH), lambda b: (b, 0)),
            pl.BlockSpec((SEQ, A_KV_WIDTH), lambda b: (b, 0)),
            pl.BlockSpec((SEQ, A_KV_WIDTH), lambda b: (b, 0)),
            pl.BlockSpec((SEQ, B_WIDTH), lambda b: (b, 0)),
            pl.BlockSpec((SEQ, B_WIDTH), lambda b: (b, 0)),
        ],
        out_specs=pl.BlockSpec((SEQ, MIX_WIDTH), lambda b: (b, 0)),
        out_shape=jax.ShapeDtypeStruct((N_PROMPT, MIX_WIDTH), BF16),
        scratch_shapes=[pltpu.VMEM((stack_rows, SEQ), F32),
                        pltpu.VMEM((stack_rows, SEQ), BF16)],
        compiler_params=_params(1),
        name="ctx_attention",
    )(sink, qg, ka, va, kb, vb)


def _split_cache_heads(c_ref, scr, n_heads):
    for h in range(n_heads):
        scr[h] = c_ref[pl.ds(h, PAST_LEN, stride=n_heads), :].astype(scr.dtype)


def _win_attn_kernel(sink_ref, q_ref, k_ref, v_ref, ck_ref, cv_ref, gate_ref,
                     cos_ref, sin_ref, o_ref, qs_ref, ks_ref, ck_scr, cv_scr,
                     band_scr, sw_scr, sc_scr, ew_scr, ec_scr):
    kv = pl.program_id(1)
    n = DEC_SEQ

    @pl.when(kv == 0)
    def _():
        _split_cache_heads(ck_ref, ck_scr, A_KV_HEADS)
        _split_cache_heads(cv_ref, cv_scr, A_KV_HEADS)

    cos = cos_ref[...]
    sin = sin_ref[...]
    lane = lax.broadcasted_iota(jnp.int32, (n, HEAD_DIM), 1)
    first_quarter = (lane % (HEAD_DIM // 2)) < (HEAD_DIM // 4)

    def rope(x):
        rot = jnp.where(first_quarter,
                        pltpu.roll(x, HEAD_DIM - HEAD_DIM // 4, 1),
                        pltpu.roll(x, HEAD_DIM // 4, 1))
        return x * cos + rot * sin

    ks_ref[...] = rope(k_ref[...].astype(F32)).astype(BF16)
    for g in range(A_GROUPS):
        qs_ref[g] = (rope(q_ref[:, g * HEAD_DIM:(g + 1) * HEAD_DIM].astype(F32))
                     * LOGIT_SCALE).astype(BF16)

    ck = ck_scr[kv]
    cv = cv_scr[kv]
    rows = A_GROUPS * A_BLOCK
    sink2 = jnp.concatenate(
        [jnp.full((A_BLOCK, 1), sink_ref[kv * A_GROUPS + g] * LOG2E, F32)
         for g in range(A_GROUPS)], axis=0)
    qi = lax.broadcasted_iota(jnp.int32, (rows, 3 * A_BLOCK), 0) % A_BLOCK
    rel = lax.broadcasted_iota(jnp.int32, (rows, 3 * A_BLOCK), 1) - A_BLOCK - qi
    band_scr[...] = jnp.where(jnp.abs(rel) <= A_HALF_WIN, 0.0, NEG_INF)
    for blk in range(n // A_BLOCK):
        r0 = blk * A_BLOCK
        lo = max(0, r0 - A_BLOCK)
        hi = min(n, r0 + 2 * A_BLOCK)
        span = hi - lo
        b0 = lo - (r0 - A_BLOCK)
        q = jnp.concatenate([qs_ref[g, r0:r0 + A_BLOCK, :] for g in range(A_GROUPS)], axis=0)
        sw_scr[:, :span] = _dot_nt(q, ks_ref[lo:hi, :])
        sc_scr[...] = _dot_nt(q, ck)

        def load_w(r, span=span, b0=b0):
            return sw_scr[r, :span] + band_scr[r, b0:b0 + span]

        def store_w(r, e, span=span):
            ew_scr[r, :span] = e

        def store_c(r, e):
            ec_scr[r, :] = e

        inv = _chunked_softmax(rows, WIN_CHUNK, [load_w, lambda r: sc_scr[r, :]],
                               [store_w, store_c], sink2)
        o = (_dot(ew_scr[:, :span], v_ref[lo:hi, :]) + _dot(ec_scr[...], cv)) * inv
        for g in range(A_GROUPS):
            hc = slice(g * HEAD_DIM, (g + 1) * HEAD_DIM)
            gate = gate_ref[r0:r0 + A_BLOCK, hc].astype(F32)
            o_ref[r0:r0 + A_BLOCK, hc] = (
                o[g * A_BLOCK:(g + 1) * A_BLOCK] * _silu(gate)).astype(o_ref.dtype)


def _win_attention(sink, proj, cache_k, cache_v, cos, sin_signed):
    gw = A_GROUPS * HEAD_DIM
    stack_rows = A_GROUPS * A_BLOCK
    n = DEC_SEQ
    cache_spec = pl.BlockSpec((None, PAST_LEN * A_KV_HEADS, HEAD_DIM), lambda b, k: (b, 0, 0))
    return pl.pallas_call(
        _win_attn_kernel,
        grid=(DEC_BATCH, A_KV_HEADS),
        in_specs=[
            pl.BlockSpec(memory_space=pltpu.SMEM),
            pl.BlockSpec((n, gw), lambda b, k: (b, COL_QA // gw + k)),
            pl.BlockSpec((n, HEAD_DIM), lambda b, k: (b, COL_KA // HEAD_DIM + k)),
            pl.BlockSpec((n, HEAD_DIM), lambda b, k: (b, COL_VA // HEAD_DIM + k)),
            cache_spec,
            cache_spec,
            pl.BlockSpec((n, gw), lambda b, k: (b, COL_GATE // gw + k)),
            pl.BlockSpec((n, HEAD_DIM), lambda b, k: (0, 0)),
            pl.BlockSpec((n, HEAD_DIM), lambda b, k: (0, 0)),
        ],
        out_specs=pl.BlockSpec((n, gw), lambda b, k: (b, k)),
        out_shape=jax.ShapeDtypeStruct((N_SAMPLE, A_WIDTH), BF16),
        scratch_shapes=[pltpu.VMEM((A_GROUPS, n, HEAD_DIM), BF16),
                        pltpu.VMEM((n, HEAD_DIM), BF16),
                        pltpu.VMEM((A_KV_HEADS, PAST_LEN, HEAD_DIM), BF16),
                        pltpu.VMEM((A_KV_HEADS, PAST_LEN, HEAD_DIM), BF16),
                        pltpu.VMEM((stack_rows, 3 * A_BLOCK), F32),
                        pltpu.VMEM((stack_rows, 3 * A_BLOCK), F32),
                        pltpu.VMEM((stack_rows, PAST_LEN), F32),
                        pltpu.VMEM((stack_rows, 3 * A_BLOCK), BF16),
                        pltpu.VMEM((stack_rows, PAST_LEN), BF16)],
        compiler_params=_params(2),
        name="win_attention",
    )(sink, proj, proj, proj, cache_k, cache_v, proj, cos, sin_signed)


def _bias_kernel(rpb_ref, o_ref):
    h = pl.program_id(0)
    shape = (GRID_W, 2 * GRID_W)
    c = lax.broadcasted_iota(jnp.int32, shape, 0)
    j2 = lax.broadcasted_iota(jnp.int32, shape, 1)
    kc = j2 % GRID_W
    second = j2 >= GRID_W
    col_start = jnp.clip(c - B_WIN_COLS // 2, 0, GRID_W - B_WIN_COLS)
    ok = (kc >= col_start) & (kc < col_start + B_WIN_COLS)
    dc = kc - c + B_WIN_COLS - 1
    base = h * (RPB_ROWS * RPB_COLS)
    pair_scr = []
    for i in range(N_BIAS_PAIRS):
        acc = jnp.full(shape, NEG_INF, F32)
        for d in range(RPB_COLS):
            val = jnp.where(second, rpb_ref[base + (i + 1) * RPB_COLS + d],
                            rpb_ref[base + i * RPB_COLS + d]) * LOG2E
            acc = jnp.where(ok & (dc == d), val, acc)
        pair_scr.append(acc)
    for r in range(GRID_ROWS):
        dr0 = _na_key_row0(r) - r + B_WIN_ROWS - 1
        for i in range(B_WIN_ROWS // 2):
            o_ref[r * GRID_W:(r + 1) * GRID_W, i * 2 * GRID_W:(i + 1) * 2 * GRID_W] = (
                pair_scr[dr0 + 2 * i])


def _expand_bias(rpb):
    kw = B_WIN_ROWS * GRID_W
    return pl.pallas_call(
        _bias_kernel,
        grid=(B_HEADS,),
        in_specs=[pl.BlockSpec(memory_space=pltpu.SMEM)],
        out_specs=pl.BlockSpec((None, DEC_SEQ, kw), lambda h: (h, 0, 0)),
        out_shape=jax.ShapeDtypeStruct((B_HEADS, DEC_SEQ, kw), F32),
        compiler_params=_params(1),
        name="expand_bias",
    )(rpb.reshape(-1))


def _na_key_row0(r):
    return min(max(r - B_WIN_ROWS // 2, 0), GRID_ROWS - B_WIN_ROWS)


def _na_row_groups():
    groups = []
    for r in range(GRID_ROWS):
        rs = _na_key_row0(r)
        if groups and groups[-1][2] == rs:
            groups[-1] = (groups[-1][0], r + 1, rs)
        else:
            groups.append((r, r + 1, rs))
    return groups


def _na_attn_kernel(q_ref, k_ref, v_ref, ck_ref, cv_ref, gate_ref, bias_ref, o_ref,
                    ck_scr, cv_scr, sn_scr, sc_scr, en_scr, ec_scr):
    kw = B_WIN_ROWS * GRID_W
    h = pl.program_id(1)

    @pl.when(h == 0)
    def _():
        _split_cache_heads(ck_ref, ck_scr, B_HEADS)
        _split_cache_heads(cv_ref, cv_scr, B_HEADS)

    groups = _na_row_groups()
    sc_scr[...] = _dot_nt(q_ref[...], ck_scr[h])
    for r0, r1, rs in groups:
        sn_scr[r0 * GRID_W:r1 * GRID_W, :] = _dot_nt(
            q_ref[r0 * GRID_W:r1 * GRID_W, :], k_ref[rs * GRID_W:rs * GRID_W + kw, :])

    def store_n(r, e):
        en_scr[r, :] = e

    def store_c(r, e):
        ec_scr[r, :] = e

    inv = _chunked_softmax(
        DEC_SEQ, NA_CHUNK,
        [lambda r: sn_scr[r, :] * LOGIT_SCALE + bias_ref[r, :], lambda r: sc_scr[r, :] * LOGIT_SCALE],
        [store_n, store_c])
    o_n = jnp.concatenate(
        [_dot(en_scr[r0 * GRID_W:r1 * GRID_W, :], v_ref[rs * GRID_W:rs * GRID_W + kw, :])
         for r0, r1, rs in groups], axis=0)
    o = (o_n + _dot(ec_scr[...], cv_scr[h])) * inv
    o_ref[...] = (o * _silu(gate_ref[...].astype(F32))).astype(o_ref.dtype)


def _na_attention(proj, cache_k, cache_v, bias):
    n = DEC_SEQ
    hd = HEAD_DIM
    cache_spec = pl.BlockSpec((None, PAST_LEN * B_HEADS, hd), lambda b, h: (b, 0, 0))
    return pl.pallas_call(
        _na_attn_kernel,
        grid=(DEC_BATCH, B_HEADS),
        in_specs=[
            pl.BlockSpec((n, hd), lambda b, h: (b, COL_QB // hd + h)),
            pl.BlockSpec((n, hd), lambda b, h: (b, COL_KB // hd + h)),
            pl.BlockSpec((n, hd), lambda b, h: (b, COL_VB // hd + h)),
            cache_spec,
            cache_spec,
            pl.BlockSpec((n, hd), lambda b, h: (b, (COL_GATE + A_WIDTH) // hd + h)),
            pl.BlockSpec((None, n, B_WIN_ROWS * GRID_W), lambda b, h: (h, 0, 0)),
        ],
        out_specs=pl.BlockSpec((n, hd), lambda b, h: (b, h)),
        out_shape=jax.ShapeDtypeStruct((N_SAMPLE, B_WIDTH), BF16),
        scratch_shapes=[pltpu.VMEM((B_HEADS, PAST_LEN, hd), BF16),
                        pltpu.VMEM((B_HEADS, PAST_LEN, hd), BF16),
                        pltpu.VMEM((n, B_WIN_ROWS * GRID_W), F32),
                        pltpu.VMEM((n, PAST_LEN), F32),
                        pltpu.VMEM((n, B_WIN_ROWS * GRID_W), BF16),
                        pltpu.VMEM((n, PAST_LEN), BF16)],
        compiler_params=_params(2),
        name="na_attention",
    )(proj, proj, proj, cache_k, cache_v, proj, bias)


def _pool_kernel(u_ref, gate_ref, band_ref, inv_count_ref, w_ref, scale_ref, o_ref, *, seq, rows):
    cb = POOL_BAND_BLOCK
    parts = []
    for i in range(rows // cb):
        lo = i * cb if seq <= cb else max(0, (i - 1) * cb)
        hi = (i + 1) * cb if seq <= cb else min(rows, (i + 2) * cb)
        parts.append(_dot(band_ref[i * cb:(i + 1) * cb, lo:hi], u_ref[lo:hi, :]))
    wsum = jnp.concatenate(parts, axis=0)
    pooled = wsum * inv_count_ref[...] - u_ref[...].astype(F32)
    y = _dot(pooled.astype(BF16), w_ref[...]) * scale_ref[...]
    o_ref[...] = (y * _silu(gate_ref[...].astype(F32))).astype(o_ref.dtype)


def _pool_operators(seq, rows):
    t = jnp.arange(rows)[:, None]
    j = jnp.arange(rows)[None, :]
    same_seq = (t // seq) == (j // seq)
    pos = t % seq
    bands, inv_counts = [], []
    for window in POOL_WINDOWS:
        half = window // 2
        assert half <= POOL_BAND_BLOCK
        bands.append(((j - t >= -half) & (j - t < half) & same_seq).astype(BF16))
        inv_counts.append(1.0 / (jnp.minimum(pos + half, seq) - jnp.maximum(pos - half, 0)).astype(F32))
    return jnp.stack(bands), jnp.stack(inv_counts)


def _pool_mixer(ug, w_grp, scale, seq, rows=1024):
    m = ug.shape[0]
    n_groups = len(POOL_WINDOWS)
    pg = POOL_GROUP
    band, inv_count = _pool_operators(seq, rows)
    return pl.pallas_call(
        functools.partial(_pool_kernel, seq=seq, rows=rows),
        grid=(m // rows, n_groups),
        in_specs=[
            pl.BlockSpec((rows, pg), lambda i, g: (i, g)),
            pl.BlockSpec((rows, pg), lambda i, g: (i, n_groups + g)),
            pl.BlockSpec((None, rows, rows), lambda i, g: (g, 0, 0)),
            pl.BlockSpec((None, rows, 1), lambda i, g: (g, 0, 0)),
            pl.BlockSpec((None, pg, pg), lambda i, g: (g, 0, 0)),
            pl.BlockSpec((1, pg), lambda i, g: (0, g)),
        ],
        out_specs=pl.BlockSpec((rows, pg), lambda i, g: (i, g)),
        out_shape=jax.ShapeDtypeStruct((m, n_groups * pg), BF16),
        compiler_params=_params(2),
        name="pool_mixer",
    )(ug, ug, band, inv_count, w_grp, scale.reshape(1, -1))


def _rope_tables():
    t = jnp.arange(DEC_SEQ)
    quarter = HEAD_DIM // 4
    inv_freq = ROPE_BASE ** (-jnp.arange(quarter, dtype=F32) / quarter)
    ang_r = (t // GRID_W).astype(F32)[:, None] * inv_freq
    ang_c = (t % GRID_W).astype(F32)[:, None] * inv_freq
    ang = jnp.concatenate([ang_r, ang_r, ang_c, ang_c], axis=-1)
    sign = jnp.tile(jnp.concatenate([-jnp.ones((quarter,), F32), jnp.ones((quarter,), F32)]), 2)
    return jnp.cos(ang), jnp.sin(ang) * sign


def kernel(x_prompt, x_sample, c, cache_a_k, cache_a_v, cache_b_k, cache_b_v, c_ctx,
           w_ada, b_ada, norm_g, w_in_attn, a_sink, b_rpb, w_out_attn,
           w_in_pool, w_grp_pool, pool_scale, w_out_pool, final_g):
    d = D_MODEL
    xp = x_prompt.reshape(N_PROMPT, d)
    xs = x_sample.reshape(N_SAMPLE, d)

    w = w_in_attn[0]
    o_qa = 0
    o_ka = o_qa + A_WIDTH
    o_va = o_ka + A_KV_WIDTH
    o_qb = o_va + A_KV_WIDTH
    o_kb = o_qb + B_WIDTH
    o_vb = o_kb + B_WIDTH
    o_gate = o_vb + B_WIDTH
    w_in0 = jnp.concatenate([
        w[:, o_ka:o_va], w[:, o_va:o_qb], w[:, o_kb:o_vb], w[:, o_vb:o_gate],
        w[:, o_qa:o_ka], w[:, o_qb:o_kb], w[:, o_gate:]], axis=1).astype(BF16)
    w_out0 = w_out_attn[0].astype(BF16)
    w_in1 = w_in_pool[0].astype(BF16)
    w_grp = w_grp_pool[0].astype(BF16)
    w_out1 = w_out_pool[0].astype(BF16)

    cond = jnp.zeros((MOD_ROWS, d), F32).at[:DEC_BATCH].set(c).at[CTX_MOD_ROW].set(c_ctx)
    mod = _ada(cond, w_ada, b_ada)
    mod0 = mod[0].reshape(MOD_ROWS, 1, 3 * d)
    mod1 = mod[1].reshape(MOD_ROWS, 1, 3 * d)

    hp = _norm_mod(xp, norm_g[0], mod0, True)
    hs = _norm_mod(xs, norm_g[0], mod0, False)

    ka = _matmul(hp, w_in0, COL_KA, A_KV_WIDTH, F32, tn=512)
    va = _matmul(hp, w_in0, COL_VA, A_KV_WIDTH, F32, tn=512)
    kb = _matmul(hp, w_in0, COL_KB, B_WIDTH, F32)
    vb = _matmul(hp, w_in0, COL_VB, B_WIDTH, F32)
    qg_p = _matmul(hp, w_in0, COL_QA, 2 * MIX_WIDTH, BF16)
    proj_s = _matmul(hs, w_in0, 0, ATTN_IN_WIDTH, BF16)

    sink = a_sink[0]
    og_p = _ctx_attention(sink, qg_p, ka, va, kb, vb)

    cos, sin_signed = _rope_tables()
    flat = lambda cache: cache.reshape(DEC_BATCH, -1, HEAD_DIM)
    oa_s = _win_attention(sink, proj_s, flat(cache_a_k), flat(cache_a_v), cos, sin_signed)
    ob_s = _na_attention(proj_s, flat(cache_b_k), flat(cache_b_v), _expand_bias(b_rpb[0]))

    xp1, hp1 = _matmul_residual_norm([(og_p, 0), (og_p, 1)], w_out0, xp, mod0, True,
                                     norm_g[1], mod1)
    xs1, hs1 = _matmul_residual_norm([(oa_s, 0), (ob_s, 0)], w_out0, xs, mod0, False,
                                     norm_g[1], mod1)

    ug_p = _matmul(hp1, w_in1, 0, 2 * d, BF16)
    ug_s = _matmul(hs1, w_in1, 0, 2 * d, BF16)
    y_p = _pool_mixer(ug_p, w_grp, pool_scale[0], SEQ)
    y_s = _pool_mixer(ug_s, w_grp, pool_scale[0], DEC_SEQ)
    y_prompt = _matmul_residual_norm([(y_p, 0), (y_p, 1)], w_out1, xp1, mod1, True,
                                     final_g).reshape(BATCH, SEQ, d)
    y_sample = _matmul_residual_norm([(y_s, 0), (y_s, 1)], w_out1, xs1, mod1, False,
                                     final_g).reshape(DEC_BATCH, DEC_SEQ, d)

    kv_shape_a = (BATCH, 1, SEQ, A_KV_HEADS, HEAD_DIM)
    kv_shape_b = (BATCH, 1, SEQ, B_HEADS, HEAD_DIM)
    return (y_prompt, y_sample, ka.reshape(kv_shape_a), va.reshape(kv_shape_a),
            kb.reshape(kv_shape_b), vb.reshape(kv_shape_b))
```

```python
import functools

import jax
import jax.numpy as jnp
from jax import lax
from jax.experimental import pallas as pl
from jax.experimental.pallas import tpu as pltpu

F32 = jnp.float32
BF16 = jnp.bfloat16

D_MODEL = 4096
BATCH = 32
SEQ = 256
DEC_BATCH = 8
DEC_SEQ = 1024
PAST_LEN = 512
GRID_W = 64
HEAD_DIM = 128
A_Q_HEADS = 16
A_KV_HEADS = 4
A_GROUPS = 4
A_HALF_WIN = 128
A_BLOCK = 128
B_HEADS = 16
B_WIN_ROWS = 8
B_WIN_COLS = 16
A_WIDTH = A_Q_HEADS * HEAD_DIM
A_KV_WIDTH = A_KV_HEADS * HEAD_DIM
B_WIDTH = B_HEADS * HEAD_DIM
MIX_WIDTH = A_WIDTH + B_WIDTH
POOL_WINDOWS = (2, 4, 8, 16)
POOL_GROUP = 1024
POOL_BAND_BLOCK = 256
CAST_CHUNK_ROWS = 64
ROPE_BASE = 10000.0
NORM_EPS = 1e-6
NEG_INF = -1e30
ATTN_SCALE = HEAD_DIM ** -0.5
LOG2E = 1.4426950408889634
LOGIT_SCALE = ATTN_SCALE * LOG2E
CTX_CHUNK = 1024
WIN_CHUNK = 512
NA_CHUNK = 1024

N_PROMPT = BATCH * SEQ
N_SAMPLE = DEC_BATCH * DEC_SEQ
MOD_ROWS = 16
CTX_MOD_ROW = DEC_BATCH
GRID_ROWS = DEC_SEQ // GRID_W
N_BIAS_PAIRS = 2 * B_WIN_ROWS - 2
RPB_ROWS = 2 * B_WIN_ROWS - 1
RPB_COLS = 2 * B_WIN_COLS - 1

COL_KA = 0
COL_VA = COL_KA + A_KV_WIDTH
COL_KB = COL_VA + A_KV_WIDTH
COL_VB = COL_KB + B_WIDTH
COL_QA = COL_VB + B_WIDTH
COL_QB = COL_QA + A_WIDTH
COL_GATE = COL_QB + B_WIDTH
ATTN_IN_WIDTH = COL_GATE + MIX_WIDTH

VMEM_LIMIT = 48 * 1024 * 1024
VMEM_LIMIT_ROWS = 56 * 1024 * 1024


def _params(n_grid_dims):
    return pltpu.CompilerParams(
        dimension_semantics=("arbitrary",) * n_grid_dims,
        vmem_limit_bytes=VMEM_LIMIT,
    )


def _silu(x):
    return x / (1.0 + jnp.exp(-x))


def _dot_nt(a, b):
    return lax.dot_general(a, b, (((1,), (1,)), ((), ())), preferred_element_type=F32)


def _dot(a, b):
    return jnp.dot(a, b, preferred_element_type=F32)


def _ada_kernel(cond_ref, w_ref, b_ref, o_ref):
    a = _silu(cond_ref[...]).astype(BF16)
    o_ref[...] = _dot(a, w_ref[...].astype(BF16)) + b_ref[...]


def _ada(cond, w_ada, b_ada, tn=512):
    depth, d, n = w_ada.shape
    return pl.pallas_call(
        _ada_kernel,
        grid=(depth, n // tn),
        in_specs=[
            pl.BlockSpec((MOD_ROWS, d), lambda l, j: (0, 0)),
            pl.BlockSpec((None, d, tn), lambda l, j: (l, 0, j)),
            pl.BlockSpec((None, 1, tn), lambda l, j: (l, 0, j)),
        ],
        out_specs=pl.BlockSpec((None, MOD_ROWS, tn), lambda l, j: (l, 0, j)),
        out_shape=jax.ShapeDtypeStruct((depth, MOD_ROWS, n), F32),
        compiler_params=_params(2),
        name="ada",
    )(cond, w_ada, b_ada.reshape(depth, 1, n))


def _mod_row_fn(is_prompt, tm):
    if is_prompt:
        return lambda i: CTX_MOD_ROW
    return lambda i: (i * tm) // DEC_SEQ


def _norm_mod_kernel(x_ref, g_ref, sh_ref, sc_ref, o_ref):
    x = x_ref[...]
    ms = jnp.mean(x * x, axis=-1, keepdims=True)
    y = x * lax.rsqrt(ms + NORM_EPS) * g_ref[...]
    o_ref[...] = (y * (1.0 + sc_ref[...]) + sh_ref[...]).astype(o_ref.dtype)


def _norm_mod(x, gain, mod, is_prompt, tm=256):
    m, d = x.shape
    row = _mod_row_fn(is_prompt, tm)
    return pl.pallas_call(
        _norm_mod_kernel,
        grid=(m // tm,),
        in_specs=[
            pl.BlockSpec((tm, d), lambda i: (i, 0)),
            pl.BlockSpec((1, d), lambda i: (0, 0)),
            pl.BlockSpec((None, 1, d), lambda i: (row(i), 0, 0)),
            pl.BlockSpec((None, 1, d), lambda i: (row(i), 0, 1)),
        ],
        out_specs=pl.BlockSpec((tm, d), lambda i: (i, 0)),
        out_shape=jax.ShapeDtypeStruct((m, d), BF16),
        compiler_params=_params(1),
        name="norm_mod",
    )(x, gain.reshape(1, d), mod, mod)


def _mm_kernel(a_ref, w_ref, *refs, n_casts):
    cast_in, o_ref, cast_out = refs[:n_casts], refs[n_casts], refs[n_casts + 1:]
    o_ref[...] = _dot(a_ref[...], w_ref[...]).astype(o_ref.dtype)
    for src, dst in zip(cast_in, cast_out):
        dst[...] = src[...].astype(dst.dtype)


def _matmul(a, w, col0, ncols, out_dtype, tn=1024, cast_weights=()):
    m, k = a.shape
    tm = 1024 if out_dtype == BF16 else 512
    assert col0 % tn == 0 and ncols % tn == 0 and m % tm == 0
    j0 = col0 // tn
    n_i = m // tm
    n_steps = (ncols // tn) * n_i
    in_specs = [pl.BlockSpec((tm, k), lambda j, i: (i, 0)),
                pl.BlockSpec((k, tn), lambda j, i: (0, j + j0))]
    out_specs = [pl.BlockSpec((tm, tn), lambda j, i: (i, j))]
    out_shape = [jax.ShapeDtypeStruct((m, ncols), out_dtype)]
    for cw in cast_weights:
        rows, cols = cw.shape
        chunk = CAST_CHUNK_ROWS
        n_chunks = rows // chunk
        assert rows % chunk == 0 and n_chunks <= n_steps
        spec = pl.BlockSpec((chunk, cols),
                            lambda j, i, n_chunks=n_chunks: (jnp.minimum(j * n_i + i, n_chunks - 1), 0))
        in_specs.append(spec)
        out_specs.append(spec)
        out_shape.append(jax.ShapeDtypeStruct((rows, cols), BF16))
    outs = pl.pallas_call(
        functools.partial(_mm_kernel, n_casts=len(cast_weights)),
        grid=(ncols // tn, n_i),
        in_specs=in_specs,
        out_specs=out_specs,
        out_shape=out_shape,
        compiler_params=pltpu.CompilerParams(
            dimension_semantics=("arbitrary", "arbitrary"),
            vmem_limit_bytes=VMEM_LIMIT_ROWS if cast_weights else VMEM_LIMIT),
        name="proj",
    )(a, w, *cast_weights)
    return outs if cast_weights else outs[0]


def _kv_proj_kernel(a_ref, w_ref, ka_ref, va_ref, kb_ref, vb_ref, *, kb_tile0, vb_tile0):
    j = pl.program_id(0)
    acc = _dot(a_ref[...], w_ref[...])

    @pl.when(j < kb_tile0)
    def _():
        ka_ref[...] = acc[:, :A_KV_WIDTH]
        va_ref[...] = acc[:, A_KV_WIDTH:]

    @pl.when((j >= kb_tile0) & (j < vb_tile0))
    def _():
        kb_ref[...] = acc

    @pl.when(j >= vb_tile0)
    def _():
        vb_ref[...] = acc


def _kv_projection(a, w, tn=1024, tm=512):
    m, k = a.shape
    assert COL_KA == 0 and COL_KB == tn and 2 * A_KV_WIDTH == tn and B_WIDTH % tn == 0
    kb_tile0, vb_tile0, end_tile = COL_KB // tn, COL_VB // tn, COL_QA // tn
    n_i = m // tm
    last = n_i - 1

    def parked(j, i, t0, t1):
        row = jnp.where(j < t0, 0, jnp.where(j < t1, i, last))
        return row, jnp.clip(j - t0, 0, t1 - t0 - 1)

    return pl.pallas_call(
        functools.partial(_kv_proj_kernel, kb_tile0=kb_tile0, vb_tile0=vb_tile0),
        grid=(end_tile, n_i),
        in_specs=[pl.BlockSpec((tm, k), lambda j, i: (i, 0)),
                  pl.BlockSpec((k, tn), lambda j, i: (0, j))],
        out_specs=[pl.BlockSpec((tm, A_KV_WIDTH), lambda j, i: parked(j, i, 0, kb_tile0)),
                   pl.BlockSpec((tm, A_KV_WIDTH), lambda j, i: parked(j, i, 0, kb_tile0)),
                   pl.BlockSpec((tm, tn), lambda j, i: parked(j, i, kb_tile0, vb_tile0)),
                   pl.BlockSpec((tm, tn), lambda j, i: parked(j, i, vb_tile0, end_tile))],
        out_shape=[jax.ShapeDtypeStruct((m, A_KV_WIDTH), F32),
                   jax.ShapeDtypeStruct((m, A_KV_WIDTH), F32),
                   jax.ShapeDtypeStruct((m, B_WIDTH), F32),
                   jax.ShapeDtypeStruct((m, B_WIDTH), F32)],
        compiler_params=_params(2),
        name="kv_proj",
    )(a, w)


def _row_rsqrt(row_scr, n_tiles):
    ss = None
    for jj in range(n_tiles):
        t = row_scr[jj]
        part = jnp.sum(t * t, axis=-1, keepdims=True)
        ss = part if ss is None else ss + part
    width = n_tiles * row_scr.shape[2]
    return lax.rsqrt(ss / width + NORM_EPS)


def _dot_halves(a1_ref, a2_ref, w_ref):
    kh = a1_ref.shape[1]
    return _dot(a1_ref[...], w_ref[:kh, :]) + _dot(a2_ref[...], w_ref[kh:, :])


def _mm_res_normmod_kernel(a1_ref, a2_ref, w_ref, x_ref, g_ref, gain_ref, sh_ref, sc_ref,
                           x1_ref, h_ref, row_scr, *, n_tiles, tn):
    j = pl.program_id(1)
    x1 = x_ref[...] + g_ref[...] * _dot_halves(a1_ref, a2_ref, w_ref)
    x1_ref[...] = x1
    row_scr[j] = x1

    @pl.when(j == n_tiles - 1)
    def _():
        r = _row_rsqrt(row_scr, n_tiles)
        for jj in range(n_tiles):
            cols = slice(jj * tn, (jj + 1) * tn)
            y = row_scr[jj] * r * gain_ref[:, cols]
            h_ref[:, cols] = (y * (1.0 + sc_ref[:, cols]) + sh_ref[:, cols]).astype(h_ref.dtype)


def _mm_res_norm_kernel(a1_ref, a2_ref, w_ref, x_ref, g_ref, gain_ref, y_ref, *, n_tiles, tn):
    j = pl.program_id(1)
    x2 = x_ref[...] + g_ref[...] * _dot_halves(a1_ref, a2_ref, w_ref)
    for jj in range(n_tiles):
        @pl.when(j == jj)
        def _(jj=jj):
            y_ref[:, jj * tn:(jj + 1) * tn] = x2

    @pl.when(j == n_tiles - 1)
    def _():
        ss = None
        for jj in range(n_tiles):
            t = y_ref[:, jj * tn:(jj + 1) * tn]
            part = jnp.sum(t * t, axis=-1, keepdims=True)
            ss = part if ss is None else ss + part
        r = lax.rsqrt(ss / (n_tiles * tn) + NORM_EPS)
        for jj in range(n_tiles):
            cols = slice(jj * tn, (jj + 1) * tn)
            y_ref[:, cols] = y_ref[:, cols] * r * gain_ref[:, cols]


def _matmul_residual_norm(a_halves, w, x, mod, is_prompt, gain, next_mod=None, tn=1024, tm=512):
    (a1, c1), (a2, c2) = a_halves
    m = a1.shape[0]
    k, n = w.shape
    n_tiles = n // tn
    row = _mod_row_fn(is_prompt, tm)
    gate_block0 = 2 * n // tn
    in_specs = [pl.BlockSpec((tm, k // 2), lambda i, j: (i, c1)),
                pl.BlockSpec((tm, k // 2), lambda i, j: (i, c2)),
                pl.BlockSpec((k, tn), lambda i, j: (0, j)),
                pl.BlockSpec((tm, tn), lambda i, j: (i, j)),
                pl.BlockSpec((None, 1, tn), lambda i, j: (row(i), 0, gate_block0 + j)),
                pl.BlockSpec((1, n), lambda i, j: (0, 0))]
    args = [a1, a2, w, x, mod, gain.reshape(1, n)]
    scratch = [pltpu.VMEM((n_tiles, tm, tn), F32)]
    params = pltpu.CompilerParams(dimension_semantics=("arbitrary", "arbitrary"),
                                  vmem_limit_bytes=VMEM_LIMIT_ROWS)
    if next_mod is None:
        return pl.pallas_call(
            functools.partial(_mm_res_norm_kernel, n_tiles=n_tiles, tn=tn),
            grid=(m // tm, n_tiles),
            in_specs=in_specs,
            out_specs=pl.BlockSpec((tm, n), lambda i, j: (i, 0)),
            out_shape=jax.ShapeDtypeStruct((m, n), F32),
            compiler_params=params,
            name="proj_residual_norm",
        )(*args)
    in_specs += [pl.BlockSpec((None, 1, n), lambda i, j: (row(i), 0, 0)),
                 pl.BlockSpec((None, 1, n), lambda i, j: (row(i), 0, 1))]
    args += [next_mod, next_mod]
    return pl.pallas_call(
        functools.partial(_mm_res_normmod_kernel, n_tiles=n_tiles, tn=tn),
        grid=(m // tm, n_tiles),
        in_specs=in_specs,
        out_specs=[pl.BlockSpec((tm, tn), lambda i, j: (i, j)),
                   pl.BlockSpec((tm, n), lambda i, j: (i, 0))],
        out_shape=[jax.ShapeDtypeStruct((m, n), F32), jax.ShapeDtypeStruct((m, n), BF16)],
        scratch_shapes=scratch,
        compiler_params=params,
        name="proj_residual_normmod",
    )(*args)


def _chunked_softmax(n_rows, chunk, loads, stores, sink2=None):
    inv = []
    for c in range(n_rows // chunk):
        rows = slice(c * chunk, (c + 1) * chunk)
        ts = [load(rows) for load in loads]
        m = functools.reduce(jnp.maximum, [jnp.max(t, axis=-1, keepdims=True) for t in ts])
        if sink2 is not None:
            m = jnp.maximum(m, sink2[rows])
        es = [jnp.exp2(t - m) for t in ts]
        l = functools.reduce(jnp.add, [jnp.sum(e, axis=-1, keepdims=True) for e in es])
        if sink2 is not None:
            l = l + jnp.exp2(sink2[rows] - m)
        for store, e in zip(stores, es):
            store(rows, e.astype(BF16))
        inv.append(1.0 / l)
    return jnp.concatenate(inv, axis=0)


def _ctx_attn_kernel(sink_ref, qg_ref, ka_ref, va_ref, kb_ref, vb_ref, o_ref, s_scr, e_scr):
    n = SEQ
    gate0 = MIX_WIDTH
    n_stack = A_GROUPS

    def softmax(sink2):
        def store(rows, e):
            e_scr[rows, :] = e
        return _chunked_softmax(n_stack * n, CTX_CHUNK, [lambda rows: s_scr[rows, :] * LOGIT_SCALE],
                                [store], sink2)

    def emit(o, g, out_col):
        cols = slice(out_col, out_col + HEAD_DIM)
        gate = qg_ref[:, gate0 + out_col:gate0 + out_col + HEAD_DIM].astype(F32)
        o_ref[:, cols] = (o[g * n:(g + 1) * n] * _silu(gate)).astype(o_ref.dtype)

    for kv in range(A_KV_HEADS):
        cols = slice(kv * HEAD_DIM, (kv + 1) * HEAD_DIM)
        heads = [kv * A_GROUPS + g for g in range(A_GROUPS)]
        q = jnp.concatenate(
            [qg_ref[:, h * HEAD_DIM:(h + 1) * HEAD_DIM] for h in heads], axis=0)
        sink2 = jnp.concatenate(
            [jnp.full((n, 1), sink_ref[h] * LOG2E, F32) for h in heads], axis=0)
        s_scr[...] = _dot_nt(q, ka_ref[:, cols].astype(BF16))
        inv = softmax(sink2)
        o = _dot(e_scr[...], va_ref[:, cols].astype(BF16)) * inv
        for g, h in enumerate(heads):
            emit(o, g, h * HEAD_DIM)
    for h0 in range(0, B_HEADS, n_stack):
        heads = range(h0, h0 + n_stack)
        for g, h in enumerate(heads):
            cols = slice(h * HEAD_DIM, (h + 1) * HEAD_DIM)
            q = qg_ref[:, A_WIDTH + h * HEAD_DIM:A_WIDTH + (h + 1) * HEAD_DIM]
            s_scr[g * n:(g + 1) * n, :] = _dot_nt(q, kb_ref[:, cols].astype(BF16))
        inv = softmax(None)
        o = jnp.concatenate(
            [_dot(e_scr[g * n:(g + 1) * n, :],
                  vb_ref[:, h * HEAD_DIM:(h + 1) * HEAD_DIM].astype(BF16))
             for g, h in enumerate(heads)], axis=0) * inv
        for g, h in enumerate(heads):
            emit(o, g, A_WIDTH + h * HEAD_DIM)


def _ctx_attention(sink, qg, ka, va, kb, vb):
    stack_rows = A_GROUPS * SEQ
    return pl.pallas_call(
        _ctx_attn_kernel,
        grid=(BATCH,),
        in_specs=[
            pl.BlockSpec(memory_space=pltpu.SMEM),
            pl.BlockSpec((SEQ, 2 * MIX_WIDTH), lambda b: (b, 0)),
            pl.BlockSpec((SEQ, A_KV_WIDTH), lambda b: (b, 0)),
            pl.BlockSpec((SEQ, A_KV_WIDTH), lambda b: (b, 0)),
            pl.BlockSpec((SEQ, B_WIDTH), lambda b: (b, 0)),
            pl.BlockSpec((SEQ, B_WIDTH), lambda b: (b, 0)),
        ],
        out_specs=pl.BlockSpec((SEQ, MIX_WIDTH), lambda b: (b, 0)),
        out_shape=jax.ShapeDtypeStruct((N_PROMPT, MIX_WIDTH), BF16),
        scratch_shapes=[pltpu.VMEM((stack_rows, SEQ), F32),
                        pltpu.VMEM((stack_rows, SEQ), BF16)],
        compiler_params=_params(1),
        name="ctx_attention",
    )(sink, qg, ka, va, kb, vb)


def _split_cache_heads(c_ref, scr, n_heads):
    for h in range(n_heads):
        scr[h] = c_ref[pl.ds(h, PAST_LEN, stride=n_heads), :].astype(scr.dtype)


def _win_attn_kernel(sink_ref, q_ref, k_ref, v_ref, ck_ref, cv_ref, gate_ref,
                     cos_ref, sin_ref, o_ref, qs_ref, ks_ref, ck_scr, cv_scr,
                     band_scr, sw_scr, sc_scr, ew_scr, ec_scr):
    kv = pl.program_id(1)
    n = DEC_SEQ

    @pl.when(kv == 0)
    def _():
        _split_cache_heads(ck_ref, ck_scr, A_KV_HEADS)
        _split_cache_heads(cv_ref, cv_scr, A_KV_HEADS)

    cos = cos_ref[...]
    sin = sin_ref[...]
    lane = lax.broadcasted_iota(jnp.int32, (n, HEAD_DIM), 1)
    first_quarter = (lane % (HEAD_DIM // 2)) < (HEAD_DIM // 4)

    def rope(x):
        rot = jnp.where(first_quarter,
                        pltpu.roll(x, HEAD_DIM - HEAD_DIM // 4, 1),
                        pltpu.roll(x, HEAD_DIM // 4, 1))
        return x * cos + rot * sin

    ks_ref[...] = rope(k_ref[...].astype(F32)).astype(BF16)
    for g in range(A_GROUPS):
        qs_ref[g] = (rope(q_ref[:, g * HEAD_DIM:(g + 1) * HEAD_DIM].astype(F32))
                     * LOGIT_SCALE).astype(BF16)

    ck = ck_scr[kv]
    cv = cv_scr[kv]
    rows = A_GROUPS * A_BLOCK
    sink2 = jnp.concatenate(
        [jnp.full((A_BLOCK, 1), sink_ref[kv * A_GROUPS + g] * LOG2E, F32)
         for g in range(A_GROUPS)], axis=0)
    qi = lax.broadcasted_iota(jnp.int32, (rows, 3 * A_BLOCK), 0) % A_BLOCK
    rel = lax.broadcasted_iota(jnp.int32, (rows, 3 * A_BLOCK), 1) - A_BLOCK - qi
    band_scr[...] = jnp.where(jnp.abs(rel) <= A_HALF_WIN, 0.0, NEG_INF)
    for blk in range(n // A_BLOCK):
        r0 = blk * A_BLOCK
        lo = max(0, r0 - A_BLOCK)
        hi = min(n, r0 + 2 * A_BLOCK)
        span = hi - lo
        b0 = lo - (r0 - A_BLOCK)
        q = jnp.concatenate([qs_ref[g, r0:r0 + A_BLOCK, :] for g in range(A_GROUPS)], axis=0)
        sw_scr[:, :span] = _dot_nt(q, ks_ref[lo:hi, :])
        sc_scr[...] = _dot_nt(q, ck)

        def load_w(r, span=span, b0=b0):
            return sw_scr[r, :span] + band_scr[r, b0:b0 + span]

        def store_w(r, e, span=span):
            ew_scr[r, :span] = e

        def store_c(r, e):
            ec_scr[r, :] = e

        inv = _chunked_softmax(rows, WIN_CHUNK, [load_w, lambda r: sc_scr[r, :]],
                               [store_w, store_c], sink2)
        o = (_dot(ew_scr[:, :span], v_ref[lo:hi, :]) + _dot(ec_scr[...], cv)) * inv
        for g in range(A_GROUPS):
            hc = slice(g * HEAD_DIM, (g + 1) * HEAD_DIM)
            gate = gate_ref[r0:r0 + A_BLOCK, hc].astype(F32)
            o_ref[r0:r0 + A_BLOCK, hc] = (
                o[g * A_BLOCK:(g + 1) * A_BLOCK] * _silu(gate)).astype(o_ref.dtype)


def _win_attention(sink, proj, cache_k, cache_v, cos, sin_signed):
    gw = A_GROUPS * HEAD_DIM
    stack_rows = A_GROUPS * A_BLOCK
    n = DEC_SEQ
    cache_spec = pl.BlockSpec((None, PAST_LEN * A_KV_HEADS, HEAD_DIM), lambda b, k: (b, 0, 0))
    return pl.pallas_call(
        _win_attn_kernel,
        grid=(DEC_BATCH, A_KV_HEADS),
        in_specs=[
            pl.BlockSpec(memory_space=pltpu.SMEM),
            pl.BlockSpec((n, gw), lambda b, k: (b, COL_QA // gw + k)),
            pl.BlockSpec((n, HEAD_DIM), lambda b, k: (b, COL_KA // HEAD_DIM + k)),
            pl.BlockSpec((n, HEAD_DIM), lambda b, k: (b, COL_VA // HEAD_DIM + k)),
            cache_spec,
            cache_spec,
            pl.BlockSpec((n, gw), lambda b, k: (b, COL_GATE // gw + k)),
            pl.BlockSpec((n, HEAD_DIM), lambda b, k: (0, 0)),
            pl.BlockSpec((n, HEAD_DIM), lambda b, k: (0, 0)),
        ],
        out_specs=pl.BlockSpec((n, gw), lambda b, k: (b, k)),
        out_shape=jax.ShapeDtypeStruct((N_SAMPLE, A_WIDTH), BF16),
        scratch_shapes=[pltpu.VMEM((A_GROUPS, n, HEAD_DIM), BF16),
                        pltpu.VMEM((n, HEAD_DIM), BF16),
                        pltpu.VMEM((A_KV_HEADS, PAST_LEN, HEAD_DIM), BF16),
                        pltpu.VMEM((A_KV_HEADS, PAST_LEN, HEAD_DIM), BF16),
                        pltpu.VMEM((stack_rows, 3 * A_BLOCK), F32),
                        pltpu.VMEM((stack_rows, 3 * A_BLOCK), F32),
                        pltpu.VMEM((stack_rows, PAST_LEN), F32),
                        pltpu.VMEM((stack_rows, 3 * A_BLOCK), BF16),
                        pltpu.VMEM((stack_rows, PAST_LEN), BF16)],
        compiler_params=_params(2),
        name="win_attention",
    )(sink, proj, proj, proj, cache_k, cache_v, proj, cos, sin_signed)


def _bias_kernel(rpb_ref, o_ref):
    h = pl.program_id(0)
    shape = (GRID_W, 2 * GRID_W)
    c = lax.broadcasted_iota(jnp.int32, shape, 0)
    j2 = lax.broadcasted_iota(jnp.int32, shape, 1)
    kc = j2 % GRID_W
    second = j2 >= GRID_W
    col_start = jnp.clip(c - B_WIN_COLS // 2, 0, GRID_W - B_WIN_COLS)
    ok = (kc >= col_start) & (kc < col_start + B_WIN_COLS)
    dc = kc - c + B_WIN_COLS - 1
    base = h * (RPB_ROWS * RPB_COLS)
    pair_scr = []
    for i in range(N_BIAS_PAIRS):
        acc = jnp.full(shape, NEG_INF, F32)
        for d in range(RPB_COLS):
            val = jnp.where(second, rpb_ref[base + (i + 1) * RPB_COLS + d],
                            rpb_ref[base + i * RPB_COLS + d]) * LOG2E
            acc = jnp.where(ok & (dc == d), val, acc)
        pair_scr.append(acc)
    for r in range(GRID_ROWS):
        dr0 = _na_key_row0(r) - r + B_WIN_ROWS - 1
        for i in range(B_WIN_ROWS // 2):
            o_ref[r * GRID_W:(r + 1) * GRID_W, i * 2 * GRID_W:(i + 1) * 2 * GRID_W] = (
                pair_scr[dr0 + 2 * i])


def _expand_bias(rpb):
    kw = B_WIN_ROWS * GRID_W
    return pl.pallas_call(
        _bias_kernel,
        grid=(B_HEADS,),
        in_specs=[pl.BlockSpec(memory_space=pltpu.SMEM)],
        out_specs=pl.BlockSpec((None, DEC_SEQ, kw), lambda h: (h, 0, 0)),
        out_shape=jax.ShapeDtypeStruct((B_HEADS, DEC_SEQ, kw), F32),
        compiler_params=_params(1),
        name="expand_bias",
    )(rpb.reshape(-1))


def _na_key_row0(r):
    return min(max(r - B_WIN_ROWS // 2, 0), GRID_ROWS - B_WIN_ROWS)


def _na_row_groups():
    groups = []
    for r in range(GRID_ROWS):
        rs = _na_key_row0(r)
        if groups and groups[-1][2] == rs:
            groups[-1] = (groups[-1][0], r + 1, rs)
        else:
            groups.append((r, r + 1, rs))
    return groups


def _na_attn_kernel(q_ref, k_ref, v_ref, ck_ref, cv_ref, gate_ref, bias_ref, o_ref,
                    ck_scr, cv_scr, sn_scr, sc_scr, en_scr, ec_scr):
    kw = B_WIN_ROWS * GRID_W
    h = pl.program_id(1)

    @pl.when(h == 0)
    def _():
        _split_cache_heads(ck_ref, ck_scr, B_HEADS)
        _split_cache_heads(cv_ref, cv_scr, B_HEADS)

    groups = _na_row_groups()
    sc_scr[...] = _dot_nt(q_ref[...], ck_scr[h])
    for r0, r1, rs in groups:
        sn_scr[r0 * GRID_W:r1 * GRID_W, :] = _dot_nt(
            q_ref[r0 * GRID_W:r1 * GRID_W, :], k_ref[rs * GRID_W:rs * GRID_W + kw, :])

    def store_n(r, e):
        en_scr[r, :] = e

    def store_c(r, e):
        ec_scr[r, :] = e

    inv = _chunked_softmax(
        DEC_SEQ, NA_CHUNK,
        [lambda r: sn_scr[r, :] * LOGIT_SCALE + bias_ref[r, :], lambda r: sc_scr[r, :] * LOGIT_SCALE],
        [store_n, store_c])
    o_n = jnp.concatenate(
        [_dot(en_scr[r0 * GRID_W:r1 * GRID_W, :], v_ref[rs * GRID_W:rs * GRID_W + kw, :])
         for r0, r1, rs in groups], axis=0)
    o = (o_n + _dot(ec_scr[...], cv_scr[h])) * inv
    o_ref[...] = (o * _silu(gate_ref[...].astype(F32))).astype(o_ref.dtype)


def _na_attention(proj, cache_k, cache_v, bias):
    n = DEC_SEQ
    hd = HEAD_DIM
    cache_spec = pl.BlockSpec((None, PAST_LEN * B_HEADS, hd), lambda b, h: (b, 0, 0))
    return pl.pallas_call(
        _na_attn_kernel,
        grid=(DEC_BATCH, B_HEADS),
        in_specs=[
            pl.BlockSpec((n, hd), lambda b, h: (b, COL_QB // hd + h)),
            pl.BlockSpec((n, hd), lambda b, h: (b, COL_KB // hd + h)),
            pl.BlockSpec((n, hd), lambda b, h: (b, COL_VB // hd + h)),
            cache_spec,
            cache_spec,
            pl.BlockSpec((n, hd), lambda b, h: (b, (COL_GATE + A_WIDTH) // hd + h)),
            pl.BlockSpec((None, n, B_WIN_ROWS * GRID_W), lambda b, h: (h, 0, 0)),
        ],
        out_specs=pl.BlockSpec((n, hd), lambda b, h: (b, h)),
        out_shape=jax.ShapeDtypeStruct((N_SAMPLE, B_WIDTH), BF16),
        scratch_shapes=[pltpu.VMEM((B_HEADS, PAST_LEN, hd), BF16),
                        pltpu.VMEM((B_HEADS, PAST_LEN, hd), BF16),
                        pltpu.VMEM((n, B_WIN_ROWS * GRID_W), F32),
                        pltpu.VMEM((n, PAST_LEN), F32),
                        pltpu.VMEM((n, B_WIN_ROWS * GRID_W), BF16),
                        pltpu.VMEM((n, PAST_LEN), BF16)],
        compiler_params=_params(2),
        name="na_attention",
    )(proj, proj, proj, cache_k, cache_v, proj, bias)


def _pool_kernel(u_ref, gate_ref, band_ref, inv_count_ref, w_ref, scale_ref, o_ref, *, seq, rows):
    cb = POOL_BAND_BLOCK
    parts = []
    for i in range(rows // cb):
        lo = i * cb if seq <= cb else max(0, (i - 1) * cb)
        hi = (i + 1) * cb if seq <= cb else min(rows, (i + 2) * cb)
        parts.append(_dot(band_ref[i * cb:(i + 1) * cb, lo:hi], u_ref[lo:hi, :]))
    wsum = jnp.concatenate(parts, axis=0)
    pooled = wsum * inv_count_ref[...] - u_ref[...].astype(F32)
    y = _dot(pooled.astype(BF16), w_ref[...]) * scale_ref[...]
    o_ref[...] = (y * _silu(gate_ref[...].astype(F32))).astype(o_ref.dtype)


def _pool_operators(seq, rows):
    t = jnp.arange(rows)[:, None]
    j = jnp.arange(rows)[None, :]
    same_seq = (t // seq) == (j // seq)
    pos = t % seq
    bands, inv_counts = [], []
    for window in POOL_WINDOWS:
        half = window // 2
        assert half <= POOL_BAND_BLOCK
        bands.append(((j - t >= -half) & (j - t < half) & same_seq).astype(BF16))
        inv_counts.append(1.0 / (jnp.minimum(pos + half, seq) - jnp.maximum(pos - half, 0)).astype(F32))
    return jnp.stack(bands), jnp.stack(inv_counts)


def _pool_mixer(ug, w_grp, scale, seq, rows=1024):
    m = ug.shape[0]
    n_groups = len(POOL_WINDOWS)
    pg = POOL_GROUP
    band, inv_count = _pool_operators(seq, rows)
    return pl.pallas_call(
        functools.partial(_pool_kernel, seq=seq, rows=rows),
        grid=(n_groups, m // rows),
        in_specs=[
            pl.BlockSpec((rows, pg), lambda g, i: (i, g)),
            pl.BlockSpec((rows, pg), lambda g, i: (i, n_groups + g)),
            pl.BlockSpec((None, rows, rows), lambda g, i: (g, 0, 0)),
            pl.BlockSpec((None, rows, 1), lambda g, i: (g, 0, 0)),
            pl.BlockSpec((None, pg, pg), lambda g, i: (g, 0, 0)),
            pl.BlockSpec((1, pg), lambda g, i: (0, g)),
        ],
        out_specs=pl.BlockSpec((rows, pg), lambda g, i: (i, g)),
        out_shape=jax.ShapeDtypeStruct((m, n_groups * pg), BF16),
        compiler_params=_params(2),
        name="pool_mixer",
    )(ug, ug, band, inv_count, w_grp, scale.reshape(1, -1))


def _rope_tables():
    t = jnp.arange(DEC_SEQ)
    quarter = HEAD_DIM // 4
    inv_freq = ROPE_BASE ** (-jnp.arange(quarter, dtype=F32) / quarter)
    ang_r = (t // GRID_W).astype(F32)[:, None] * inv_freq
    ang_c = (t % GRID_W).astype(F32)[:, None] * inv_freq
    ang = jnp.concatenate([ang_r, ang_r, ang_c, ang_c], axis=-1)
    sign = jnp.tile(jnp.concatenate([-jnp.ones((quarter,), F32), jnp.ones((quarter,), F32)]), 2)
    return jnp.cos(ang), jnp.sin(ang) * sign


def kernel(x_prompt, x_sample, c, cache_a_k, cache_a_v, cache_b_k, cache_b_v, c_ctx,
           w_ada, b_ada, norm_g, w_in_attn, a_sink, b_rpb, w_out_attn,
           w_in_pool, w_grp_pool, pool_scale, w_out_pool, final_g):
    d = D_MODEL
    xp = x_prompt.reshape(N_PROMPT, d)
    xs = x_sample.reshape(N_SAMPLE, d)

    w = w_in_attn[0]
    o_qa = 0
    o_ka = o_qa + A_WIDTH
    o_va = o_ka + A_KV_WIDTH
    o_qb = o_va + A_KV_WIDTH
    o_kb = o_qb + B_WIDTH
    o_vb = o_kb + B_WIDTH
    o_gate = o_vb + B_WIDTH
    w_in0 = jnp.concatenate([
        w[:, o_ka:o_va], w[:, o_va:o_qb], w[:, o_kb:o_vb], w[:, o_vb:o_gate],
        w[:, o_qa:o_ka], w[:, o_qb:o_kb], w[:, o_gate:]], axis=1).astype(BF16)

    cond = jnp.zeros((MOD_ROWS, d), F32).at[:DEC_BATCH].set(c).at[CTX_MOD_ROW].set(c_ctx)
    mod = _ada(cond, w_ada, b_ada)
    mod0 = mod[0].reshape(MOD_ROWS, 1, 3 * d)
    mod1 = mod[1].reshape(MOD_ROWS, 1, 3 * d)

    hp = _norm_mod(xp, norm_g[0], mod0, True)
    hs = _norm_mod(xs, norm_g[0], mod0, False)

    ka, va, kb, vb = _kv_projection(hp, w_in0)
    qg_p, w_in1 = _matmul(hp, w_in0, COL_QA, 2 * MIX_WIDTH, BF16, cast_weights=(w_in_pool[0],))
    n_groups = len(POOL_WINDOWS)
    proj_s, w_out0, w_out1, w_grp = _matmul(
        hs, w_in0, 0, ATTN_IN_WIDTH, BF16,
        cast_weights=(w_out_attn[0], w_out_pool[0],
                      w_grp_pool[0].reshape(n_groups * POOL_GROUP, POOL_GROUP)))
    w_grp = w_grp.reshape(n_groups, POOL_GROUP, POOL_GROUP)

    sink = a_sink[0]
    og_p = _ctx_attention(sink, qg_p, ka, va, kb, vb)

    cos, sin_signed = _rope_tables()
    flat = lambda cache: cache.reshape(DEC_BATCH, -1, HEAD_DIM)
    oa_s = _win_attention(sink, proj_s, flat(cache_a_k), flat(cache_a_v), cos, sin_signed)
    ob_s = _na_attention(proj_s, flat(cache_b_k), flat(cache_b_v), _expand_bias(b_rpb[0]))

    xp1, hp1 = _matmul_residual_norm([(og_p, 0), (og_p, 1)], w_out0, xp, mod0, True,
                                     norm_g[1], mod1)
    xs1, hs1 = _matmul_residual_norm([(oa_s, 0), (ob_s, 0)], w_out0, xs, mod0, False,
                                     norm_g[1], mod1)

    ug_p = _matmul(hp1, w_in1, 0, 2 * d, BF16)
    ug_s = _matmul(hs1, w_in1, 0, 2 * d, BF16)
    y_p = _pool_mixer(ug_p, w_grp, pool_scale[0], SEQ)
    y_s = _pool_mixer(ug_s, w_grp, pool_scale[0], DEC_SEQ)
    y_prompt = _matmul_residual_norm([(y_p, 0), (y_p, 1)], w_out1, xp1, mod1, True,
                                     final_g).reshape(BATCH, SEQ, d)
    y_sample = _matmul_residual_norm([(y_s, 0), (y_s, 1)], w_out1, xs1, mod1, False,
                                     final_g).reshape(DEC_BATCH, DEC_SEQ, d)

    kv_shape_a = (BATCH, 1, SEQ, A_KV_HEADS, HEAD_DIM)
    kv_shape_b = (BATCH, 1, SEQ, B_HEADS, HEAD_DIM)
    return (y_prompt, y_sample, ka.reshape(kv_shape_a), va.reshape(kv_shape_a),
            kb.reshape(kv_shape_b), vb.reshape(kv_shape_b))
```

```python
import functools

import jax
import jax.numpy as jnp
from jax import lax
from jax.experimental import pallas as pl
from jax.experimental.pallas import tpu as pltpu

F32 = jnp.float32
BF16 = jnp.bfloat16

D_MODEL = 4096
BATCH = 32
SEQ = 256
DEC_BATCH = 8
DEC_SEQ = 1024
PAST_LEN = 512
GRID_W = 64
HEAD_DIM = 128
A_Q_HEADS = 16
A_KV_HEADS = 4
A_GROUPS = 4
A_HALF_WIN = 128
A_BLOCK = 128
B_HEADS = 16
B_WIN_ROWS = 8
B_WIN_COLS = 16
A_WIDTH = A_Q_HEADS * HEAD_DIM
A_KV_WIDTH = A_KV_HEADS * HEAD_DIM
B_WIDTH = B_HEADS * HEAD_DIM
MIX_WIDTH = A_WIDTH + B_WIDTH
POOL_WINDOWS = (2, 4, 8, 16)
POOL_GROUP = 1024
POOL_BAND_BLOCK = 256
CAST_CHUNK_ROWS = 64
ROPE_BASE = 10000.0
NORM_EPS = 1e-6
NEG_INF = -1e30
ATTN_SCALE = HEAD_DIM ** -0.5
LOG2E = 1.4426950408889634
LOGIT_SCALE = ATTN_SCALE * LOG2E
CTX_CHUNK = 1024
WIN_CHUNK = 512
NA_CHUNK = 1024

N_PROMPT = BATCH * SEQ
N_SAMPLE = DEC_BATCH * DEC_SEQ
MOD_ROWS = 16
CTX_MOD_ROW = DEC_BATCH
GRID_ROWS = DEC_SEQ // GRID_W
N_BIAS_PAIRS = 2 * B_WIN_ROWS - 2
RPB_ROWS = 2 * B_WIN_ROWS - 1
RPB_COLS = 2 * B_WIN_COLS - 1

COL_QA = 0
COL_KA = COL_QA + A_WIDTH
COL_VA = COL_KA + A_KV_WIDTH
COL_QB = COL_VA + A_KV_WIDTH
COL_KB = COL_QB + B_WIDTH
COL_VB = COL_KB + B_WIDTH
COL_GATE = COL_VB + B_WIDTH
ATTN_IN_WIDTH = COL_GATE + MIX_WIDTH
PROJ_TN = 1024


def _col_tiles(col0, width):
    assert col0 % PROJ_TN == 0 and width % PROJ_TN == 0
    return tuple(range(col0 // PROJ_TN, (col0 + width) // PROJ_TN))


def _tile_lookup(tiles):
    runs = []
    for jj, t in enumerate(tiles):
        if not runs or runs[-1][1] != t - jj:
            runs.append((jj, t - jj))

    def lookup(j):
        off = runs[0][1]
        for start, o in runs[1:]:
            off = jnp.where(j >= start, o, off)
        return j + off
    return lookup

VMEM_LIMIT = 48 * 1024 * 1024
VMEM_LIMIT_ROWS = 56 * 1024 * 1024


def _params(n_grid_dims):
    return pltpu.CompilerParams(
        dimension_semantics=("arbitrary",) * n_grid_dims,
        vmem_limit_bytes=VMEM_LIMIT,
    )


def _silu(x):
    return x / (1.0 + jnp.exp(-x))


def _dot_nt(a, b):
    return lax.dot_general(a, b, (((1,), (1,)), ((), ())), preferred_element_type=F32)


def _dot(a, b):
    return jnp.dot(a, b, preferred_element_type=F32)


def _ada_kernel(cond_ref, w_ref, b_ref, o_ref):
    a = _silu(cond_ref[...]).astype(BF16)
    o_ref[...] = _dot(a, w_ref[...].astype(BF16)) + b_ref[...]


def _ada(cond, w_ada, b_ada, tn=512):
    depth, d, n = w_ada.shape
    return pl.pallas_call(
        _ada_kernel,
        grid=(depth, n // tn),
        in_specs=[
            pl.BlockSpec((MOD_ROWS, d), lambda l, j: (0, 0)),
            pl.BlockSpec((None, d, tn), lambda l, j: (l, 0, j)),
            pl.BlockSpec((None, 1, tn), lambda l, j: (l, 0, j)),
        ],
        out_specs=pl.BlockSpec((None, MOD_ROWS, tn), lambda l, j: (l, 0, j)),
        out_shape=jax.ShapeDtypeStruct((depth, MOD_ROWS, n), F32),
        compiler_params=_params(2),
        name="ada",
    )(cond, w_ada, b_ada.reshape(depth, 1, n))


def _mod_row_fn(is_prompt, tm):
    if is_prompt:
        return lambda i: CTX_MOD_ROW
    return lambda i: (i * tm) // DEC_SEQ


def _norm_mod_kernel(x_ref, g_ref, sh_ref, sc_ref, o_ref):
    x = x_ref[...]
    ms = jnp.mean(x * x, axis=-1, keepdims=True)
    y = x * lax.rsqrt(ms + NORM_EPS) * g_ref[...]
    o_ref[...] = (y * (1.0 + sc_ref[...]) + sh_ref[...]).astype(o_ref.dtype)


def _norm_mod(x, gain, mod, is_prompt, tm=256):
    m, d = x.shape
    row = _mod_row_fn(is_prompt, tm)
    return pl.pallas_call(
        _norm_mod_kernel,
        grid=(m // tm,),
        in_specs=[
            pl.BlockSpec((tm, d), lambda i: (i, 0)),
            pl.BlockSpec((1, d), lambda i: (0, 0)),
            pl.BlockSpec((None, 1, d), lambda i: (row(i), 0, 0)),
            pl.BlockSpec((None, 1, d), lambda i: (row(i), 0, 1)),
        ],
        out_specs=pl.BlockSpec((tm, d), lambda i: (i, 0)),
        out_shape=jax.ShapeDtypeStruct((m, d), BF16),
        compiler_params=_params(1),
        name="norm_mod",
    )(x, gain.reshape(1, d), mod, mod)


def _mm_kernel(a_ref, w_ref, *refs, n_casts):
    cast_in, o_ref, cast_out = refs[:n_casts], refs[n_casts], refs[n_casts + 1:]
    o_ref[...] = _dot(a_ref[...], w_ref[...]).astype(o_ref.dtype)
    for src, dst in zip(cast_in, cast_out):
        dst[...] = src[...].astype(dst.dtype)


def _matmul(a, w, w_tiles, out_dtype, cast_weights=()):
    m, k = a.shape
    tn = PROJ_TN
    tm = 1024 if out_dtype == BF16 else 512
    assert m % tm == 0
    w_tile = _tile_lookup(w_tiles)
    ncols = len(w_tiles) * tn
    n_i = m // tm
    n_steps = len(w_tiles) * n_i
    in_specs = [pl.BlockSpec((tm, k), lambda j, i: (i, 0)),
                pl.BlockSpec((k, tn), lambda j, i: (0, w_tile(j)))]
    out_specs = [pl.BlockSpec((tm, tn), lambda j, i: (i, j))]
    out_shape = [jax.ShapeDtypeStruct((m, ncols), out_dtype)]
    for cw in cast_weights:
        rows, cols = cw.shape
        chunk = CAST_CHUNK_ROWS
        n_chunks = rows // chunk
        assert rows % chunk == 0 and n_chunks <= n_steps
        spec = pl.BlockSpec((chunk, cols),
                            lambda j, i, n_chunks=n_chunks: (jnp.minimum(j * n_i + i, n_chunks - 1), 0))
        in_specs.append(spec)
        out_specs.append(spec)
        out_shape.append(jax.ShapeDtypeStruct((rows, cols), BF16))
    outs = pl.pallas_call(
        functools.partial(_mm_kernel, n_casts=len(cast_weights)),
        grid=(ncols // tn, n_i),
        in_specs=in_specs,
        out_specs=out_specs,
        out_shape=out_shape,
        compiler_params=pltpu.CompilerParams(
            dimension_semantics=("arbitrary", "arbitrary"),
            vmem_limit_bytes=VMEM_LIMIT_ROWS if cast_weights else VMEM_LIMIT),
        name="proj",
    )(a, w, *cast_weights)
    return outs if cast_weights else outs[0]


def _kv_proj_kernel(a_ref, w_ref, ka_ref, va_ref, kb_ref, vb_ref, *, kb_tile0, vb_tile0):
    j = pl.program_id(0)
    acc = _dot(a_ref[...], w_ref[...])

    @pl.when(j < kb_tile0)
    def _():
        ka_ref[...] = acc[:, :A_KV_WIDTH]
        va_ref[...] = acc[:, A_KV_WIDTH:]

    @pl.when((j >= kb_tile0) & (j < vb_tile0))
    def _():
        kb_ref[...] = acc

    @pl.when(j >= vb_tile0)
    def _():
        vb_ref[...] = acc


def _kv_projection(a, w, tm=512):
    m, k = a.shape
    tn = PROJ_TN
    assert COL_VA == COL_KA + A_KV_WIDTH and 2 * A_KV_WIDTH == tn
    w_tiles = _col_tiles(COL_KA, tn) + _col_tiles(COL_KB, B_WIDTH) + _col_tiles(COL_VB, B_WIDTH)
    w_tile = _tile_lookup(w_tiles)
    kb_tile0 = 1
    vb_tile0 = kb_tile0 + B_WIDTH // tn
    end_tile = len(w_tiles)
    n_i = m // tm
    last = n_i - 1

    def parked(j, i, t0, t1):
        row = jnp.where(j < t0, 0, jnp.where(j < t1, i, last))
        return row, jnp.clip(j - t0, 0, t1 - t0 - 1)

    return pl.pallas_call(
        functools.partial(_kv_proj_kernel, kb_tile0=kb_tile0, vb_tile0=vb_tile0),
        grid=(end_tile, n_i),
        in_specs=[pl.BlockSpec((tm, k), lambda j, i: (i, 0)),
                  pl.BlockSpec((k, tn), lambda j, i: (0, w_tile(j)))],
        out_specs=[pl.BlockSpec((tm, A_KV_WIDTH), lambda j, i: parked(j, i, 0, kb_tile0)),
                   pl.BlockSpec((tm, A_KV_WIDTH), lambda j, i: parked(j, i, 0, kb_tile0)),
                   pl.BlockSpec((tm, tn), lambda j, i: parked(j, i, kb_tile0, vb_tile0)),
                   pl.BlockSpec((tm, tn), lambda j, i: parked(j, i, vb_tile0, end_tile))],
        out_shape=[jax.ShapeDtypeStruct((m, A_KV_WIDTH), F32),
                   jax.ShapeDtypeStruct((m, A_KV_WIDTH), F32),
                   jax.ShapeDtypeStruct((m, B_WIDTH), F32),
                   jax.ShapeDtypeStruct((m, B_WIDTH), F32)],
        compiler_params=_params(2),
        name="kv_proj",
    )(a, w)


def _row_rsqrt(row_scr, n_tiles):
    ss = None
    for jj in range(n_tiles):
        t = row_scr[jj]
        part = jnp.sum(t * t, axis=-1, keepdims=True)
        ss = part if ss is None else ss + part
    width = n_tiles * row_scr.shape[2]
    return lax.rsqrt(ss / width + NORM_EPS)


def _dot_halves(a1_ref, a2_ref, w_ref):
    kh = a1_ref.shape[1]
    return _dot(a1_ref[...], w_ref[:kh, :]) + _dot(a2_ref[...], w_ref[kh:, :])


def _mm_res_normmod_kernel(a1_ref, a2_ref, w_ref, x_ref, g_ref, gain_ref, sh_ref, sc_ref,
                           x1_ref, h_ref, row_scr, *, n_tiles, tn):
    j = pl.program_id(1)
    x1 = x_ref[...] + g_ref[...] * _dot_halves(a1_ref, a2_ref, w_ref)
    x1_ref[...] = x1
    row_scr[j] = x1

    @pl.when(j == n_tiles - 1)
    def _():
        r = _row_rsqrt(row_scr, n_tiles)
        for jj in range(n_tiles):
            cols = slice(jj * tn, (jj + 1) * tn)
            y = row_scr[jj] * r * gain_ref[:, cols]
            h_ref[:, cols] = (y * (1.0 + sc_ref[:, cols]) + sh_ref[:, cols]).astype(h_ref.dtype)


def _mm_res_norm_kernel(a1_ref, a2_ref, w_ref, x_ref, g_ref, gain_ref, y_ref, *, n_tiles, tn):
    j = pl.program_id(1)
    x2 = x_ref[...] + g_ref[...] * _dot_halves(a1_ref, a2_ref, w_ref)
    for jj in range(n_tiles):
        @pl.when(j == jj)
        def _(jj=jj):
            y_ref[:, jj * tn:(jj + 1) * tn] = x2

    @pl.when(j == n_tiles - 1)
    def _():
        ss = None
        for jj in range(n_tiles):
            t = y_ref[:, jj * tn:(jj + 1) * tn]
            part = jnp.sum(t * t, axis=-1, keepdims=True)
            ss = part if ss is None else ss + part
        r = lax.rsqrt(ss / (n_tiles * tn) + NORM_EPS)
        for jj in range(n_tiles):
            cols = slice(jj * tn, (jj + 1) * tn)
            y_ref[:, cols] = y_ref[:, cols] * r * gain_ref[:, cols]


def _matmul_residual_norm(a_halves, w, x, mod, is_prompt, gain, next_mod=None, tn=1024, tm=512):
    (a1, c1), (a2, c2) = a_halves
    m = a1.shape[0]
    k, n = w.shape
    n_tiles = n // tn
    row = _mod_row_fn(is_prompt, tm)
    gate_block0 = 2 * n // tn
    in_specs = [pl.BlockSpec((tm, k // 2), lambda i, j: (i, c1)),
                pl.BlockSpec((tm, k // 2), lambda i, j: (i, c2)),
                pl.BlockSpec((k, tn), lambda i, j: (0, j)),
                pl.BlockSpec((tm, tn), lambda i, j: (i, j)),
                pl.BlockSpec((None, 1, tn), lambda i, j: (row(i), 0, gate_block0 + j)),
                pl.BlockSpec((1, n), lambda i, j: (0, 0))]
    args = [a1, a2, w, x, mod, gain.reshape(1, n)]
    scratch = [pltpu.VMEM((n_tiles, tm, tn), F32)]
    params = pltpu.CompilerParams(dimension_semantics=("arbitrary", "arbitrary"),
                                  vmem_limit_bytes=VMEM_LIMIT_ROWS)
    if next_mod is None:
        return pl.pallas_call(
            functools.partial(_mm_res_norm_kernel, n_tiles=n_tiles, tn=tn),
            grid=(m // tm, n_tiles),
            in_specs=in_specs,
            out_specs=pl.BlockSpec((tm, n), lambda i, j: (i, 0)),
            out_shape=jax.ShapeDtypeStruct((m, n), F32),
            compiler_params=params,
            name="proj_residual_norm",
        )(*args)
    in_specs += [pl.BlockSpec((None, 1, n), lambda i, j: (row(i), 0, 0)),
                 pl.BlockSpec((None, 1, n), lambda i, j: (row(i), 0, 1))]
    args += [next_mod, next_mod]
    return pl.pallas_call(
        functools.partial(_mm_res_normmod_kernel, n_tiles=n_tiles, tn=tn),
        grid=(m // tm, n_tiles),
        in_specs=in_specs,
        out_specs=[pl.BlockSpec((tm, tn), lambda i, j: (i, j)),
                   pl.BlockSpec((tm, n), lambda i, j: (i, 0))],
        out_shape=[jax.ShapeDtypeStruct((m, n), F32), jax.ShapeDtypeStruct((m, n), BF16)],
        scratch_shapes=scratch,
        compiler_params=params,
        name="proj_residual_normmod",
    )(*args)


def _chunked_softmax(n_rows, chunk, loads, stores, sink2=None):
    inv = []
    for c in range(n_rows // chunk):
        rows = slice(c * chunk, (c + 1) * chunk)
        ts = [load(rows) for load in loads]
        m = functools.reduce(jnp.maximum, [jnp.max(t, axis=-1, keepdims=True) for t in ts])
        if sink2 is not None:
            m = jnp.maximum(m, sink2[rows])
        es = [jnp.exp2(t - m) for t in ts]
        l = functools.reduce(jnp.add, [jnp.sum(e, axis=-1, keepdims=True) for e in es])
        if sink2 is not None:
            l = l + jnp.exp2(sink2[rows] - m)
        for store, e in zip(stores, es):
            store(rows, e.astype(BF16))
        inv.append(1.0 / l)
    return jnp.concatenate(inv, axis=0)


def _ctx_attn_kernel(sink_ref, qg_ref, ka_ref, va_ref, kb_ref, vb_ref, o_ref, s_scr, e_scr):
    n = SEQ
    gate0 = MIX_WIDTH
    n_stack = A_GROUPS

    def softmax(sink2):
        def store(rows, e):
            e_scr[rows, :] = e
        return _chunked_softmax(n_stack * n, CTX_CHUNK, [lambda rows: s_scr[rows, :] * LOGIT_SCALE],
                                [store], sink2)

    def emit(o, g, out_col):
        cols = slice(out_col, out_col + HEAD_DIM)
        gate = qg_ref[:, gate0 + out_col:gate0 + out_col + HEAD_DIM].astype(F32)
        o_ref[:, cols] = (o[g * n:(g + 1) * n] * _silu(gate)).astype(o_ref.dtype)

    for kv in range(A_KV_HEADS):
        cols = slice(kv * HEAD_DIM, (kv + 1) * HEAD_DIM)
        heads = [kv * A_GROUPS + g for g in range(A_GROUPS)]
        q = jnp.concatenate(
            [qg_ref[:, h * HEAD_DIM:(h + 1) * HEAD_DIM] for h in heads], axis=0)
        sink2 = jnp.concatenate(
            [jnp.full((n, 1), sink_ref[h] * LOG2E, F32) for h in heads], axis=0)
        s_scr[...] = _dot_nt(q, ka_ref[:, cols].astype(BF16))
        inv = softmax(sink2)
        o = _dot(e_scr[...], va_ref[:, cols].astype(BF16)) * inv
        for g, h in enumerate(heads):
            emit(o, g, h * HEAD_DIM)
    for h0 in range(0, B_HEADS, n_stack):
        heads = range(h0, h0 + n_stack)
        for g, h in enumerate(heads):
            cols = slice(h * HEAD_DIM, (h + 1) * HEAD_DIM)
            q = qg_ref[:, A_WIDTH + h * HEAD_DIM:A_WIDTH + (h + 1) * HEAD_DIM]
            s_scr[g * n:(g + 1) * n, :] = _dot_nt(q, kb_ref[:, cols].astype(BF16))
        inv = softmax(None)
        o = jnp.concatenate(
            [_dot(e_scr[g * n:(g + 1) * n, :],
                  vb_ref[:, h * HEAD_DIM:(h + 1) * HEAD_DIM].astype(BF16))
             for g, h in enumerate(heads)], axis=0) * inv
        for g, h in enumerate(heads):
            emit(o, g, A_WIDTH + h * HEAD_DIM)


def _ctx_attention(sink, qg, ka, va, kb, vb):
    stack_rows = A_GROUPS * SEQ
    return pl.pallas_call(
        _ctx_attn_kernel,
        grid=(BATCH,),
        in_specs=[
            pl.BlockSpec(memory_space=pltpu.SMEM),
            pl.BlockSpec((SEQ, 2 * MIX_WIDTH), lambda b: (b, 0)),
            pl.BlockSpec((SEQ, A_KV_WIDTH), lambda b: (b, 0)),
            pl.BlockSpec((SEQ, A_KV_WIDTH), lambda b: (b, 0)),
            pl.BlockSpec((SEQ, B_WIDTH), lambda b: (b, 0)),
            pl.BlockSpec((SEQ, B_WIDTH), lambda b: (b, 0)),
        ],
        out_specs=pl.BlockSpec((SEQ, MIX_WIDTH), lambda b: (b, 0)),
        out_shape=jax.ShapeDtypeStruct((N_PROMPT, MIX_WIDTH), BF16),
        scratch_shapes=[pltpu.VMEM((stack_rows, SEQ), F32),
                        pltpu.VMEM((stack_rows, SEQ), BF16)],
        compiler_params=_params(1),
        name="ctx_attention",
    )(sink, qg, ka, va, kb, vb)


def _split_cache_heads(c_ref, scr, n_heads):
    for h in range(n_heads):
        scr[h] = c_ref[pl.ds(h, PAST_LEN, stride=n_heads), :].astype(scr.dtype)


def _win_attn_kernel(sink_ref, q_ref, k_ref, v_ref, ck_ref, cv_ref, gate_ref,
                     cos_ref, sin_ref, o_ref, qs_ref, ks_ref, ck_scr, cv_scr,
                     band_scr, sw_scr, sc_scr, ew_scr, ec_scr):
    kv = pl.program_id(1)
    n = DEC_SEQ

    @pl.when(kv == 0)
    def _():
        _split_cache_heads(ck_ref, ck_scr, A_KV_HEADS)
        _split_cache_heads(cv_ref, cv_scr, A_KV_HEADS)

    cos = cos_ref[...]
    sin = sin_ref[...]
    lane = lax.broadcasted_iota(jnp.int32, (n, HEAD_DIM), 1)
    first_quarter = (lane % (HEAD_DIM // 2)) < (HEAD_DIM // 4)

    def rope(x):
        rot = jnp.where(first_quarter,
                        pltpu.roll(x, HEAD_DIM - HEAD_DIM // 4, 1),
                        pltpu.roll(x, HEAD_DIM // 4, 1))
        return x * cos + rot * sin

    ks_ref[...] = rope(k_ref[...].astype(F32)).astype(BF16)
    for g in range(A_GROUPS):
        qs_ref[g] = (rope(q_ref[:, g * HEAD_DIM:(g + 1) * HEAD_DIM].astype(F32))
                     * LOGIT_SCALE).astype(BF16)

    ck = ck_scr[kv]
    cv = cv_scr[kv]
    rows = A_GROUPS * A_BLOCK
    sink2 = jnp.concatenate(
        [jnp.full((A_BLOCK, 1), sink_ref[kv * A_GROUPS + g] * LOG2E, F32)
         for g in range(A_GROUPS)], axis=0)
    qi = lax.broadcasted_iota(jnp.int32, (rows, 3 * A_BLOCK), 0) % A_BLOCK
    rel = lax.broadcasted_iota(jnp.int32, (rows, 3 * A_BLOCK), 1) - A_BLOCK - qi
    band_scr[...] = jnp.where(jnp.abs(rel) <= A_HALF_WIN, 0.0, NEG_INF)
    for blk in range(n // A_BLOCK):
        r0 = blk * A_BLOCK
        lo = max(0, r0 - A_BLOCK)
        hi = min(n, r0 + 2 * A_BLOCK)
        span = hi - lo
        b0 = lo - (r0 - A_BLOCK)
        q = jnp.concatenate([qs_ref[g, r0:r0 + A_BLOCK, :] for g in range(A_GROUPS)], axis=0)
        sw_scr[:, :span] = _dot_nt(q, ks_ref[lo:hi, :])
        sc_scr[...] = _dot_nt(q, ck)

        def load_w(r, span=span, b0=b0):
            return sw_scr[r, :span] + band_scr[r, b0:b0 + span]

        def store_w(r, e, span=span):
            ew_scr[r, :span] = e

        def store_c(r, e):
            ec_scr[r, :] = e

        inv = _chunked_softmax(rows, WIN_CHUNK, [load_w, lambda r: sc_scr[r, :]],
                               [store_w, store_c], sink2)
        o = (_dot(ew_scr[:, :span], v_ref[lo:hi, :]) + _dot(ec_scr[...], cv)) * inv
        for g in range(A_GROUPS):
            hc = slice(g * HEAD_DIM, (g + 1) * HEAD_DIM)
            gate = gate_ref[r0:r0 + A_BLOCK, hc].astype(F32)
            o_ref[r0:r0 + A_BLOCK, hc] = (
                o[g * A_BLOCK:(g + 1) * A_BLOCK] * _silu(gate)).astype(o_ref.dtype)


def _win_attention(sink, proj, cache_k, cache_v, cos, sin_signed):
    gw = A_GROUPS * HEAD_DIM
    stack_rows = A_GROUPS * A_BLOCK
    n = DEC_SEQ
    cache_spec = pl.BlockSpec((None, PAST_LEN * A_KV_HEADS, HEAD_DIM), lambda b, k: (b, 0, 0))
    return pl.pallas_call(
        _win_attn_kernel,
        grid=(DEC_BATCH, A_KV_HEADS),
        in_specs=[
            pl.BlockSpec(memory_space=pltpu.SMEM),
            pl.BlockSpec((n, gw), lambda b, k: (b, COL_QA // gw + k)),
            pl.BlockSpec((n, HEAD_DIM), lambda b, k: (b, COL_KA // HEAD_DIM + k)),
            pl.BlockSpec((n, HEAD_DIM), lambda b, k: (b, COL_VA // HEAD_DIM + k)),
            cache_spec,
            cache_spec,
            pl.BlockSpec((n, gw), lambda b, k: (b, COL_GATE // gw + k)),
            pl.BlockSpec((n, HEAD_DIM), lambda b, k: (0, 0)),
            pl.BlockSpec((n, HEAD_DIM), lambda b, k: (0, 0)),
        ],
        out_specs=pl.BlockSpec((n, gw), lambda b, k: (b, k)),
        out_shape=jax.ShapeDtypeStruct((N_SAMPLE, A_WIDTH), BF16),
        scratch_shapes=[pltpu.VMEM((A_GROUPS, n, HEAD_DIM), BF16),
                        pltpu.VMEM((n, HEAD_DIM), BF16),
                        pltpu.VMEM((A_KV_HEADS, PAST_LEN, HEAD_DIM), BF16),
                        pltpu.VMEM((A_KV_HEADS, PAST_LEN, HEAD_DIM), BF16),
                        pltpu.VMEM((stack_rows, 3 * A_BLOCK), F32),
                        pltpu.VMEM((stack_rows, 3 * A_BLOCK), F32),
                        pltpu.VMEM((stack_rows, PAST_LEN), F32),
                        pltpu.VMEM((stack_rows, 3 * A_BLOCK), BF16),
                        pltpu.VMEM((stack_rows, PAST_LEN), BF16)],
        compiler_params=_params(2),
        name="win_attention",
    )(sink, proj, proj, proj, cache_k, cache_v, proj, cos, sin_signed)


def _bias_kernel(rpb_ref, o_ref):
    h = pl.program_id(0)
    shape = (GRID_W, 2 * GRID_W)
    c = lax.broadcasted_iota(jnp.int32, shape, 0)
    j2 = lax.broadcasted_iota(jnp.int32, shape, 1)
    kc = j2 % GRID_W
    second = j2 >= GRID_W
    col_start = jnp.clip(c - B_WIN_COLS // 2, 0, GRID_W - B_WIN_COLS)
    ok = (kc >= col_start) & (kc < col_start + B_WIN_COLS)
    dc = kc - c + B_WIN_COLS - 1
    base = h * (RPB_ROWS * RPB_COLS)
    pair_scr = []
    for i in range(N_BIAS_PAIRS):
        acc = jnp.full(shape, NEG_INF, F32)
        for d in range(RPB_COLS):
            val = jnp.where(second, rpb_ref[base + (i + 1) * RPB_COLS + d],
                            rpb_ref[base + i * RPB_COLS + d]) * LOG2E
            acc = jnp.where(ok & (dc == d), val, acc)
        pair_scr.append(acc)
    for r in range(GRID_ROWS):
        dr0 = _na_key_row0(r) - r + B_WIN_ROWS - 1
        for i in range(B_WIN_ROWS // 2):
            o_ref[r * GRID_W:(r + 1) * GRID_W, i * 2 * GRID_W:(i + 1) * 2 * GRID_W] = (
                pair_scr[dr0 + 2 * i])


def _expand_bias(rpb):
    kw = B_WIN_ROWS * GRID_W
    return pl.pallas_call(
        _bias_kernel,
        grid=(B_HEADS,),
        in_specs=[pl.BlockSpec(memory_space=pltpu.SMEM)],
        out_specs=pl.BlockSpec((None, DEC_SEQ, kw), lambda h: (h, 0, 0)),
        out_shape=jax.ShapeDtypeStruct((B_HEADS, DEC_SEQ, kw), F32),
        compiler_params=_params(1),
        name="expand_bias",
    )(rpb.reshape(-1))


def _na_key_row0(r):
    return min(max(r - B_WIN_ROWS // 2, 0), GRID_ROWS - B_WIN_ROWS)


def _na_row_groups():
    groups = []
    for r in range(GRID_ROWS):
        rs = _na_key_row0(r)
        if groups and groups[-1][2] == rs:
            groups[-1] = (groups[-1][0], r + 1, rs)
        else:
            groups.append((r, r + 1, rs))
    return groups


def _na_attn_kernel(q_ref, k_ref, v_ref, ck_ref, cv_ref, gate_ref, bias_ref, o_ref,
                    ck_scr, cv_scr, sn_scr, sc_scr, en_scr, ec_scr):
    kw = B_WIN_ROWS * GRID_W
    h = pl.program_id(1)

    @pl.when(h == 0)
    def _():
        _split_cache_heads(ck_ref, ck_scr, B_HEADS)
        _split_cache_heads(cv_ref, cv_scr, B_HEADS)

    groups = _na_row_groups()
    sc_scr[...] = _dot_nt(q_ref[...], ck_scr[h])
    for r0, r1, rs in groups:
        sn_scr[r0 * GRID_W:r1 * GRID_W, :] = _dot_nt(
            q_ref[r0 * GRID_W:r1 * GRID_W, :], k_ref[rs * GRID_W:rs * GRID_W + kw, :])

    def store_n(r, e):
        en_scr[r, :] = e

    def store_c(r, e):
        ec_scr[r, :] = e

    inv = _chunked_softmax(
        DEC_SEQ, NA_CHUNK,
        [lambda r: sn_scr[r, :] * LOGIT_SCALE + bias_ref[r, :], lambda r: sc_scr[r, :] * LOGIT_SCALE],
        [store_n, store_c])
    o_n = jnp.concatenate(
        [_dot(en_scr[r0 * GRID_W:r1 * GRID_W, :], v_ref[rs * GRID_W:rs * GRID_W + kw, :])
         for r0, r1, rs in groups], axis=0)
    o = (o_n + _dot(ec_scr[...], cv_scr[h])) * inv
    o_ref[...] = (o * _silu(gate_ref[...].astype(F32))).astype(o_ref.dtype)


def _na_attention(proj, cache_k, cache_v, bias):
    n = DEC_SEQ
    hd = HEAD_DIM
    cache_spec = pl.BlockSpec((None, PAST_LEN * B_HEADS, hd), lambda b, h: (b, 0, 0))
    return pl.pallas_call(
        _na_attn_kernel,
        grid=(DEC_BATCH, B_HEADS),
        in_specs=[
            pl.BlockSpec((n, hd), lambda b, h: (b, COL_QB // hd + h)),
            pl.BlockSpec((n, hd), lambda b, h: (b, COL_KB // hd + h)),
            pl.BlockSpec((n, hd), lambda b, h: (b, COL_VB // hd + h)),
            cache_spec,
            cache_spec,
            pl.BlockSpec((n, hd), lambda b, h: (b, (COL_GATE + A_WIDTH) // hd + h)),
            pl.BlockSpec((None, n, B_WIN_ROWS * GRID_W), lambda b, h: (h, 0, 0)),
        ],
        out_specs=pl.BlockSpec((n, hd), lambda b, h: (b, h)),
        out_shape=jax.ShapeDtypeStruct((N_SAMPLE, B_WIDTH), BF16),
        scratch_shapes=[pltpu.VMEM((B_HEADS, PAST_LEN, hd), BF16),
                        pltpu.VMEM((B_HEADS, PAST_LEN, hd), BF16),
                        pltpu.VMEM((n, B_WIN_ROWS * GRID_W), F32),
                        pltpu.VMEM((n, PAST_LEN), F32),
                        pltpu.VMEM((n, B_WIN_ROWS * GRID_W), BF16),
                        pltpu.VMEM((n, PAST_LEN), BF16)],
        compiler_params=_params(2),
        name="na_attention",
    )(proj, proj, proj, cache_k, cache_v, proj, bias)


def _pool_kernel(u_ref, gate_ref, band_ref, inv_count_ref, w_ref, scale_ref, o_ref, *, seq, rows):
    cb = POOL_BAND_BLOCK
    parts = []
    for i in range(rows // cb):
        lo = i * cb if seq <= cb else max(0, (i - 1) * cb)
        hi = (i + 1) * cb if seq <= cb else min(rows, (i + 2) * cb)
        parts.append(_dot(band_ref[i * cb:(i + 1) * cb, lo:hi], u_ref[lo:hi, :]))
    wsum = jnp.concatenate(parts, axis=0)
    pooled = wsum * inv_count_ref[...] - u_ref[...].astype(F32)
    y = _dot(pooled.astype(BF16), w_ref[...]) * scale_ref[...]
    o_ref[...] = (y * _silu(gate_ref[...].astype(F32))).astype(o_ref.dtype)


def _pool_operators(seq, rows):
    t = jnp.arange(rows)[:, None]
    j = jnp.arange(rows)[None, :]
    same_seq = (t // seq) == (j // seq)
    pos = t % seq
    bands, inv_counts = [], []
    for window in POOL_WINDOWS:
        half = window // 2
        assert half <= POOL_BAND_BLOCK
        bands.append(((j - t >= -half) & (j - t < half) & same_seq).astype(BF16))
        inv_counts.append(1.0 / (jnp.minimum(pos + half, seq) - jnp.maximum(pos - half, 0)).astype(F32))
    return jnp.stack(bands), jnp.stack(inv_counts)


def _pool_mixer(ug, w_grp, scale, seq, rows=1024):
    m = ug.shape[0]
    n_groups = len(POOL_WINDOWS)
    pg = POOL_GROUP
    band, inv_count = _pool_operators(seq, rows)
    return pl.pallas_call(
        functools.partial(_pool_kernel, seq=seq, rows=rows),
        grid=(n_groups, m // rows),
        in_specs=[
            pl.BlockSpec((rows, pg), lambda g, i: (i, g)),
            pl.BlockSpec((rows, pg), lambda g, i: (i, n_groups + g)),
            pl.BlockSpec((None, rows, rows), lambda g, i: (g, 0, 0)),
            pl.BlockSpec((None, rows, 1), lambda g, i: (g, 0, 0)),
            pl.BlockSpec((None, pg, pg), lambda g, i: (g, 0, 0)),
            pl.BlockSpec((1, pg), lambda g, i: (0, g)),
        ],
        out_specs=pl.BlockSpec((rows, pg), lambda g, i: (i, g)),
        out_shape=jax.ShapeDtypeStruct((m, n_groups * pg), BF16),
        compiler_params=_params(2),
        name="pool_mixer",
    )(ug, ug, band, inv_count, w_grp, scale.reshape(1, -1))


def _rope_tables():
    t = jnp.arange(DEC_SEQ)
    quarter = HEAD_DIM // 4
    inv_freq = ROPE_BASE ** (-jnp.arange(quarter, dtype=F32) / quarter)
    ang_r = (t // GRID_W).astype(F32)[:, None] * inv_freq
    ang_c = (t % GRID_W).astype(F32)[:, None] * inv_freq
    ang = jnp.concatenate([ang_r, ang_r, ang_c, ang_c], axis=-1)
    sign = jnp.tile(jnp.concatenate([-jnp.ones((quarter,), F32), jnp.ones((quarter,), F32)]), 2)
    return jnp.cos(ang), jnp.sin(ang) * sign


def kernel(x_prompt, x_sample, c, cache_a_k, cache_a_v, cache_b_k, cache_b_v, c_ctx,
           w_ada, b_ada, norm_g, w_in_attn, a_sink, b_rpb, w_out_attn,
           w_in_pool, w_grp_pool, pool_scale, w_out_pool, final_g):
    d = D_MODEL
    xp = x_prompt.reshape(N_PROMPT, d)
    xs = x_sample.reshape(N_SAMPLE, d)

    w_in0 = w_in_attn[0].astype(BF16)

    cond = jnp.zeros((MOD_ROWS, d), F32).at[:DEC_BATCH].set(c).at[CTX_MOD_ROW].set(c_ctx)
    mod = _ada(cond, w_ada, b_ada)
    mod0 = mod[0].reshape(MOD_ROWS, 1, 3 * d)
    mod1 = mod[1].reshape(MOD_ROWS, 1, 3 * d)

    hp = _norm_mod(xp, norm_g[0], mod0, True)
    hs = _norm_mod(xs, norm_g[0], mod0, False)

    ka, va, kb, vb = _kv_projection(hp, w_in0)
    qg_tiles = (_col_tiles(COL_QA, A_WIDTH) + _col_tiles(COL_QB, B_WIDTH)
                + _col_tiles(COL_GATE, MIX_WIDTH))
    qg_p, w_in1 = _matmul(hp, w_in0, qg_tiles, BF16, cast_weights=(w_in_pool[0],))
    n_groups = len(POOL_WINDOWS)
    proj_s, w_out0, w_out1, w_grp = _matmul(
        hs, w_in0, _col_tiles(0, ATTN_IN_WIDTH), BF16,
        cast_weights=(w_out_attn[0], w_out_pool[0],
                      w_grp_pool[0].reshape(n_groups * POOL_GROUP, POOL_GROUP)))
    w_grp = w_grp.reshape(n_groups, POOL_GROUP, POOL_GROUP)

    sink = a_sink[0]
    og_p = _ctx_attention(sink, qg_p, ka, va, kb, vb)

    cos, sin_signed = _rope_tables()
    flat = lambda cache: cache.reshape(DEC_BATCH, -1, HEAD_DIM)
    oa_s = _win_attention(sink, proj_s, flat(cache_a_k), flat(cache_a_v), cos, sin_signed)
    ob_s = _na_attention(proj_s, flat(cache_b_k), flat(cache_b_v), _expand_bias(b_rpb[0]))

    xp1, hp1 = _matmul_residual_norm([(og_p, 0), (og_p, 1)], w_out0, xp, mod0, True,
                                     norm_g[1], mod1)
    xs1, hs1 = _matmul_residual_norm([(oa_s, 0), (ob_s, 0)], w_out0, xs, mod0, False,
                                     norm_g[1], mod1)

    ug_p = _matmul(hp1, w_in1, _col_tiles(0, 2 * d), BF16)
    ug_s = _matmul(hs1, w_in1, _col_tiles(0, 2 * d), BF16)
    y_p = _pool_mixer(ug_p, w_grp, pool_scale[0], SEQ)
    y_s = _pool_mixer(ug_s, w_grp, pool_scale[0], DEC_SEQ)
    y_prompt = _matmul_residual_norm([(y_p, 0), (y_p, 1)], w_out1, xp1, mod1, True,
                                     final_g).reshape(BATCH, SEQ, d)
    y_sample = _matmul_residual_norm([(y_s, 0), (y_s, 1)], w_out1, xs1, mod1, False,
                                     final_g).reshape(DEC_BATCH, DEC_SEQ, d)

    kv_shape_a = (BATCH, 1, SEQ, A_KV_HEADS, HEAD_DIM)
    kv_shape_b = (BATCH, 1, SEQ, B_HEADS, HEAD_DIM)
    return (y_prompt, y_sample, ka.reshape(kv_shape_a), va.reshape(kv_shape_a),
            kb.reshape(kv_shape_b), vb.reshape(kv_shape_b))
```

```python
import functools

import jax
import jax.numpy as jnp
from jax import lax
from jax.experimental import pallas as pl
from jax.experimental.pallas import tpu as pltpu

F32 = jnp.float32
BF16 = jnp.bfloat16

D_MODEL = 4096
BATCH = 32
SEQ = 256
DEC_BATCH = 8
DEC_SEQ = 1024
PAST_LEN = 512
GRID_W = 64
HEAD_DIM = 128
A_Q_HEADS = 16
A_KV_HEADS = 4
A_GROUPS = 4
A_HALF_WIN = 128
A_BLOCK = 128
B_HEADS = 16
B_WIN_ROWS = 8
B_WIN_COLS = 16
A_WIDTH = A_Q_HEADS * HEAD_DIM
A_KV_WIDTH = A_KV_HEADS * HEAD_DIM
B_WIDTH = B_HEADS * HEAD_DIM
MIX_WIDTH = A_WIDTH + B_WIDTH
POOL_WINDOWS = (2, 4, 8, 16)
POOL_GROUP = 1024
POOL_BAND_BLOCK = 256
CAST_CHUNK_ROWS = 64
ROPE_BASE = 10000.0
NORM_EPS = 1e-6
NEG_INF = -1e30
ATTN_SCALE = HEAD_DIM ** -0.5
LOG2E = 1.4426950408889634
LOGIT_SCALE = ATTN_SCALE * LOG2E
CTX_CHUNK = 1024
WIN_CHUNK = 512
NA_CHUNK = 1024

N_PROMPT = BATCH * SEQ
N_SAMPLE = DEC_BATCH * DEC_SEQ
MOD_ROWS = 16
CTX_MOD_ROW = DEC_BATCH
GRID_ROWS = DEC_SEQ // GRID_W
N_BIAS_PAIRS = 2 * B_WIN_ROWS - 2
RPB_ROWS = 2 * B_WIN_ROWS - 1
RPB_COLS = 2 * B_WIN_COLS - 1

COL_QA = 0
COL_KA = COL_QA + A_WIDTH
COL_VA = COL_KA + A_KV_WIDTH
COL_QB = COL_VA + A_KV_WIDTH
COL_KB = COL_QB + B_WIDTH
COL_VB = COL_KB + B_WIDTH
COL_GATE = COL_VB + B_WIDTH
ATTN_IN_WIDTH = COL_GATE + MIX_WIDTH
PROJ_TN = 1024


def _col_tiles(col0, width):
    assert col0 % PROJ_TN == 0 and width % PROJ_TN == 0
    return tuple(range(col0 // PROJ_TN, (col0 + width) // PROJ_TN))


def _tile_lookup(tiles):
    runs = []
    for jj, t in enumerate(tiles):
        if not runs or runs[-1][1] != t - jj:
            runs.append((jj, t - jj))

    def lookup(j):
        off = runs[0][1]
        for start, o in runs[1:]:
            off = jnp.where(j >= start, o, off)
        return j + off
    return lookup

VMEM_LIMIT = 48 * 1024 * 1024
VMEM_LIMIT_ROWS = 56 * 1024 * 1024


def _params(n_grid_dims):
    return pltpu.CompilerParams(
        dimension_semantics=("arbitrary",) * n_grid_dims,
        vmem_limit_bytes=VMEM_LIMIT,
    )


def _silu(x):
    return x / (1.0 + jnp.exp(-x))


def _dot_nt(a, b):
    return lax.dot_general(a, b, (((1,), (1,)), ((), ())), preferred_element_type=F32)


def _dot(a, b):
    return jnp.dot(a, b, preferred_element_type=F32)


def _ada_kernel(cond_ref, w_ref, b_ref, o_ref):
    a = _silu(cond_ref[...]).astype(BF16)
    o_ref[...] = _dot(a, w_ref[...].astype(BF16)) + b_ref[...]


def _ada(cond, w_ada, b_ada, tn=512):
    depth, d, n = w_ada.shape
    return pl.pallas_call(
        _ada_kernel,
        grid=(depth, n // tn),
        in_specs=[
            pl.BlockSpec((MOD_ROWS, d), lambda l, j: (0, 0)),
            pl.BlockSpec((None, d, tn), lambda l, j: (l, 0, j)),
            pl.BlockSpec((None, 1, tn), lambda l, j: (l, 0, j)),
        ],
        out_specs=pl.BlockSpec((None, MOD_ROWS, tn), lambda l, j: (l, 0, j)),
        out_shape=jax.ShapeDtypeStruct((depth, MOD_ROWS, n), F32),
        compiler_params=_params(2),
        name="ada",
    )(cond, w_ada, b_ada.reshape(depth, 1, n))


def _mod_row_fn(is_prompt, tm):
    if is_prompt:
        return lambda i: CTX_MOD_ROW
    return lambda i: (i * tm) // DEC_SEQ


def _norm_mod_kernel(xp_ref, xs_ref, g_ref, sh_ref, sc_ref, hp_ref, hs_ref, *, n_prompt_tiles):
    def norm_mod(x_ref, o_ref):
        x = x_ref[...]
        ms = jnp.mean(x * x, axis=-1, keepdims=True)
        y = x * lax.rsqrt(ms + NORM_EPS) * g_ref[...]
        o_ref[...] = (y * (1.0 + sc_ref[...]) + sh_ref[...]).astype(o_ref.dtype)

    i = pl.program_id(0)

    @pl.when(i < n_prompt_tiles)
    def _():
        norm_mod(xp_ref, hp_ref)

    @pl.when(i >= n_prompt_tiles)
    def _():
        norm_mod(xs_ref, hs_ref)


def _norm_mod(xp, xs, gain, mod, tm=256):
    d = xp.shape[1]
    n_p, n_s = xp.shape[0] // tm, xs.shape[0] // tm
    p_blk = lambda i: (jnp.minimum(i, n_p - 1), 0)
    s_blk = lambda i: (jnp.maximum(i - n_p, 0), 0)
    row = lambda i: jnp.where(i < n_p, CTX_MOD_ROW, (jnp.maximum(i - n_p, 0) * tm) // DEC_SEQ)
    return pl.pallas_call(
        functools.partial(_norm_mod_kernel, n_prompt_tiles=n_p),
        grid=(n_p + n_s,),
        in_specs=[
            pl.BlockSpec((tm, d), p_blk),
            pl.BlockSpec((tm, d), s_blk),
            pl.BlockSpec((1, d), lambda i: (0, 0)),
            pl.BlockSpec((None, 1, d), lambda i: (row(i), 0, 0)),
            pl.BlockSpec((None, 1, d), lambda i: (row(i), 0, 1)),
        ],
        out_specs=[pl.BlockSpec((tm, d), p_blk), pl.BlockSpec((tm, d), s_blk)],
        out_shape=[jax.ShapeDtypeStruct(xp.shape, BF16), jax.ShapeDtypeStruct(xs.shape, BF16)],
        compiler_params=_params(1),
        name="norm_mod",
    )(xp, xs, gain.reshape(1, d), mod, mod)


def _mm_kernel(a_ref, w_ref, *refs, n_casts):
    cast_in, o_ref, cast_out = refs[:n_casts], refs[n_casts], refs[n_casts + 1:]
    o_ref[...] = _dot(a_ref[...], w_ref[...]).astype(o_ref.dtype)
    for src, dst in zip(cast_in, cast_out):
        dst[...] = src[...].astype(dst.dtype)


def _matmul(a, w, w_tiles, out_dtype, cast_weights=()):
    m, k = a.shape
    tn = PROJ_TN
    tm = 1024 if out_dtype == BF16 else 512
    assert m % tm == 0
    w_tile = _tile_lookup(w_tiles)
    ncols = len(w_tiles) * tn
    n_i = m // tm
    n_steps = len(w_tiles) * n_i
    in_specs = [pl.BlockSpec((tm, k), lambda j, i: (i, 0)),
                pl.BlockSpec((k, tn), lambda j, i: (0, w_tile(j)))]
    out_specs = [pl.BlockSpec((tm, tn), lambda j, i: (i, j))]
    out_shape = [jax.ShapeDtypeStruct((m, ncols), out_dtype)]
    for cw in cast_weights:
        rows, cols = cw.shape
        chunk = CAST_CHUNK_ROWS
        n_chunks = rows // chunk
        assert rows % chunk == 0 and n_chunks <= n_steps
        spec = pl.BlockSpec((chunk, cols),
                            lambda j, i, n_chunks=n_chunks: (jnp.minimum(j * n_i + i, n_chunks - 1), 0))
        in_specs.append(spec)
        out_specs.append(spec)
        out_shape.append(jax.ShapeDtypeStruct((rows, cols), BF16))
    outs = pl.pallas_call(
        functools.partial(_mm_kernel, n_casts=len(cast_weights)),
        grid=(ncols // tn, n_i),
        in_specs=in_specs,
        out_specs=out_specs,
        out_shape=out_shape,
        compiler_params=pltpu.CompilerParams(
            dimension_semantics=("arbitrary", "arbitrary"),
            vmem_limit_bytes=VMEM_LIMIT_ROWS if cast_weights else VMEM_LIMIT),
        name="proj",
    )(a, w, *cast_weights)
    return outs if cast_weights else outs[0]


def _kv_proj_kernel(a_ref, w_ref, ka_ref, va_ref, kb_ref, vb_ref, *, kb_tile0, vb_tile0):
    j = pl.program_id(0)
    acc = _dot(a_ref[...], w_ref[...])

    @pl.when(j < kb_tile0)
    def _():
        ka_ref[...] = acc[:, :A_KV_WIDTH]
        va_ref[...] = acc[:, A_KV_WIDTH:]

    @pl.when((j >= kb_tile0) & (j < vb_tile0))
    def _():
        kb_ref[...] = acc

    @pl.when(j >= vb_tile0)
    def _():
        vb_ref[...] = acc


def _kv_projection(a, w, tm=512):
    m, k = a.shape
    tn = PROJ_TN
    assert COL_VA == COL_KA + A_KV_WIDTH and 2 * A_KV_WIDTH == tn
    w_tiles = _col_tiles(COL_KA, tn) + _col_tiles(COL_KB, B_WIDTH) + _col_tiles(COL_VB, B_WIDTH)
    w_tile = _tile_lookup(w_tiles)
    kb_tile0 = 1
    vb_tile0 = kb_tile0 + B_WIDTH // tn
    end_tile = len(w_tiles)
    n_i = m // tm
    last = n_i - 1

    def parked(j, i, t0, t1):
        row = jnp.where(j < t0, 0, jnp.where(j < t1, i, last))
        return row, jnp.clip(j - t0, 0, t1 - t0 - 1)

    return pl.pallas_call(
        functools.partial(_kv_proj_kernel, kb_tile0=kb_tile0, vb_tile0=vb_tile0),
        grid=(end_tile, n_i),
        in_specs=[pl.BlockSpec((tm, k), lambda j, i: (i, 0)),
                  pl.BlockSpec((k, tn), lambda j, i: (0, w_tile(j)))],
        out_specs=[pl.BlockSpec((tm, A_KV_WIDTH), lambda j, i: parked(j, i, 0, kb_tile0)),
                   pl.BlockSpec((tm, A_KV_WIDTH), lambda j, i: parked(j, i, 0, kb_tile0)),
                   pl.BlockSpec((tm, tn), lambda j, i: parked(j, i, kb_tile0, vb_tile0)),
                   pl.BlockSpec((tm, tn), lambda j, i: parked(j, i, vb_tile0, end_tile))],
        out_shape=[jax.ShapeDtypeStruct((m, A_KV_WIDTH), F32),
                   jax.ShapeDtypeStruct((m, A_KV_WIDTH), F32),
                   jax.ShapeDtypeStruct((m, B_WIDTH), F32),
                   jax.ShapeDtypeStruct((m, B_WIDTH), F32)],
        compiler_params=_params(2),
        name="kv_proj",
    )(a, w)


def _dot_halves(a1_ref, a2_ref, w_ref):
    kh = a1_ref.shape[1]
    return _dot(a1_ref[...], w_ref[:kh, :]) + _dot(a2_ref[...], w_ref[kh:, :])


def _mm_res_prenorm_kernel(a1_ref, a2_ref, w_ref, x_ref, g_ref, gain_ref, sc_ref,
                           x1_ref, hu_ref, ss_ref):
    x1 = x_ref[...] + g_ref[...] * _dot_halves(a1_ref, a2_ref, w_ref)
    x1_ref[...] = x1
    hu_ref[...] = (x1 * (gain_ref[...] * (1.0 + sc_ref[...]))).astype(hu_ref.dtype)
    ss_ref[...] = jnp.sum(x1 * x1, axis=-1, keepdims=True)


def _matmul_residual_prenorm(a_halves, w, x, mod, is_prompt, gain, next_mod, tm=512):
    (a1, c1), (a2, c2) = a_halves
    m = a1.shape[0]
    k, n = w.shape
    tn = PROJ_TN
    n_tiles = n // tn
    row = _mod_row_fn(is_prompt, tm)
    scale_block0, gate_block0 = n_tiles, 2 * n_tiles
    tile = pl.BlockSpec((tm, tn), lambda j, i: (i, j))
    return pl.pallas_call(
        _mm_res_prenorm_kernel,
        grid=(n_tiles, m // tm),
        in_specs=[pl.BlockSpec((tm, k // 2), lambda j, i: (i, c1)),
                  pl.BlockSpec((tm, k // 2), lambda j, i: (i, c2)),
                  pl.BlockSpec((k, tn), lambda j, i: (0, j)),
                  tile,
                  pl.BlockSpec((None, 1, tn), lambda j, i: (row(i), 0, gate_block0 + j)),
                  pl.BlockSpec((1, tn), lambda j, i: (0, j)),
                  pl.BlockSpec((None, 1, tn), lambda j, i: (row(i), 0, scale_block0 + j))],
        out_specs=[tile, tile, pl.BlockSpec((None, tm, 1), lambda j, i: (j, i, 0))],
        out_shape=[jax.ShapeDtypeStruct((m, n), F32), jax.ShapeDtypeStruct((m, n), BF16),
                   jax.ShapeDtypeStruct((n_tiles, m, 1), F32)],
        compiler_params=_params(2),
        name="proj_residual_prenorm",
    )(a1, a2, w, x, mod, gain.reshape(1, n), next_mod)


def _shift_proj_kernel(sh_ref, w_ref, o_ref):
    o_ref[...] = _dot(sh_ref[...].astype(BF16), w_ref[...])


def _shift_projection(mod2d, w):
    k, n = w.shape
    tn = PROJ_TN
    return pl.pallas_call(
        _shift_proj_kernel,
        grid=(n // tn,),
        in_specs=[pl.BlockSpec((MOD_ROWS, k), lambda j: (0, 0)),
                  pl.BlockSpec((k, tn), lambda j: (0, j))],
        out_specs=pl.BlockSpec((MOD_ROWS, tn), lambda j: (0, j)),
        out_shape=jax.ShapeDtypeStruct((MOD_ROWS, n), F32),
        compiler_params=_params(1),
        name="shift_proj",
    )(mod2d, w)


def _mm_postnorm_kernel(a_ref, w_ref, ss_ref, shw_ref, o_ref, *, width):
    acc = _dot(a_ref[...], w_ref[...])
    ss = ss_ref[0]
    for t in range(1, ss_ref.shape[0]):
        ss = ss + ss_ref[t]
    r = lax.rsqrt(ss / width + NORM_EPS)
    o_ref[...] = (acc * r + shw_ref[...]).astype(o_ref.dtype)


def _matmul_postnorm(hu, ss, w, shw, is_prompt, out_dtype, tm=1024):
    m, k = hu.shape
    n = w.shape[1]
    tn = PROJ_TN
    row = _mod_row_fn(is_prompt, tm)
    return pl.pallas_call(
        functools.partial(_mm_postnorm_kernel, width=k),
        grid=(n // tn, m // tm),
        in_specs=[pl.BlockSpec((tm, k), lambda j, i: (i, 0)),
                  pl.BlockSpec((k, tn), lambda j, i: (0, j)),
                  pl.BlockSpec((ss.shape[0], tm, 1), lambda j, i: (0, i, 0)),
                  pl.BlockSpec((None, 1, tn), lambda j, i: (row(i), 0, j))],
        out_specs=pl.BlockSpec((tm, tn), lambda j, i: (i, j)),
        out_shape=jax.ShapeDtypeStruct((m, n), out_dtype),
        compiler_params=pltpu.CompilerParams(dimension_semantics=("arbitrary", "arbitrary"),
                                             vmem_limit_bytes=VMEM_LIMIT_ROWS),
        name="proj_postnorm",
    )(hu, w, ss, shw)


def _mm_res_norm_kernel(a1_ref, a2_ref, w_ref, x_ref, g_ref, gain_ref, y_ref, *, n_tiles, tn):
    j = pl.program_id(1)
    x2 = x_ref[...] + g_ref[...] * _dot_halves(a1_ref, a2_ref, w_ref)
    for jj in range(n_tiles):
        @pl.when(j == jj)
        def _(jj=jj):
            y_ref[:, jj * tn:(jj + 1) * tn] = x2

    @pl.when(j == n_tiles - 1)
    def _():
        ss = None
        for jj in range(n_tiles):
            t = y_ref[:, jj * tn:(jj + 1) * tn]
            part = jnp.sum(t * t, axis=-1, keepdims=True)
            ss = part if ss is None else ss + part
        r = lax.rsqrt(ss / (n_tiles * tn) + NORM_EPS)
        for jj in range(n_tiles):
            cols = slice(jj * tn, (jj + 1) * tn)
            y_ref[:, cols] = y_ref[:, cols] * r * gain_ref[:, cols]


def _matmul_residual_norm(a_halves, w, x, mod, is_prompt, gain, tm=512):
    (a1, c1), (a2, c2) = a_halves
    m = a1.shape[0]
    k, n = w.shape
    tn = PROJ_TN
    n_tiles = n // tn
    row = _mod_row_fn(is_prompt, tm)
    gate_block0 = 2 * n_tiles
    return pl.pallas_call(
        functools.partial(_mm_res_norm_kernel, n_tiles=n_tiles, tn=tn),
        grid=(m // tm, n_tiles),
        in_specs=[pl.BlockSpec((tm, k // 2), lambda i, j: (i, c1)),
                  pl.BlockSpec((tm, k // 2), lambda i, j: (i, c2)),
                  pl.BlockSpec((k, tn), lambda i, j: (0, j)),
                  pl.BlockSpec((tm, tn), lambda i, j: (i, j)),
                  pl.BlockSpec((None, 1, tn), lambda i, j: (row(i), 0, gate_block0 + j)),
                  pl.BlockSpec((1, n), lambda i, j: (0, 0))],
        out_specs=pl.BlockSpec((tm, n), lambda i, j: (i, 0)),
        out_shape=jax.ShapeDtypeStruct((m, n), F32),
        compiler_params=pltpu.CompilerParams(dimension_semantics=("arbitrary", "arbitrary"),
                                             vmem_limit_bytes=VMEM_LIMIT_ROWS),
        name="proj_residual_norm",
    )(a1, a2, w, x, mod, gain.reshape(1, n))


def _chunked_softmax(n_rows, chunk, loads, stores, sink2=None):
    inv = []
    for c in range(n_rows // chunk):
        rows = slice(c * chunk, (c + 1) * chunk)
        ts = [load(rows) for load in loads]
        m = functools.reduce(jnp.maximum, [jnp.max(t, axis=-1, keepdims=True) for t in ts])
        if sink2 is not None:
            m = jnp.maximum(m, sink2[rows])
        es = [jnp.exp2(t - m) for t in ts]
        l = functools.reduce(jnp.add, [jnp.sum(e, axis=-1, keepdims=True) for e in es])
        if sink2 is not None:
            l = l + jnp.exp2(sink2[rows] - m)
        for store, e in zip(stores, es):
            store(rows, e.astype(BF16))
        inv.append(1.0 / l)
    return jnp.concatenate(inv, axis=0)


def _ctx_attn_kernel(sink_ref, qg_ref, ka_ref, va_ref, kb_ref, vb_ref, o_ref, s_scr, e_scr):
    n = SEQ
    gate0 = MIX_WIDTH
    n_stack = A_GROUPS

    def softmax(sink2):
        def store(rows, e):
            e_scr[rows, :] = e
        return _chunked_softmax(n_stack * n, CTX_CHUNK, [lambda rows: s_scr[rows, :] * LOGIT_SCALE],
                                [store], sink2)

    def emit(o, g, out_col):
        cols = slice(out_col, out_col + HEAD_DIM)
        gate = qg_ref[:, gate0 + out_col:gate0 + out_col + HEAD_DIM].astype(F32)
        o_ref[:, cols] = (o[g * n:(g + 1) * n] * _silu(gate)).astype(o_ref.dtype)

    for kv in range(A_KV_HEADS):
        cols = slice(kv * HEAD_DIM, (kv + 1) * HEAD_DIM)
        heads = [kv * A_GROUPS + g for g in range(A_GROUPS)]
        q = jnp.concatenate(
            [qg_ref[:, h * HEAD_DIM:(h + 1) * HEAD_DIM] for h in heads], axis=0)
        sink2 = jnp.concatenate(
            [jnp.full((n, 1), sink_ref[h] * LOG2E, F32) for h in heads], axis=0)
        s_scr[...] = _dot_nt(q, ka_ref[:, cols].astype(BF16))
        inv = softmax(sink2)
        o = _dot(e_scr[...], va_ref[:, cols].astype(BF16)) * inv
        for g, h in enumerate(heads):
            emit(o, g, h * HEAD_DIM)
    for h0 in range(0, B_HEADS, n_stack):
        heads = range(h0, h0 + n_stack)
        for g, h in enumerate(heads):
            cols = slice(h * HEAD_DIM, (h + 1) * HEAD_DIM)
            q = qg_ref[:, A_WIDTH + h * HEAD_DIM:A_WIDTH + (h + 1) * HEAD_DIM]
            s_scr[g * n:(g + 1) * n, :] = _dot_nt(q, kb_ref[:, cols].astype(BF16))
        inv = softmax(None)
        o = jnp.concatenate(
            [_dot(e_scr[g * n:(g + 1) * n, :],
                  vb_ref[:, h * HEAD_DIM:(h + 1) * HEAD_DIM].astype(BF16))
             for g, h in enumerate(heads)], axis=0) * inv
        for g, h in enumerate(heads):
            emit(o, g, A_WIDTH + h * HEAD_DIM)


def _ctx_attention(sink, qg, ka, va, kb, vb):
    stack_rows = A_GROUPS * SEQ
    return pl.pallas_call(
        _ctx_attn_kernel,
        grid=(BATCH,),
        in_specs=[
            pl.BlockSpec(memory_space=pltpu.SMEM),
            pl.BlockSpec((SEQ, 2 * MIX_WIDTH), lambda b: (b, 0)),
            pl.BlockSpec((SEQ, A_KV_WIDTH), lambda b: (b, 0)),
            pl.BlockSpec((SEQ, A_KV_WIDTH), lambda b: (b, 0)),
            pl.BlockSpec((SEQ, B_WIDTH), lambda b: (b, 0)),
            pl.BlockSpec((SEQ, B_WIDTH), lambda b: (b, 0)),
        ],
        out_specs=pl.BlockSpec((SEQ, MIX_WIDTH), lambda b: (b, 0)),
        out_shape=jax.ShapeDtypeStruct((N_PROMPT, MIX_WIDTH), BF16),
        scratch_shapes=[pltpu.VMEM((stack_rows, SEQ), F32),
                        pltpu.VMEM((stack_rows, SEQ), BF16)],
        compiler_params=_params(1),
        name="ctx_attention",
    )(sink, qg, ka, va, kb, vb)


def _split_cache_heads(c_ref, scr, n_heads):
    for h in range(n_heads):
        scr[h] = c_ref[pl.ds(h, PAST_LEN, stride=n_heads), :].astype(scr.dtype)


def _win_attn_kernel(sink_ref, q_ref, k_ref, v_ref, ck_ref, cv_ref, gate_ref,
                     cos_ref, sin_ref, o_ref, qs_ref, ks_ref, ck_scr, cv_scr,
                     band_scr, sw_scr, sc_scr, ew_scr, ec_scr):
    kv = pl.program_id(1)
    n = DEC_SEQ

    @pl.when(kv == 0)
    def _():
        _split_cache_heads(ck_ref, ck_scr, A_KV_HEADS)
        _split_cache_heads(cv_ref, cv_scr, A_KV_HEADS)

    cos = cos_ref[...]
    sin = sin_ref[...]
    lane = lax.broadcasted_iota(jnp.int32, (n, HEAD_DIM), 1)
    first_quarter = (lane % (HEAD_DIM // 2)) < (HEAD_DIM // 4)

    def rope(x):
        rot = jnp.where(first_quarter,
                        pltpu.roll(x, HEAD_DIM - HEAD_DIM // 4, 1),
                        pltpu.roll(x, HEAD_DIM // 4, 1))
        return x * cos + rot * sin

    ks_ref[...] = rope(k_ref[...].astype(F32)).astype(BF16)
    for g in range(A_GROUPS):
        qs_ref[g] = (rope(q_ref[:, g * HEAD_DIM:(g + 1) * HEAD_DIM].astype(F32))
                     * LOGIT_SCALE).astype(BF16)

    ck = ck_scr[kv]
    cv = cv_scr[kv]
    rows = A_GROUPS * A_BLOCK
    sink2 = jnp.concatenate(
        [jnp.full((A_BLOCK, 1), sink_ref[kv * A_GROUPS + g] * LOG2E, F32)
         for g in range(A_GROUPS)], axis=0)
    qi = lax.broadcasted_iota(jnp.int32, (rows, 3 * A_BLOCK), 0) % A_BLOCK
    rel = lax.broadcasted_iota(jnp.int32, (rows, 3 * A_BLOCK), 1) - A_BLOCK - qi
    band_scr[...] = jnp.where(jnp.abs(rel) <= A_HALF_WIN, 0.0, NEG_INF)
    for blk in range(n // A_BLOCK):
        r0 = blk * A_BLOCK
        lo = max(0, r0 - A_BLOCK)
        hi = min(n, r0 + 2 * A_BLOCK)
        span = hi - lo
        b0 = lo - (r0 - A_BLOCK)
        q = jnp.concatenate([qs_ref[g, r0:r0 + A_BLOCK, :] for g in range(A_GROUPS)], axis=0)
        sw_scr[:, :span] = _dot_nt(q, ks_ref[lo:hi, :])
        sc_scr[...] = _dot_nt(q, ck)

        def load_w(r, span=span, b0=b0):
            return sw_scr[r, :span] + band_scr[r, b0:b0 + span]

        def store_w(r, e, span=span):
            ew_scr[r, :span] = e

        def store_c(r, e):
            ec_scr[r, :] = e

        inv = _chunked_softmax(rows, WIN_CHUNK, [load_w, lambda r: sc_scr[r, :]],
                               [store_w, store_c], sink2)
        o = (_dot(ew_scr[:, :span], v_ref[lo:hi, :]) + _dot(ec_scr[...], cv)) * inv
        for g in range(A_GROUPS):
            hc = slice(g * HEAD_DIM, (g + 1) * HEAD_DIM)
            gate = gate_ref[r0:r0 + A_BLOCK, hc].astype(F32)
            o_ref[r0:r0 + A_BLOCK, hc] = (
                o[g * A_BLOCK:(g + 1) * A_BLOCK] * _silu(gate)).astype(o_ref.dtype)


def _win_attention(sink, proj, cache_k, cache_v, cos, sin_signed):
    gw = A_GROUPS * HEAD_DIM
    stack_rows = A_GROUPS * A_BLOCK
    n = DEC_SEQ
    cache_spec = pl.BlockSpec((None, PAST_LEN * A_KV_HEADS, HEAD_DIM), lambda b, k: (b, 0, 0))
    return pl.pallas_call(
        _win_attn_kernel,
        grid=(DEC_BATCH, A_KV_HEADS),
        in_specs=[
            pl.BlockSpec(memory_space=pltpu.SMEM),
            pl.BlockSpec((n, gw), lambda b, k: (b, COL_QA // gw + k)),
            pl.BlockSpec((n, HEAD_DIM), lambda b, k: (b, COL_KA // HEAD_DIM + k)),
            pl.BlockSpec((n, HEAD_DIM), lambda b, k: (b, COL_VA // HEAD_DIM + k)),
            cache_spec,
            cache_spec,
            pl.BlockSpec((n, gw), lambda b, k: (b, COL_GATE // gw + k)),
            pl.BlockSpec((n, HEAD_DIM), lambda b, k: (0, 0)),
            pl.BlockSpec((n, HEAD_DIM), lambda b, k: (0, 0)),
        ],
        out_specs=pl.BlockSpec((n, gw), lambda b, k: (b, k)),
        out_shape=jax.ShapeDtypeStruct((N_SAMPLE, A_WIDTH), BF16),
        scratch_shapes=[pltpu.VMEM((A_GROUPS, n, HEAD_DIM), BF16),
                        pltpu.VMEM((n, HEAD_DIM), BF16),
                        pltpu.VMEM((A_KV_HEADS, PAST_LEN, HEAD_DIM), BF16),
                        pltpu.VMEM((A_KV_HEADS, PAST_LEN, HEAD_DIM), BF16),
                        pltpu.VMEM((stack_rows, 3 * A_BLOCK), F32),
                        pltpu.VMEM((stack_rows, 3 * A_BLOCK), F32),
                        pltpu.VMEM((stack_rows, PAST_LEN), F32),
                        pltpu.VMEM((stack_rows, 3 * A_BLOCK), BF16),
                        pltpu.VMEM((stack_rows, PAST_LEN), BF16)],
        compiler_params=_params(2),
        name="win_attention",
    )(sink, proj, proj, proj, cache_k, cache_v, proj, cos, sin_signed)


def _bias_kernel(rpb_ref, o_ref):
    h = pl.program_id(0)
    shape = (GRID_W, 2 * GRID_W)
    c = lax.broadcasted_iota(jnp.int32, shape, 0)
    j2 = lax.broadcasted_iota(jnp.int32, shape, 1)
    kc = j2 % GRID_W
    second = j2 >= GRID_W
    col_start = jnp.clip(c - B_WIN_COLS // 2, 0, GRID_W - B_WIN_COLS)
    ok = (kc >= col_start) & (kc < col_start + B_WIN_COLS)
    dc = kc - c + B_WIN_COLS - 1
    base = h * (RPB_ROWS * RPB_COLS)
    pair_scr = []
    for i in range(N_BIAS_PAIRS):
        acc = jnp.full(shape, NEG_INF, F32)
        for d in range(RPB_COLS):
            val = jnp.where(second, rpb_ref[base + (i + 1) * RPB_COLS + d],
                            rpb_ref[base + i * RPB_COLS + d]) * LOG2E
            acc = jnp.where(ok & (dc == d), val, acc)
        pair_scr.append(acc)
    for r in range(GRID_ROWS):
        dr0 = _na_key_row0(r) - r + B_WIN_ROWS - 1
        for i in range(B_WIN_ROWS // 2):
            o_ref[r * GRID_W:(r + 1) * GRID_W, i * 2 * GRID_W:(i + 1) * 2 * GRID_W] = (
                pair_scr[dr0 + 2 * i])


def _expand_bias(rpb):
    kw = B_WIN_ROWS * GRID_W
    return pl.pallas_call(
        _bias_kernel,
        grid=(B_HEADS,),
        in_specs=[pl.BlockSpec(memory_space=pltpu.SMEM)],
        out_specs=pl.BlockSpec((None, DEC_SEQ, kw), lambda h: (h, 0, 0)),
        out_shape=jax.ShapeDtypeStruct((B_HEADS, DEC_SEQ, kw), F32),
        compiler_params=_params(1),
        name="expand_bias",
    )(rpb.reshape(-1))


def _na_key_row0(r):
    return min(max(r - B_WIN_ROWS // 2, 0), GRID_ROWS - B_WIN_ROWS)


def _na_row_groups():
    groups = []
    for r in range(GRID_ROWS):
        rs = _na_key_row0(r)
        if groups and groups[-1][2] == rs:
            groups[-1] = (groups[-1][0], r + 1, rs)
        else:
            groups.append((r, r + 1, rs))
    return groups


def _na_attn_kernel(q_ref, k_ref, v_ref, ck_ref, cv_ref, gate_ref, bias_ref, o_ref,
                    ck_scr, cv_scr, sn_scr, sc_scr, en_scr, ec_scr):
    kw = B_WIN_ROWS * GRID_W
    h = pl.program_id(1)

    @pl.when(h == 0)
    def _():
        _split_cache_heads(ck_ref, ck_scr, B_HEADS)
        _split_cache_heads(cv_ref, cv_scr, B_HEADS)

    groups = _na_row_groups()
    sc_scr[...] = _dot_nt(q_ref[...], ck_scr[h])
    for r0, r1, rs in groups:
        sn_scr[r0 * GRID_W:r1 * GRID_W, :] = _dot_nt(
            q_ref[r0 * GRID_W:r1 * GRID_W, :], k_ref[rs * GRID_W:rs * GRID_W + kw, :])

    def store_n(r, e):
        en_scr[r, :] = e

    def store_c(r, e):
        ec_scr[r, :] = e

    inv = _chunked_softmax(
        DEC_SEQ, NA_CHUNK,
        [lambda r: sn_scr[r, :] * LOGIT_SCALE + bias_ref[r, :], lambda r: sc_scr[r, :] * LOGIT_SCALE],
        [store_n, store_c])
    o_n = jnp.concatenate(
        [_dot(en_scr[r0 * GRID_W:r1 * GRID_W, :], v_ref[rs * GRID_W:rs * GRID_W + kw, :])
         for r0, r1, rs in groups], axis=0)
    o = (o_n + _dot(ec_scr[...], cv_scr[h])) * inv
    o_ref[...] = (o * _silu(gate_ref[...].astype(F32))).astype(o_ref.dtype)


def _na_attention(proj, cache_k, cache_v, bias):
    n = DEC_SEQ
    hd = HEAD_DIM
    cache_spec = pl.BlockSpec((None, PAST_LEN * B_HEADS, hd), lambda b, h: (b, 0, 0))
    return pl.pallas_call(
        _na_attn_kernel,
        grid=(DEC_BATCH, B_HEADS),
        in_specs=[
            pl.BlockSpec((n, hd), lambda b, h: (b, COL_QB // hd + h)),
            pl.BlockSpec((n, hd), lambda b, h: (b, COL_KB // hd + h)),
            pl.BlockSpec((n, hd), lambda b, h: (b, COL_VB // hd + h)),
            cache_spec,
            cache_spec,
            pl.BlockSpec((n, hd), lambda b, h: (b, (COL_GATE + A_WIDTH) // hd + h)),
            pl.BlockSpec((None, n, B_WIN_ROWS * GRID_W), lambda b, h: (h, 0, 0)),
        ],
        out_specs=pl.BlockSpec((n, hd), lambda b, h: (b, h)),
        out_shape=jax.ShapeDtypeStruct((N_SAMPLE, B_WIDTH), BF16),
        scratch_shapes=[pltpu.VMEM((B_HEADS, PAST_LEN, hd), BF16),
                        pltpu.VMEM((B_HEADS, PAST_LEN, hd), BF16),
                        pltpu.VMEM((n, B_WIN_ROWS * GRID_W), F32),
                        pltpu.VMEM((n, PAST_LEN), F32),
                        pltpu.VMEM((n, B_WIN_ROWS * GRID_W), BF16),
                        pltpu.VMEM((n, PAST_LEN), BF16)],
        compiler_params=_params(2),
        name="na_attention",
    )(proj, proj, proj, cache_k, cache_v, proj, bias)


def _pool_kernel(u_ref, gate_ref, band_ref, inv_count_ref, w_ref, scale_ref, o_ref, *, seq, rows):
    cb = POOL_BAND_BLOCK
    parts = []
    for i in range(rows // cb):
        lo = i * cb if seq <= cb else max(0, (i - 1) * cb)
        hi = (i + 1) * cb if seq <= cb else min(rows, (i + 2) * cb)
        parts.append(_dot(band_ref[i * cb:(i + 1) * cb, lo:hi], u_ref[lo:hi, :]))
    wsum = jnp.concatenate(parts, axis=0)
    pooled = wsum * inv_count_ref[...] - u_ref[...].astype(F32)
    y = _dot(pooled.astype(BF16), w_ref[...]) * scale_ref[...]
    o_ref[...] = (y * _silu(gate_ref[...].astype(F32))).astype(o_ref.dtype)


def _pool_operators(seq, rows):
    t = jnp.arange(rows)[:, None]
    j = jnp.arange(rows)[None, :]
    same_seq = (t // seq) == (j // seq)
    pos = t % seq
    bands, inv_counts = [], []
    for window in POOL_WINDOWS:
        half = window // 2
        assert half <= POOL_BAND_BLOCK
        bands.append(((j - t >= -half) & (j - t < half) & same_seq).astype(BF16))
        inv_counts.append(1.0 / (jnp.minimum(pos + half, seq) - jnp.maximum(pos - half, 0)).astype(F32))
    return jnp.stack(bands), jnp.stack(inv_counts)


def _pool_mixer(ug, w_grp, scale, seq, rows=1024):
    m = ug.shape[0]
    n_groups = len(POOL_WINDOWS)
    pg = POOL_GROUP
    band, inv_count = _pool_operators(seq, rows)
    return pl.pallas_call(
        functools.partial(_pool_kernel, seq=seq, rows=rows),
        grid=(n_groups, m // rows),
        in_specs=[
            pl.BlockSpec((rows, pg), lambda g, i: (i, g)),
            pl.BlockSpec((rows, pg), lambda g, i: (i, n_groups + g)),
            pl.BlockSpec((None, rows, rows), lambda g, i: (g, 0, 0)),
            pl.BlockSpec((None, rows, 1), lambda g, i: (g, 0, 0)),
            pl.BlockSpec((None, pg, pg), lambda g, i: (g, 0, 0)),
            pl.BlockSpec((1, pg), lambda g, i: (0, g)),
        ],
        out_specs=pl.BlockSpec((rows, pg), lambda g, i: (i, g)),
        out_shape=jax.ShapeDtypeStruct((m, n_groups * pg), BF16),
        compiler_params=_params(2),
        name="pool_mixer",
    )(ug, ug, band, inv_count, w_grp, scale.reshape(1, -1))


def _rope_tables():
    t = jnp.arange(DEC_SEQ)
    quarter = HEAD_DIM // 4
    inv_freq = ROPE_BASE ** (-jnp.arange(quarter, dtype=F32) / quarter)
    ang_r = (t // GRID_W).astype(F32)[:, None] * inv_freq
    ang_c = (t % GRID_W).astype(F32)[:, None] * inv_freq
    ang = jnp.concatenate([ang_r, ang_r, ang_c, ang_c], axis=-1)
    sign = jnp.tile(jnp.concatenate([-jnp.ones((quarter,), F32), jnp.ones((quarter,), F32)]), 2)
    return jnp.cos(ang), jnp.sin(ang) * sign


def kernel(x_prompt, x_sample, c, cache_a_k, cache_a_v, cache_b_k, cache_b_v, c_ctx,
           w_ada, b_ada, norm_g, w_in_attn, a_sink, b_rpb, w_out_attn,
           w_in_pool, w_grp_pool, pool_scale, w_out_pool, final_g):
    d = D_MODEL
    xp = x_prompt.reshape(N_PROMPT, d)
    xs = x_sample.reshape(N_SAMPLE, d)

    w_in0 = w_in_attn[0].astype(BF16)

    cond = jnp.zeros((MOD_ROWS, d), F32).at[:DEC_BATCH].set(c).at[CTX_MOD_ROW].set(c_ctx)
    mod = _ada(cond, w_ada, b_ada)
    mod0 = mod[0].reshape(MOD_ROWS, 1, 3 * d)
    mod1 = mod[1].reshape(MOD_ROWS, 1, 3 * d)

    hp, hs = _norm_mod(xp, xs, norm_g[0], mod0)

    ka, va, kb, vb = _kv_projection(hp, w_in0)
    qg_tiles = (_col_tiles(COL_QA, A_WIDTH) + _col_tiles(COL_QB, B_WIDTH)
                + _col_tiles(COL_GATE, MIX_WIDTH))
    qg_p, w_in1 = _matmul(hp, w_in0, qg_tiles, BF16, cast_weights=(w_in_pool[0],))
    n_groups = len(POOL_WINDOWS)
    proj_s, w_out0, w_out1, w_grp = _matmul(
        hs, w_in0, _col_tiles(0, ATTN_IN_WIDTH), BF16,
        cast_weights=(w_out_attn[0], w_out_pool[0],
                      w_grp_pool[0].reshape(n_groups * POOL_GROUP, POOL_GROUP)))
    w_grp = w_grp.reshape(n_groups, POOL_GROUP, POOL_GROUP)

    sink = a_sink[0]
    og_p = _ctx_attention(sink, qg_p, ka, va, kb, vb)

    cos, sin_signed = _rope_tables()
    flat = lambda cache: cache.reshape(DEC_BATCH, -1, HEAD_DIM)
    oa_s = _win_attention(sink, proj_s, flat(cache_a_k), flat(cache_a_v), cos, sin_signed)
    ob_s = _na_attention(proj_s, flat(cache_b_k), flat(cache_b_v), _expand_bias(b_rpb[0]))

    xp1, hu_p, ss_p = _matmul_residual_prenorm([(og_p, 0), (og_p, 1)], w_out0, xp, mod0, True,
                                               norm_g[1], mod1)
    xs1, hu_s, ss_s = _matmul_residual_prenorm([(oa_s, 0), (ob_s, 0)], w_out0, xs, mod0, False,
                                               norm_g[1], mod1)

    shw = _shift_projection(mod[1], w_in1).reshape(MOD_ROWS, 1, 2 * d)
    ug_p = _matmul_postnorm(hu_p, ss_p, w_in1, shw, True, BF16)
    ug_s = _matmul_postnorm(hu_s, ss_s, w_in1, shw, False, BF16)
    y_p = _pool_mixer(ug_p, w_grp, pool_scale[0], SEQ)
    y_s = _pool_mixer(ug_s, w_grp, pool_scale[0], DEC_SEQ)
    y_prompt = _matmul_residual_norm([(y_p, 0), (y_p, 1)], w_out1, xp1, mod1, True,
                                     final_g).reshape(BATCH, SEQ, d)
    y_sample = _matmul_residual_norm([(y_s, 0), (y_s, 1)], w_out1, xs1, mod1, False,
                                     final_g).reshape(DEC_BATCH, DEC_SEQ, d)

    kv_shape_a = (BATCH, 1, SEQ, A_KV_HEADS, HEAD_DIM)
    kv_shape_b = (BATCH, 1, SEQ, B_HEADS, HEAD_DIM)
    return (y_prompt, y_sample, ka.reshape(kv_shape_a), va.reshape(kv_shape_a),
            kb.reshape(kv_shape_b), vb.reshape(kv_shape_b))
```

```python
import functools

import jax
import jax.numpy as jnp
from jax import lax
from jax.experimental import pallas as pl
from jax.experimental.pallas import tpu as pltpu

F32 = jnp.float32
BF16 = jnp.bfloat16

D_MODEL = 4096
BATCH = 32
SEQ = 256
DEC_BATCH = 8
DEC_SEQ = 1024
PAST_LEN = 512
GRID_W = 64
HEAD_DIM = 128
A_Q_HEADS = 16
A_KV_HEADS = 4
A_GROUPS = 4
A_HALF_WIN = 128
A_BLOCK = 128
B_HEADS = 16
B_WIN_ROWS = 8
B_WIN_COLS = 16
A_WIDTH = A_Q_HEADS * HEAD_DIM
A_KV_WIDTH = A_KV_HEADS * HEAD_DIM
B_WIDTH = B_HEADS * HEAD_DIM
MIX_WIDTH = A_WIDTH + B_WIDTH
POOL_WINDOWS = (2, 4, 8, 16)
POOL_GROUP = 1024
POOL_BAND_BLOCK = 256
CAST_CHUNK_ROWS = 64
ADA_SIDE_CHUNK = 128
ROPE_BASE = 10000.0
NORM_EPS = 1e-6
NEG_INF = -1e30
ATTN_SCALE = HEAD_DIM ** -0.5
LOG2E = 1.4426950408889634
LOGIT_SCALE = ATTN_SCALE * LOG2E
CTX_CHUNK = 1024
WIN_CHUNK = 512
NA_CHUNK = 1024

N_PROMPT = BATCH * SEQ
N_SAMPLE = DEC_BATCH * DEC_SEQ
MOD_ROWS = 16
CTX_MOD_ROW = DEC_BATCH
GRID_ROWS = DEC_SEQ // GRID_W
N_BIAS_PAIRS = 2 * B_WIN_ROWS - 2
RPB_ROWS = 2 * B_WIN_ROWS - 1
RPB_COLS = 2 * B_WIN_COLS - 1

COL_QA = 0
COL_KA = COL_QA + A_WIDTH
COL_VA = COL_KA + A_KV_WIDTH
COL_QB = COL_VA + A_KV_WIDTH
COL_KB = COL_QB + B_WIDTH
COL_VB = COL_KB + B_WIDTH
COL_GATE = COL_VB + B_WIDTH
ATTN_IN_WIDTH = COL_GATE + MIX_WIDTH
PROJ_TN = 1024


def _col_tiles(col0, width):
    assert col0 % PROJ_TN == 0 and width % PROJ_TN == 0
    return tuple(range(col0 // PROJ_TN, (col0 + width) // PROJ_TN))


def _tile_lookup(tiles):
    runs = []
    for jj, t in enumerate(tiles):
        if not runs or runs[-1][1] != t - jj:
            runs.append((jj, t - jj))

    def lookup(j):
        off = runs[0][1]
        for start, o in runs[1:]:
            off = jnp.where(j >= start, o, off)
        return j + off
    return lookup

VMEM_LIMIT = 48 * 1024 * 1024
VMEM_LIMIT_ROWS = 56 * 1024 * 1024


def _params(n_grid_dims):
    return pltpu.CompilerParams(
        dimension_semantics=("arbitrary",) * n_grid_dims,
        vmem_limit_bytes=VMEM_LIMIT,
    )


def _silu(x):
    return 0.5 * x * (1.0 + jnp.tanh(0.5 * x))


def _dot_nt(a, b):
    return lax.dot_general(a, b, (((1,), (1,)), ((), ())), preferred_element_type=F32)


def _dot(a, b):
    return jnp.dot(a, b, preferred_element_type=F32)


def _ada_kernel(cond_ref, w_ref, b_ref, o_ref):
    a = _silu(cond_ref[...]).astype(BF16)
    o_ref[...] = _dot(a, w_ref[...].astype(BF16)) + b_ref[...]


def _ada(cond, w_ada, b_ada, layer, tn=512):
    depth, d, n = w_ada.shape
    return pl.pallas_call(
        _ada_kernel,
        grid=(n // tn,),
        in_specs=[
            pl.BlockSpec((MOD_ROWS, d), lambda j: (0, 0)),
            pl.BlockSpec((None, d, tn), lambda j: (layer, 0, j)),
            pl.BlockSpec((None, 1, tn), lambda j: (layer, 0, j)),
        ],
        out_specs=pl.BlockSpec((MOD_ROWS, tn), lambda j: (0, j)),
        out_shape=jax.ShapeDtypeStruct((MOD_ROWS, n), F32),
        compiler_params=_params(1),
        name="ada",
    )(cond, w_ada, b_ada.reshape(depth, 1, n))


def _mod_row_fn(is_prompt, tm):
    if is_prompt:
        return lambda i: CTX_MOD_ROW
    return lambda i: (i * tm) // DEC_SEQ


def _norm_mod_kernel(xp_ref, xs_ref, g_ref, sh_ref, sc_ref, hp_ref, hs_ref, *, n_prompt_tiles):
    def norm_mod(x_ref, o_ref):
        x = x_ref[...]
        ms = jnp.mean(x * x, axis=-1, keepdims=True)
        y = x * lax.rsqrt(ms + NORM_EPS) * g_ref[...]
        o_ref[...] = (y * (1.0 + sc_ref[...]) + sh_ref[...]).astype(o_ref.dtype)

    i = pl.program_id(0)

    @pl.when(i < n_prompt_tiles)
    def _():
        norm_mod(xp_ref, hp_ref)

    @pl.when(i >= n_prompt_tiles)
    def _():
        norm_mod(xs_ref, hs_ref)


def _norm_mod(xp, xs, gain, mod, tm=256):
    d = xp.shape[1]
    n_p, n_s = xp.shape[0] // tm, xs.shape[0] // tm
    p_blk = lambda i: (jnp.minimum(i, n_p - 1), 0)
    s_blk = lambda i: (jnp.maximum(i - n_p, 0), 0)
    row = lambda i: jnp.where(i < n_p, CTX_MOD_ROW, (jnp.maximum(i - n_p, 0) * tm) // DEC_SEQ)
    return pl.pallas_call(
        functools.partial(_norm_mod_kernel, n_prompt_tiles=n_p),
        grid=(n_p + n_s,),
        in_specs=[
            pl.BlockSpec((tm, d), p_blk),
            pl.BlockSpec((tm, d), s_blk),
            pl.BlockSpec((1, d), lambda i: (0, 0)),
            pl.BlockSpec((None, 1, d), lambda i: (row(i), 0, 0)),
            pl.BlockSpec((None, 1, d), lambda i: (row(i), 0, 1)),
        ],
        out_specs=[pl.BlockSpec((tm, d), p_blk), pl.BlockSpec((tm, d), s_blk)],
        out_shape=[jax.ShapeDtypeStruct(xp.shape, BF16), jax.ShapeDtypeStruct(xs.shape, BF16)],
        compiler_params=_params(1),
        name="norm_mod",
    )(xp, xs, gain.reshape(1, d), mod, mod)


def _mm_kernel(a_ref, w_ref, *refs, n_casts, has_ada):
    n_in = n_casts + (3 if has_ada else 0)
    cast_in = refs[:n_casts]
    o_ref = refs[n_in]
    cast_out = refs[n_in + 1:n_in + 1 + n_casts]
    o_ref[...] = _dot(a_ref[...], w_ref[...]).astype(o_ref.dtype)
    for src, dst in zip(cast_in, cast_out):
        dst[...] = src[...].astype(dst.dtype)
    if has_ada:
        _ada_kernel(*refs[n_casts:n_in], refs[-1])


def _matmul(a, w, w_tiles, out_dtype, cast_weights=(), ada_job=None):
    m, k = a.shape
    tn = PROJ_TN
    tm = 1024 if out_dtype == BF16 else 512
    assert m % tm == 0
    w_tile = _tile_lookup(w_tiles)
    ncols = len(w_tiles) * tn
    n_i = m // tm
    n_steps = len(w_tiles) * n_i
    in_specs = [pl.BlockSpec((tm, k), lambda j, i: (i, 0)),
                pl.BlockSpec((k, tn), lambda j, i: (0, w_tile(j)))]
    out_specs = [pl.BlockSpec((tm, tn), lambda j, i: (i, j))]
    out_shape = [jax.ShapeDtypeStruct((m, ncols), out_dtype)]
    for cw in cast_weights:
        rows, cols = cw.shape
        chunk = CAST_CHUNK_ROWS
        n_chunks = rows // chunk
        assert rows % chunk == 0 and n_chunks <= n_steps
        spec = pl.BlockSpec((chunk, cols),
                            lambda j, i, n_chunks=n_chunks: (jnp.minimum(j * n_i + i, n_chunks - 1), 0))
        in_specs.append(spec)
        out_specs.append(spec)
        out_shape.append(jax.ShapeDtypeStruct((rows, cols), BF16))
    cast_in_specs, cast_out_specs = in_specs[2:], out_specs[1:]
    ada_args = ()
    if ada_job is not None:
        cond, w_ada, b_ada, layer = ada_job
        depth, d, n_mod = w_ada.shape
        tc = ADA_SIDE_CHUNK
        n_chunks = n_mod // tc
        assert n_mod % tc == 0 and n_chunks <= n_steps
        chunk_of = lambda j, i: jnp.minimum(j * n_i + i, n_chunks - 1)
        in_specs = in_specs[:2] + cast_in_specs + [
            pl.BlockSpec((MOD_ROWS, d), lambda j, i: (0, 0)),
            pl.BlockSpec((None, d, tc), lambda j, i: (layer, 0, chunk_of(j, i))),
            pl.BlockSpec((None, 1, tc), lambda j, i: (layer, 0, chunk_of(j, i)))]
        out_specs = out_specs[:1] + cast_out_specs + [
            pl.BlockSpec((MOD_ROWS, tc), lambda j, i: (0, chunk_of(j, i)))]
        out_shape.append(jax.ShapeDtypeStruct((MOD_ROWS, n_mod), F32))
        ada_args = (cond, w_ada, b_ada.reshape(depth, 1, n_mod))
    side = bool(cast_weights) or ada_job is not None
    outs = pl.pallas_call(
        functools.partial(_mm_kernel, n_casts=len(cast_weights), has_ada=ada_job is not None),
        grid=(ncols // tn, n_i),
        in_specs=in_specs,
        out_specs=out_specs,
        out_shape=out_shape,
        compiler_params=pltpu.CompilerParams(
            dimension_semantics=("arbitrary", "arbitrary"),
            vmem_limit_bytes=VMEM_LIMIT_ROWS if side else VMEM_LIMIT),
        name="proj",
    )(a, w, *cast_weights, *ada_args)
    return outs if side else outs[0]


def _kv_proj_kernel(a_ref, w_ref, ka_ref, va_ref, kb_ref, vb_ref, *, kb_tile0, vb_tile0):
    j = pl.program_id(0)
    acc = _dot(a_ref[...], w_ref[...])

    @pl.when(j < kb_tile0)
    def _():
        ka_ref[...] = acc[:, :A_KV_WIDTH]
        va_ref[...] = acc[:, A_KV_WIDTH:]

    @pl.when((j >= kb_tile0) & (j < vb_tile0))
    def _():
        kb_ref[...] = acc

    @pl.when(j >= vb_tile0)
    def _():
        vb_ref[...] = acc


def _kv_projection(a, w, tm=512):
    m, k = a.shape
    tn = PROJ_TN
    assert COL_VA == COL_KA + A_KV_WIDTH and 2 * A_KV_WIDTH == tn
    w_tiles = _col_tiles(COL_KA, tn) + _col_tiles(COL_KB, B_WIDTH) + _col_tiles(COL_VB, B_WIDTH)
    w_tile = _tile_lookup(w_tiles)
    kb_tile0 = 1
    vb_tile0 = kb_tile0 + B_WIDTH // tn
    end_tile = len(w_tiles)
    n_i = m // tm
    last = n_i - 1

    def parked(j, i, t0, t1):
        row = jnp.where(j < t0, 0, jnp.where(j < t1, i, last))
        return row, jnp.clip(j - t0, 0, t1 - t0 - 1)

    return pl.pallas_call(
        functools.partial(_kv_proj_kernel, kb_tile0=kb_tile0, vb_tile0=vb_tile0),
        grid=(end_tile, n_i),
        in_specs=[pl.BlockSpec((tm, k), lambda j, i: (i, 0)),
                  pl.BlockSpec((k, tn), lambda j, i: (0, w_tile(j)))],
        out_specs=[pl.BlockSpec((tm, A_KV_WIDTH), lambda j, i: parked(j, i, 0, kb_tile0)),
                   pl.BlockSpec((tm, A_KV_WIDTH), lambda j, i: parked(j, i, 0, kb_tile0)),
                   pl.BlockSpec((tm, tn), lambda j, i: parked(j, i, kb_tile0, vb_tile0)),
                   pl.BlockSpec((tm, tn), lambda j, i: parked(j, i, vb_tile0, end_tile))],
        out_shape=[jax.ShapeDtypeStruct((m, A_KV_WIDTH), F32),
                   jax.ShapeDtypeStruct((m, A_KV_WIDTH), F32),
                   jax.ShapeDtypeStruct((m, B_WIDTH), F32),
                   jax.ShapeDtypeStruct((m, B_WIDTH), F32)],
        compiler_params=_params(2),
        name="kv_proj",
    )(a, w)


def _dot_halves(a1_ref, a2_ref, w_ref):
    kh = a1_ref.shape[1]
    return _dot(a1_ref[...], w_ref[:kh, :]) + _dot(a2_ref[...], w_ref[kh:, :])


def _mm_res_prenorm_kernel(a1_ref, a2_ref, w_ref, x_ref, g_ref, gain_ref, sc_ref,
                           x1_ref, hu_ref, ss_ref):
    x1 = x_ref[...] + g_ref[...] * _dot_halves(a1_ref, a2_ref, w_ref)
    x1_ref[...] = x1
    hu_ref[...] = (x1 * (gain_ref[...] * (1.0 + sc_ref[...]))).astype(hu_ref.dtype)
    ss_ref[...] = jnp.sum(x1 * x1, axis=-1, keepdims=True)


def _matmul_residual_prenorm(a_halves, w, x, mod, is_prompt, gain, next_mod, tm=512):
    (a1, c1), (a2, c2) = a_halves
    m = a1.shape[0]
    k, n = w.shape
    tn = PROJ_TN
    n_tiles = n // tn
    row = _mod_row_fn(is_prompt, tm)
    scale_block0, gate_block0 = n_tiles, 2 * n_tiles
    tile = pl.BlockSpec((tm, tn), lambda j, i: (i, j))
    return pl.pallas_call(
        _mm_res_prenorm_kernel,
        grid=(n_tiles, m // tm),
        in_specs=[pl.BlockSpec((tm, k // 2), lambda j, i: (i, c1)),
                  pl.BlockSpec((tm, k // 2), lambda j, i: (i, c2)),
                  pl.BlockSpec((k, tn), lambda j, i: (0, j)),
                  tile,
                  pl.BlockSpec((None, 1, tn), lambda j, i: (row(i), 0, gate_block0 + j)),
                  pl.BlockSpec((1, tn), lambda j, i: (0, j)),
                  pl.BlockSpec((None, 1, tn), lambda j, i: (row(i), 0, scale_block0 + j))],
        out_specs=[tile, tile, pl.BlockSpec((None, tm, 1), lambda j, i: (j, i, 0))],
        out_shape=[jax.ShapeDtypeStruct((m, n), F32), jax.ShapeDtypeStruct((m, n), BF16),
                   jax.ShapeDtypeStruct((n_tiles, m, 1), F32)],
        compiler_params=_params(2),
        name="proj_residual_prenorm",
    )(a1, a2, w, x, mod, gain.reshape(1, n), next_mod)


def _shift_proj_kernel(sh_ref, w_ref, o_ref):
    o_ref[...] = _dot(sh_ref[...].astype(BF16), w_ref[...])


def _shift_projection(mod2d, w):
    k, n = w.shape
    tn = PROJ_TN
    return pl.pallas_call(
        _shift_proj_kernel,
        grid=(n // tn,),
        in_specs=[pl.BlockSpec((MOD_ROWS, k), lambda j: (0, 0)),
                  pl.BlockSpec((k, tn), lambda j: (0, j))],
        out_specs=pl.BlockSpec((MOD_ROWS, tn), lambda j: (0, j)),
        out_shape=jax.ShapeDtypeStruct((MOD_ROWS, n), F32),
        compiler_params=_params(1),
        name="shift_proj",
    )(mod2d, w)


def _mm_postnorm_kernel(a_ref, w_ref, ss_ref, shw_ref, o_ref, *, width):
    acc = _dot(a_ref[...], w_ref[...])
    ss = ss_ref[0]
    for t in range(1, ss_ref.shape[0]):
        ss = ss + ss_ref[t]
    r = lax.rsqrt(ss / width + NORM_EPS)
    o_ref[...] = (acc * r + shw_ref[...]).astype(o_ref.dtype)


def _matmul_postnorm(hu, ss, w, shw, is_prompt, out_dtype, tm=1024):
    m, k = hu.shape
    n = w.shape[1]
    tn = PROJ_TN
    row = _mod_row_fn(is_prompt, tm)
    return pl.pallas_call(
        functools.partial(_mm_postnorm_kernel, width=k),
        grid=(n // tn, m // tm),
        in_specs=[pl.BlockSpec((tm, k), lambda j, i: (i, 0)),
                  pl.BlockSpec((k, tn), lambda j, i: (0, j)),
                  pl.BlockSpec((ss.shape[0], tm, 1), lambda j, i: (0, i, 0)),
                  pl.BlockSpec((None, 1, tn), lambda j, i: (row(i), 0, j))],
        out_specs=pl.BlockSpec((tm, tn), lambda j, i: (i, j)),
        out_shape=jax.ShapeDtypeStruct((m, n), out_dtype),
        compiler_params=pltpu.CompilerParams(dimension_semantics=("arbitrary", "arbitrary"),
                                             vmem_limit_bytes=VMEM_LIMIT_ROWS),
        name="proj_postnorm",
    )(hu, w, ss, shw)


def _mm_res_norm_kernel(a1_ref, a2_ref, w_ref, x_ref, g_ref, gain_ref, y_ref, *, n_tiles, tn):
    j = pl.program_id(1)
    x2 = x_ref[...] + g_ref[...] * _dot_halves(a1_ref, a2_ref, w_ref)
    for jj in range(n_tiles):
        @pl.when(j == jj)
        def _(jj=jj):
            y_ref[:, jj * tn:(jj + 1) * tn] = x2

    @pl.when(j == n_tiles - 1)
    def _():
        ss = None
        for jj in range(n_tiles):
            t = y_ref[:, jj * tn:(jj + 1) * tn]
            part = jnp.sum(t * t, axis=-1, keepdims=True)
            ss = part if ss is None else ss + part
        r = lax.rsqrt(ss / (n_tiles * tn) + NORM_EPS)
        for jj in range(n_tiles):
            cols = slice(jj * tn, (jj + 1) * tn)
            y_ref[:, cols] = y_ref[:, cols] * r * gain_ref[:, cols]


def _matmul_residual_norm(a_halves, w, x, mod, is_prompt, gain, tm=512):
    (a1, c1), (a2, c2) = a_halves
    m = a1.shape[0]
    k, n = w.shape
    tn = PROJ_TN
    n_tiles = n // tn
    row = _mod_row_fn(is_prompt, tm)
    gate_block0 = 2 * n_tiles
    return pl.pallas_call(
        functools.partial(_mm_res_norm_kernel, n_tiles=n_tiles, tn=tn),
        grid=(m // tm, n_tiles),
        in_specs=[pl.BlockSpec((tm, k // 2), lambda i, j: (i, c1)),
                  pl.BlockSpec((tm, k // 2), lambda i, j: (i, c2)),
                  pl.BlockSpec((k, tn), lambda i, j: (0, j)),
                  pl.BlockSpec((tm, tn), lambda i, j: (i, j)),
                  pl.BlockSpec((None, 1, tn), lambda i, j: (row(i), 0, gate_block0 + j)),
                  pl.BlockSpec((1, n), lambda i, j: (0, 0))],
        out_specs=pl.BlockSpec((tm, n), lambda i, j: (i, 0)),
        out_shape=jax.ShapeDtypeStruct((m, n), F32),
        compiler_params=pltpu.CompilerParams(dimension_semantics=("arbitrary", "arbitrary"),
                                             vmem_limit_bytes=VMEM_LIMIT_ROWS),
        name="proj_residual_norm",
    )(a1, a2, w, x, mod, gain.reshape(1, n))


def _chunked_softmax(n_rows, chunk, loads, stores, sink2=None):
    inv = []
    for c in range(n_rows // chunk):
        rows = slice(c * chunk, (c + 1) * chunk)
        ts = [load(rows) for load in loads]
        m = functools.reduce(jnp.maximum, [jnp.max(t, axis=-1, keepdims=True) for t in ts])
        if sink2 is not None:
            m = jnp.maximum(m, sink2[rows])
        es = [jnp.exp2(t - m) for t in ts]
        l = functools.reduce(jnp.add, [jnp.sum(e, axis=-1, keepdims=True) for e in es])
        if sink2 is not None:
            l = l + jnp.exp2(sink2[rows] - m)
        for store, e in zip(stores, es):
            store(rows, e.astype(BF16))
        inv.append(1.0 / l)
    return jnp.concatenate(inv, axis=0)


def _ctx_attn_kernel(sink_ref, qg_ref, ka_ref, va_ref, kb_ref, vb_ref, o_ref, s_scr, e_scr):
    n = SEQ
    gate0 = MIX_WIDTH
    n_stack = A_GROUPS

    def softmax(sink2):
        def store(rows, e):
            e_scr[rows, :] = e
        return _chunked_softmax(n_stack * n, CTX_CHUNK, [lambda rows: s_scr[rows, :] * LOGIT_SCALE],
                                [store], sink2)

    def emit(o, g, out_col):
        cols = slice(out_col, out_col + HEAD_DIM)
        gate = qg_ref[:, gate0 + out_col:gate0 + out_col + HEAD_DIM].astype(F32)
        o_ref[:, cols] = (o[g * n:(g + 1) * n] * _silu(gate)).astype(o_ref.dtype)

    for kv in range(A_KV_HEADS):
        cols = slice(kv * HEAD_DIM, (kv + 1) * HEAD_DIM)
        heads = [kv * A_GROUPS + g for g in range(A_GROUPS)]
        q = jnp.concatenate(
            [qg_ref[:, h * HEAD_DIM:(h + 1) * HEAD_DIM] for h in heads], axis=0)
        sink2 = jnp.concatenate(
            [jnp.full((n, 1), sink_ref[h] * LOG2E, F32) for h in heads], axis=0)
        s_scr[...] = _dot_nt(q, ka_ref[:, cols].astype(BF16))
        inv = softmax(sink2)
        o = _dot(e_scr[...], va_ref[:, cols].astype(BF16)) * inv
        for g, h in enumerate(heads):
            emit(o, g, h * HEAD_DIM)
    for h0 in range(0, B_HEADS, n_stack):
        heads = range(h0, h0 + n_stack)
        for g, h in enumerate(heads):
            cols = slice(h * HEAD_DIM, (h + 1) * HEAD_DIM)
            q = qg_ref[:, A_WIDTH + h * HEAD_DIM:A_WIDTH + (h + 1) * HEAD_DIM]
            s_scr[g * n:(g + 1) * n, :] = _dot_nt(q, kb_ref[:, cols].astype(BF16))
        inv = softmax(None)
        o = jnp.concatenate(
            [_dot(e_scr[g * n:(g + 1) * n, :],
                  vb_ref[:, h * HEAD_DIM:(h + 1) * HEAD_DIM].astype(BF16))
             for g, h in enumerate(heads)], axis=0) * inv
        for g, h in enumerate(heads):
            emit(o, g, A_WIDTH + h * HEAD_DIM)


def _ctx_attention(sink, qg, ka, va, kb, vb):
    stack_rows = A_GROUPS * SEQ
    return pl.pallas_call(
        _ctx_attn_kernel,
        grid=(BATCH,),
        in_specs=[
            pl.BlockSpec(memory_space=pltpu.SMEM),
            pl.BlockSpec((SEQ, 2 * MIX_WIDTH), lambda b: (b, 0)),
            pl.BlockSpec((SEQ, A_KV_WIDTH), lambda b: (b, 0)),
            pl.BlockSpec((SEQ, A_KV_WIDTH), lambda b: (b, 0)),
            pl.BlockSpec((SEQ, B_WIDTH), lambda b: (b, 0)),
            pl.BlockSpec((SEQ, B_WIDTH), lambda b: (b, 0)),
        ],
        out_specs=pl.BlockSpec((SEQ, MIX_WIDTH), lambda b: (b, 0)),
        out_shape=jax.ShapeDtypeStruct((N_PROMPT, MIX_WIDTH), BF16),
        scratch_shapes=[pltpu.VMEM((stack_rows, SEQ), F32),
                        pltpu.VMEM((stack_rows, SEQ), BF16)],
        compiler_params=_params(1),
        name="ctx_attention",
    )(sink, qg, ka, va, kb, vb)


def _split_cache_heads(c_ref, scr, n_heads):
    for h in range(n_heads):
        scr[h] = c_ref[pl.ds(h, PAST_LEN, stride=n_heads), :].astype(scr.dtype)


def _win_attn_kernel(sink_ref, q_ref, k_ref, v_ref, ck_ref, cv_ref, gate_ref,
                     cos_ref, sin_ref, o_ref, qs_ref, ks_ref, ck_scr, cv_scr,
                     band_scr, sw_scr, sc_scr, ew_scr, ec_scr):
    kv = pl.program_id(1)
    n = DEC_SEQ

    @pl.when(kv == 0)
    def _():
        _split_cache_heads(ck_ref, ck_scr, A_KV_HEADS)
        _split_cache_heads(cv_ref, cv_scr, A_KV_HEADS)

    cos = cos_ref[...]
    sin = sin_ref[...]
    lane = lax.broadcasted_iota(jnp.int32, (n, HEAD_DIM), 1)
    first_quarter = (lane % (HEAD_DIM // 2)) < (HEAD_DIM // 4)

    def rope(x):
        rot = jnp.where(first_quarter,
                        pltpu.roll(x, HEAD_DIM - HEAD_DIM // 4, 1),
                        pltpu.roll(x, HEAD_DIM // 4, 1))
        return x * cos + rot * sin

    ks_ref[...] = rope(k_ref[...].astype(F32)).astype(BF16)
    for g in range(A_GROUPS):
        qs_ref[g] = (rope(q_ref[:, g * HEAD_DIM:(g + 1) * HEAD_DIM].astype(F32))
                     * LOGIT_SCALE).astype(BF16)

    ck = ck_scr[kv]
    cv = cv_scr[kv]
    rows = A_GROUPS * A_BLOCK
    sink2 = jnp.concatenate(
        [jnp.full((A_BLOCK, 1), sink_ref[kv * A_GROUPS + g] * LOG2E, F32)
         for g in range(A_GROUPS)], axis=0)
    qi = lax.broadcasted_iota(jnp.int32, (rows, 3 * A_BLOCK), 0) % A_BLOCK
    rel = lax.broadcasted_iota(jnp.int32, (rows, 3 * A_BLOCK), 1) - A_BLOCK - qi
    band_scr[...] = jnp.where(jnp.abs(rel) <= A_HALF_WIN, 0.0, NEG_INF)
    for blk in range(n // A_BLOCK):
        r0 = blk * A_BLOCK
        lo = max(0, r0 - A_BLOCK)
        hi = min(n, r0 + 2 * A_BLOCK)
        span = hi - lo
        b0 = lo - (r0 - A_BLOCK)
        q = jnp.concatenate([qs_ref[g, r0:r0 + A_BLOCK, :] for g in range(A_GROUPS)], axis=0)
        sw_scr[:, :span] = _dot_nt(q, ks_ref[lo:hi, :])
        sc_scr[...] = _dot_nt(q, ck)

        def load_w(r, span=span, b0=b0):
            return sw_scr[r, :span] + band_scr[r, b0:b0 + span]

        def store_w(r, e, span=span):
            ew_scr[r, :span] = e

        def store_c(r, e):
            ec_scr[r, :] = e

        inv = _chunked_softmax(rows, WIN_CHUNK, [load_w, lambda r: sc_scr[r, :]],
                               [store_w, store_c], sink2)
        o = (_dot(ew_scr[:, :span], v_ref[lo:hi, :]) + _dot(ec_scr[...], cv)) * inv
        for g in range(A_GROUPS):
            hc = slice(g * HEAD_DIM, (g + 1) * HEAD_DIM)
            gate = gate_ref[r0:r0 + A_BLOCK, hc].astype(F32)
            o_ref[r0:r0 + A_BLOCK, hc] = (
                o[g * A_BLOCK:(g + 1) * A_BLOCK] * _silu(gate)).astype(o_ref.dtype)


def _win_attention(sink, proj, cache_k, cache_v, cos, sin_signed):
    gw = A_GROUPS * HEAD_DIM
    stack_rows = A_GROUPS * A_BLOCK
    n = DEC_SEQ
    cache_spec = pl.BlockSpec((None, PAST_LEN * A_KV_HEADS, HEAD_DIM), lambda b, k: (b, 0, 0))
    return pl.pallas_call(
        _win_attn_kernel,
        grid=(DEC_BATCH, A_KV_HEADS),
        in_specs=[
            pl.BlockSpec(memory_space=pltpu.SMEM),
            pl.BlockSpec((n, gw), lambda b, k: (b, COL_QA // gw + k)),
            pl.BlockSpec((n, HEAD_DIM), lambda b, k: (b, COL_KA // HEAD_DIM + k)),
            pl.BlockSpec((n, HEAD_DIM), lambda b, k: (b, COL_VA // HEAD_DIM + k)),
            cache_spec,
            cache_spec,
            pl.BlockSpec((n, gw), lambda b, k: (b, COL_GATE // gw + k)),
            pl.BlockSpec((n, HEAD_DIM), lambda b, k: (0, 0)),
            pl.BlockSpec((n, HEAD_DIM), lambda b, k: (0, 0)),
        ],
        out_specs=pl.BlockSpec((n, gw), lambda b, k: (b, k)),
        out_shape=jax.ShapeDtypeStruct((N_SAMPLE, A_WIDTH), BF16),
        scratch_shapes=[pltpu.VMEM((A_GROUPS, n, HEAD_DIM), BF16),
                        pltpu.VMEM((n, HEAD_DIM), BF16),
                        pltpu.VMEM((A_KV_HEADS, PAST_LEN, HEAD_DIM), BF16),
                        pltpu.VMEM((A_KV_HEADS, PAST_LEN, HEAD_DIM), BF16),
                        pltpu.VMEM((stack_rows, 3 * A_BLOCK), F32),
                        pltpu.VMEM((stack_rows, 3 * A_BLOCK), F32),
                        pltpu.VMEM((stack_rows, PAST_LEN), F32),
                        pltpu.VMEM((stack_rows, 3 * A_BLOCK), BF16),
                        pltpu.VMEM((stack_rows, PAST_LEN), BF16)],
        compiler_params=_params(2),
        name="win_attention",
    )(sink, proj, proj, proj, cache_k, cache_v, proj, cos, sin_signed)


def _bias_kernel(rpb_ref, o_ref):
    h = pl.program_id(0)
    shape = (GRID_W, 2 * GRID_W)
    c = lax.broadcasted_iota(jnp.int32, shape, 0)
    j2 = lax.broadcasted_iota(jnp.int32, shape, 1)
    kc = j2 % GRID_W
    second = j2 >= GRID_W
    col_start = jnp.clip(c - B_WIN_COLS // 2, 0, GRID_W - B_WIN_COLS)
    ok = (kc >= col_start) & (kc < col_start + B_WIN_COLS)
    dc = kc - c + B_WIN_COLS - 1
    base = h * (RPB_ROWS * RPB_COLS)
    pair_scr = []
    for i in range(N_BIAS_PAIRS):
        acc = jnp.full(shape, NEG_INF, F32)
        for d in range(RPB_COLS):
            val = jnp.where(second, rpb_ref[base + (i + 1) * RPB_COLS + d],
                            rpb_ref[base + i * RPB_COLS + d]) * LOG2E
            acc = jnp.where(ok & (dc == d), val, acc)
        pair_scr.append(acc)
    for r in range(GRID_ROWS):
        dr0 = _na_key_row0(r) - r + B_WIN_ROWS - 1
        for i in range(B_WIN_ROWS // 2):
            o_ref[r * GRID_W:(r + 1) * GRID_W, i * 2 * GRID_W:(i + 1) * 2 * GRID_W] = (
                pair_scr[dr0 + 2 * i])


def _expand_bias(rpb):
    kw = B_WIN_ROWS * GRID_W
    return pl.pallas_call(
        _bias_kernel,
        grid=(B_HEADS,),
        in_specs=[pl.BlockSpec(memory_space=pltpu.SMEM)],
        out_specs=pl.BlockSpec((None, DEC_SEQ, kw), lambda h: (h, 0, 0)),
        out_shape=jax.ShapeDtypeStruct((B_HEADS, DEC_SEQ, kw), F32),
        compiler_params=_params(1),
        name="expand_bias",
    )(rpb.reshape(-1))


def _na_key_row0(r):
    return min(max(r - B_WIN_ROWS // 2, 0), GRID_ROWS - B_WIN_ROWS)


def _na_row_groups():
    groups = []
    for r in range(GRID_ROWS):
        rs = _na_key_row0(r)
        if groups and groups[-1][2] == rs:
            groups[-1] = (groups[-1][0], r + 1, rs)
        else:
            groups.append((r, r + 1, rs))
    return groups


def _na_attn_kernel(q_ref, k_ref, v_ref, ck_ref, cv_ref, gate_ref, bias_ref, o_ref,
                    ck_scr, cv_scr, sn_scr, sc_scr, en_scr, ec_scr):
    kw = B_WIN_ROWS * GRID_W
    h = pl.program_id(1)

    @pl.when(h == 0)
    def _():
        _split_cache_heads(ck_ref, ck_scr, B_HEADS)
        _split_cache_heads(cv_ref, cv_scr, B_HEADS)

    groups = _na_row_groups()
    sc_scr[...] = _dot_nt(q_ref[...], ck_scr[h])
    for r0, r1, rs in groups:
        sn_scr[r0 * GRID_W:r1 * GRID_W, :] = _dot_nt(
            q_ref[r0 * GRID_W:r1 * GRID_W, :], k_ref[rs * GRID_W:rs * GRID_W + kw, :])

    def store_n(r, e):
        en_scr[r, :] = e

    def store_c(r, e):
        ec_scr[r, :] = e

    inv = _chunked_softmax(
        DEC_SEQ, NA_CHUNK,
        [lambda r: sn_scr[r, :] * LOGIT_SCALE + bias_ref[r, :], lambda r: sc_scr[r, :] * LOGIT_SCALE],
        [store_n, store_c])
    o_n = jnp.concatenate(
        [_dot(en_scr[r0 * GRID_W:r1 * GRID_W, :], v_ref[rs * GRID_W:rs * GRID_W + kw, :])
         for r0, r1, rs in groups], axis=0)
    o = (o_n + _dot(ec_scr[...], cv_scr[h])) * inv
    o_ref[...] = (o * _silu(gate_ref[...].astype(F32))).astype(o_ref.dtype)


def _na_attention(proj, cache_k, cache_v, bias):
    n = DEC_SEQ
    hd = HEAD_DIM
    cache_spec = pl.BlockSpec((None, PAST_LEN * B_HEADS, hd), lambda b, h: (b, 0, 0))
    return pl.pallas_call(
        _na_attn_kernel,
        grid=(DEC_BATCH, B_HEADS),
        in_specs=[
            pl.BlockSpec((n, hd), lambda b, h: (b, COL_QB // hd + h)),
            pl.BlockSpec((n, hd), lambda b, h: (b, COL_KB // hd + h)),
            pl.BlockSpec((n, hd), lambda b, h: (b, COL_VB // hd + h)),
            cache_spec,
            cache_spec,
            pl.BlockSpec((n, hd), lambda b, h: (b, (COL_GATE + A_WIDTH) // hd + h)),
            pl.BlockSpec((None, n, B_WIN_ROWS * GRID_W), lambda b, h: (h, 0, 0)),
        ],
        out_specs=pl.BlockSpec((n, hd), lambda b, h: (b, h)),
        out_shape=jax.ShapeDtypeStruct((N_SAMPLE, B_WIDTH), BF16),
        scratch_shapes=[pltpu.VMEM((B_HEADS, PAST_LEN, hd), BF16),
                        pltpu.VMEM((B_HEADS, PAST_LEN, hd), BF16),
                        pltpu.VMEM((n, B_WIN_ROWS * GRID_W), F32),
                        pltpu.VMEM((n, PAST_LEN), F32),
                        pltpu.VMEM((n, B_WIN_ROWS * GRID_W), BF16),
                        pltpu.VMEM((n, PAST_LEN), BF16)],
        compiler_params=_params(2),
        name="na_attention",
    )(proj, proj, proj, cache_k, cache_v, proj, bias)


def _pool_kernel(u_ref, gate_ref, band_ref, inv_count_ref, w_ref, scale_ref, o_ref, *, seq, rows):
    cb = POOL_BAND_BLOCK
    parts = []
    for i in range(rows // cb):
        lo = i * cb if seq <= cb else max(0, (i - 1) * cb)
        hi = (i + 1) * cb if seq <= cb else min(rows, (i + 2) * cb)
        parts.append(_dot(band_ref[i * cb:(i + 1) * cb, lo:hi], u_ref[lo:hi, :]))
    wsum = jnp.concatenate(parts, axis=0)
    pooled = wsum * inv_count_ref[...] - u_ref[...].astype(F32)
    y = _dot(pooled.astype(BF16), w_ref[...]) * scale_ref[...]
    o_ref[...] = (y * _silu(gate_ref[...].astype(F32))).astype(o_ref.dtype)


def _pool_operators(seq, rows):
    t = jnp.arange(rows)[:, None]
    j = jnp.arange(rows)[None, :]
    same_seq = (t // seq) == (j // seq)
    pos = t % seq
    bands, inv_counts = [], []
    for window in POOL_WINDOWS:
        half = window // 2
        assert half <= POOL_BAND_BLOCK
        bands.append(((j - t >= -half) & (j - t < half) & same_seq).astype(BF16))
        inv_counts.append(1.0 / (jnp.minimum(pos + half, seq) - jnp.maximum(pos - half, 0)).astype(F32))
    return jnp.stack(bands), jnp.stack(inv_counts)


def _pool_mixer(ug, w_grp, scale, seq, rows=1024):
    m = ug.shape[0]
    n_groups = len(POOL_WINDOWS)
    pg = POOL_GROUP
    band, inv_count = _pool_operators(seq, rows)
    return pl.pallas_call(
        functools.partial(_pool_kernel, seq=seq, rows=rows),
        grid=(n_groups, m // rows),
        in_specs=[
            pl.BlockSpec((rows, pg), lambda g, i: (i, g)),
            pl.BlockSpec((rows, pg), lambda g, i: (i, n_groups + g)),
            pl.BlockSpec((None, rows, rows), lambda g, i: (g, 0, 0)),
            pl.BlockSpec((None, rows, 1), lambda g, i: (g, 0, 0)),
            pl.BlockSpec((None, pg, pg), lambda g, i: (g, 0, 0)),
            pl.BlockSpec((1, pg), lambda g, i: (0, g)),
        ],
        out_specs=pl.BlockSpec((rows, pg), lambda g, i: (i, g)),
        out_shape=jax.ShapeDtypeStruct((m, n_groups * pg), BF16),
        compiler_params=_params(2),
        name="pool_mixer",
    )(ug, ug, band, inv_count, w_grp, scale.reshape(1, -1))


def _rope_tables():
    t = jnp.arange(DEC_SEQ)
    quarter = HEAD_DIM // 4
    inv_freq = ROPE_BASE ** (-jnp.arange(quarter, dtype=F32) / quarter)
    ang_r = (t // GRID_W).astype(F32)[:, None] * inv_freq
    ang_c = (t % GRID_W).astype(F32)[:, None] * inv_freq
    ang = jnp.concatenate([ang_r, ang_r, ang_c, ang_c], axis=-1)
    sign = jnp.tile(jnp.concatenate([-jnp.ones((quarter,), F32), jnp.ones((quarter,), F32)]), 2)
    return jnp.cos(ang), jnp.sin(ang) * sign


def kernel(x_prompt, x_sample, c, cache_a_k, cache_a_v, cache_b_k, cache_b_v, c_ctx,
           w_ada, b_ada, norm_g, w_in_attn, a_sink, b_rpb, w_out_attn,
           w_in_pool, w_grp_pool, pool_scale, w_out_pool, final_g):
    d = D_MODEL
    xp = x_prompt.reshape(N_PROMPT, d)
    xs = x_sample.reshape(N_SAMPLE, d)

    w_in0 = w_in_attn[0].astype(BF16)

    cond = jnp.zeros((MOD_ROWS, d), F32).at[:DEC_BATCH].set(c).at[CTX_MOD_ROW].set(c_ctx)
    mod0 = _ada(cond, w_ada, b_ada, 0).reshape(MOD_ROWS, 1, 3 * d)

    hp, hs = _norm_mod(xp, xs, norm_g[0], mod0)

    ka, va, kb, vb = _kv_projection(hp, w_in0)
    qg_tiles = (_col_tiles(COL_QA, A_WIDTH) + _col_tiles(COL_QB, B_WIDTH)
                + _col_tiles(COL_GATE, MIX_WIDTH))
    qg_p, w_in1 = _matmul(hp, w_in0, qg_tiles, BF16, cast_weights=(w_in_pool[0],))
    n_groups = len(POOL_WINDOWS)
    proj_s, w_out0, w_out1, w_grp, mod1_rows = _matmul(
        hs, w_in0, _col_tiles(0, ATTN_IN_WIDTH), BF16,
        cast_weights=(w_out_attn[0], w_out_pool[0],
                      w_grp_pool[0].reshape(n_groups * POOL_GROUP, POOL_GROUP)),
        ada_job=(cond, w_ada, b_ada, 1))
    w_grp = w_grp.reshape(n_groups, POOL_GROUP, POOL_GROUP)
    mod1 = mod1_rows.reshape(MOD_ROWS, 1, 3 * d)

    sink = a_sink[0]
    og_p = _ctx_attention(sink, qg_p, ka, va, kb, vb)

    cos, sin_signed = _rope_tables()
    flat = lambda cache: cache.reshape(DEC_BATCH, -1, HEAD_DIM)
    oa_s = _win_attention(sink, proj_s, flat(cache_a_k), flat(cache_a_v), cos, sin_signed)
    ob_s = _na_attention(proj_s, flat(cache_b_k), flat(cache_b_v), _expand_bias(b_rpb[0]))

    xp1, hu_p, ss_p = _matmul_residual_prenorm([(og_p, 0), (og_p, 1)], w_out0, xp, mod0, True,
                                               norm_g[1], mod1)
    xs1, hu_s, ss_s = _matmul_residual_prenorm([(oa_s, 0), (ob_s, 0)], w_out0, xs, mod0, False,
                                               norm_g[1], mod1)

    shw = _shift_projection(mod1_rows, w_in1).reshape(MOD_ROWS, 1, 2 * d)
    ug_p = _matmul_postnorm(hu_p, ss_p, w_in1, shw, True, BF16)
    ug_s = _matmul_postnorm(hu_s, ss_s, w_in1, shw, False, BF16)
    y_p = _pool_mixer(ug_p, w_grp, pool_scale[0], SEQ)
    y_s = _pool_mixer(ug_s, w_grp, pool_scale[0], DEC_SEQ)
    y_prompt = _matmul_residual_norm([(y_p, 0), (y_p, 1)], w_out1, xp1, mod1, True,
                                     final_g).reshape(BATCH, SEQ, d)
    y_sample = _matmul_residual_norm([(y_s, 0), (y_s, 1)], w_out1, xs1, mod1, False,
                                     final_g).reshape(DEC_BATCH, DEC_SEQ, d)

    kv_shape_a = (BATCH, 1, SEQ, A_KV_HEADS, HEAD_DIM)
    kv_shape_b = (BATCH, 1, SEQ, B_HEADS, HEAD_DIM)
    return (y_prompt, y_sample, ka.reshape(kv_shape_a), va.reshape(kv_shape_a),
            kb.reshape(kv_shape_b), vb.reshape(kv_shape_b))
```

```python
import functools

import jax
import jax.numpy as jnp
from jax import lax
from jax.experimental import pallas as pl
from jax.experimental.pallas import tpu as pltpu

F32 = jnp.float32
BF16 = jnp.bfloat16

D_MODEL = 4096
BATCH = 32
SEQ = 256
DEC_BATCH = 8
DEC_SEQ = 1024
PAST_LEN = 512
GRID_W = 64
HEAD_DIM = 128
A_Q_HEADS = 16
A_KV_HEADS = 4
A_GROUPS = 4
A_HALF_WIN = 128
A_BLOCK = 128
B_HEADS = 16
B_WIN_ROWS = 8
B_WIN_COLS = 16
A_WIDTH = A_Q_HEADS * HEAD_DIM
A_KV_WIDTH = A_KV_HEADS * HEAD_DIM
B_WIDTH = B_HEADS * HEAD_DIM
MIX_WIDTH = A_WIDTH + B_WIDTH
POOL_WINDOWS = (2, 4, 8, 16)
POOL_GROUP = 1024
POOL_BAND_BLOCK = 256
CAST_CHUNK_ROWS = 64
ROPE_BASE = 10000.0
NORM_EPS = 1e-6
NEG_INF = -1e30
ATTN_SCALE = HEAD_DIM ** -0.5
LOG2E = 1.4426950408889634
LOGIT_SCALE = ATTN_SCALE * LOG2E
CTX_CHUNK = 1024
WIN_CHUNK = 512
NA_CHUNK = 1024

N_PROMPT = BATCH * SEQ
N_SAMPLE = DEC_BATCH * DEC_SEQ
MOD_ROWS = 16
CTX_MOD_ROW = DEC_BATCH
GRID_ROWS = DEC_SEQ // GRID_W
N_BIAS_PAIRS = 2 * B_WIN_ROWS - 2
RPB_ROWS = 2 * B_WIN_ROWS - 1
RPB_COLS = 2 * B_WIN_COLS - 1

COL_QA = 0
COL_KA = COL_QA + A_WIDTH
COL_VA = COL_KA + A_KV_WIDTH
COL_QB = COL_VA + A_KV_WIDTH
COL_KB = COL_QB + B_WIDTH
COL_VB = COL_KB + B_WIDTH
COL_GATE = COL_VB + B_WIDTH
ATTN_IN_WIDTH = COL_GATE + MIX_WIDTH
PROJ_TN = 1024


def _col_tiles(col0, width):
    assert col0 % PROJ_TN == 0 and width % PROJ_TN == 0
    return tuple(range(col0 // PROJ_TN, (col0 + width) // PROJ_TN))


def _tile_lookup(tiles):
    runs = []
    for jj, t in enumerate(tiles):
        if not runs or runs[-1][1] != t - jj:
            runs.append((jj, t - jj))

    def lookup(j):
        off = runs[0][1]
        for start, o in runs[1:]:
            off = jnp.where(j >= start, o, off)
        return j + off
    return lookup

VMEM_LIMIT = 48 * 1024 * 1024
VMEM_LIMIT_ROWS = 56 * 1024 * 1024


def _params(n_grid_dims):
    return pltpu.CompilerParams(
        dimension_semantics=("arbitrary",) * n_grid_dims,
        vmem_limit_bytes=VMEM_LIMIT,
    )


def _silu(x):
    return 0.5 * x * (1.0 + jnp.tanh(0.5 * x))


def _dot_nt(a, b):
    return lax.dot_general(a, b, (((1,), (1,)), ((), ())), preferred_element_type=F32)


def _dot(a, b):
    return jnp.dot(a, b, preferred_element_type=F32)


def _ada_kernel(cond_ref, w_ref, b_ref, o_ref):
    a = _silu(cond_ref[...]).astype(BF16)
    o_ref[...] = _dot(a, w_ref[...].astype(BF16)) + b_ref[...]


def _ada(cond, w_ada, b_ada, layer, tn=512):
    depth, d, n = w_ada.shape
    return pl.pallas_call(
        _ada_kernel,
        grid=(n // tn,),
        in_specs=[
            pl.BlockSpec((MOD_ROWS, d), lambda j: (0, 0)),
            pl.BlockSpec((None, d, tn), lambda j: (layer, 0, j)),
            pl.BlockSpec((None, 1, tn), lambda j: (layer, 0, j)),
        ],
        out_specs=pl.BlockSpec((MOD_ROWS, tn), lambda j: (0, j)),
        out_shape=jax.ShapeDtypeStruct((MOD_ROWS, n), F32),
        compiler_params=_params(1),
        name="ada",
    )(cond, w_ada, b_ada.reshape(depth, 1, n))


def _mod_row_fn(is_prompt, tm):
    if is_prompt:
        return lambda i: CTX_MOD_ROW
    return lambda i: (i * tm) // DEC_SEQ


def _norm_mod_kernel(xp_ref, xs_ref, g_ref, sh_ref, sc_ref, hp_ref, hs_ref, *, n_prompt_tiles):
    def norm_mod(x_ref, o_ref):
        x = x_ref[...]
        ms = jnp.mean(x * x, axis=-1, keepdims=True)
        y = x * lax.rsqrt(ms + NORM_EPS) * g_ref[...]
        o_ref[...] = (y * (1.0 + sc_ref[...]) + sh_ref[...]).astype(o_ref.dtype)

    i = pl.program_id(0)

    @pl.when(i < n_prompt_tiles)
    def _():
        norm_mod(xp_ref, hp_ref)

    @pl.when(i >= n_prompt_tiles)
    def _():
        norm_mod(xs_ref, hs_ref)


def _norm_mod(xp, xs, gain, mod, tm=256):
    d = xp.shape[1]
    n_p, n_s = xp.shape[0] // tm, xs.shape[0] // tm
    p_blk = lambda i: (jnp.minimum(i, n_p - 1), 0)
    s_blk = lambda i: (jnp.maximum(i - n_p, 0), 0)
    row = lambda i: jnp.where(i < n_p, CTX_MOD_ROW, (jnp.maximum(i - n_p, 0) * tm) // DEC_SEQ)
    return pl.pallas_call(
        functools.partial(_norm_mod_kernel, n_prompt_tiles=n_p),
        grid=(n_p + n_s,),
        in_specs=[
            pl.BlockSpec((tm, d), p_blk),
            pl.BlockSpec((tm, d), s_blk),
            pl.BlockSpec((1, d), lambda i: (0, 0)),
            pl.BlockSpec((None, 1, d), lambda i: (row(i), 0, 0)),
            pl.BlockSpec((None, 1, d), lambda i: (row(i), 0, 1)),
        ],
        out_specs=[pl.BlockSpec((tm, d), p_blk), pl.BlockSpec((tm, d), s_blk)],
        out_shape=[jax.ShapeDtypeStruct(xp.shape, BF16), jax.ShapeDtypeStruct(xs.shape, BF16)],
        compiler_params=_params(1),
        name="norm_mod",
    )(xp, xs, gain.reshape(1, d), mod, mod)


def _mm_kernel(a_ref, w_ref, *refs, n_casts):
    cast_in, o_ref, cast_out = refs[:n_casts], refs[n_casts], refs[n_casts + 1:]
    o_ref[...] = _dot(a_ref[...], w_ref[...]).astype(o_ref.dtype)
    for src, dst in zip(cast_in, cast_out):
        dst[...] = src[...].astype(dst.dtype)


def _matmul(a, w, w_tiles, out_dtype, cast_weights=()):
    m, k = a.shape
    tn = PROJ_TN
    tm = 1024 if out_dtype == BF16 else 512
    assert m % tm == 0
    w_tile = _tile_lookup(w_tiles)
    ncols = len(w_tiles) * tn
    n_i = m // tm
    n_steps = len(w_tiles) * n_i
    in_specs = [pl.BlockSpec((tm, k), lambda j, i: (i, 0)),
                pl.BlockSpec((k, tn), lambda j, i: (0, w_tile(j)))]
    out_specs = [pl.BlockSpec((tm, tn), lambda j, i: (i, j))]
    out_shape = [jax.ShapeDtypeStruct((m, ncols), out_dtype)]
    for cw in cast_weights:
        rows, cols = cw.shape
        chunk = CAST_CHUNK_ROWS
        n_chunks = rows // chunk
        assert rows % chunk == 0 and n_chunks <= n_steps
        spec = pl.BlockSpec((chunk, cols),
                            lambda j, i, n_chunks=n_chunks: (jnp.minimum(j * n_i + i, n_chunks - 1), 0))
        in_specs.append(spec)
        out_specs.append(spec)
        out_shape.append(jax.ShapeDtypeStruct((rows, cols), BF16))
    outs = pl.pallas_call(
        functools.partial(_mm_kernel, n_casts=len(cast_weights)),
        grid=(ncols // tn, n_i),
        in_specs=in_specs,
        out_specs=out_specs,
        out_shape=out_shape,
        compiler_params=pltpu.CompilerParams(
            dimension_semantics=("arbitrary", "arbitrary"),
            vmem_limit_bytes=VMEM_LIMIT_ROWS if cast_weights else VMEM_LIMIT),
        name="proj",
    )(a, w, *cast_weights)
    return outs if cast_weights else outs[0]


def _kv_proj_kernel(a_ref, w_ref, ka_ref, va_ref, kb_ref, vb_ref, *, kb_tile0, vb_tile0):
    j = pl.program_id(0)
    acc = _dot(a_ref[...], w_ref[...])

    @pl.when(j < kb_tile0)
    def _():
        ka_ref[...] = acc[:, :A_KV_WIDTH]
        va_ref[...] = acc[:, A_KV_WIDTH:]

    @pl.when((j >= kb_tile0) & (j < vb_tile0))
    def _():
        kb_ref[...] = acc

    @pl.when(j >= vb_tile0)
    def _():
        vb_ref[...] = acc


def _kv_projection(a, w, tm=512):
    m, k = a.shape
    tn = PROJ_TN
    assert COL_VA == COL_KA + A_KV_WIDTH and 2 * A_KV_WIDTH == tn
    w_tiles = _col_tiles(COL_KA, tn) + _col_tiles(COL_KB, B_WIDTH) + _col_tiles(COL_VB, B_WIDTH)
    w_tile = _tile_lookup(w_tiles)
    kb_tile0 = 1
    vb_tile0 = kb_tile0 + B_WIDTH // tn
    end_tile = len(w_tiles)
    n_i = m // tm
    last = n_i - 1

    def parked(j, i, t0, t1):
        row = jnp.where(j < t0, 0, jnp.where(j < t1, i, last))
        return row, jnp.clip(j - t0, 0, t1 - t0 - 1)

    return pl.pallas_call(
        functools.partial(_kv_proj_kernel, kb_tile0=kb_tile0, vb_tile0=vb_tile0),
        grid=(end_tile, n_i),
        in_specs=[pl.BlockSpec((tm, k), lambda j, i: (i, 0)),
                  pl.BlockSpec((k, tn), lambda j, i: (0, w_tile(j)))],
        out_specs=[pl.BlockSpec((tm, A_KV_WIDTH), lambda j, i: parked(j, i, 0, kb_tile0)),
                   pl.BlockSpec((tm, A_KV_WIDTH), lambda j, i: parked(j, i, 0, kb_tile0)),
                   pl.BlockSpec((tm, tn), lambda j, i: parked(j, i, kb_tile0, vb_tile0)),
                   pl.BlockSpec((tm, tn), lambda j, i: parked(j, i, vb_tile0, end_tile))],
        out_shape=[jax.ShapeDtypeStruct((m, A_KV_WIDTH), F32),
                   jax.ShapeDtypeStruct((m, A_KV_WIDTH), F32),
                   jax.ShapeDtypeStruct((m, B_WIDTH), F32),
                   jax.ShapeDtypeStruct((m, B_WIDTH), F32)],
        compiler_params=_params(2),
        name="kv_proj",
    )(a, w)


def _dot_halves(a1_ref, a2_ref, w_ref):
    kh = a1_ref.shape[1]
    return _dot(a1_ref[...], w_ref[:kh, :]) + _dot(a2_ref[...], w_ref[kh:, :])


def _mm_res_prenorm_kernel(a1_ref, a2_ref, w_ref, x_ref, g_ref, gain_ref, sc_ref,
                           x1_ref, hu_ref, ss_ref):
    x1 = x_ref[...] + g_ref[...] * _dot_halves(a1_ref, a2_ref, w_ref)
    x1_ref[...] = x1
    hu_ref[...] = (x1 * (gain_ref[...] * (1.0 + sc_ref[...]))).astype(hu_ref.dtype)
    ss_ref[...] = jnp.sum(x1 * x1, axis=-1, keepdims=True)


def _matmul_residual_prenorm(a_halves, w, x, mod, is_prompt, gain, next_mod, tm=512):
    (a1, c1), (a2, c2) = a_halves
    m = a1.shape[0]
    k, n = w.shape
    tn = PROJ_TN
    n_tiles = n // tn
    row = _mod_row_fn(is_prompt, tm)
    scale_block0, gate_block0 = n_tiles, 2 * n_tiles
    tile = pl.BlockSpec((tm, tn), lambda j, i: (i, j))
    return pl.pallas_call(
        _mm_res_prenorm_kernel,
        grid=(n_tiles, m // tm),
        in_specs=[pl.BlockSpec((tm, k // 2), lambda j, i: (i, c1)),
                  pl.BlockSpec((tm, k // 2), lambda j, i: (i, c2)),
                  pl.BlockSpec((k, tn), lambda j, i: (0, j)),
                  tile,
                  pl.BlockSpec((None, 1, tn), lambda j, i: (row(i), 0, gate_block0 + j)),
                  pl.BlockSpec((1, tn), lambda j, i: (0, j)),
                  pl.BlockSpec((None, 1, tn), lambda j, i: (row(i), 0, scale_block0 + j))],
        out_specs=[tile, tile, pl.BlockSpec((None, tm, 1), lambda j, i: (j, i, 0))],
        out_shape=[jax.ShapeDtypeStruct((m, n), F32), jax.ShapeDtypeStruct((m, n), BF16),
                   jax.ShapeDtypeStruct((n_tiles, m, 1), F32)],
        compiler_params=_params(2),
        name="proj_residual_prenorm",
    )(a1, a2, w, x, mod, gain.reshape(1, n), next_mod)


def _shift_proj_kernel(sh_ref, w_ref, o_ref):
    o_ref[...] = _dot(sh_ref[...].astype(BF16), w_ref[...])


def _shift_projection(mod2d, w):
    k, n = w.shape
    tn = PROJ_TN
    return pl.pallas_call(
        _shift_proj_kernel,
        grid=(n // tn,),
        in_specs=[pl.BlockSpec((MOD_ROWS, k), lambda j: (0, 0)),
                  pl.BlockSpec((k, tn), lambda j: (0, j))],
        out_specs=pl.BlockSpec((MOD_ROWS, tn), lambda j: (0, j)),
        out_shape=jax.ShapeDtypeStruct((MOD_ROWS, n), F32),
        compiler_params=_params(1),
        name="shift_proj",
    )(mod2d, w)


def _mm_postnorm_kernel(a_ref, w_ref, ss_ref, shw_ref, o_ref, *, width):
    acc = _dot(a_ref[...], w_ref[...])
    ss = ss_ref[0]
    for t in range(1, ss_ref.shape[0]):
        ss = ss + ss_ref[t]
    r = lax.rsqrt(ss / width + NORM_EPS)
    o_ref[...] = (acc * r + shw_ref[...]).astype(o_ref.dtype)


def _matmul_postnorm(hu, ss, w, shw, is_prompt, out_dtype, tm=1024):
    m, k = hu.shape
    n = w.shape[1]
    tn = PROJ_TN
    row = _mod_row_fn(is_prompt, tm)
    return pl.pallas_call(
        functools.partial(_mm_postnorm_kernel, width=k),
        grid=(n // tn, m // tm),
        in_specs=[pl.BlockSpec((tm, k), lambda j, i: (i, 0)),
                  pl.BlockSpec((k, tn), lambda j, i: (0, j)),
                  pl.BlockSpec((ss.shape[0], tm, 1), lambda j, i: (0, i, 0)),
                  pl.BlockSpec((None, 1, tn), lambda j, i: (row(i), 0, j))],
        out_specs=pl.BlockSpec((tm, tn), lambda j, i: (i, j)),
        out_shape=jax.ShapeDtypeStruct((m, n), out_dtype),
        compiler_params=pltpu.CompilerParams(dimension_semantics=("arbitrary", "arbitrary"),
                                             vmem_limit_bytes=VMEM_LIMIT_ROWS),
        name="proj_postnorm",
    )(hu, w, ss, shw)


def _mm_res_norm_kernel(a1_ref, a2_ref, w_ref, x_ref, g_ref, gain_ref, y_ref, *, n_tiles, tn):
    j = pl.program_id(1)
    x2 = x_ref[...] + g_ref[...] * _dot_halves(a1_ref, a2_ref, w_ref)
    for jj in range(n_tiles):
        @pl.when(j == jj)
        def _(jj=jj):
            y_ref[:, jj * tn:(jj + 1) * tn] = x2

    @pl.when(j == n_tiles - 1)
    def _():
        ss = None
        for jj in range(n_tiles):
            t = y_ref[:, jj * tn:(jj + 1) * tn]
            part = jnp.sum(t * t, axis=-1, keepdims=True)
            ss = part if ss is None else ss + part
        r = lax.rsqrt(ss / (n_tiles * tn) + NORM_EPS)
        for jj in range(n_tiles):
            cols = slice(jj * tn, (jj + 1) * tn)
            y_ref[:, cols] = y_ref[:, cols] * r * gain_ref[:, cols]


def _matmul_residual_norm(a_halves, w, x, mod, is_prompt, gain, tm=512):
    (a1, c1), (a2, c2) = a_halves
    m = a1.shape[0]
    k, n = w.shape
    tn = PROJ_TN
    n_tiles = n // tn
    row = _mod_row_fn(is_prompt, tm)
    gate_block0 = 2 * n_tiles
    return pl.pallas_call(
        functools.partial(_mm_res_norm_kernel, n_tiles=n_tiles, tn=tn),
        grid=(m // tm, n_tiles),
        in_specs=[pl.BlockSpec((tm, k // 2), lambda i, j: (i, c1)),
                  pl.BlockSpec((tm, k // 2), lambda i, j: (i, c2)),
                  pl.BlockSpec((k, tn), lambda i, j: (0, j)),
                  pl.BlockSpec((tm, tn), lambda i, j: (i, j)),
                  pl.BlockSpec((None, 1, tn), lambda i, j: (row(i), 0, gate_block0 + j)),
                  pl.BlockSpec((1, n), lambda i, j: (0, 0))],
        out_specs=pl.BlockSpec((tm, n), lambda i, j: (i, 0)),
        out_shape=jax.ShapeDtypeStruct((m, n), F32),
        compiler_params=pltpu.CompilerParams(dimension_semantics=("arbitrary", "arbitrary"),
                                             vmem_limit_bytes=VMEM_LIMIT_ROWS),
        name="proj_residual_norm",
    )(a1, a2, w, x, mod, gain.reshape(1, n))


def _chunked_softmax(n_rows, chunk, loads, stores, sink2=None):
    inv = []
    for c in range(n_rows // chunk):
        rows = slice(c * chunk, (c + 1) * chunk)
        ts = [load(rows) for load in loads]
        m = functools.reduce(jnp.maximum, [jnp.max(t, axis=-1, keepdims=True) for t in ts])
        if sink2 is not None:
            m = jnp.maximum(m, sink2[rows])
        es = [jnp.exp2(t - m) for t in ts]
        l = functools.reduce(jnp.add, [jnp.sum(e, axis=-1, keepdims=True) for e in es])
        if sink2 is not None:
            l = l + jnp.exp2(sink2[rows] - m)
        for store, e in zip(stores, es):
            store(rows, e.astype(BF16))
        inv.append(1.0 / l)
    return jnp.concatenate(inv, axis=0)


def _ctx_attn_kernel(sink_ref, qg_ref, ka_ref, va_ref, kb_ref, vb_ref, o_ref, s_scr, e_scr):
    n = SEQ
    gate0 = MIX_WIDTH
    n_stack = A_GROUPS

    def softmax(sink2):
        def store(rows, e):
            e_scr[rows, :] = e
        return _chunked_softmax(n_stack * n, CTX_CHUNK, [lambda rows: s_scr[rows, :] * LOGIT_SCALE],
                                [store], sink2)

    def emit(o, g, out_col):
        cols = slice(out_col, out_col + HEAD_DIM)
        gate = qg_ref[:, gate0 + out_col:gate0 + out_col + HEAD_DIM].astype(F32)
        o_ref[:, cols] = (o[g * n:(g + 1) * n] * _silu(gate)).astype(o_ref.dtype)

    for kv in range(A_KV_HEADS):
        cols = slice(kv * HEAD_DIM, (kv + 1) * HEAD_DIM)
        heads = [kv * A_GROUPS + g for g in range(A_GROUPS)]
        q = jnp.concatenate(
            [qg_ref[:, h * HEAD_DIM:(h + 1) * HEAD_DIM] for h in heads], axis=0)
        sink2 = jnp.concatenate(
            [jnp.full((n, 1), sink_ref[h] * LOG2E, F32) for h in heads], axis=0)
        s_scr[...] = _dot_nt(q, ka_ref[:, cols].astype(BF16))
        inv = softmax(sink2)
        o = _dot(e_scr[...], va_ref[:, cols].astype(BF16)) * inv
        for g, h in enumerate(heads):
            emit(o, g, h * HEAD_DIM)
    for h0 in range(0, B_HEADS, n_stack):
        heads = range(h0, h0 + n_stack)
        for g, h in enumerate(heads):
            cols = slice(h * HEAD_DIM, (h + 1) * HEAD_DIM)
            q = qg_ref[:, A_WIDTH + h * HEAD_DIM:A_WIDTH + (h + 1) * HEAD_DIM]
            s_scr[g * n:(g + 1) * n, :] = _dot_nt(q, kb_ref[:, cols].astype(BF16))
        inv = softmax(None)
        o = jnp.concatenate(
            [_dot(e_scr[g * n:(g + 1) * n, :],
                  vb_ref[:, h * HEAD_DIM:(h + 1) * HEAD_DIM].astype(BF16))
             for g, h in enumerate(heads)], axis=0) * inv
        for g, h in enumerate(heads):
            emit(o, g, A_WIDTH + h * HEAD_DIM)


def _ctx_attention(sink, qg, ka, va, kb, vb):
    stack_rows = A_GROUPS * SEQ
    return pl.pallas_call(
        _ctx_attn_kernel,
        grid=(BATCH,),
        in_specs=[
            pl.BlockSpec(memory_space=pltpu.SMEM),
            pl.BlockSpec((SEQ, 2 * MIX_WIDTH), lambda b: (b, 0)),
            pl.BlockSpec((SEQ, A_KV_WIDTH), lambda b: (b, 0)),
            pl.BlockSpec((SEQ, A_KV_WIDTH), lambda b: (b, 0)),
            pl.BlockSpec((SEQ, B_WIDTH), lambda b: (b, 0)),
            pl.BlockSpec((SEQ, B_WIDTH), lambda b: (b, 0)),
        ],
        out_specs=pl.BlockSpec((SEQ, MIX_WIDTH), lambda b: (b, 0)),
        out_shape=jax.ShapeDtypeStruct((N_PROMPT, MIX_WIDTH), BF16),
        scratch_shapes=[pltpu.VMEM((stack_rows, SEQ), F32),
                        pltpu.VMEM((stack_rows, SEQ), BF16)],
        compiler_params=_params(1),
        name="ctx_attention",
    )(sink, qg, ka, va, kb, vb)


def _split_cache_heads(c_ref, scr, n_heads):
    for h in range(n_heads):
        scr[h] = c_ref[pl.ds(h, PAST_LEN, stride=n_heads), :].astype(scr.dtype)


def _win_attn_kernel(sink_ref, q_ref, k_ref, v_ref, ck_ref, cv_ref, gate_ref,
                     cos_ref, sin_ref, o_ref, qs_ref, ks_ref, ck_scr, cv_scr,
                     band_scr, sw_scr, sc_scr, ew_scr, ec_scr):
    kv = pl.program_id(1)
    n = DEC_SEQ

    @pl.when(kv == 0)
    def _():
        _split_cache_heads(ck_ref, ck_scr, A_KV_HEADS)
        _split_cache_heads(cv_ref, cv_scr, A_KV_HEADS)

    cos = cos_ref[...]
    sin = sin_ref[...]
    lane = lax.broadcasted_iota(jnp.int32, (n, HEAD_DIM), 1)
    first_quarter = (lane % (HEAD_DIM // 2)) < (HEAD_DIM // 4)

    def rope(x):
        rot = jnp.where(first_quarter,
                        pltpu.roll(x, HEAD_DIM - HEAD_DIM // 4, 1),
                        pltpu.roll(x, HEAD_DIM // 4, 1))
        return x * cos + rot * sin

    ks_ref[...] = rope(k_ref[...].astype(F32)).astype(BF16)
    for g in range(A_GROUPS):
        qs_ref[g] = (rope(q_ref[:, g * HEAD_DIM:(g + 1) * HEAD_DIM].astype(F32))
                     * LOGIT_SCALE).astype(BF16)

    ck = ck_scr[kv]
    cv = cv_scr[kv]
    rows = A_GROUPS * A_BLOCK
    sink2 = jnp.concatenate(
        [jnp.full((A_BLOCK, 1), sink_ref[kv * A_GROUPS + g] * LOG2E, F32)
         for g in range(A_GROUPS)], axis=0)
    qi = lax.broadcasted_iota(jnp.int32, (rows, 3 * A_BLOCK), 0) % A_BLOCK
    rel = lax.broadcasted_iota(jnp.int32, (rows, 3 * A_BLOCK), 1) - A_BLOCK - qi
    band_scr[...] = jnp.where(jnp.abs(rel) <= A_HALF_WIN, 0.0, NEG_INF)
    for blk in range(n // A_BLOCK):
        r0 = blk * A_BLOCK
        lo = max(0, r0 - A_BLOCK)
        hi = min(n, r0 + 2 * A_BLOCK)
        span = hi - lo
        b0 = lo - (r0 - A_BLOCK)
        q = jnp.concatenate([qs_ref[g, r0:r0 + A_BLOCK, :] for g in range(A_GROUPS)], axis=0)
        sw_scr[:, :span] = _dot_nt(q, ks_ref[lo:hi, :])
        sc_scr[...] = _dot_nt(q, ck)

        def load_w(r, span=span, b0=b0):
            return sw_scr[r, :span] + band_scr[r, b0:b0 + span]

        def store_w(r, e, span=span):
            ew_scr[r, :span] = e

        def store_c(r, e):
            ec_scr[r, :] = e

        inv = _chunked_softmax(rows, WIN_CHUNK, [load_w, lambda r: sc_scr[r, :]],
                               [store_w, store_c], sink2)
        o = (_dot(ew_scr[:, :span], v_ref[lo:hi, :]) + _dot(ec_scr[...], cv)) * inv
        for g in range(A_GROUPS):
            hc = slice(g * HEAD_DIM, (g + 1) * HEAD_DIM)
            gate = gate_ref[r0:r0 + A_BLOCK, hc].astype(F32)
            o_ref[r0:r0 + A_BLOCK, hc] = (
                o[g * A_BLOCK:(g + 1) * A_BLOCK] * _silu(gate)).astype(o_ref.dtype)


def _win_attention(sink, proj, cache_k, cache_v, cos, sin_signed):
    gw = A_GROUPS * HEAD_DIM
    stack_rows = A_GROUPS * A_BLOCK
    n = DEC_SEQ
    cache_spec = pl.BlockSpec((None, PAST_LEN * A_KV_HEADS, HEAD_DIM), lambda b, k: (b, 0, 0))
    return pl.pallas_call(
        _win_attn_kernel,
        grid=(DEC_BATCH, A_KV_HEADS),
        in_specs=[
            pl.BlockSpec(memory_space=pltpu.SMEM),
            pl.BlockSpec((n, gw), lambda b, k: (b, COL_QA // gw + k)),
            pl.BlockSpec((n, HEAD_DIM), lambda b, k: (b, COL_KA // HEAD_DIM + k)),
            pl.BlockSpec((n, HEAD_DIM), lambda b, k: (b, COL_VA // HEAD_DIM + k)),
            cache_spec,
            cache_spec,
            pl.BlockSpec((n, gw), lambda b, k: (b, COL_GATE // gw + k)),
            pl.BlockSpec((n, HEAD_DIM), lambda b, k: (0, 0)),
            pl.BlockSpec((n, HEAD_DIM), lambda b, k: (0, 0)),
        ],
        out_specs=pl.BlockSpec((n, gw), lambda b, k: (b, k)),
        out_shape=jax.ShapeDtypeStruct((N_SAMPLE, A_WIDTH), BF16),
        scratch_shapes=[pltpu.VMEM((A_GROUPS, n, HEAD_DIM), BF16),
                        pltpu.VMEM((n, HEAD_DIM), BF16),
                        pltpu.VMEM((A_KV_HEADS, PAST_LEN, HEAD_DIM), BF16),
                        pltpu.VMEM((A_KV_HEADS, PAST_LEN, HEAD_DIM), BF16),
                        pltpu.VMEM((stack_rows, 3 * A_BLOCK), F32),
                        pltpu.VMEM((stack_rows, 3 * A_BLOCK), F32),
                        pltpu.VMEM((stack_rows, PAST_LEN), F32),
                        pltpu.VMEM((stack_rows, 3 * A_BLOCK), BF16),
                        pltpu.VMEM((stack_rows, PAST_LEN), BF16)],
        compiler_params=_params(2),
        name="win_attention",
    )(sink, proj, proj, proj, cache_k, cache_v, proj, cos, sin_signed)


def _bias_kernel(rpb_ref, o_ref):
    h = pl.program_id(0)
    shape = (GRID_W, 2 * GRID_W)
    c = lax.broadcasted_iota(jnp.int32, shape, 0)
    j2 = lax.broadcasted_iota(jnp.int32, shape, 1)
    kc = j2 % GRID_W
    second = j2 >= GRID_W
    col_start = jnp.clip(c - B_WIN_COLS // 2, 0, GRID_W - B_WIN_COLS)
    ok = (kc >= col_start) & (kc < col_start + B_WIN_COLS)
    dc = kc - c + B_WIN_COLS - 1
    base = h * (RPB_ROWS * RPB_COLS)
    pair_scr = []
    for i in range(N_BIAS_PAIRS):
        acc = jnp.full(shape, NEG_INF, F32)
        for d in range(RPB_COLS):
            val = jnp.where(second, rpb_ref[base + (i + 1) * RPB_COLS + d],
                            rpb_ref[base + i * RPB_COLS + d]) * LOG2E
            acc = jnp.where(ok & (dc == d), val, acc)
        pair_scr.append(acc)
    for r in range(GRID_ROWS):
        dr0 = _na_key_row0(r) - r + B_WIN_ROWS - 1
        for i in range(B_WIN_ROWS // 2):
            o_ref[r * GRID_W:(r + 1) * GRID_W, i * 2 * GRID_W:(i + 1) * 2 * GRID_W] = (
                pair_scr[dr0 + 2 * i])


def _expand_bias(rpb):
    kw = B_WIN_ROWS * GRID_W
    return pl.pallas_call(
        _bias_kernel,
        grid=(B_HEADS,),
        in_specs=[pl.BlockSpec(memory_space=pltpu.SMEM)],
        out_specs=pl.BlockSpec((None, DEC_SEQ, kw), lambda h: (h, 0, 0)),
        out_shape=jax.ShapeDtypeStruct((B_HEADS, DEC_SEQ, kw), F32),
        compiler_params=_params(1),
        name="expand_bias",
    )(rpb.reshape(-1))


def _na_key_row0(r):
    return min(max(r - B_WIN_ROWS // 2, 0), GRID_ROWS - B_WIN_ROWS)


def _na_row_groups():
    groups = []
    for r in range(GRID_ROWS):
        rs = _na_key_row0(r)
        if groups and groups[-1][2] == rs:
            groups[-1] = (groups[-1][0], r + 1, rs)
        else:
            groups.append((r, r + 1, rs))
    return groups


def _na_attn_kernel(q_ref, k_ref, v_ref, ck_ref, cv_ref, gate_ref, bias_ref, o_ref,
                    ck_scr, cv_scr, sn_scr, sc_scr, en_scr, ec_scr):
    kw = B_WIN_ROWS * GRID_W
    h = pl.program_id(1)

    @pl.when(h == 0)
    def _():
        _split_cache_heads(ck_ref, ck_scr, B_HEADS)
        _split_cache_heads(cv_ref, cv_scr, B_HEADS)

    groups = _na_row_groups()
    sc_scr[...] = _dot_nt(q_ref[...], ck_scr[h])
    for r0, r1, rs in groups:
        sn_scr[r0 * GRID_W:r1 * GRID_W, :] = _dot_nt(
            q_ref[r0 * GRID_W:r1 * GRID_W, :], k_ref[rs * GRID_W:rs * GRID_W + kw, :])

    def store_n(r, e):
        en_scr[r, :] = e

    def store_c(r, e):
        ec_scr[r, :] = e

    inv = _chunked_softmax(
        DEC_SEQ, NA_CHUNK,
        [lambda r: sn_scr[r, :] * LOGIT_SCALE + bias_ref[r, :], lambda r: sc_scr[r, :] * LOGIT_SCALE],
        [store_n, store_c])
    o_n = jnp.concatenate(
        [_dot(en_scr[r0 * GRID_W:r1 * GRID_W, :], v_ref[rs * GRID_W:rs * GRID_W + kw, :])
         for r0, r1, rs in groups], axis=0)
    o = (o_n + _dot(ec_scr[...], cv_scr[h])) * inv
    o_ref[...] = (o * _silu(gate_ref[...].astype(F32))).astype(o_ref.dtype)


def _na_attention(proj, cache_k, cache_v, bias):
    n = DEC_SEQ
    hd = HEAD_DIM
    cache_spec = pl.BlockSpec((None, PAST_LEN * B_HEADS, hd), lambda b, h: (b, 0, 0))
    return pl.pallas_call(
        _na_attn_kernel,
        grid=(DEC_BATCH, B_HEADS),
        in_specs=[
            pl.BlockSpec((n, hd), lambda b, h: (b, COL_QB // hd + h)),
            pl.BlockSpec((n, hd), lambda b, h: (b, COL_KB // hd + h)),
            pl.BlockSpec((n, hd), lambda b, h: (b, COL_VB // hd + h)),
            cache_spec,
            cache_spec,
            pl.BlockSpec((n, hd), lambda b, h: (b, (COL_GATE + A_WIDTH) // hd + h)),
            pl.BlockSpec((None, n, B_WIN_ROWS * GRID_W), lambda b, h: (h, 0, 0)),
        ],
        out_specs=pl.BlockSpec((n, hd), lambda b, h: (b, h)),
        out_shape=jax.ShapeDtypeStruct((N_SAMPLE, B_WIDTH), BF16),
        scratch_shapes=[pltpu.VMEM((B_HEADS, PAST_LEN, hd), BF16),
                        pltpu.VMEM((B_HEADS, PAST_LEN, hd), BF16),
                        pltpu.VMEM((n, B_WIN_ROWS * GRID_W), F32),
                        pltpu.VMEM((n, PAST_LEN), F32),
                        pltpu.VMEM((n, B_WIN_ROWS * GRID_W), BF16),
                        pltpu.VMEM((n, PAST_LEN), BF16)],
        compiler_params=_params(2),
        name="na_attention",
    )(proj, proj, proj, cache_k, cache_v, proj, bias)


def _pool_kernel(u_ref, gate_ref, band_ref, inv_count_ref, w_ref, scale_ref, o_ref, *, seq, rows):
    cb = POOL_BAND_BLOCK
    parts = []
    for i in range(rows // cb):
        lo = i * cb if seq <= cb else max(0, (i - 1) * cb)
        hi = (i + 1) * cb if seq <= cb else min(rows, (i + 2) * cb)
        parts.append(_dot(band_ref[i * cb:(i + 1) * cb, lo:hi], u_ref[lo:hi, :]))
    wsum = jnp.concatenate(parts, axis=0)
    pooled = wsum * inv_count_ref[...] - u_ref[...].astype(F32)
    y = _dot(pooled.astype(BF16), w_ref[...]) * scale_ref[...]
    o_ref[...] = (y * _silu(gate_ref[...].astype(F32))).astype(o_ref.dtype)


def _pool_operators(seq, rows):
    t = jnp.arange(rows)[:, None]
    j = jnp.arange(rows)[None, :]
    same_seq = (t // seq) == (j // seq)
    pos = t % seq
    bands, inv_counts = [], []
    for window in POOL_WINDOWS:
        half = window // 2
        assert half <= POOL_BAND_BLOCK
        bands.append(((j - t >= -half) & (j - t < half) & same_seq).astype(BF16))
        inv_counts.append(1.0 / (jnp.minimum(pos + half, seq) - jnp.maximum(pos - half, 0)).astype(F32))
    return jnp.stack(bands), jnp.stack(inv_counts)


def _pool_mixer(ug, w_grp, scale, seq, rows=1024):
    m = ug.shape[0]
    n_groups = len(POOL_WINDOWS)
    pg = POOL_GROUP
    band, inv_count = _pool_operators(seq, rows)
    return pl.pallas_call(
        functools.partial(_pool_kernel, seq=seq, rows=rows),
        grid=(n_groups, m // rows),
        in_specs=[
            pl.BlockSpec((rows, pg), lambda g, i: (i, g)),
            pl.BlockSpec((rows, pg), lambda g, i: (i, n_groups + g)),
            pl.BlockSpec((None, rows, rows), lambda g, i: (g, 0, 0)),
            pl.BlockSpec((None, rows, 1), lambda g, i: (g, 0, 0)),
            pl.BlockSpec((None, pg, pg), lambda g, i: (g, 0, 0)),
            pl.BlockSpec((1, pg), lambda g, i: (0, g)),
        ],
        out_specs=pl.BlockSpec((rows, pg), lambda g, i: (i, g)),
        out_shape=jax.ShapeDtypeStruct((m, n_groups * pg), BF16),
        compiler_params=_params(2),
        name="pool_mixer",
    )(ug, ug, band, inv_count, w_grp, scale.reshape(1, -1))


def _rope_tables():
    t = jnp.arange(DEC_SEQ)
    quarter = HEAD_DIM // 4
    inv_freq = ROPE_BASE ** (-jnp.arange(quarter, dtype=F32) / quarter)
    ang_r = (t // GRID_W).astype(F32)[:, None] * inv_freq
    ang_c = (t % GRID_W).astype(F32)[:, None] * inv_freq
    ang = jnp.concatenate([ang_r, ang_r, ang_c, ang_c], axis=-1)
    sign = jnp.tile(jnp.concatenate([-jnp.ones((quarter,), F32), jnp.ones((quarter,), F32)]), 2)
    return jnp.cos(ang), jnp.sin(ang) * sign


def kernel(x_prompt, x_sample, c, cache_a_k, cache_a_v, cache_b_k, cache_b_v, c_ctx,
           w_ada, b_ada, norm_g, w_in_attn, a_sink, b_rpb, w_out_attn,
           w_in_pool, w_grp_pool, pool_scale, w_out_pool, final_g):
    d = D_MODEL
    xp = x_prompt.reshape(N_PROMPT, d)
    xs = x_sample.reshape(N_SAMPLE, d)

    w_in0 = w_in_attn[0].astype(BF16)

    cond = jnp.zeros((MOD_ROWS, d), F32).at[:DEC_BATCH].set(c).at[CTX_MOD_ROW].set(c_ctx)
    mod0 = _ada(cond, w_ada, b_ada, 0).reshape(MOD_ROWS, 1, 3 * d)

    hp, hs = _norm_mod(xp, xs, norm_g[0], mod0)

    ka, va, kb, vb = _kv_projection(hp, w_in0)
    qg_tiles = (_col_tiles(COL_QA, A_WIDTH) + _col_tiles(COL_QB, B_WIDTH)
                + _col_tiles(COL_GATE, MIX_WIDTH))
    qg_p, w_in1 = _matmul(hp, w_in0, qg_tiles, BF16, cast_weights=(w_in_pool[0],))
    n_groups = len(POOL_WINDOWS)
    proj_s, w_out0, w_out1, w_grp = _matmul(
        hs, w_in0, _col_tiles(0, ATTN_IN_WIDTH), BF16,
        cast_weights=(w_out_attn[0], w_out_pool[0],
                      w_grp_pool[0].reshape(n_groups * POOL_GROUP, POOL_GROUP)))
    w_grp = w_grp.reshape(n_groups, POOL_GROUP, POOL_GROUP)
    mod1_rows = _ada(cond, w_ada, b_ada, 1)
    mod1 = mod1_rows.reshape(MOD_ROWS, 1, 3 * d)

    sink = a_sink[0]
    og_p = _ctx_attention(sink, qg_p, ka, va, kb, vb)

    cos, sin_signed = _rope_tables()
    flat = lambda cache: cache.reshape(DEC_BATCH, -1, HEAD_DIM)
    oa_s = _win_attention(sink, proj_s, flat(cache_a_k), flat(cache_a_v), cos, sin_signed)
    ob_s = _na_attention(proj_s, flat(cache_b_k), flat(cache_b_v), _expand_bias(b_rpb[0]))

    xp1, hu_p, ss_p = _matmul_residual_prenorm([(og_p, 0), (og_p, 1)], w_out0, xp, mod0, True,
                                               norm_g[1], mod1)
    xs1, hu_s, ss_s = _matmul_residual_prenorm([(oa_s, 0), (ob_s, 0)], w_out0, xs, mod0, False,
                                               norm_g[1], mod1)

    shw = _shift_projection(mod1_rows, w_in1).reshape(MOD_ROWS, 1, 2 * d)
    ug_p = _matmul_postnorm(hu_p, ss_p, w_in1, shw, True, BF16)
    ug_s = _matmul_postnorm(hu_s, ss_s, w_in1, shw, False, BF16)
    y_p = _pool_mixer(ug_p, w_grp, pool_scale[0], SEQ)
    y_s = _pool_mixer(ug_s, w_grp, pool_scale[0], DEC_SEQ)
    y_prompt = _matmul_residual_norm([(y_p, 0), (y_p, 1)], w_out1, xp1, mod1, True,
                                     final_g).reshape(BATCH, SEQ, d)
    y_sample = _matmul_residual_norm([(y_s, 0), (y_s, 1)], w_out1, xs1, mod1, False,
                                     final_g).reshape(DEC_BATCH, DEC_SEQ, d)

    kv_shape_a = (BATCH, 1, SEQ, A_KV_HEADS, HEAD_DIM)
    kv_shape_b = (BATCH, 1, SEQ, B_HEADS, HEAD_DIM)
    return (y_prompt, y_sample, ka.reshape(kv_shape_a), va.reshape(kv_shape_a),
            kb.reshape(kv_shape_b), vb.reshape(kv_shape_b))
```

```python
import functools

import jax
import jax.numpy as jnp
from jax import lax
from jax.experimental import pallas as pl
from jax.experimental.pallas import tpu as pltpu

F32 = jnp.float32
BF16 = jnp.bfloat16

D_MODEL = 4096
BATCH = 32
SEQ = 256
DEC_BATCH = 8
DEC_SEQ = 1024
PAST_LEN = 512
GRID_W = 64
HEAD_DIM = 128
A_Q_HEADS = 16
A_KV_HEADS = 4
A_GROUPS = 4
A_HALF_WIN = 128
A_BLOCK = 128
B_HEADS = 16
B_WIN_ROWS = 8
B_WIN_COLS = 16
A_WIDTH = A_Q_HEADS * HEAD_DIM
A_KV_WIDTH = A_KV_HEADS * HEAD_DIM
B_WIDTH = B_HEADS * HEAD_DIM
MIX_WIDTH = A_WIDTH + B_WIDTH
POOL_WINDOWS = (2, 4, 8, 16)
POOL_GROUP = 1024
POOL_BAND_BLOCK = 256
CAST_CHUNK_ROWS = 64
ROPE_BASE = 10000.0
NORM_EPS = 1e-6
NEG_INF = -1e30
ATTN_SCALE = HEAD_DIM ** -0.5
LOG2E = 1.4426950408889634
LOGIT_SCALE = ATTN_SCALE * LOG2E
CTX_CHUNK = 1024
WIN_CHUNK = 512
NA_CHUNK = 1024

N_PROMPT = BATCH * SEQ
N_SAMPLE = DEC_BATCH * DEC_SEQ
MOD_ROWS = 16
CTX_MOD_ROW = DEC_BATCH
GRID_ROWS = DEC_SEQ // GRID_W
N_BIAS_PAIRS = 2 * B_WIN_ROWS - 2
RPB_ROWS = 2 * B_WIN_ROWS - 1
RPB_COLS = 2 * B_WIN_COLS - 1

COL_QA = 0
COL_KA = COL_QA + A_WIDTH
COL_VA = COL_KA + A_KV_WIDTH
COL_QB = COL_VA + A_KV_WIDTH
COL_KB = COL_QB + B_WIDTH
COL_VB = COL_KB + B_WIDTH
COL_GATE = COL_VB + B_WIDTH
ATTN_IN_WIDTH = COL_GATE + MIX_WIDTH
PROJ_TN = 1024


def _col_tiles(col0, width):
    assert col0 % PROJ_TN == 0 and width % PROJ_TN == 0
    return tuple(range(col0 // PROJ_TN, (col0 + width) // PROJ_TN))


def _tile_lookup(tiles):
    runs = []
    for jj, t in enumerate(tiles):
        if not runs or runs[-1][1] != t - jj:
            runs.append((jj, t - jj))

    def lookup(j):
        off = runs[0][1]
        for start, o in runs[1:]:
            off = jnp.where(j >= start, o, off)
        return j + off
    return lookup

VMEM_LIMIT = 48 * 1024 * 1024
VMEM_LIMIT_ROWS = 56 * 1024 * 1024


def _params(n_grid_dims):
    return pltpu.CompilerParams(
        dimension_semantics=("arbitrary",) * n_grid_dims,
        vmem_limit_bytes=VMEM_LIMIT,
    )


def _silu(x):
    return 0.5 * x * (1.0 + jnp.tanh(0.5 * x))


def _dot_nt(a, b):
    return lax.dot_general(a, b, (((1,), (1,)), ((), ())), preferred_element_type=F32)


def _dot(a, b):
    return jnp.dot(a, b, preferred_element_type=F32)


def _ada_kernel(cond_ref, w_ref, b_ref, o_ref):
    a = _silu(cond_ref[...]).astype(BF16)
    o_ref[...] = _dot(a, w_ref[...].astype(BF16)) + b_ref[...]


def _ada(cond, w_ada, b_ada, tn=512):
    depth, d, n = w_ada.shape
    return pl.pallas_call(
        _ada_kernel,
        grid=(depth, n // tn),
        in_specs=[
            pl.BlockSpec((MOD_ROWS, d), lambda l, j: (0, 0)),
            pl.BlockSpec((None, d, tn), lambda l, j: (l, 0, j)),
            pl.BlockSpec((None, 1, tn), lambda l, j: (l, 0, j)),
        ],
        out_specs=pl.BlockSpec((None, MOD_ROWS, tn), lambda l, j: (l, 0, j)),
        out_shape=jax.ShapeDtypeStruct((depth, MOD_ROWS, n), F32),
        compiler_params=_params(2),
        name="ada",
    )(cond, w_ada, b_ada.reshape(depth, 1, n))


def _mod_row_fn(is_prompt, tm):
    if is_prompt:
        return lambda i: CTX_MOD_ROW
    return lambda i: (i * tm) // DEC_SEQ


def _norm_mod_kernel(xp_ref, xs_ref, g_ref, sh_ref, sc_ref, hp_ref, hs_ref, *, n_prompt_tiles):
    def norm_mod(x_ref, o_ref):
        x = x_ref[...]
        ms = jnp.mean(x * x, axis=-1, keepdims=True)
        y = x * lax.rsqrt(ms + NORM_EPS) * g_ref[...]
        o_ref[...] = (y * (1.0 + sc_ref[...]) + sh_ref[...]).astype(o_ref.dtype)

    i = pl.program_id(0)

    @pl.when(i < n_prompt_tiles)
    def _():
        norm_mod(xp_ref, hp_ref)

    @pl.when(i >= n_prompt_tiles)
    def _():
        norm_mod(xs_ref, hs_ref)


def _norm_mod(xp, xs, gain, mod, tm=256):
    d = xp.shape[1]
    n_p, n_s = xp.shape[0] // tm, xs.shape[0] // tm
    p_blk = lambda i: (jnp.minimum(i, n_p - 1), 0)
    s_blk = lambda i: (jnp.maximum(i - n_p, 0), 0)
    row = lambda i: jnp.where(i < n_p, CTX_MOD_ROW, (jnp.maximum(i - n_p, 0) * tm) // DEC_SEQ)
    return pl.pallas_call(
        functools.partial(_norm_mod_kernel, n_prompt_tiles=n_p),
        grid=(n_p + n_s,),
        in_specs=[
            pl.BlockSpec((tm, d), p_blk),
            pl.BlockSpec((tm, d), s_blk),
            pl.BlockSpec((1, d), lambda i: (0, 0)),
            pl.BlockSpec((None, 1, d), lambda i: (row(i), 0, 0)),
            pl.BlockSpec((None, 1, d), lambda i: (row(i), 0, 1)),
        ],
        out_specs=[pl.BlockSpec((tm, d), p_blk), pl.BlockSpec((tm, d), s_blk)],
        out_shape=[jax.ShapeDtypeStruct(xp.shape, BF16), jax.ShapeDtypeStruct(xs.shape, BF16)],
        compiler_params=_params(1),
        name="norm_mod",
    )(xp, xs, gain.reshape(1, d), mod, mod)


def _mm_kernel(a_ref, w_ref, *refs, n_casts):
    cast_in, o_ref, cast_out = refs[:n_casts], refs[n_casts], refs[n_casts + 1:]
    o_ref[...] = _dot(a_ref[...], w_ref[...]).astype(o_ref.dtype)
    for src, dst in zip(cast_in, cast_out):
        dst[...] = src[...].astype(dst.dtype)


def _matmul(a, w, w_tiles, out_dtype, cast_weights=()):
    m, k = a.shape
    tn = PROJ_TN
    tm = 1024 if out_dtype == BF16 else 512
    assert m % tm == 0
    w_tile = _tile_lookup(w_tiles)
    ncols = len(w_tiles) * tn
    n_i = m // tm
    n_steps = len(w_tiles) * n_i
    in_specs = [pl.BlockSpec((tm, k), lambda j, i: (i, 0)),
                pl.BlockSpec((k, tn), lambda j, i: (0, w_tile(j)))]
    out_specs = [pl.BlockSpec((tm, tn), lambda j, i: (i, j))]
    out_shape = [jax.ShapeDtypeStruct((m, ncols), out_dtype)]
    for cw in cast_weights:
        rows, cols = cw.shape
        chunk = CAST_CHUNK_ROWS
        n_chunks = rows // chunk
        assert rows % chunk == 0 and n_chunks <= n_steps
        spec = pl.BlockSpec((chunk, cols),
                            lambda j, i, n_chunks=n_chunks: (jnp.minimum(j * n_i + i, n_chunks - 1), 0))
        in_specs.append(spec)
        out_specs.append(spec)
        out_shape.append(jax.ShapeDtypeStruct((rows, cols), BF16))
    outs = pl.pallas_call(
        functools.partial(_mm_kernel, n_casts=len(cast_weights)),
        grid=(ncols // tn, n_i),
        in_specs=in_specs,
        out_specs=out_specs,
        out_shape=out_shape,
        compiler_params=pltpu.CompilerParams(
            dimension_semantics=("arbitrary", "arbitrary"),
            vmem_limit_bytes=VMEM_LIMIT_ROWS if cast_weights else VMEM_LIMIT),
        name="proj",
    )(a, w, *cast_weights)
    return outs if cast_weights else outs[0]


def _kv_proj_kernel(a_ref, w_ref, ka_ref, va_ref, kb_ref, vb_ref, *, kb_tile0, vb_tile0):
    j = pl.program_id(0)
    acc = _dot(a_ref[...], w_ref[...])

    @pl.when(j < kb_tile0)
    def _():
        ka_ref[...] = acc[:, :A_KV_WIDTH]
        va_ref[...] = acc[:, A_KV_WIDTH:]

    @pl.when((j >= kb_tile0) & (j < vb_tile0))
    def _():
        kb_ref[...] = acc

    @pl.when(j >= vb_tile0)
    def _():
        vb_ref[...] = acc


def _kv_projection(a, w, tm=512):
    m, k = a.shape
    tn = PROJ_TN
    assert COL_VA == COL_KA + A_KV_WIDTH and 2 * A_KV_WIDTH == tn
    w_tiles = _col_tiles(COL_KA, tn) + _col_tiles(COL_KB, B_WIDTH) + _col_tiles(COL_VB, B_WIDTH)
    w_tile = _tile_lookup(w_tiles)
    kb_tile0 = 1
    vb_tile0 = kb_tile0 + B_WIDTH // tn
    end_tile = len(w_tiles)
    n_i = m // tm
    last = n_i - 1

    def parked(j, i, t0, t1):
        row = jnp.where(j < t0, 0, jnp.where(j < t1, i, last))
        return row, jnp.clip(j - t0, 0, t1 - t0 - 1)

    return pl.pallas_call(
        functools.partial(_kv_proj_kernel, kb_tile0=kb_tile0, vb_tile0=vb_tile0),
        grid=(end_tile, n_i),
        in_specs=[pl.BlockSpec((tm, k), lambda j, i: (i, 0)),
                  pl.BlockSpec((k, tn), lambda j, i: (0, w_tile(j)))],
        out_specs=[pl.BlockSpec((tm, A_KV_WIDTH), lambda j, i: parked(j, i, 0, kb_tile0)),
                   pl.BlockSpec((tm, A_KV_WIDTH), lambda j, i: parked(j, i, 0, kb_tile0)),
                   pl.BlockSpec((tm, tn), lambda j, i: parked(j, i, kb_tile0, vb_tile0)),
                   pl.BlockSpec((tm, tn), lambda j, i: parked(j, i, vb_tile0, end_tile))],
        out_shape=[jax.ShapeDtypeStruct((m, A_KV_WIDTH), F32),
                   jax.ShapeDtypeStruct((m, A_KV_WIDTH), F32),
                   jax.ShapeDtypeStruct((m, B_WIDTH), F32),
                   jax.ShapeDtypeStruct((m, B_WIDTH), F32)],
        compiler_params=_params(2),
        name="kv_proj",
    )(a, w)


def _dot_halves(a1_ref, a2_ref, w_ref):
    kh = a1_ref.shape[1]
    return _dot(a1_ref[...], w_ref[:kh, :]) + _dot(a2_ref[...], w_ref[kh:, :])


def _mm_res_prenorm_kernel(a1_ref, a2_ref, w_ref, x_ref, g_ref, gain_ref, sc_ref,
                           x1_ref, hu_ref, ss_ref):
    x1 = x_ref[...] + g_ref[...] * _dot_halves(a1_ref, a2_ref, w_ref)
    x1_ref[...] = x1
    hu_ref[...] = (x1 * (gain_ref[...] * (1.0 + sc_ref[...]))).astype(hu_ref.dtype)
    ss_ref[...] = jnp.sum(x1 * x1, axis=-1, keepdims=True)


def _matmul_residual_prenorm(a_halves, w, x, mod, is_prompt, gain, next_mod, tm=512):
    (a1, c1), (a2, c2) = a_halves
    m = a1.shape[0]
    k, n = w.shape
    tn = PROJ_TN
    n_tiles = n // tn
    row = _mod_row_fn(is_prompt, tm)
    scale_block0, gate_block0 = n_tiles, 2 * n_tiles
    tile = pl.BlockSpec((tm, tn), lambda j, i: (i, j))
    return pl.pallas_call(
        _mm_res_prenorm_kernel,
        grid=(n_tiles, m // tm),
        in_specs=[pl.BlockSpec((tm, k // 2), lambda j, i: (i, c1)),
                  pl.BlockSpec((tm, k // 2), lambda j, i: (i, c2)),
                  pl.BlockSpec((k, tn), lambda j, i: (0, j)),
                  tile,
                  pl.BlockSpec((None, 1, tn), lambda j, i: (row(i), 0, gate_block0 + j)),
                  pl.BlockSpec((1, tn), lambda j, i: (0, j)),
                  pl.BlockSpec((None, 1, tn), lambda j, i: (row(i), 0, scale_block0 + j))],
        out_specs=[tile, tile, pl.BlockSpec((None, tm, 1), lambda j, i: (j, i, 0))],
        out_shape=[jax.ShapeDtypeStruct((m, n), F32), jax.ShapeDtypeStruct((m, n), BF16),
                   jax.ShapeDtypeStruct((n_tiles, m, 1), F32)],
        compiler_params=_params(2),
        name="proj_residual_prenorm",
    )(a1, a2, w, x, mod, gain.reshape(1, n), next_mod)


def _shift_proj_kernel(sh_ref, w_ref, o_ref):
    o_ref[...] = _dot(sh_ref[...].astype(BF16), w_ref[...])


def _shift_projection(mod2d, w):
    k, n = w.shape
    tn = PROJ_TN
    return pl.pallas_call(
        _shift_proj_kernel,
        grid=(n // tn,),
        in_specs=[pl.BlockSpec((MOD_ROWS, k), lambda j: (0, 0)),
                  pl.BlockSpec((k, tn), lambda j: (0, j))],
        out_specs=pl.BlockSpec((MOD_ROWS, tn), lambda j: (0, j)),
        out_shape=jax.ShapeDtypeStruct((MOD_ROWS, n), F32),
        compiler_params=_params(1),
        name="shift_proj",
    )(mod2d, w)


def _mm_postnorm_kernel(a_ref, w_ref, ss_ref, shw_ref, o_ref, *, width):
    acc = _dot(a_ref[...], w_ref[...])
    ss = ss_ref[0]
    for t in range(1, ss_ref.shape[0]):
        ss = ss + ss_ref[t]
    r = lax.rsqrt(ss / width + NORM_EPS)
    o_ref[...] = (acc * r + shw_ref[...]).astype(o_ref.dtype)


def _matmul_postnorm(hu, ss, w, shw, is_prompt, out_dtype, tm=1024):
    m, k = hu.shape
    n = w.shape[1]
    tn = PROJ_TN
    row = _mod_row_fn(is_prompt, tm)
    return pl.pallas_call(
        functools.partial(_mm_postnorm_kernel, width=k),
        grid=(n // tn, m // tm),
        in_specs=[pl.BlockSpec((tm, k), lambda j, i: (i, 0)),
                  pl.BlockSpec((k, tn), lambda j, i: (0, j)),
                  pl.BlockSpec((ss.shape[0], tm, 1), lambda j, i: (0, i, 0)),
                  pl.BlockSpec((None, 1, tn), lambda j, i: (row(i), 0, j))],
        out_specs=pl.BlockSpec((tm, tn), lambda j, i: (i, j)),
        out_shape=jax.ShapeDtypeStruct((m, n), out_dtype),
        compiler_params=pltpu.CompilerParams(dimension_semantics=("arbitrary", "arbitrary"),
                                             vmem_limit_bytes=VMEM_LIMIT_ROWS),
        name="proj_postnorm",
    )(hu, w, ss, shw)


def _mm_res_norm_kernel(a1_ref, a2_ref, w_ref, x_ref, g_ref, gain_ref, y_ref, ss_scr, *, n_tiles, tn):
    j = pl.program_id(1)
    x2 = x_ref[...] + g_ref[...] * _dot_halves(a1_ref, a2_ref, w_ref)
    part = jnp.sum(x2 * x2, axis=-1, keepdims=True)

    @pl.when(j == 0)
    def _():
        ss_scr[...] = part

    @pl.when(j > 0)
    def _():
        ss_scr[...] += part

    for jj in range(n_tiles):
        @pl.when(j == jj)
        def _(jj=jj):
            y_ref[:, jj * tn:(jj + 1) * tn] = x2

    @pl.when(j == n_tiles - 1)
    def _():
        r = lax.rsqrt(ss_scr[...] / (n_tiles * tn) + NORM_EPS)
        for jj in range(n_tiles):
            cols = slice(jj * tn, (jj + 1) * tn)
            y_ref[:, cols] = y_ref[:, cols] * r * gain_ref[:, cols]


def _matmul_residual_norm(a_halves, w, x, mod, is_prompt, gain, tm=512):
    (a1, c1), (a2, c2) = a_halves
    m = a1.shape[0]
    k, n = w.shape
    tn = PROJ_TN
    n_tiles = n // tn
    row = _mod_row_fn(is_prompt, tm)
    gate_block0 = 2 * n_tiles
    return pl.pallas_call(
        functools.partial(_mm_res_norm_kernel, n_tiles=n_tiles, tn=tn),
        grid=(m // tm, n_tiles),
        in_specs=[pl.BlockSpec((tm, k // 2), lambda i, j: (i, c1)),
                  pl.BlockSpec((tm, k // 2), lambda i, j: (i, c2)),
                  pl.BlockSpec((k, tn), lambda i, j: (0, j)),
                  pl.BlockSpec((tm, tn), lambda i, j: (i, j)),
                  pl.BlockSpec((None, 1, tn), lambda i, j: (row(i), 0, gate_block0 + j)),
                  pl.BlockSpec((1, n), lambda i, j: (0, 0))],
        out_specs=pl.BlockSpec((tm, n), lambda i, j: (i, 0)),
        out_shape=jax.ShapeDtypeStruct((m, n), F32),
        scratch_shapes=[pltpu.VMEM((tm, 1), F32)],
        compiler_params=pltpu.CompilerParams(dimension_semantics=("arbitrary", "arbitrary"),
                                             vmem_limit_bytes=VMEM_LIMIT_ROWS),
        name="proj_residual_norm",
    )(a1, a2, w, x, mod, gain.reshape(1, n))


def _chunked_softmax(n_rows, chunk, loads, stores, sink2=None):
    inv = []
    for c in range(n_rows // chunk):
        rows = slice(c * chunk, (c + 1) * chunk)
        ts = [load(rows) for load in loads]
        m = functools.reduce(jnp.maximum, [jnp.max(t, axis=-1, keepdims=True) for t in ts])
        if sink2 is not None:
            m = jnp.maximum(m, sink2[rows])
        es = [jnp.exp2(t - m) for t in ts]
        l = functools.reduce(jnp.add, [jnp.sum(e, axis=-1, keepdims=True) for e in es])
        if sink2 is not None:
            l = l + jnp.exp2(sink2[rows] - m)
        for store, e in zip(stores, es):
            store(rows, e.astype(BF16))
        inv.append(1.0 / l)
    return jnp.concatenate(inv, axis=0)


def _ctx_attn_kernel(sink_ref, qg_ref, ka_ref, va_ref, kb_ref, vb_ref, o_ref, s_scr, e_scr):
    n = SEQ
    gate0 = MIX_WIDTH
    n_stack = A_GROUPS

    def softmax(sink2):
        def store(rows, e):
            e_scr[rows, :] = e
        return _chunked_softmax(n_stack * n, CTX_CHUNK, [lambda rows: s_scr[rows, :] * LOGIT_SCALE],
                                [store], sink2)

    def emit(o, g, out_col):
        cols = slice(out_col, out_col + HEAD_DIM)
        gate = qg_ref[:, gate0 + out_col:gate0 + out_col + HEAD_DIM].astype(F32)
        o_ref[:, cols] = (o[g * n:(g + 1) * n] * _silu(gate)).astype(o_ref.dtype)

    for kv in range(A_KV_HEADS):
        cols = slice(kv * HEAD_DIM, (kv + 1) * HEAD_DIM)
        heads = [kv * A_GROUPS + g for g in range(A_GROUPS)]
        q = jnp.concatenate(
            [qg_ref[:, h * HEAD_DIM:(h + 1) * HEAD_DIM] for h in heads], axis=0)
        sink2 = jnp.concatenate(
            [jnp.full((n, 1), sink_ref[h] * LOG2E, F32) for h in heads], axis=0)
        s_scr[...] = _dot_nt(q, ka_ref[:, cols].astype(BF16))
        inv = softmax(sink2)
        o = _dot(e_scr[...], va_ref[:, cols].astype(BF16)) * inv
        for g, h in enumerate(heads):
            emit(o, g, h * HEAD_DIM)
    for h0 in range(0, B_HEADS, n_stack):
        heads = range(h0, h0 + n_stack)
        for g, h in enumerate(heads):
            cols = slice(h * HEAD_DIM, (h + 1) * HEAD_DIM)
            q = qg_ref[:, A_WIDTH + h * HEAD_DIM:A_WIDTH + (h + 1) * HEAD_DIM]
            s_scr[g * n:(g + 1) * n, :] = _dot_nt(q, kb_ref[:, cols].astype(BF16))
        inv = softmax(None)
        o = jnp.concatenate(
            [_dot(e_scr[g * n:(g + 1) * n, :],
                  vb_ref[:, h * HEAD_DIM:(h + 1) * HEAD_DIM].astype(BF16))
             for g, h in enumerate(heads)], axis=0) * inv
        for g, h in enumerate(heads):
            emit(o, g, A_WIDTH + h * HEAD_DIM)


def _ctx_attention(sink, qg, ka, va, kb, vb):
    stack_rows = A_GROUPS * SEQ
    return pl.pallas_call(
        _ctx_attn_kernel,
        grid=(BATCH,),
        in_specs=[
            pl.BlockSpec(memory_space=pltpu.SMEM),
            pl.BlockSpec((SEQ, 2 * MIX_WIDTH), lambda b: (b, 0)),
            pl.BlockSpec((SEQ, A_KV_WIDTH), lambda b: (b, 0)),
            pl.BlockSpec((SEQ, A_KV_WIDTH), lambda b: (b, 0)),
            pl.BlockSpec((SEQ, B_WIDTH), lambda b: (b, 0)),
            pl.BlockSpec((SEQ, B_WIDTH), lambda b: (b, 0)),
        ],
        out_specs=pl.BlockSpec((SEQ, MIX_WIDTH), lambda b: (b, 0)),
        out_shape=jax.ShapeDtypeStruct((N_PROMPT, MIX_WIDTH), BF16),
        scratch_shapes=[pltpu.VMEM((stack_rows, SEQ), F32),
                        pltpu.VMEM((stack_rows, SEQ), BF16)],
        compiler_params=_params(1),
        name="ctx_attention",
    )(sink, qg, ka, va, kb, vb)


def _split_cache_heads(c_ref, scr, n_heads):
    for h in range(n_heads):
        scr[h] = c_ref[pl.ds(h, PAST_LEN, stride=n_heads), :].astype(scr.dtype)


def _win_attn_kernel(sink_ref, q_ref, k_ref, v_ref, ck_ref, cv_ref, gate_ref,
                     cos_ref, sin_ref, o_ref, qs_ref, ks_ref, ck_scr, cv_scr,
                     band_scr, sw_scr, sc_scr, ew_scr, ec_scr):
    kv = pl.program_id(1)
    n = DEC_SEQ

    @pl.when(kv == 0)
    def _():
        _split_cache_heads(ck_ref, ck_scr, A_KV_HEADS)
        _split_cache_heads(cv_ref, cv_scr, A_KV_HEADS)

    cos = cos_ref[...]
    sin = sin_ref[...]
    lane = lax.broadcasted_iota(jnp.int32, (n, HEAD_DIM), 1)
    first_quarter = (lane % (HEAD_DIM // 2)) < (HEAD_DIM // 4)

    def rope(x):
        rot = jnp.where(first_quarter,
                        pltpu.roll(x, HEAD_DIM - HEAD_DIM // 4, 1),
                        pltpu.roll(x, HEAD_DIM // 4, 1))
        return x * cos + rot * sin

    ks_ref[...] = rope(k_ref[...].astype(F32)).astype(BF16)
    for g in range(A_GROUPS):
        qs_ref[g] = (rope(q_ref[:, g * HEAD_DIM:(g + 1) * HEAD_DIM].astype(F32))
                     * LOGIT_SCALE).astype(BF16)

    ck = ck_scr[kv]
    cv = cv_scr[kv]
    rows = A_GROUPS * A_BLOCK
    sink2 = jnp.concatenate(
        [jnp.full((A_BLOCK, 1), sink_ref[kv * A_GROUPS + g] * LOG2E, F32)
         for g in range(A_GROUPS)], axis=0)
    qi = lax.broadcasted_iota(jnp.int32, (rows, 3 * A_BLOCK), 0) % A_BLOCK
    rel = lax.broadcasted_iota(jnp.int32, (rows, 3 * A_BLOCK), 1) - A_BLOCK - qi
    band_scr[...] = jnp.where(jnp.abs(rel) <= A_HALF_WIN, 0.0, NEG_INF)
    for blk in range(n // A_BLOCK):
        r0 = blk * A_BLOCK
        lo = max(0, r0 - A_BLOCK)
        hi = min(n, r0 + 2 * A_BLOCK)
        span = hi - lo
        b0 = lo - (r0 - A_BLOCK)
        q = jnp.concatenate([qs_ref[g, r0:r0 + A_BLOCK, :] for g in range(A_GROUPS)], axis=0)
        sw_scr[:, :span] = _dot_nt(q, ks_ref[lo:hi, :])
        sc_scr[...] = _dot_nt(q, ck)

        def load_w(r, span=span, b0=b0):
            return sw_scr[r, :span] + band_scr[r, b0:b0 + span]

        def store_w(r, e, span=span):
            ew_scr[r, :span] = e

        def store_c(r, e):
            ec_scr[r, :] = e

        inv = _chunked_softmax(rows, WIN_CHUNK, [load_w, lambda r: sc_scr[r, :]],
                               [store_w, store_c], sink2)
        o = (_dot(ew_scr[:, :span], v_ref[lo:hi, :]) + _dot(ec_scr[...], cv)) * inv
        for g in range(A_GROUPS):
            hc = slice(g * HEAD_DIM, (g + 1) * HEAD_DIM)
            gate = gate_ref[r0:r0 + A_BLOCK, hc].astype(F32)
            o_ref[r0:r0 + A_BLOCK, hc] = (
                o[g * A_BLOCK:(g + 1) * A_BLOCK] * _silu(gate)).astype(o_ref.dtype)


def _win_attention(sink, proj, cache_k, cache_v, cos, sin_signed):
    gw = A_GROUPS * HEAD_DIM
    stack_rows = A_GROUPS * A_BLOCK
    n = DEC_SEQ
    cache_spec = pl.BlockSpec((None, PAST_LEN * A_KV_HEADS, HEAD_DIM), lambda b, k: (b, 0, 0))
    return pl.pallas_call(
        _win_attn_kernel,
        grid=(DEC_BATCH, A_KV_HEADS),
        in_specs=[
            pl.BlockSpec(memory_space=pltpu.SMEM),
            pl.BlockSpec((n, gw), lambda b, k: (b, COL_QA // gw + k)),
            pl.BlockSpec((n, HEAD_DIM), lambda b, k: (b, COL_KA // HEAD_DIM + k)),
            pl.BlockSpec((n, HEAD_DIM), lambda b, k: (b, COL_VA // HEAD_DIM + k)),
            cache_spec,
            cache_spec,
            pl.BlockSpec((n, gw), lambda b, k: (b, COL_GATE // gw + k)),
            pl.BlockSpec((n, HEAD_DIM), lambda b, k: (0, 0)),
            pl.BlockSpec((n, HEAD_DIM), lambda b, k: (0, 0)),
        ],
        out_specs=pl.BlockSpec((n, gw), lambda b, k: (b, k)),
        out_shape=jax.ShapeDtypeStruct((N_SAMPLE, A_WIDTH), BF16),
        scratch_shapes=[pltpu.VMEM((A_GROUPS, n, HEAD_DIM), BF16),
                        pltpu.VMEM((n, HEAD_DIM), BF16),
                        pltpu.VMEM((A_KV_HEADS, PAST_LEN, HEAD_DIM), BF16),
                        pltpu.VMEM((A_KV_HEADS, PAST_LEN, HEAD_DIM), BF16),
                        pltpu.VMEM((stack_rows, 3 * A_BLOCK), F32),
                        pltpu.VMEM((stack_rows, 3 * A_BLOCK), F32),
                        pltpu.VMEM((stack_rows, PAST_LEN), F32),
                        pltpu.VMEM((stack_rows, 3 * A_BLOCK), BF16),
                        pltpu.VMEM((stack_rows, PAST_LEN), BF16)],
        compiler_params=_params(2),
        name="win_attention",
    )(sink, proj, proj, proj, cache_k, cache_v, proj, cos, sin_signed)


def _bias_kernel(rpb_ref, o_ref):
    h = pl.program_id(0)
    shape = (GRID_W, 2 * GRID_W)
    c = lax.broadcasted_iota(jnp.int32, shape, 0)
    j2 = lax.broadcasted_iota(jnp.int32, shape, 1)
    kc = j2 % GRID_W
    second = j2 >= GRID_W
    col_start = jnp.clip(c - B_WIN_COLS // 2, 0, GRID_W - B_WIN_COLS)
    ok = (kc >= col_start) & (kc < col_start + B_WIN_COLS)
    dc = kc - c + B_WIN_COLS - 1
    base = h * (RPB_ROWS * RPB_COLS)
    pair_scr = []
    for i in range(N_BIAS_PAIRS):
        acc = jnp.full(shape, NEG_INF, F32)
        for d in range(RPB_COLS):
            val = jnp.where(second, rpb_ref[base + (i + 1) * RPB_COLS + d],
                            rpb_ref[base + i * RPB_COLS + d]) * LOG2E
            acc = jnp.where(ok & (dc == d), val, acc)
        pair_scr.append(acc)
    for r in range(GRID_ROWS):
        dr0 = _na_key_row0(r) - r + B_WIN_ROWS - 1
        for i in range(B_WIN_ROWS // 2):
            o_ref[r * GRID_W:(r + 1) * GRID_W, i * 2 * GRID_W:(i + 1) * 2 * GRID_W] = (
                pair_scr[dr0 + 2 * i])


def _expand_bias(rpb):
    kw = B_WIN_ROWS * GRID_W
    return pl.pallas_call(
        _bias_kernel,
        grid=(B_HEADS,),
        in_specs=[pl.BlockSpec(memory_space=pltpu.SMEM)],
        out_specs=pl.BlockSpec((None, DEC_SEQ, kw), lambda h: (h, 0, 0)),
        out_shape=jax.ShapeDtypeStruct((B_HEADS, DEC_SEQ, kw), F32),
        compiler_params=_params(1),
        name="expand_bias",
    )(rpb.reshape(-1))


def _na_key_row0(r):
    return min(max(r - B_WIN_ROWS // 2, 0), GRID_ROWS - B_WIN_ROWS)


def _na_row_groups():
    groups = []
    for r in range(GRID_ROWS):
        rs = _na_key_row0(r)
        if groups and groups[-1][2] == rs:
            groups[-1] = (groups[-1][0], r + 1, rs)
        else:
            groups.append((r, r + 1, rs))
    return groups


def _na_attn_kernel(q_ref, k_ref, v_ref, ck_ref, cv_ref, gate_ref, bias_ref, o_ref,
                    ck_scr, cv_scr, sn_scr, sc_scr, en_scr, ec_scr):
    kw = B_WIN_ROWS * GRID_W
    h = pl.program_id(1)

    @pl.when(h == 0)
    def _():
        _split_cache_heads(ck_ref, ck_scr, B_HEADS)
        _split_cache_heads(cv_ref, cv_scr, B_HEADS)

    groups = _na_row_groups()
    sc_scr[...] = _dot_nt(q_ref[...], ck_scr[h])
    for r0, r1, rs in groups:
        sn_scr[r0 * GRID_W:r1 * GRID_W, :] = _dot_nt(
            q_ref[r0 * GRID_W:r1 * GRID_W, :], k_ref[rs * GRID_W:rs * GRID_W + kw, :])

    def store_n(r, e):
        en_scr[r, :] = e

    def store_c(r, e):
        ec_scr[r, :] = e

    inv = _chunked_softmax(
        DEC_SEQ, NA_CHUNK,
        [lambda r: sn_scr[r, :] * LOGIT_SCALE + bias_ref[r, :], lambda r: sc_scr[r, :] * LOGIT_SCALE],
        [store_n, store_c])
    o_n = jnp.concatenate(
        [_dot(en_scr[r0 * GRID_W:r1 * GRID_W, :], v_ref[rs * GRID_W:rs * GRID_W + kw, :])
         for r0, r1, rs in groups], axis=0)
    o = (o_n + _dot(ec_scr[...], cv_scr[h])) * inv
    o_ref[...] = (o * _silu(gate_ref[...].astype(F32))).astype(o_ref.dtype)


def _na_attention(proj, cache_k, cache_v, bias):
    n = DEC_SEQ
    hd = HEAD_DIM
    cache_spec = pl.BlockSpec((None, PAST_LEN * B_HEADS, hd), lambda b, h: (b, 0, 0))
    return pl.pallas_call(
        _na_attn_kernel,
        grid=(DEC_BATCH, B_HEADS),
        in_specs=[
            pl.BlockSpec((n, hd), lambda b, h: (b, COL_QB // hd + h)),
            pl.BlockSpec((n, hd), lambda b, h: (b, COL_KB // hd + h)),
            pl.BlockSpec((n, hd), lambda b, h: (b, COL_VB // hd + h)),
            cache_spec,
            cache_spec,
            pl.BlockSpec((n, hd), lambda b, h: (b, (COL_GATE + A_WIDTH) // hd + h)),
            pl.BlockSpec((None, n, B_WIN_ROWS * GRID_W), lambda b, h: (h, 0, 0)),
        ],
        out_specs=pl.BlockSpec((n, hd), lambda b, h: (b, h)),
        out_shape=jax.ShapeDtypeStruct((N_SAMPLE, B_WIDTH), BF16),
        scratch_shapes=[pltpu.VMEM((B_HEADS, PAST_LEN, hd), BF16),
                        pltpu.VMEM((B_HEADS, PAST_LEN, hd), BF16),
                        pltpu.VMEM((n, B_WIN_ROWS * GRID_W), F32),
                        pltpu.VMEM((n, PAST_LEN), F32),
                        pltpu.VMEM((n, B_WIN_ROWS * GRID_W), BF16),
                        pltpu.VMEM((n, PAST_LEN), BF16)],
        compiler_params=_params(2),
        name="na_attention",
    )(proj, proj, proj, cache_k, cache_v, proj, bias)


def _pool_kernel(u_ref, gate_ref, band_ref, inv_count_ref, w_ref, scale_ref, o_ref, *, seq, rows):
    cb = POOL_BAND_BLOCK
    parts = []
    for i in range(rows // cb):
        lo = i * cb if seq <= cb else max(0, (i - 1) * cb)
        hi = (i + 1) * cb if seq <= cb else min(rows, (i + 2) * cb)
        parts.append(_dot(band_ref[i * cb:(i + 1) * cb, lo:hi], u_ref[lo:hi, :]))
    wsum = jnp.concatenate(parts, axis=0)
    pooled = wsum * inv_count_ref[...] - u_ref[...].astype(F32)
    y = _dot(pooled.astype(BF16), w_ref[...]) * scale_ref[...]
    o_ref[...] = (y * _silu(gate_ref[...].astype(F32))).astype(o_ref.dtype)


def _pool_operators(seq, rows):
    t = jnp.arange(rows)[:, None]
    j = jnp.arange(rows)[None, :]
    same_seq = (t // seq) == (j // seq)
    pos = t % seq
    bands, inv_counts = [], []
    for window in POOL_WINDOWS:
        half = window // 2
        assert half <= POOL_BAND_BLOCK
        bands.append(((j - t >= -half) & (j - t < half) & same_seq).astype(BF16))
        inv_counts.append(1.0 / (jnp.minimum(pos + half, seq) - jnp.maximum(pos - half, 0)).astype(F32))
    return jnp.stack(bands), jnp.stack(inv_counts)


def _pool_mixer(ug, w_grp, scale, seq, rows=1024):
    m = ug.shape[0]
    n_groups = len(POOL_WINDOWS)
    pg = POOL_GROUP
    band, inv_count = _pool_operators(seq, rows)
    return pl.pallas_call(
        functools.partial(_pool_kernel, seq=seq, rows=rows),
        grid=(n_groups, m // rows),
        in_specs=[
            pl.BlockSpec((rows, pg), lambda g, i: (i, g)),
            pl.BlockSpec((rows, pg), lambda g, i: (i, n_groups + g)),
            pl.BlockSpec((None, rows, rows), lambda g, i: (g, 0, 0)),
            pl.BlockSpec((None, rows, 1), lambda g, i: (g, 0, 0)),
            pl.BlockSpec((None, pg, pg), lambda g, i: (g, 0, 0)),
            pl.BlockSpec((1, pg), lambda g, i: (0, g)),
        ],
        out_specs=pl.BlockSpec((rows, pg), lambda g, i: (i, g)),
        out_shape=jax.ShapeDtypeStruct((m, n_groups * pg), BF16),
        compiler_params=_params(2),
        name="pool_mixer",
    )(ug, ug, band, inv_count, w_grp, scale.reshape(1, -1))


def _rope_tables():
    t = jnp.arange(DEC_SEQ)
    quarter = HEAD_DIM // 4
    inv_freq = ROPE_BASE ** (-jnp.arange(quarter, dtype=F32) / quarter)
    ang_r = (t // GRID_W).astype(F32)[:, None] * inv_freq
    ang_c = (t % GRID_W).astype(F32)[:, None] * inv_freq
    ang = jnp.concatenate([ang_r, ang_r, ang_c, ang_c], axis=-1)
    sign = jnp.tile(jnp.concatenate([-jnp.ones((quarter,), F32), jnp.ones((quarter,), F32)]), 2)
    return jnp.cos(ang), jnp.sin(ang) * sign


def kernel(x_prompt, x_sample, c, cache_a_k, cache_a_v, cache_b_k, cache_b_v, c_ctx,
           w_ada, b_ada, norm_g, w_in_attn, a_sink, b_rpb, w_out_attn,
           w_in_pool, w_grp_pool, pool_scale, w_out_pool, final_g):
    d = D_MODEL
    xp = x_prompt.reshape(N_PROMPT, d)
    xs = x_sample.reshape(N_SAMPLE, d)

    w_in0 = w_in_attn[0].astype(BF16)

    cond = jnp.zeros((MOD_ROWS, d), F32).at[:DEC_BATCH].set(c).at[CTX_MOD_ROW].set(c_ctx)
    mod = _ada(cond, w_ada, b_ada)
    mod0 = mod[0].reshape(MOD_ROWS, 1, 3 * d)
    mod1 = mod[1].reshape(MOD_ROWS, 1, 3 * d)

    hp, hs = _norm_mod(xp, xs, norm_g[0], mod0)

    ka, va, kb, vb = _kv_projection(hp, w_in0)
    qg_tiles = (_col_tiles(COL_QA, A_WIDTH) + _col_tiles(COL_QB, B_WIDTH)
                + _col_tiles(COL_GATE, MIX_WIDTH))
    qg_p, w_in1 = _matmul(hp, w_in0, qg_tiles, BF16, cast_weights=(w_in_pool[0],))
    n_groups = len(POOL_WINDOWS)
    proj_s, w_out0, w_out1, w_grp = _matmul(
        hs, w_in0, _col_tiles(0, ATTN_IN_WIDTH), BF16,
        cast_weights=(w_out_attn[0], w_out_pool[0],
                      w_grp_pool[0].reshape(n_groups * POOL_GROUP, POOL_GROUP)))
    w_grp = w_grp.reshape(n_groups, POOL_GROUP, POOL_GROUP)

    sink = a_sink[0]
    og_p = _ctx_attention(sink, qg_p, ka, va, kb, vb)

    cos, sin_signed = _rope_tables()
    flat = lambda cache: cache.reshape(DEC_BATCH, -1, HEAD_DIM)
    oa_s = _win_attention(sink, proj_s, flat(cache_a_k), flat(cache_a_v), cos, sin_signed)
    ob_s = _na_attention(proj_s, flat(cache_b_k), flat(cache_b_v), _expand_bias(b_rpb[0]))

    xp1, hu_p, ss_p = _matmul_residual_prenorm([(og_p, 0), (og_p, 1)], w_out0, xp, mod0, True,
                                               norm_g[1], mod1)
    xs1, hu_s, ss_s = _matmul_residual_prenorm([(oa_s, 0), (ob_s, 0)], w_out0, xs, mod0, False,
                                               norm_g[1], mod1)

    shw = _shift_projection(mod[1], w_in1).reshape(MOD_ROWS, 1, 2 * d)
    ug_p = _matmul_postnorm(hu_p, ss_p, w_in1, shw, True, BF16)
    ug_s = _matmul_postnorm(hu_s, ss_s, w_in1, shw, False, BF16)
    y_p = _pool_mixer(ug_p, w_grp, pool_scale[0], SEQ)
    y_s = _pool_mixer(ug_s, w_grp, pool_scale[0], DEC_SEQ)
    y_prompt = _matmul_residual_norm([(y_p, 0), (y_p, 1)], w_out1, xp1, mod1, True,
                                     final_g).reshape(BATCH, SEQ, d)
    y_sample = _matmul_residual_norm([(y_s, 0), (y_s, 1)], w_out1, xs1, mod1, False,
                                     final_g).reshape(DEC_BATCH, DEC_SEQ, d)

    kv_shape_a = (BATCH, 1, SEQ, A_KV_HEADS, HEAD_DIM)
    kv_shape_b = (BATCH, 1, SEQ, B_HEADS, HEAD_DIM)
    return (y_prompt, y_sample, ka.reshape(kv_shape_a), va.reshape(kv_shape_a),
            kb.reshape(kv_shape_b), vb.reshape(kv_shape_b))
```

```python
import functools

import jax
import jax.numpy as jnp
from jax import lax
from jax.experimental import pallas as pl
from jax.experimental.pallas import tpu as pltpu

F32 = jnp.float32
BF16 = jnp.bfloat16

D_MODEL = 4096
BATCH = 32
SEQ = 256
DEC_BATCH = 8
DEC_SEQ = 1024
PAST_LEN = 512
GRID_W = 64
HEAD_DIM = 128
A_Q_HEADS = 16
A_KV_HEADS = 4
A_GROUPS = 4
A_HALF_WIN = 128
A_BLOCK = 128
B_HEADS = 16
B_WIN_ROWS = 8
B_WIN_COLS = 16
A_WIDTH = A_Q_HEADS * HEAD_DIM
A_KV_WIDTH = A_KV_HEADS * HEAD_DIM
B_WIDTH = B_HEADS * HEAD_DIM
MIX_WIDTH = A_WIDTH + B_WIDTH
POOL_WINDOWS = (2, 4, 8, 16)
POOL_GROUP = 1024
POOL_BAND_BLOCK = 256
CAST_CHUNK_ROWS = 64
ROPE_BASE = 10000.0
NORM_EPS = 1e-6
NEG_INF = -1e30
ATTN_SCALE = HEAD_DIM ** -0.5
LOG2E = 1.4426950408889634
LOGIT_SCALE = ATTN_SCALE * LOG2E
CTX_CHUNK = 1024
WIN_CHUNK = 512
NA_CHUNK = 1024

N_PROMPT = BATCH * SEQ
N_SAMPLE = DEC_BATCH * DEC_SEQ
MOD_ROWS = 16
CTX_MOD_ROW = DEC_BATCH
GRID_ROWS = DEC_SEQ // GRID_W
N_BIAS_PAIRS = 2 * B_WIN_ROWS - 2
RPB_ROWS = 2 * B_WIN_ROWS - 1
RPB_COLS = 2 * B_WIN_COLS - 1

COL_QA = 0
COL_KA = COL_QA + A_WIDTH
COL_VA = COL_KA + A_KV_WIDTH
COL_QB = COL_VA + A_KV_WIDTH
COL_KB = COL_QB + B_WIDTH
COL_VB = COL_KB + B_WIDTH
COL_GATE = COL_VB + B_WIDTH
ATTN_IN_WIDTH = COL_GATE + MIX_WIDTH
PROJ_TN = 1024


def _col_tiles(col0, width):
    assert col0 % PROJ_TN == 0 and width % PROJ_TN == 0
    return tuple(range(col0 // PROJ_TN, (col0 + width) // PROJ_TN))


def _tile_lookup(tiles):
    runs = []
    for jj, t in enumerate(tiles):
        if not runs or runs[-1][1] != t - jj:
            runs.append((jj, t - jj))

    def lookup(j):
        off = runs[0][1]
        for start, o in runs[1:]:
            off = jnp.where(j >= start, o, off)
        return j + off
    return lookup

VMEM_LIMIT = 48 * 1024 * 1024
VMEM_LIMIT_ROWS = 56 * 1024 * 1024


def _params(n_grid_dims):
    return pltpu.CompilerParams(
        dimension_semantics=("arbitrary",) * n_grid_dims,
        vmem_limit_bytes=VMEM_LIMIT,
    )


def _silu(x):
    return 0.5 * x * (1.0 + jnp.tanh(0.5 * x))


def _dot_nt(a, b):
    return lax.dot_general(a, b, (((1,), (1,)), ((), ())), preferred_element_type=F32)


def _dot(a, b):
    return jnp.dot(a, b, preferred_element_type=F32)


def _ada_kernel(cond_ref, w_ref, b_ref, o_ref):
    a = _silu(cond_ref[...]).astype(BF16)
    o_ref[...] = _dot(a, w_ref[...].astype(BF16)) + b_ref[...]


def _ada(cond, w_ada, b_ada, tn=512):
    depth, d, n = w_ada.shape
    return pl.pallas_call(
        _ada_kernel,
        grid=(depth, n // tn),
        in_specs=[
            pl.BlockSpec((MOD_ROWS, d), lambda l, j: (0, 0)),
            pl.BlockSpec((None, d, tn), lambda l, j: (l, 0, j)),
            pl.BlockSpec((None, 1, tn), lambda l, j: (l, 0, j)),
        ],
        out_specs=pl.BlockSpec((None, MOD_ROWS, tn), lambda l, j: (l, 0, j)),
        out_shape=jax.ShapeDtypeStruct((depth, MOD_ROWS, n), F32),
        compiler_params=_params(2),
        name="ada",
    )(cond, w_ada, b_ada.reshape(depth, 1, n))


def _mod_row_fn(is_prompt, tm):
    if is_prompt:
        return lambda i: CTX_MOD_ROW
    return lambda i: (i * tm) // DEC_SEQ


def _norm_mod_kernel(xp_ref, xs_ref, g_ref, sh_ref, sc_ref, hp_ref, hs_ref, *, n_prompt_tiles):
    def norm_mod(x_ref, o_ref):
        x = x_ref[...]
        ms = jnp.mean(x * x, axis=-1, keepdims=True)
        y = x * lax.rsqrt(ms + NORM_EPS) * g_ref[...]
        o_ref[...] = (y * (1.0 + sc_ref[...]) + sh_ref[...]).astype(o_ref.dtype)

    i = pl.program_id(0)

    @pl.when(i < n_prompt_tiles)
    def _():
        norm_mod(xp_ref, hp_ref)

    @pl.when(i >= n_prompt_tiles)
    def _():
        norm_mod(xs_ref, hs_ref)


def _norm_mod(xp, xs, gain, mod, tm=256):
    d = xp.shape[1]
    n_p, n_s = xp.shape[0] // tm, xs.shape[0] // tm
    p_blk = lambda i: (jnp.minimum(i, n_p - 1), 0)
    s_blk = lambda i: (jnp.maximum(i - n_p, 0), 0)
    row = lambda i: jnp.where(i < n_p, CTX_MOD_ROW, (jnp.maximum(i - n_p, 0) * tm) // DEC_SEQ)
    return pl.pallas_call(
        functools.partial(_norm_mod_kernel, n_prompt_tiles=n_p),
        grid=(n_p + n_s,),
        in_specs=[
            pl.BlockSpec((tm, d), p_blk),
            pl.BlockSpec((tm, d), s_blk),
            pl.BlockSpec((1, d), lambda i: (0, 0)),
            pl.BlockSpec((None, 1, d), lambda i: (row(i), 0, 0)),
            pl.BlockSpec((None, 1, d), lambda i: (row(i), 0, 1)),
        ],
        out_specs=[pl.BlockSpec((tm, d), p_blk), pl.BlockSpec((tm, d), s_blk)],
        out_shape=[jax.ShapeDtypeStruct(xp.shape, BF16), jax.ShapeDtypeStruct(xs.shape, BF16)],
        compiler_params=_params(1),
        name="norm_mod",
    )(xp, xs, gain.reshape(1, d), mod, mod)


def _mm_kernel(a_ref, w_ref, *refs, n_casts):
    cast_in, o_ref, cast_out = refs[:n_casts], refs[n_casts], refs[n_casts + 1:]
    o_ref[...] = _dot(a_ref[...], w_ref[...]).astype(o_ref.dtype)
    for src, dst in zip(cast_in, cast_out):
        dst[...] = src[...].astype(dst.dtype)


def _matmul(a, w, w_tiles, out_dtype, cast_weights=()):
    m, k = a.shape
    tn = PROJ_TN
    tm = 1024 if out_dtype == BF16 else 512
    assert m % tm == 0
    w_tile = _tile_lookup(w_tiles)
    ncols = len(w_tiles) * tn
    n_i = m // tm
    n_steps = len(w_tiles) * n_i
    in_specs = [pl.BlockSpec((tm, k), lambda j, i: (i, 0)),
                pl.BlockSpec((k, tn), lambda j, i: (0, w_tile(j)))]
    out_specs = [pl.BlockSpec((tm, tn), lambda j, i: (i, j))]
    out_shape = [jax.ShapeDtypeStruct((m, ncols), out_dtype)]
    for cw in cast_weights:
        rows, cols = cw.shape
        chunk = CAST_CHUNK_ROWS
        n_chunks = rows // chunk
        assert rows % chunk == 0 and n_chunks <= n_steps
        spec = pl.BlockSpec((chunk, cols),
                            lambda j, i, n_chunks=n_chunks: (jnp.minimum(j * n_i + i, n_chunks - 1), 0))
        in_specs.append(spec)
        out_specs.append(spec)
        out_shape.append(jax.ShapeDtypeStruct((rows, cols), BF16))
    outs = pl.pallas_call(
        functools.partial(_mm_kernel, n_casts=len(cast_weights)),
        grid=(ncols // tn, n_i),
        in_specs=in_specs,
        out_specs=out_specs,
        out_shape=out_shape,
        compiler_params=pltpu.CompilerParams(
            dimension_semantics=("arbitrary", "arbitrary"),
            vmem_limit_bytes=VMEM_LIMIT_ROWS if cast_weights else VMEM_LIMIT),
        name="proj",
    )(a, w, *cast_weights)
    return outs if cast_weights else outs[0]


def _kv_proj_kernel(a_ref, w_ref, ka_ref, va_ref, kb_ref, vb_ref, ka5_ref, va5_ref, kb5_ref, vb5_ref,
                    *, kb_tile0, vb_tile0):
    j = pl.program_id(0)
    acc = _dot(a_ref[...], w_ref[...])

    def emit(cols, out_ref, out5_ref):
        part = acc[:, cols]
        out_ref[...] = part
        out5_ref[...] = part.reshape(out5_ref.shape)

    @pl.when(j < kb_tile0)
    def _():
        emit(slice(0, A_KV_WIDTH), ka_ref, ka5_ref)
        emit(slice(A_KV_WIDTH, 2 * A_KV_WIDTH), va_ref, va5_ref)

    @pl.when((j >= kb_tile0) & (j < vb_tile0))
    def _():
        emit(slice(None), kb_ref, kb5_ref)

    @pl.when(j >= vb_tile0)
    def _():
        emit(slice(None), vb_ref, vb5_ref)


def _kv_projection(a, w, tm=512):
    m, k = a.shape
    tn = PROJ_TN
    assert COL_VA == COL_KA + A_KV_WIDTH and 2 * A_KV_WIDTH == tn and tm % SEQ == 0
    w_tiles = _col_tiles(COL_KA, tn) + _col_tiles(COL_KB, B_WIDTH) + _col_tiles(COL_VB, B_WIDTH)
    w_tile = _tile_lookup(w_tiles)
    kb_tile0 = 1
    vb_tile0 = kb_tile0 + B_WIDTH // tn
    end_tile = len(w_tiles)
    n_i = m // tm
    last = n_i - 1

    def parked(j, i, t0, t1):
        row = jnp.where(j < t0, 0, jnp.where(j < t1, i, last))
        return row, jnp.clip(j - t0, 0, t1 - t0 - 1)

    def parked5(j, i, t0, t1):
        row, col = parked(j, i, t0, t1)
        return row, 0, 0, col, 0

    req = tm // SEQ
    a_spec = pl.BlockSpec((tm, A_KV_WIDTH), lambda j, i: parked(j, i, 0, kb_tile0))
    a_spec5 = pl.BlockSpec((req, None, SEQ, A_KV_HEADS, HEAD_DIM),
                           lambda j, i: parked5(j, i, 0, kb_tile0))
    b_block5 = (req, None, SEQ, tn // HEAD_DIM, HEAD_DIM)
    cache_shape_a = (m // SEQ, 1, SEQ, A_KV_HEADS, HEAD_DIM)
    cache_shape_b = (m // SEQ, 1, SEQ, B_HEADS, HEAD_DIM)
    return pl.pallas_call(
        functools.partial(_kv_proj_kernel, kb_tile0=kb_tile0, vb_tile0=vb_tile0),
        grid=(end_tile, n_i),
        in_specs=[pl.BlockSpec((tm, k), lambda j, i: (i, 0)),
                  pl.BlockSpec((k, tn), lambda j, i: (0, w_tile(j)))],
        out_specs=[a_spec, a_spec,
                   pl.BlockSpec((tm, tn), lambda j, i: parked(j, i, kb_tile0, vb_tile0)),
                   pl.BlockSpec((tm, tn), lambda j, i: parked(j, i, vb_tile0, end_tile)),
                   a_spec5, a_spec5,
                   pl.BlockSpec(b_block5, lambda j, i: parked5(j, i, kb_tile0, vb_tile0)),
                   pl.BlockSpec(b_block5, lambda j, i: parked5(j, i, vb_tile0, end_tile))],
        out_shape=[jax.ShapeDtypeStruct((m, A_KV_WIDTH), F32),
                   jax.ShapeDtypeStruct((m, A_KV_WIDTH), F32),
                   jax.ShapeDtypeStruct((m, B_WIDTH), F32),
                   jax.ShapeDtypeStruct((m, B_WIDTH), F32),
                   jax.ShapeDtypeStruct(cache_shape_a, F32),
                   jax.ShapeDtypeStruct(cache_shape_a, F32),
                   jax.ShapeDtypeStruct(cache_shape_b, F32),
                   jax.ShapeDtypeStruct(cache_shape_b, F32)],
        compiler_params=pltpu.CompilerParams(dimension_semantics=("arbitrary", "arbitrary"),
                                             vmem_limit_bytes=VMEM_LIMIT_ROWS),
        name="kv_proj",
    )(a, w)


def _dot_halves(a1_ref, a2_ref, w_ref):
    kh = a1_ref.shape[1]
    return _dot(a1_ref[...], w_ref[:kh, :]) + _dot(a2_ref[...], w_ref[kh:, :])


def _mm_res_prenorm_kernel(a1_ref, a2_ref, w_ref, x_ref, g_ref, gain_ref, sc_ref,
                           x1_ref, hu_ref, ss_ref):
    x1 = x_ref[...] + g_ref[...] * _dot_halves(a1_ref, a2_ref, w_ref)
    x1_ref[...] = x1
    hu_ref[...] = (x1 * (gain_ref[...] * (1.0 + sc_ref[...]))).astype(hu_ref.dtype)
    ss_ref[...] = jnp.sum(x1 * x1, axis=-1, keepdims=True)


def _matmul_residual_prenorm(a_halves, w, x, mod, is_prompt, gain, next_mod, tm=512):
    (a1, c1), (a2, c2) = a_halves
    m = a1.shape[0]
    k, n = w.shape
    tn = PROJ_TN
    n_tiles = n // tn
    row = _mod_row_fn(is_prompt, tm)
    scale_block0, gate_block0 = n_tiles, 2 * n_tiles
    tile = pl.BlockSpec((tm, tn), lambda j, i: (i, j))
    return pl.pallas_call(
        _mm_res_prenorm_kernel,
        grid=(n_tiles, m // tm),
        in_specs=[pl.BlockSpec((tm, k // 2), lambda j, i: (i, c1)),
                  pl.BlockSpec((tm, k // 2), lambda j, i: (i, c2)),
                  pl.BlockSpec((k, tn), lambda j, i: (0, j)),
                  tile,
                  pl.BlockSpec((None, 1, tn), lambda j, i: (row(i), 0, gate_block0 + j)),
                  pl.BlockSpec((1, tn), lambda j, i: (0, j)),
                  pl.BlockSpec((None, 1, tn), lambda j, i: (row(i), 0, scale_block0 + j))],
        out_specs=[tile, tile, pl.BlockSpec((None, tm, 1), lambda j, i: (j, i, 0))],
        out_shape=[jax.ShapeDtypeStruct((m, n), F32), jax.ShapeDtypeStruct((m, n), BF16),
                   jax.ShapeDtypeStruct((n_tiles, m, 1), F32)],
        compiler_params=_params(2),
        name="proj_residual_prenorm",
    )(a1, a2, w, x, mod, gain.reshape(1, n), next_mod)


def _shift_proj_kernel(sh_ref, w_ref, o_ref):
    o_ref[...] = _dot(sh_ref[...].astype(BF16), w_ref[...])


def _shift_projection(mod2d, w):
    k, n = w.shape
    tn = PROJ_TN
    return pl.pallas_call(
        _shift_proj_kernel,
        grid=(n // tn,),
        in_specs=[pl.BlockSpec((MOD_ROWS, k), lambda j: (0, 0)),
                  pl.BlockSpec((k, tn), lambda j: (0, j))],
        out_specs=pl.BlockSpec((MOD_ROWS, tn), lambda j: (0, j)),
        out_shape=jax.ShapeDtypeStruct((MOD_ROWS, n), F32),
        compiler_params=_params(1),
        name="shift_proj",
    )(mod2d, w)


def _mm_postnorm_kernel(a_ref, w_ref, ss_ref, shw_ref, o_ref, *, width):
    acc = _dot(a_ref[...], w_ref[...])
    ss = ss_ref[0]
    for t in range(1, ss_ref.shape[0]):
        ss = ss + ss_ref[t]
    r = lax.rsqrt(ss / width + NORM_EPS)
    o_ref[...] = (acc * r + shw_ref[...]).astype(o_ref.dtype)


def _matmul_postnorm(hu, ss, w, shw, is_prompt, out_dtype, tm=1024):
    m, k = hu.shape
    n = w.shape[1]
    tn = PROJ_TN
    row = _mod_row_fn(is_prompt, tm)
    return pl.pallas_call(
        functools.partial(_mm_postnorm_kernel, width=k),
        grid=(n // tn, m // tm),
        in_specs=[pl.BlockSpec((tm, k), lambda j, i: (i, 0)),
                  pl.BlockSpec((k, tn), lambda j, i: (0, j)),
                  pl.BlockSpec((ss.shape[0], tm, 1), lambda j, i: (0, i, 0)),
                  pl.BlockSpec((None, 1, tn), lambda j, i: (row(i), 0, j))],
        out_specs=pl.BlockSpec((tm, tn), lambda j, i: (i, j)),
        out_shape=jax.ShapeDtypeStruct((m, n), out_dtype),
        compiler_params=pltpu.CompilerParams(dimension_semantics=("arbitrary", "arbitrary"),
                                             vmem_limit_bytes=VMEM_LIMIT_ROWS),
        name="proj_postnorm",
    )(hu, w, ss, shw)


def _mm_res_norm_kernel(a1_ref, a2_ref, w_ref, x_ref, g_ref, gain_ref, y_ref, *, n_tiles, tn):
    j = pl.program_id(1)
    x2 = x_ref[...] + g_ref[...] * _dot_halves(a1_ref, a2_ref, w_ref)
    for jj in range(n_tiles):
        @pl.when(j == jj)
        def _(jj=jj):
            y_ref[:, jj * tn:(jj + 1) * tn] = x2

    @pl.when(j == n_tiles - 1)
    def _():
        ss = None
        for jj in range(n_tiles):
            t = y_ref[:, jj * tn:(jj + 1) * tn]
            part = jnp.sum(t * t, axis=-1, keepdims=True)
            ss = part if ss is None else ss + part
        r = lax.rsqrt(ss / (n_tiles * tn) + NORM_EPS)
        for jj in range(n_tiles):
            cols = slice(jj * tn, (jj + 1) * tn)
            y_ref[:, cols] = y_ref[:, cols] * r * gain_ref[:, cols]


def _matmul_residual_norm(a_halves, w, x, mod, is_prompt, gain, tm=512):
    (a1, c1), (a2, c2) = a_halves
    m = a1.shape[0]
    k, n = w.shape
    tn = PROJ_TN
    n_tiles = n // tn
    row = _mod_row_fn(is_prompt, tm)
    gate_block0 = 2 * n_tiles
    return pl.pallas_call(
        functools.partial(_mm_res_norm_kernel, n_tiles=n_tiles, tn=tn),
        grid=(m // tm, n_tiles),
        in_specs=[pl.BlockSpec((tm, k // 2), lambda i, j: (i, c1)),
                  pl.BlockSpec((tm, k // 2), lambda i, j: (i, c2)),
                  pl.BlockSpec((k, tn), lambda i, j: (0, j)),
                  pl.BlockSpec((tm, tn), lambda i, j: (i, j)),
                  pl.BlockSpec((None, 1, tn), lambda i, j: (row(i), 0, gate_block0 + j)),
                  pl.BlockSpec((1, n), lambda i, j: (0, 0))],
        out_specs=pl.BlockSpec((tm, n), lambda i, j: (i, 0)),
        out_shape=jax.ShapeDtypeStruct((m, n), F32),
        compiler_params=pltpu.CompilerParams(dimension_semantics=("arbitrary", "arbitrary"),
                                             vmem_limit_bytes=VMEM_LIMIT_ROWS),
        name="proj_residual_norm",
    )(a1, a2, w, x, mod, gain.reshape(1, n))


def _chunked_softmax(n_rows, chunk, loads, stores, sink2=None):
    inv = []
    for c in range(n_rows // chunk):
        rows = slice(c * chunk, (c + 1) * chunk)
        ts = [load(rows) for load in loads]
        m = functools.reduce(jnp.maximum, [jnp.max(t, axis=-1, keepdims=True) for t in ts])
        if sink2 is not None:
            m = jnp.maximum(m, sink2[rows])
        es = [jnp.exp2(t - m) for t in ts]
        l = functools.reduce(jnp.add, [jnp.sum(e, axis=-1, keepdims=True) for e in es])
        if sink2 is not None:
            l = l + jnp.exp2(sink2[rows] - m)
        for store, e in zip(stores, es):
            store(rows, e.astype(BF16))
        inv.append(1.0 / l)
    return jnp.concatenate(inv, axis=0)


def _ctx_attn_kernel(sink_ref, qg_ref, ka_ref, va_ref, kb_ref, vb_ref, o_ref, s_scr, e_scr):
    n = SEQ
    gate0 = MIX_WIDTH
    n_stack = A_GROUPS

    def softmax(sink2):
        def store(rows, e):
            e_scr[rows, :] = e
        return _chunked_softmax(n_stack * n, CTX_CHUNK, [lambda rows: s_scr[rows, :] * LOGIT_SCALE],
                                [store], sink2)

    def emit(o, g, out_col):
        cols = slice(out_col, out_col + HEAD_DIM)
        gate = qg_ref[:, gate0 + out_col:gate0 + out_col + HEAD_DIM].astype(F32)
        o_ref[:, cols] = (o[g * n:(g + 1) * n] * _silu(gate)).astype(o_ref.dtype)

    for kv in range(A_KV_HEADS):
        cols = slice(kv * HEAD_DIM, (kv + 1) * HEAD_DIM)
        heads = [kv * A_GROUPS + g for g in range(A_GROUPS)]
        q = jnp.concatenate(
            [qg_ref[:, h * HEAD_DIM:(h + 1) * HEAD_DIM] for h in heads], axis=0)
        sink2 = jnp.concatenate(
            [jnp.full((n, 1), sink_ref[h] * LOG2E, F32) for h in heads], axis=0)
        s_scr[...] = _dot_nt(q, ka_ref[:, cols].astype(BF16))
        inv = softmax(sink2)
        o = _dot(e_scr[...], va_ref[:, cols].astype(BF16)) * inv
        for g, h in enumerate(heads):
            emit(o, g, h * HEAD_DIM)
    for h0 in range(0, B_HEADS, n_stack):
        heads = range(h0, h0 + n_stack)
        for g, h in enumerate(heads):
            cols = slice(h * HEAD_DIM, (h + 1) * HEAD_DIM)
            q = qg_ref[:, A_WIDTH + h * HEAD_DIM:A_WIDTH + (h + 1) * HEAD_DIM]
            s_scr[g * n:(g + 1) * n, :] = _dot_nt(q, kb_ref[:, cols].astype(BF16))
        inv = softmax(None)
        o = jnp.concatenate(
            [_dot(e_scr[g * n:(g + 1) * n, :],
                  vb_ref[:, h * HEAD_DIM:(h + 1) * HEAD_DIM].astype(BF16))
             for g, h in enumerate(heads)], axis=0) * inv
        for g, h in enumerate(heads):
            emit(o, g, A_WIDTH + h * HEAD_DIM)


def _ctx_attention(sink, qg, ka, va, kb, vb):
    stack_rows = A_GROUPS * SEQ
    return pl.pallas_call(
        _ctx_attn_kernel,
        grid=(BATCH,),
        in_specs=[
            pl.BlockSpec(memory_space=pltpu.SMEM),
            pl.BlockSpec((SEQ, 2 * MIX_WIDTH), lambda b: (b, 0)),
            pl.BlockSpec((SEQ, A_KV_WIDTH), lambda b: (b, 0)),
            pl.BlockSpec((SEQ, A_KV_WIDTH), lambda b: (b, 0)),
            pl.BlockSpec((SEQ, B_WIDTH), lambda b: (b, 0)),
            pl.BlockSpec((SEQ, B_WIDTH), lambda b: (b, 0)),
        ],
        out_specs=pl.BlockSpec((SEQ, MIX_WIDTH), lambda b: (b, 0)),
        out_shape=jax.ShapeDtypeStruct((N_PROMPT, MIX_WIDTH), BF16),
        scratch_shapes=[pltpu.VMEM((stack_rows, SEQ), F32),
                        pltpu.VMEM((stack_rows, SEQ), BF16)],
        compiler_params=_params(1),
        name="ctx_attention",
    )(sink, qg, ka, va, kb, vb)


def _split_cache_heads(c_ref, scr, n_heads):
    for h in range(n_heads):
        scr[h] = c_ref[pl.ds(h, PAST_LEN, stride=n_heads), :].astype(scr.dtype)


def _win_attn_kernel(sink_ref, q_ref, k_ref, v_ref, ck_ref, cv_ref, gate_ref,
                     cos_ref, sin_ref, o_ref, qs_ref, ks_ref, ck_scr, cv_scr,
                     band_scr, sw_scr, sc_scr, ew_scr, ec_scr):
    kv = pl.program_id(1)
    n = DEC_SEQ

    @pl.when(kv == 0)
    def _():
        _split_cache_heads(ck_ref, ck_scr, A_KV_HEADS)
        _split_cache_heads(cv_ref, cv_scr, A_KV_HEADS)

    cos = cos_ref[...]
    sin = sin_ref[...]
    lane = lax.broadcasted_iota(jnp.int32, (n, HEAD_DIM), 1)
    first_quarter = (lane % (HEAD_DIM // 2)) < (HEAD_DIM // 4)

    def rope(x):
        rot = jnp.where(first_quarter,
                        pltpu.roll(x, HEAD_DIM - HEAD_DIM // 4, 1),
                        pltpu.roll(x, HEAD_DIM // 4, 1))
        return x * cos + rot * sin

    ks_ref[...] = rope(k_ref[...].astype(F32)).astype(BF16)
    for g in range(A_GROUPS):
        qs_ref[g] = (rope(q_ref[:, g * HEAD_DIM:(g + 1) * HEAD_DIM].astype(F32))
                     * LOGIT_SCALE).astype(BF16)

    ck = ck_scr[kv]
    cv = cv_scr[kv]
    rows = A_GROUPS * A_BLOCK
    sink2 = jnp.concatenate(
        [jnp.full((A_BLOCK, 1), sink_ref[kv * A_GROUPS + g] * LOG2E, F32)
         for g in range(A_GROUPS)], axis=0)
    qi = lax.broadcasted_iota(jnp.int32, (rows, 3 * A_BLOCK), 0) % A_BLOCK
    rel = lax.broadcasted_iota(jnp.int32, (rows, 3 * A_BLOCK), 1) - A_BLOCK - qi
    band_scr[...] = jnp.where(jnp.abs(rel) <= A_HALF_WIN, 0.0, NEG_INF)
    for blk in range(n // A_BLOCK):
        r0 = blk * A_BLOCK
        lo = max(0, r0 - A_BLOCK)
        hi = min(n, r0 + 2 * A_BLOCK)
        span = hi - lo
        b0 = lo - (r0 - A_BLOCK)
        q = jnp.concatenate([qs_ref[g, r0:r0 + A_BLOCK, :] for g in range(A_GROUPS)], axis=0)
        sw_scr[:, :span] = _dot_nt(q, ks_ref[lo:hi, :])
        sc_scr[...] = _dot_nt(q, ck)

        def load_w(r, span=span, b0=b0):
            return sw_scr[r, :span] + band_scr[r, b0:b0 + span]

        def store_w(r, e, span=span):
            ew_scr[r, :span] = e

        def store_c(r, e):
            ec_scr[r, :] = e

        inv = _chunked_softmax(rows, WIN_CHUNK, [load_w, lambda r: sc_scr[r, :]],
                               [store_w, store_c], sink2)
        o = (_dot(ew_scr[:, :span], v_ref[lo:hi, :]) + _dot(ec_scr[...], cv)) * inv
        for g in range(A_GROUPS):
            hc = slice(g * HEAD_DIM, (g + 1) * HEAD_DIM)
            gate = gate_ref[r0:r0 + A_BLOCK, hc].astype(F32)
            o_ref[r0:r0 + A_BLOCK, hc] = (
                o[g * A_BLOCK:(g + 1) * A_BLOCK] * _silu(gate)).astype(o_ref.dtype)


def _win_attention(sink, proj, cache_k, cache_v, cos, sin_signed):
    gw = A_GROUPS * HEAD_DIM
    stack_rows = A_GROUPS * A_BLOCK
    n = DEC_SEQ
    cache_spec = pl.BlockSpec((None, PAST_LEN * A_KV_HEADS, HEAD_DIM), lambda b, k: (b, 0, 0))
    return pl.pallas_call(
        _win_attn_kernel,
        grid=(DEC_BATCH, A_KV_HEADS),
        in_specs=[
            pl.BlockSpec(memory_space=pltpu.SMEM),
            pl.BlockSpec((n, gw), lambda b, k: (b, COL_QA // gw + k)),
            pl.BlockSpec((n, HEAD_DIM), lambda b, k: (b, COL_KA // HEAD_DIM + k)),
            pl.BlockSpec((n, HEAD_DIM), lambda b, k: (b, COL_VA // HEAD_DIM + k)),
            cache_spec,
            cache_spec,
            pl.BlockSpec((n, gw), lambda b, k: (b, COL_GATE // gw + k)),
            pl.BlockSpec((n, HEAD_DIM), lambda b, k: (0, 0)),
            pl.BlockSpec((n, HEAD_DIM), lambda b, k: (0, 0)),
        ],
        out_specs=pl.BlockSpec((n, gw), lambda b, k: (b, k)),
        out_shape=jax.ShapeDtypeStruct((N_SAMPLE, A_WIDTH), BF16),
        scratch_shapes=[pltpu.VMEM((A_GROUPS, n, HEAD_DIM), BF16),
                        pltpu.VMEM((n, HEAD_DIM), BF16),
                        pltpu.VMEM((A_KV_HEADS, PAST_LEN, HEAD_DIM), BF16),
                        pltpu.VMEM((A_KV_HEADS, PAST_LEN, HEAD_DIM), BF16),
                        pltpu.VMEM((stack_rows, 3 * A_BLOCK), F32),
                        pltpu.VMEM((stack_rows, 3 * A_BLOCK), F32),
                        pltpu.VMEM((stack_rows, PAST_LEN), F32),
                        pltpu.VMEM((stack_rows, 3 * A_BLOCK), BF16),
                        pltpu.VMEM((stack_rows, PAST_LEN), BF16)],
        compiler_params=_params(2),
        name="win_attention",
    )(sink, proj, proj, proj, cache_k, cache_v, proj, cos, sin_signed)


def _bias_kernel(rpb_ref, o_ref):
    h = pl.program_id(0)
    shape = (GRID_W, 2 * GRID_W)
    c = lax.broadcasted_iota(jnp.int32, shape, 0)
    j2 = lax.broadcasted_iota(jnp.int32, shape, 1)
    kc = j2 % GRID_W
    second = j2 >= GRID_W
    col_start = jnp.clip(c - B_WIN_COLS // 2, 0, GRID_W - B_WIN_COLS)
    ok = (kc >= col_start) & (kc < col_start + B_WIN_COLS)
    dc = kc - c + B_WIN_COLS - 1
    base = h * (RPB_ROWS * RPB_COLS)
    pair_scr = []
    for i in range(N_BIAS_PAIRS):
        acc = jnp.full(shape, NEG_INF, F32)
        for d in range(RPB_COLS):
            val = jnp.where(second, rpb_ref[base + (i + 1) * RPB_COLS + d],
                            rpb_ref[base + i * RPB_COLS + d]) * LOG2E
            acc = jnp.where(ok & (dc == d), val, acc)
        pair_scr.append(acc)
    for r in range(GRID_ROWS):
        dr0 = _na_key_row0(r) - r + B_WIN_ROWS - 1
        for i in range(B_WIN_ROWS // 2):
            o_ref[r * GRID_W:(r + 1) * GRID_W, i * 2 * GRID_W:(i + 1) * 2 * GRID_W] = (
                pair_scr[dr0 + 2 * i])


def _expand_bias(rpb):
    kw = B_WIN_ROWS * GRID_W
    return pl.pallas_call(
        _bias_kernel,
        grid=(B_HEADS,),
        in_specs=[pl.BlockSpec(memory_space=pltpu.SMEM)],
        out_specs=pl.BlockSpec((None, DEC_SEQ, kw), lambda h: (h, 0, 0)),
        out_shape=jax.ShapeDtypeStruct((B_HEADS, DEC_SEQ, kw), F32),
        compiler_params=_params(1),
        name="expand_bias",
    )(rpb.reshape(-1))


def _na_key_row0(r):
    return min(max(r - B_WIN_ROWS // 2, 0), GRID_ROWS - B_WIN_ROWS)


def _na_row_groups():
    groups = []
    for r in range(GRID_ROWS):
        rs = _na_key_row0(r)
        if groups and groups[-1][2] == rs:
            groups[-1] = (groups[-1][0], r + 1, rs)
        else:
            groups.append((r, r + 1, rs))
    return groups


def _na_attn_kernel(q_ref, k_ref, v_ref, ck_ref, cv_ref, gate_ref, bias_ref, o_ref,
                    ck_scr, cv_scr, sn_scr, sc_scr, en_scr, ec_scr):
    kw = B_WIN_ROWS * GRID_W
    h = pl.program_id(1)

    @pl.when(h == 0)
    def _():
        _split_cache_heads(ck_ref, ck_scr, B_HEADS)
        _split_cache_heads(cv_ref, cv_scr, B_HEADS)

    groups = _na_row_groups()
    sc_scr[...] = _dot_nt(q_ref[...], ck_scr[h])
    for r0, r1, rs in groups:
        sn_scr[r0 * GRID_W:r1 * GRID_W, :] = _dot_nt(
            q_ref[r0 * GRID_W:r1 * GRID_W, :], k_ref[rs * GRID_W:rs * GRID_W + kw, :])

    def store_n(r, e):
        en_scr[r, :] = e

    def store_c(r, e):
        ec_scr[r, :] = e

    inv = _chunked_softmax(
        DEC_SEQ, NA_CHUNK,
        [lambda r: sn_scr[r, :] * LOGIT_SCALE + bias_ref[r, :], lambda r: sc_scr[r, :] * LOGIT_SCALE],
        [store_n, store_c])
    o_n = jnp.concatenate(
        [_dot(en_scr[r0 * GRID_W:r1 * GRID_W, :], v_ref[rs * GRID_W:rs * GRID_W + kw, :])
         for r0, r1, rs in groups], axis=0)
    o = (o_n + _dot(ec_scr[...], cv_scr[h])) * inv
    o_ref[...] = (o * _silu(gate_ref[...].astype(F32))).astype(o_ref.dtype)


def _na_attention(proj, cache_k, cache_v, bias):
    n = DEC_SEQ
    hd = HEAD_DIM
    cache_spec = pl.BlockSpec((None, PAST_LEN * B_HEADS, hd), lambda b, h: (b, 0, 0))
    return pl.pallas_call(
        _na_attn_kernel,
        grid=(DEC_BATCH, B_HEADS),
        in_specs=[
            pl.BlockSpec((n, hd), lambda b, h: (b, COL_QB // hd + h)),
            pl.BlockSpec((n, hd), lambda b, h: (b, COL_KB // hd + h)),
            pl.BlockSpec((n, hd), lambda b, h: (b, COL_VB // hd + h)),
            cache_spec,
            cache_spec,
            pl.BlockSpec((n, hd), lambda b, h: (b, (COL_GATE + A_WIDTH) // hd + h)),
            pl.BlockSpec((None, n, B_WIN_ROWS * GRID_W), lambda b, h: (h, 0, 0)),
        ],
        out_specs=pl.BlockSpec((n, hd), lambda b, h: (b, h)),
        out_shape=jax.ShapeDtypeStruct((N_SAMPLE, B_WIDTH), BF16),
        scratch_shapes=[pltpu.VMEM((B_HEADS, PAST_LEN, hd), BF16),
                        pltpu.VMEM((B_HEADS, PAST_LEN, hd), BF16),
                        pltpu.VMEM((n, B_WIN_ROWS * GRID_W), F32),
                        pltpu.VMEM((n, PAST_LEN), F32),
                        pltpu.VMEM((n, B_WIN_ROWS * GRID_W), BF16),
                        pltpu.VMEM((n, PAST_LEN), BF16)],
        compiler_params=_params(2),
        name="na_attention",
    )(proj, proj, proj, cache_k, cache_v, proj, bias)


def _pool_kernel(u_ref, gate_ref, band_ref, inv_count_ref, w_ref, scale_ref, o_ref, *, seq, rows):
    cb = POOL_BAND_BLOCK
    parts = []
    for i in range(rows // cb):
        lo = i * cb if seq <= cb else max(0, (i - 1) * cb)
        hi = (i + 1) * cb if seq <= cb else min(rows, (i + 2) * cb)
        parts.append(_dot(band_ref[i * cb:(i + 1) * cb, lo:hi], u_ref[lo:hi, :]))
    wsum = jnp.concatenate(parts, axis=0)
    pooled = wsum * inv_count_ref[...] - u_ref[...].astype(F32)
    y = _dot(pooled.astype(BF16), w_ref[...]) * scale_ref[...]
    o_ref[...] = (y * _silu(gate_ref[...].astype(F32))).astype(o_ref.dtype)


def _pool_operators(seq, rows):
    t = jnp.arange(rows)[:, None]
    j = jnp.arange(rows)[None, :]
    same_seq = (t // seq) == (j // seq)
    pos = t % seq
    bands, inv_counts = [], []
    for window in POOL_WINDOWS:
        half = window // 2
        assert half <= POOL_BAND_BLOCK
        bands.append(((j - t >= -half) & (j - t < half) & same_seq).astype(BF16))
        inv_counts.append(1.0 / (jnp.minimum(pos + half, seq) - jnp.maximum(pos - half, 0)).astype(F32))
    return jnp.stack(bands), jnp.stack(inv_counts)


def _pool_mixer(ug, w_grp, scale, seq, rows=1024):
    m = ug.shape[0]
    n_groups = len(POOL_WINDOWS)
    pg = POOL_GROUP
    band, inv_count = _pool_operators(seq, rows)
    return pl.pallas_call(
        functools.partial(_pool_kernel, seq=seq, rows=rows),
        grid=(n_groups, m // rows),
        in_specs=[
            pl.BlockSpec((rows, pg), lambda g, i: (i, g)),
            pl.BlockSpec((rows, pg), lambda g, i: (i, n_groups + g)),
            pl.BlockSpec((None, rows, rows), lambda g, i: (g, 0, 0)),
            pl.BlockSpec((None, rows, 1), lambda g, i: (g, 0, 0)),
            pl.BlockSpec((None, pg, pg), lambda g, i: (g, 0, 0)),
            pl.BlockSpec((1, pg), lambda g, i: (0, g)),
        ],
        out_specs=pl.BlockSpec((rows, pg), lambda g, i: (i, g)),
        out_shape=jax.ShapeDtypeStruct((m, n_groups * pg), BF16),
        compiler_params=_params(2),
        name="pool_mixer",
    )(ug, ug, band, inv_count, w_grp, scale.reshape(1, -1))


def _rope_tables():
    t = jnp.arange(DEC_SEQ)
    quarter = HEAD_DIM // 4
    inv_freq = ROPE_BASE ** (-jnp.arange(quarter, dtype=F32) / quarter)
    ang_r = (t // GRID_W).astype(F32)[:, None] * inv_freq
    ang_c = (t % GRID_W).astype(F32)[:, None] * inv_freq
    ang = jnp.concatenate([ang_r, ang_r, ang_c, ang_c], axis=-1)
    sign = jnp.tile(jnp.concatenate([-jnp.ones((quarter,), F32), jnp.ones((quarter,), F32)]), 2)
    return jnp.cos(ang), jnp.sin(ang) * sign


def kernel(x_prompt, x_sample, c, cache_a_k, cache_a_v, cache_b_k, cache_b_v, c_ctx,
           w_ada, b_ada, norm_g, w_in_attn, a_sink, b_rpb, w_out_attn,
           w_in_pool, w_grp_pool, pool_scale, w_out_pool, final_g):
    d = D_MODEL
    xp = x_prompt.reshape(N_PROMPT, d)
    xs = x_sample.reshape(N_SAMPLE, d)

    w_in0 = w_in_attn[0].astype(BF16)

    cond = jnp.zeros((MOD_ROWS, d), F32).at[:DEC_BATCH].set(c).at[CTX_MOD_ROW].set(c_ctx)
    mod = _ada(cond, w_ada, b_ada)
    mod0 = mod[0].reshape(MOD_ROWS, 1, 3 * d)
    mod1 = mod[1].reshape(MOD_ROWS, 1, 3 * d)

    hp, hs = _norm_mod(xp, xs, norm_g[0], mod0)

    ka, va, kb, vb, new_a_k, new_a_v, new_b_k, new_b_v = _kv_projection(hp, w_in0)
    qg_tiles = (_col_tiles(COL_QA, A_WIDTH) + _col_tiles(COL_QB, B_WIDTH)
                + _col_tiles(COL_GATE, MIX_WIDTH))
    qg_p, w_in1 = _matmul(hp, w_in0, qg_tiles, BF16, cast_weights=(w_in_pool[0],))
    n_groups = len(POOL_WINDOWS)
    proj_s, w_out0, w_out1, w_grp = _matmul(
        hs, w_in0, _col_tiles(0, ATTN_IN_WIDTH), BF16,
        cast_weights=(w_out_attn[0], w_out_pool[0],
                      w_grp_pool[0].reshape(n_groups * POOL_GROUP, POOL_GROUP)))
    w_grp = w_grp.reshape(n_groups, POOL_GROUP, POOL_GROUP)

    sink = a_sink[0]
    og_p = _ctx_attention(sink, qg_p, ka, va, kb, vb)

    cos, sin_signed = _rope_tables()
    flat = lambda cache: cache.reshape(DEC_BATCH, -1, HEAD_DIM)
    oa_s = _win_attention(sink, proj_s, flat(cache_a_k), flat(cache_a_v), cos, sin_signed)
    ob_s = _na_attention(proj_s, flat(cache_b_k), flat(cache_b_v), _expand_bias(b_rpb[0]))

    xp1, hu_p, ss_p = _matmul_residual_prenorm([(og_p, 0), (og_p, 1)], w_out0, xp, mod0, True,
                                               norm_g[1], mod1)
    xs1, hu_s, ss_s = _matmul_residual_prenorm([(oa_s, 0), (ob_s, 0)], w_out0, xs, mod0, False,
                                               norm_g[1], mod1)

    shw = _shift_projection(mod[1], w_in1).reshape(MOD_ROWS, 1, 2 * d)
    ug_p = _matmul_postnorm(hu_p, ss_p, w_in1, shw, True, BF16)
    ug_s = _matmul_postnorm(hu_s, ss_s, w_in1, shw, False, BF16)
    y_p = _pool_mixer(ug_p, w_grp, pool_scale[0], SEQ)
    y_s = _pool_mixer(ug_s, w_grp, pool_scale[0], DEC_SEQ)
    y_prompt = _matmul_residual_norm([(y_p, 0), (y_p, 1)], w_out1, xp1, mod1, True,
                                     final_g).reshape(BATCH, SEQ, d)
    y_sample = _matmul_residual_norm([(y_s, 0), (y_s, 1)], w_out1, xs1, mod1, False,
                                     final_g).reshape(DEC_BATCH, DEC_SEQ, d)

    return (y_prompt, y_sample, new_a_k, new_a_v, new_b_k, new_b_v)
```

```python
import functools

import jax
import jax.numpy as jnp
from jax import lax
from jax.experimental import pallas as pl
from jax.experimental.pallas import tpu as pltpu

F32 = jnp.float32
BF16 = jnp.bfloat16

D_MODEL = 4096
BATCH = 32
SEQ = 256
DEC_BATCH = 8
DEC_SEQ = 1024
PAST_LEN = 512
GRID_W = 64
HEAD_DIM = 128
A_Q_HEADS = 16
A_KV_HEADS = 4
A_GROUPS = 4
A_HALF_WIN = 128
A_BLOCK = 128
B_HEADS = 16
B_WIN_ROWS = 8
B_WIN_COLS = 16
A_WIDTH = A_Q_HEADS * HEAD_DIM
A_KV_WIDTH = A_KV_HEADS * HEAD_DIM
B_WIDTH = B_HEADS * HEAD_DIM
MIX_WIDTH = A_WIDTH + B_WIDTH
POOL_WINDOWS = (2, 4, 8, 16)
POOL_GROUP = 1024
POOL_BAND_BLOCK = 256
CAST_CHUNK_ROWS = 64
ROPE_BASE = 10000.0
NORM_EPS = 1e-6
NEG_INF = -1e30
ATTN_SCALE = HEAD_DIM ** -0.5
LOG2E = 1.4426950408889634
LOGIT_SCALE = ATTN_SCALE * LOG2E

N_PROMPT = BATCH * SEQ
N_SAMPLE = DEC_BATCH * DEC_SEQ
MOD_ROWS = 16
CTX_MOD_ROW = DEC_BATCH
GRID_ROWS = DEC_SEQ // GRID_W
N_BIAS_PAIRS = 2 * B_WIN_ROWS - 2
RPB_ROWS = 2 * B_WIN_ROWS - 1
RPB_COLS = 2 * B_WIN_COLS - 1

COL_QA = 0
COL_KA = COL_QA + A_WIDTH
COL_VA = COL_KA + A_KV_WIDTH
COL_QB = COL_VA + A_KV_WIDTH
COL_KB = COL_QB + B_WIDTH
COL_VB = COL_KB + B_WIDTH
COL_GATE = COL_VB + B_WIDTH
ATTN_IN_WIDTH = COL_GATE + MIX_WIDTH
PROJ_TN = 1024


def _col_tiles(col0, width):
    assert col0 % PROJ_TN == 0 and width % PROJ_TN == 0
    return tuple(range(col0 // PROJ_TN, (col0 + width) // PROJ_TN))


def _tile_lookup(tiles):
    runs = []
    for jj, t in enumerate(tiles):
        if not runs or runs[-1][1] != t - jj:
            runs.append((jj, t - jj))

    def lookup(j):
        off = runs[0][1]
        for start, o in runs[1:]:
            off = jnp.where(j >= start, o, off)
        return j + off
    return lookup

VMEM_LIMIT = 48 * 1024 * 1024
VMEM_LIMIT_ROWS = 56 * 1024 * 1024


def _params(n_grid_dims):
    return pltpu.CompilerParams(
        dimension_semantics=("arbitrary",) * n_grid_dims,
        vmem_limit_bytes=VMEM_LIMIT,
    )


def _silu(x):
    return 0.5 * x * (1.0 + jnp.tanh(0.5 * x))


def _dot_nt(a, b):
    return lax.dot_general(a, b, (((1,), (1,)), ((), ())), preferred_element_type=F32)


def _dot(a, b):
    return jnp.dot(a, b, preferred_element_type=F32)


def _ada_kernel(cond_ref, w_ref, b_ref, o_ref):
    a = _silu(cond_ref[...]).astype(BF16)
    o_ref[...] = _dot(a, w_ref[...].astype(BF16)) + b_ref[...]


def _ada(cond, w_ada, b_ada, tn=512):
    depth, d, n = w_ada.shape
    return pl.pallas_call(
        _ada_kernel,
        grid=(depth, n // tn),
        in_specs=[
            pl.BlockSpec((MOD_ROWS, d), lambda l, j: (0, 0)),
            pl.BlockSpec((None, d, tn), lambda l, j: (l, 0, j)),
            pl.BlockSpec((None, 1, tn), lambda l, j: (l, 0, j)),
        ],
        out_specs=pl.BlockSpec((None, MOD_ROWS, tn), lambda l, j: (l, 0, j)),
        out_shape=jax.ShapeDtypeStruct((depth, MOD_ROWS, n), F32),
        compiler_params=_params(2),
        name="ada",
    )(cond, w_ada, b_ada.reshape(depth, 1, n))


def _mod_row_fn(is_prompt, tm):
    if is_prompt:
        return lambda i: CTX_MOD_ROW
    return lambda i: (i * tm) // DEC_SEQ


def _norm_mod_kernel(xp_ref, xs_ref, g_ref, sh_ref, sc_ref, hp_ref, hs_ref, *, n_prompt_tiles):
    def norm_mod(x_ref, o_ref):
        x = x_ref[...]
        ms = jnp.mean(x * x, axis=-1, keepdims=True)
        y = x * lax.rsqrt(ms + NORM_EPS) * g_ref[...]
        o_ref[...] = (y * (1.0 + sc_ref[...]) + sh_ref[...]).astype(o_ref.dtype)

    i = pl.program_id(0)

    @pl.when(i < n_prompt_tiles)
    def _():
        norm_mod(xp_ref, hp_ref)

    @pl.when(i >= n_prompt_tiles)
    def _():
        norm_mod(xs_ref, hs_ref)


def _norm_mod(xp, xs, gain, mod, tm=256):
    d = xp.shape[1]
    n_p, n_s = xp.shape[0] // tm, xs.shape[0] // tm
    p_blk = lambda i: (jnp.minimum(i, n_p - 1), 0)
    s_blk = lambda i: (jnp.maximum(i - n_p, 0), 0)
    row = lambda i: jnp.where(i < n_p, CTX_MOD_ROW, (jnp.maximum(i - n_p, 0) * tm) // DEC_SEQ)
    return pl.pallas_call(
        functools.partial(_norm_mod_kernel, n_prompt_tiles=n_p),
        grid=(n_p + n_s,),
        in_specs=[
            pl.BlockSpec((tm, d), p_blk),
            pl.BlockSpec((tm, d), s_blk),
            pl.BlockSpec((1, d), lambda i: (0, 0)),
            pl.BlockSpec((None, 1, d), lambda i: (row(i), 0, 0)),
            pl.BlockSpec((None, 1, d), lambda i: (row(i), 0, 1)),
        ],
        out_specs=[pl.BlockSpec((tm, d), p_blk), pl.BlockSpec((tm, d), s_blk)],
        out_shape=[jax.ShapeDtypeStruct(xp.shape, BF16), jax.ShapeDtypeStruct(xs.shape, BF16)],
        compiler_params=_params(1),
        name="norm_mod",
    )(xp, xs, gain.reshape(1, d), mod, mod)


def _mm_kernel(a_ref, w_ref, *refs, n_casts):
    cast_in, o_ref, cast_out = refs[:n_casts], refs[n_casts], refs[n_casts + 1:]
    o_ref[...] = _dot(a_ref[...], w_ref[...]).astype(o_ref.dtype)
    for src, dst in zip(cast_in, cast_out):
        dst[...] = src[...].astype(dst.dtype)


def _matmul(a, w, w_tiles, out_dtype, cast_weights=()):
    m, k = a.shape
    tn = PROJ_TN
    tm = 1024 if out_dtype == BF16 else 512
    assert m % tm == 0
    w_tile = _tile_lookup(w_tiles)
    ncols = len(w_tiles) * tn
    n_i = m // tm
    n_steps = len(w_tiles) * n_i
    in_specs = [pl.BlockSpec((tm, k), lambda j, i: (i, 0)),
                pl.BlockSpec((k, tn), lambda j, i: (0, w_tile(j)))]
    out_specs = [pl.BlockSpec((tm, tn), lambda j, i: (i, j))]
    out_shape = [jax.ShapeDtypeStruct((m, ncols), out_dtype)]
    for cw in cast_weights:
        rows, cols = cw.shape
        chunk = CAST_CHUNK_ROWS
        n_chunks = rows // chunk
        assert rows % chunk == 0 and n_chunks <= n_steps
        spec = pl.BlockSpec((chunk, cols),
                            lambda j, i, n_chunks=n_chunks: (jnp.minimum(j * n_i + i, n_chunks - 1), 0))
        in_specs.append(spec)
        out_specs.append(spec)
        out_shape.append(jax.ShapeDtypeStruct((rows, cols), BF16))
    outs = pl.pallas_call(
        functools.partial(_mm_kernel, n_casts=len(cast_weights)),
        grid=(ncols // tn, n_i),
        in_specs=in_specs,
        out_specs=out_specs,
        out_shape=out_shape,
        compiler_params=pltpu.CompilerParams(
            dimension_semantics=("arbitrary", "arbitrary"),
            vmem_limit_bytes=VMEM_LIMIT_ROWS if cast_weights else VMEM_LIMIT),
        name="proj",
    )(a, w, *cast_weights)
    return outs if cast_weights else outs[0]


def _kv_proj_kernel(a_ref, w_ref, ka_ref, va_ref, kb_ref, vb_ref, ka5_ref, va5_ref, kb5_ref, vb5_ref,
                    *, kb_tile0, vb_tile0):
    j = pl.program_id(0)
    acc = _dot(a_ref[...], w_ref[...])

    def emit(cols, out_ref, out5_ref):
        part = acc[:, cols]
        out_ref[...] = part
        out5_ref[...] = part.reshape(out5_ref.shape)

    @pl.when(j < kb_tile0)
    def _():
        emit(slice(0, A_KV_WIDTH), ka_ref, ka5_ref)
        emit(slice(A_KV_WIDTH, 2 * A_KV_WIDTH), va_ref, va5_ref)

    @pl.when((j >= kb_tile0) & (j < vb_tile0))
    def _():
        emit(slice(None), kb_ref, kb5_ref)

    @pl.when(j >= vb_tile0)
    def _():
        emit(slice(None), vb_ref, vb5_ref)


def _kv_projection(a, w, tm=512):
    m, k = a.shape
    tn = PROJ_TN
    assert COL_VA == COL_KA + A_KV_WIDTH and 2 * A_KV_WIDTH == tn and tm % SEQ == 0
    w_tiles = _col_tiles(COL_KA, tn) + _col_tiles(COL_KB, B_WIDTH) + _col_tiles(COL_VB, B_WIDTH)
    w_tile = _tile_lookup(w_tiles)
    kb_tile0 = 1
    vb_tile0 = kb_tile0 + B_WIDTH // tn
    end_tile = len(w_tiles)
    n_i = m // tm
    last = n_i - 1

    def parked(j, i, t0, t1):
        row = jnp.where(j < t0, 0, jnp.where(j < t1, i, last))
        return row, jnp.clip(j - t0, 0, t1 - t0 - 1)

    def parked5(j, i, t0, t1):
        row, col = parked(j, i, t0, t1)
        return row, 0, 0, col, 0

    req = tm // SEQ
    a_spec = pl.BlockSpec((tm, A_KV_WIDTH), lambda j, i: parked(j, i, 0, kb_tile0))
    a_spec5 = pl.BlockSpec((req, None, SEQ, A_KV_HEADS, HEAD_DIM),
                           lambda j, i: parked5(j, i, 0, kb_tile0))
    b_block5 = (req, None, SEQ, tn // HEAD_DIM, HEAD_DIM)
    cache_shape_a = (m // SEQ, 1, SEQ, A_KV_HEADS, HEAD_DIM)
    cache_shape_b = (m // SEQ, 1, SEQ, B_HEADS, HEAD_DIM)
    return pl.pallas_call(
        functools.partial(_kv_proj_kernel, kb_tile0=kb_tile0, vb_tile0=vb_tile0),
        grid=(end_tile, n_i),
        in_specs=[pl.BlockSpec((tm, k), lambda j, i: (i, 0)),
                  pl.BlockSpec((k, tn), lambda j, i: (0, w_tile(j)))],
        out_specs=[a_spec, a_spec,
                   pl.BlockSpec((tm, tn), lambda j, i: parked(j, i, kb_tile0, vb_tile0)),
                   pl.BlockSpec((tm, tn), lambda j, i: parked(j, i, vb_tile0, end_tile)),
                   a_spec5, a_spec5,
                   pl.BlockSpec(b_block5, lambda j, i: parked5(j, i, kb_tile0, vb_tile0)),
                   pl.BlockSpec(b_block5, lambda j, i: parked5(j, i, vb_tile0, end_tile))],
        out_shape=[jax.ShapeDtypeStruct((m, A_KV_WIDTH), F32),
                   jax.ShapeDtypeStruct((m, A_KV_WIDTH), F32),
                   jax.ShapeDtypeStruct((m, B_WIDTH), F32),
                   jax.ShapeDtypeStruct((m, B_WIDTH), F32),
                   jax.ShapeDtypeStruct(cache_shape_a, F32),
                   jax.ShapeDtypeStruct(cache_shape_a, F32),
                   jax.ShapeDtypeStruct(cache_shape_b, F32),
                   jax.ShapeDtypeStruct(cache_shape_b, F32)],
        compiler_params=pltpu.CompilerParams(dimension_semantics=("arbitrary", "arbitrary"),
                                             vmem_limit_bytes=VMEM_LIMIT_ROWS),
        name="kv_proj",
    )(a, w)


def _dot_halves(a1_ref, a2_ref, w_ref):
    kh = a1_ref.shape[1]
    return _dot(a1_ref[...], w_ref[:kh, :]) + _dot(a2_ref[...], w_ref[kh:, :])


def _mm_res_prenorm_kernel(a1_ref, a2_ref, w_ref, x_ref, g_ref, gain_ref, sc_ref,
                           x1_ref, hu_ref, ss_ref):
    x1 = x_ref[...] + g_ref[...] * _dot_halves(a1_ref, a2_ref, w_ref)
    x1_ref[...] = x1
    hu_ref[...] = (x1 * (gain_ref[...] * (1.0 + sc_ref[...]))).astype(hu_ref.dtype)
    ss_ref[...] = jnp.sum(x1 * x1, axis=-1, keepdims=True)


def _matmul_residual_prenorm(a_halves, w, x, mod, is_prompt, gain, next_mod, tm=512):
    (a1, c1), (a2, c2) = a_halves
    m = a1.shape[0]
    k, n = w.shape
    tn = PROJ_TN
    n_tiles = n // tn
    row = _mod_row_fn(is_prompt, tm)
    scale_block0, gate_block0 = n_tiles, 2 * n_tiles
    tile = pl.BlockSpec((tm, tn), lambda j, i: (i, j))
    return pl.pallas_call(
        _mm_res_prenorm_kernel,
        grid=(n_tiles, m // tm),
        in_specs=[pl.BlockSpec((tm, k // 2), lambda j, i: (i, c1)),
                  pl.BlockSpec((tm, k // 2), lambda j, i: (i, c2)),
                  pl.BlockSpec((k, tn), lambda j, i: (0, j)),
                  tile,
                  pl.BlockSpec((None, 1, tn), lambda j, i: (row(i), 0, gate_block0 + j)),
                  pl.BlockSpec((1, tn), lambda j, i: (0, j)),
                  pl.BlockSpec((None, 1, tn), lambda j, i: (row(i), 0, scale_block0 + j))],
        out_specs=[tile, tile, pl.BlockSpec((None, tm, 1), lambda j, i: (j, i, 0))],
        out_shape=[jax.ShapeDtypeStruct((m, n), F32), jax.ShapeDtypeStruct((m, n), BF16),
                   jax.ShapeDtypeStruct((n_tiles, m, 1), F32)],
        compiler_params=_params(2),
        name="proj_residual_prenorm",
    )(a1, a2, w, x, mod, gain.reshape(1, n), next_mod)


def _shift_proj_kernel(sh_ref, w_ref, o_ref):
    o_ref[...] = _dot(sh_ref[...].astype(BF16), w_ref[...])


def _shift_projection(mod2d, w):
    k, n = w.shape
    tn = PROJ_TN
    return pl.pallas_call(
        _shift_proj_kernel,
        grid=(n // tn,),
        in_specs=[pl.BlockSpec((MOD_ROWS, k), lambda j: (0, 0)),
                  pl.BlockSpec((k, tn), lambda j: (0, j))],
        out_specs=pl.BlockSpec((MOD_ROWS, tn), lambda j: (0, j)),
        out_shape=jax.ShapeDtypeStruct((MOD_ROWS, n), F32),
        compiler_params=_params(1),
        name="shift_proj",
    )(mod2d, w)


def _mm_postnorm_kernel(a_ref, w_ref, ss_ref, shw_ref, o_ref, *, width):
    acc = _dot(a_ref[...], w_ref[...])
    ss = ss_ref[0]
    for t in range(1, ss_ref.shape[0]):
        ss = ss + ss_ref[t]
    r = lax.rsqrt(ss / width + NORM_EPS)
    o_ref[...] = (acc * r + shw_ref[...]).astype(o_ref.dtype)


def _matmul_postnorm(hu, ss, w, shw, is_prompt, out_dtype, tm=1024):
    m, k = hu.shape
    n = w.shape[1]
    tn = PROJ_TN
    row = _mod_row_fn(is_prompt, tm)
    return pl.pallas_call(
        functools.partial(_mm_postnorm_kernel, width=k),
        grid=(n // tn, m // tm),
        in_specs=[pl.BlockSpec((tm, k), lambda j, i: (i, 0)),
                  pl.BlockSpec((k, tn), lambda j, i: (0, j)),
                  pl.BlockSpec((ss.shape[0], tm, 1), lambda j, i: (0, i, 0)),
                  pl.BlockSpec((None, 1, tn), lambda j, i: (row(i), 0, j))],
        out_specs=pl.BlockSpec((tm, tn), lambda j, i: (i, j)),
        out_shape=jax.ShapeDtypeStruct((m, n), out_dtype),
        compiler_params=pltpu.CompilerParams(dimension_semantics=("arbitrary", "arbitrary"),
                                             vmem_limit_bytes=VMEM_LIMIT_ROWS),
        name="proj_postnorm",
    )(hu, w, ss, shw)


def _mm_res_norm_kernel(a1_ref, a2_ref, w_ref, x_ref, g_ref, gain_ref, y_ref, *, n_tiles, tn):
    j = pl.program_id(1)
    x2 = x_ref[...] + g_ref[...] * _dot_halves(a1_ref, a2_ref, w_ref)
    for jj in range(n_tiles):
        @pl.when(j == jj)
        def _(jj=jj):
            y_ref[:, jj * tn:(jj + 1) * tn] = x2

    @pl.when(j == n_tiles - 1)
    def _():
        ss = None
        for jj in range(n_tiles):
            t = y_ref[:, jj * tn:(jj + 1) * tn]
            part = jnp.sum(t * t, axis=-1, keepdims=True)
            ss = part if ss is None else ss + part
        r = lax.rsqrt(ss / (n_tiles * tn) + NORM_EPS)
        for jj in range(n_tiles):
            cols = slice(jj * tn, (jj + 1) * tn)
            y_ref[:, cols] = y_ref[:, cols] * r * gain_ref[:, cols]


def _matmul_residual_norm(a_halves, w, x, mod, is_prompt, gain, tm=512):
    (a1, c1), (a2, c2) = a_halves
    m = a1.shape[0]
    k, n = w.shape
    tn = PROJ_TN
    n_tiles = n // tn
    row = _mod_row_fn(is_prompt, tm)
    gate_block0 = 2 * n_tiles
    return pl.pallas_call(
        functools.partial(_mm_res_norm_kernel, n_tiles=n_tiles, tn=tn),
        grid=(m // tm, n_tiles),
        in_specs=[pl.BlockSpec((tm, k // 2), lambda i, j: (i, c1)),
                  pl.BlockSpec((tm, k // 2), lambda i, j: (i, c2)),
                  pl.BlockSpec((k, tn), lambda i, j: (0, j)),
                  pl.BlockSpec((tm, tn), lambda i, j: (i, j)),
                  pl.BlockSpec((None, 1, tn), lambda i, j: (row(i), 0, gate_block0 + j)),
                  pl.BlockSpec((1, n), lambda i, j: (0, 0))],
        out_specs=pl.BlockSpec((tm, n), lambda i, j: (i, 0)),
        out_shape=jax.ShapeDtypeStruct((m, n), F32),
        compiler_params=pltpu.CompilerParams(dimension_semantics=("arbitrary", "arbitrary"),
                                             vmem_limit_bytes=VMEM_LIMIT_ROWS),
        name="proj_residual_norm",
    )(a1, a2, w, x, mod, gain.reshape(1, n))


def _softmax_numerators(logits, sink2=None):
    m = functools.reduce(jnp.maximum, [jnp.max(t, axis=-1, keepdims=True) for t in logits])
    if sink2 is not None:
        m = jnp.maximum(m, sink2)
    es = [jnp.exp2(t - m) for t in logits]
    l = functools.reduce(jnp.add, [jnp.sum(e, axis=-1, keepdims=True) for e in es])
    if sink2 is not None:
        l = l + jnp.exp2(sink2 - m)
    return [e.astype(BF16) for e in es], 1.0 / l


def _ctx_attn_kernel(sink_ref, qg_ref, ka_ref, va_ref, kb_ref, vb_ref, o_ref, s_scr, e_scr):
    n = SEQ
    gate0 = MIX_WIDTH
    n_stack = A_GROUPS

    def softmax(sink2):
        (e,), inv = _softmax_numerators([s_scr[...] * LOGIT_SCALE], sink2)
        e_scr[...] = e
        return inv

    def emit(o, g, out_col):
        cols = slice(out_col, out_col + HEAD_DIM)
        gate = qg_ref[:, gate0 + out_col:gate0 + out_col + HEAD_DIM].astype(F32)
        o_ref[:, cols] = (o[g * n:(g + 1) * n] * _silu(gate)).astype(o_ref.dtype)

    for kv in range(A_KV_HEADS):
        cols = slice(kv * HEAD_DIM, (kv + 1) * HEAD_DIM)
        heads = [kv * A_GROUPS + g for g in range(A_GROUPS)]
        q = jnp.concatenate(
            [qg_ref[:, h * HEAD_DIM:(h + 1) * HEAD_DIM] for h in heads], axis=0)
        sink2 = jnp.concatenate(
            [jnp.full((n, 1), sink_ref[h] * LOG2E, F32) for h in heads], axis=0)
        s_scr[...] = _dot_nt(q, ka_ref[:, cols].astype(BF16))
        inv = softmax(sink2)
        o = _dot(e_scr[...], va_ref[:, cols].astype(BF16)) * inv
        for g, h in enumerate(heads):
            emit(o, g, h * HEAD_DIM)
    for h0 in range(0, B_HEADS, n_stack):
        heads = range(h0, h0 + n_stack)
        for g, h in enumerate(heads):
            cols = slice(h * HEAD_DIM, (h + 1) * HEAD_DIM)
            q = qg_ref[:, A_WIDTH + h * HEAD_DIM:A_WIDTH + (h + 1) * HEAD_DIM]
            s_scr[g * n:(g + 1) * n, :] = _dot_nt(q, kb_ref[:, cols].astype(BF16))
        inv = softmax(None)
        o = jnp.concatenate(
            [_dot(e_scr[g * n:(g + 1) * n, :],
                  vb_ref[:, h * HEAD_DIM:(h + 1) * HEAD_DIM].astype(BF16))
             for g, h in enumerate(heads)], axis=0) * inv
        for g, h in enumerate(heads):
            emit(o, g, A_WIDTH + h * HEAD_DIM)


def _ctx_attention(sink, qg, ka, va, kb, vb):
    stack_rows = A_GROUPS * SEQ
    return pl.pallas_call(
        _ctx_attn_kernel,
        grid=(BATCH,),
        in_specs=[
            pl.BlockSpec(memory_space=pltpu.SMEM),
            pl.BlockSpec((SEQ, 2 * MIX_WIDTH), lambda b: (b, 0)),
            pl.BlockSpec((SEQ, A_KV_WIDTH), lambda b: (b, 0)),
            pl.BlockSpec((SEQ, A_KV_WIDTH), lambda b: (b, 0)),
            pl.BlockSpec((SEQ, B_WIDTH), lambda b: (b, 0)),
            pl.BlockSpec((SEQ, B_WIDTH), lambda b: (b, 0)),
        ],
        out_specs=pl.BlockSpec((SEQ, MIX_WIDTH), lambda b: (b, 0)),
        out_shape=jax.ShapeDtypeStruct((N_PROMPT, MIX_WIDTH), BF16),
        scratch_shapes=[pltpu.VMEM((stack_rows, SEQ), F32),
                        pltpu.VMEM((stack_rows, SEQ), BF16)],
        compiler_params=_params(1),
        name="ctx_attention",
    )(sink, qg, ka, va, kb, vb)


def _split_cache_heads(c_ref, scr, n_heads):
    for h in range(n_heads):
        scr[h] = c_ref[pl.ds(h, PAST_LEN, stride=n_heads), :].astype(scr.dtype)


def _win_attn_kernel(sink_ref, q_ref, k_ref, v_ref, ck_ref, cv_ref, gate_ref,
                     cos_ref, sin_ref, o_ref, qs_ref, ks_ref, ck_scr, cv_scr,
                     band_scr, sw_scr, sc_scr, ew_scr, ec_scr):
    kv = pl.program_id(1)
    n = DEC_SEQ

    @pl.when(kv == 0)
    def _():
        _split_cache_heads(ck_ref, ck_scr, A_KV_HEADS)
        _split_cache_heads(cv_ref, cv_scr, A_KV_HEADS)

    cos = cos_ref[...]
    sin = sin_ref[...]
    lane = lax.broadcasted_iota(jnp.int32, (n, HEAD_DIM), 1)
    first_quarter = (lane % (HEAD_DIM // 2)) < (HEAD_DIM // 4)

    def rope(x):
        rot = jnp.where(first_quarter,
                        pltpu.roll(x, HEAD_DIM - HEAD_DIM // 4, 1),
                        pltpu.roll(x, HEAD_DIM // 4, 1))
        return x * cos + rot * sin

    ks_ref[...] = rope(k_ref[...].astype(F32)).astype(BF16)
    for g in range(A_GROUPS):
        qs_ref[g] = (rope(q_ref[:, g * HEAD_DIM:(g + 1) * HEAD_DIM].astype(F32))
                     * LOGIT_SCALE).astype(BF16)

    ck = ck_scr[kv]
    cv = cv_scr[kv]
    rows = A_GROUPS * A_BLOCK
    sink2 = jnp.concatenate(
        [jnp.full((A_BLOCK, 1), sink_ref[kv * A_GROUPS + g] * LOG2E, F32)
         for g in range(A_GROUPS)], axis=0)
    qi = lax.broadcasted_iota(jnp.int32, (rows, 3 * A_BLOCK), 0) % A_BLOCK
    rel = lax.broadcasted_iota(jnp.int32, (rows, 3 * A_BLOCK), 1) - A_BLOCK - qi
    band_scr[...] = jnp.where(jnp.abs(rel) <= A_HALF_WIN, 0.0, NEG_INF)
    for blk in range(n // A_BLOCK):
        r0 = blk * A_BLOCK
        lo = max(0, r0 - A_BLOCK)
        hi = min(n, r0 + 2 * A_BLOCK)
        span = hi - lo
        b0 = lo - (r0 - A_BLOCK)
        q = jnp.concatenate([qs_ref[g, r0:r0 + A_BLOCK, :] for g in range(A_GROUPS)], axis=0)
        sw_scr[:, :span] = _dot_nt(q, ks_ref[lo:hi, :])
        sc_scr[...] = _dot_nt(q, ck)

        (e_w, e_c), inv = _softmax_numerators(
            [sw_scr[:, :span] + band_scr[:, b0:b0 + span], sc_scr[...]], sink2)
        ew_scr[:, :span] = e_w
        ec_scr[...] = e_c
        o = (_dot(ew_scr[:, :span], v_ref[lo:hi, :]) + _dot(ec_scr[...], cv)) * inv
        for g in range(A_GROUPS):
            hc = slice(g * HEAD_DIM, (g + 1) * HEAD_DIM)
            gate = gate_ref[r0:r0 + A_BLOCK, hc].astype(F32)
            o_ref[r0:r0 + A_BLOCK, hc] = (
                o[g * A_BLOCK:(g + 1) * A_BLOCK] * _silu(gate)).astype(o_ref.dtype)


def _win_attention(sink, proj, cache_k, cache_v, cos, sin_signed):
    gw = A_GROUPS * HEAD_DIM
    stack_rows = A_GROUPS * A_BLOCK
    n = DEC_SEQ
    cache_spec = pl.BlockSpec((None, PAST_LEN * A_KV_HEADS, HEAD_DIM), lambda b, k: (b, 0, 0))
    return pl.pallas_call(
        _win_attn_kernel,
        grid=(DEC_BATCH, A_KV_HEADS),
        in_specs=[
            pl.BlockSpec(memory_space=pltpu.SMEM),
            pl.BlockSpec((n, gw), lambda b, k: (b, COL_QA // gw + k)),
            pl.BlockSpec((n, HEAD_DIM), lambda b, k: (b, COL_KA // HEAD_DIM + k)),
            pl.BlockSpec((n, HEAD_DIM), lambda b, k: (b, COL_VA // HEAD_DIM + k)),
            cache_spec,
            cache_spec,
            pl.BlockSpec((n, gw), lambda b, k: (b, COL_GATE // gw + k)),
            pl.BlockSpec((n, HEAD_DIM), lambda b, k: (0, 0)),
            pl.BlockSpec((n, HEAD_DIM), lambda b, k: (0, 0)),
        ],
        out_specs=pl.BlockSpec((n, gw), lambda b, k: (b, k)),
        out_shape=jax.ShapeDtypeStruct((N_SAMPLE, A_WIDTH), BF16),
        scratch_shapes=[pltpu.VMEM((A_GROUPS, n, HEAD_DIM), BF16),
                        pltpu.VMEM((n, HEAD_DIM), BF16),
                        pltpu.VMEM((A_KV_HEADS, PAST_LEN, HEAD_DIM), BF16),
                        pltpu.VMEM((A_KV_HEADS, PAST_LEN, HEAD_DIM), BF16),
                        pltpu.VMEM((stack_rows, 3 * A_BLOCK), F32),
                        pltpu.VMEM((stack_rows, 3 * A_BLOCK), F32),
                        pltpu.VMEM((stack_rows, PAST_LEN), F32),
                        pltpu.VMEM((stack_rows, 3 * A_BLOCK), BF16),
                        pltpu.VMEM((stack_rows, PAST_LEN), BF16)],
        compiler_params=_params(2),
        name="win_attention",
    )(sink, proj, proj, proj, cache_k, cache_v, proj, cos, sin_signed)


def _bias_kernel(rpb_ref, o_ref):
    h = pl.program_id(0)
    shape = (GRID_W, 2 * GRID_W)
    c = lax.broadcasted_iota(jnp.int32, shape, 0)
    j2 = lax.broadcasted_iota(jnp.int32, shape, 1)
    kc = j2 % GRID_W
    second = j2 >= GRID_W
    col_start = jnp.clip(c - B_WIN_COLS // 2, 0, GRID_W - B_WIN_COLS)
    ok = (kc >= col_start) & (kc < col_start + B_WIN_COLS)
    dc = kc - c + B_WIN_COLS - 1
    base = h * (RPB_ROWS * RPB_COLS)
    pair_scr = []
    for i in range(N_BIAS_PAIRS):
        acc = jnp.full(shape, NEG_INF, F32)
        for d in range(RPB_COLS):
            val = jnp.where(second, rpb_ref[base + (i + 1) * RPB_COLS + d],
                            rpb_ref[base + i * RPB_COLS + d]) * LOG2E
            acc = jnp.where(ok & (dc == d), val, acc)
        pair_scr.append(acc)
    for r in range(GRID_ROWS):
        dr0 = _na_key_row0(r) - r + B_WIN_ROWS - 1
        for i in range(B_WIN_ROWS // 2):
            o_ref[r * GRID_W:(r + 1) * GRID_W, i * 2 * GRID_W:(i + 1) * 2 * GRID_W] = (
                pair_scr[dr0 + 2 * i])


def _expand_bias(rpb):
    kw = B_WIN_ROWS * GRID_W
    return pl.pallas_call(
        _bias_kernel,
        grid=(B_HEADS,),
        in_specs=[pl.BlockSpec(memory_space=pltpu.SMEM)],
        out_specs=pl.BlockSpec((None, DEC_SEQ, kw), lambda h: (h, 0, 0)),
        out_shape=jax.ShapeDtypeStruct((B_HEADS, DEC_SEQ, kw), F32),
        compiler_params=_params(1),
        name="expand_bias",
    )(rpb.reshape(-1))


def _na_key_row0(r):
    return min(max(r - B_WIN_ROWS // 2, 0), GRID_ROWS - B_WIN_ROWS)


def _na_row_groups():
    groups = []
    for r in range(GRID_ROWS):
        rs = _na_key_row0(r)
        if groups and groups[-1][2] == rs:
            groups[-1] = (groups[-1][0], r + 1, rs)
        else:
            groups.append((r, r + 1, rs))
    return groups


def _na_attn_kernel(q_ref, k_ref, v_ref, ck_ref, cv_ref, gate_ref, bias_ref, o_ref,
                    ck_scr, cv_scr, sn_scr, sc_scr, en_scr, ec_scr):
    kw = B_WIN_ROWS * GRID_W
    h = pl.program_id(1)

    @pl.when(h == 0)
    def _():
        _split_cache_heads(ck_ref, ck_scr, B_HEADS)
        _split_cache_heads(cv_ref, cv_scr, B_HEADS)

    groups = _na_row_groups()
    sc_scr[...] = _dot_nt(q_ref[...], ck_scr[h])
    for r0, r1, rs in groups:
        sn_scr[r0 * GRID_W:r1 * GRID_W, :] = _dot_nt(
            q_ref[r0 * GRID_W:r1 * GRID_W, :], k_ref[rs * GRID_W:rs * GRID_W + kw, :])

    (e_n, e_c), inv = _softmax_numerators(
        [sn_scr[...] * LOGIT_SCALE + bias_ref[...], sc_scr[...] * LOGIT_SCALE])
    en_scr[...] = e_n
    ec_scr[...] = e_c
    o_n = jnp.concatenate(
        [_dot(en_scr[r0 * GRID_W:r1 * GRID_W, :], v_ref[rs * GRID_W:rs * GRID_W + kw, :])
         for r0, r1, rs in groups], axis=0)
    o = (o_n + _dot(ec_scr[...], cv_scr[h])) * inv
    o_ref[...] = (o * _silu(gate_ref[...].astype(F32))).astype(o_ref.dtype)


def _na_attention(proj, cache_k, cache_v, bias):
    n = DEC_SEQ
    hd = HEAD_DIM
    cache_spec = pl.BlockSpec((None, PAST_LEN * B_HEADS, hd), lambda b, h: (b, 0, 0))
    return pl.pallas_call(
        _na_attn_kernel,
        grid=(DEC_BATCH, B_HEADS),
        in_specs=[
            pl.BlockSpec((n, hd), lambda b, h: (b, COL_QB // hd + h)),
            pl.BlockSpec((n, hd), lambda b, h: (b, COL_KB // hd + h)),
            pl.BlockSpec((n, hd), lambda b, h: (b, COL_VB // hd + h)),
            cache_spec,
            cache_spec,
            pl.BlockSpec((n, hd), lambda b, h: (b, (COL_GATE + A_WIDTH) // hd + h)),
            pl.BlockSpec((None, n, B_WIN_ROWS * GRID_W), lambda b, h: (h, 0, 0)),
        ],
        out_specs=pl.BlockSpec((n, hd), lambda b, h: (b, h)),
        out_shape=jax.ShapeDtypeStruct((N_SAMPLE, B_WIDTH), BF16),
        scratch_shapes=[pltpu.VMEM((B_HEADS, PAST_LEN, hd), BF16),
                        pltpu.VMEM((B_HEADS, PAST_LEN, hd), BF16),
                        pltpu.VMEM((n, B_WIN_ROWS * GRID_W), F32),
                        pltpu.VMEM((n, PAST_LEN), F32),
                        pltpu.VMEM((n, B_WIN_ROWS * GRID_W), BF16),
                        pltpu.VMEM((n, PAST_LEN), BF16)],
        compiler_params=_params(2),
        name="na_attention",
    )(proj, proj, proj, cache_k, cache_v, proj, bias)


def _pool_kernel(u_ref, gate_ref, band_ref, inv_count_ref, w_ref, scale_ref, o_ref, *, seq, rows):
    cb = POOL_BAND_BLOCK
    parts = []
    for i in range(rows // cb):
        lo = i * cb if seq <= cb else max(0, (i - 1) * cb)
        hi = (i + 1) * cb if seq <= cb else min(rows, (i + 2) * cb)
        parts.append(_dot(band_ref[i * cb:(i + 1) * cb, lo:hi], u_ref[lo:hi, :]))
    wsum = jnp.concatenate(parts, axis=0)
    pooled = wsum * inv_count_ref[...] - u_ref[...].astype(F32)
    y = _dot(pooled.astype(BF16), w_ref[...]) * scale_ref[...]
    o_ref[...] = (y * _silu(gate_ref[...].astype(F32))).astype(o_ref.dtype)


def _pool_operators(seq, rows):
    t = jnp.arange(rows)[:, None]
    j = jnp.arange(rows)[None, :]
    same_seq = (t // seq) == (j // seq)
    pos = t % seq
    bands, inv_counts = [], []
    for window in POOL_WINDOWS:
        half = window // 2
        assert half <= POOL_BAND_BLOCK
        bands.append(((j - t >= -half) & (j - t < half) & same_seq).astype(BF16))
        inv_counts.append(1.0 / (jnp.minimum(pos + half, seq) - jnp.maximum(pos - half, 0)).astype(F32))
    return jnp.stack(bands), jnp.stack(inv_counts)


def _pool_mixer(ug, w_grp, scale, seq, rows=1024):
    m = ug.shape[0]
    n_groups = len(POOL_WINDOWS)
    pg = POOL_GROUP
    band, inv_count = _pool_operators(seq, rows)
    return pl.pallas_call(
        functools.partial(_pool_kernel, seq=seq, rows=rows),
        grid=(n_groups, m // rows),
        in_specs=[
            pl.BlockSpec((rows, pg), lambda g, i: (i, g)),
            pl.BlockSpec((rows, pg), lambda g, i: (i, n_groups + g)),
            pl.BlockSpec((None, rows, rows), lambda g, i: (g, 0, 0)),
            pl.BlockSpec((None, rows, 1), lambda g, i: (g, 0, 0)),
            pl.BlockSpec((None, pg, pg), lambda g, i: (g, 0, 0)),
            pl.BlockSpec((1, pg), lambda g, i: (0, g)),
        ],
        out_specs=pl.BlockSpec((rows, pg), lambda g, i: (i, g)),
        out_shape=jax.ShapeDtypeStruct((m, n_groups * pg), BF16),
        compiler_params=_params(2),
        name="pool_mixer",
    )(ug, ug, band, inv_count, w_grp, scale.reshape(1, -1))


def _rope_tables():
    t = jnp.arange(DEC_SEQ)
    quarter = HEAD_DIM // 4
    inv_freq = ROPE_BASE ** (-jnp.arange(quarter, dtype=F32) / quarter)
    ang_r = (t // GRID_W).astype(F32)[:, None] * inv_freq
    ang_c = (t % GRID_W).astype(F32)[:, None] * inv_freq
    ang = jnp.concatenate([ang_r, ang_r, ang_c, ang_c], axis=-1)
    sign = jnp.tile(jnp.concatenate([-jnp.ones((quarter,), F32), jnp.ones((quarter,), F32)]), 2)
    return jnp.cos(ang), jnp.sin(ang) * sign


def kernel(x_prompt, x_sample, c, cache_a_k, cache_a_v, cache_b_k, cache_b_v, c_ctx,
           w_ada, b_ada, norm_g, w_in_attn, a_sink, b_rpb, w_out_attn,
           w_in_pool, w_grp_pool, pool_scale, w_out_pool, final_g):
    d = D_MODEL
    xp = x_prompt.reshape(N_PROMPT, d)
    xs = x_sample.reshape(N_SAMPLE, d)

    w_in0 = w_in_attn[0].astype(BF16)

    cond = jnp.zeros((MOD_ROWS, d), F32).at[:DEC_BATCH].set(c).at[CTX_MOD_ROW].set(c_ctx)
    mod = _ada(cond, w_ada, b_ada)
    mod0 = mod[0].reshape(MOD_ROWS, 1, 3 * d)
    mod1 = mod[1].reshape(MOD_ROWS, 1, 3 * d)

    hp, hs = _norm_mod(xp, xs, norm_g[0], mod0)

    ka, va, kb, vb, new_a_k, new_a_v, new_b_k, new_b_v = _kv_projection(hp, w_in0)
    qg_tiles = (_col_tiles(COL_QA, A_WIDTH) + _col_tiles(COL_QB, B_WIDTH)
                + _col_tiles(COL_GATE, MIX_WIDTH))
    qg_p, w_in1 = _matmul(hp, w_in0, qg_tiles, BF16, cast_weights=(w_in_pool[0],))
    n_groups = len(POOL_WINDOWS)
    proj_s, w_out0, w_out1, w_grp = _matmul(
        hs, w_in0, _col_tiles(0, ATTN_IN_WIDTH), BF16,
        cast_weights=(w_out_attn[0], w_out_pool[0],
                      w_grp_pool[0].reshape(n_groups * POOL_GROUP, POOL_GROUP)))
    w_grp = w_grp.reshape(n_groups, POOL_GROUP, POOL_GROUP)

    sink = a_sink[0]
    og_p = _ctx_attention(sink, qg_p, ka, va, kb, vb)

    cos, sin_signed = _rope_tables()
    flat = lambda cache: cache.reshape(DEC_BATCH, -1, HEAD_DIM)
    oa_s = _win_attention(sink, proj_s, flat(cache_a_k), flat(cache_a_v), cos, sin_signed)
    ob_s = _na_attention(proj_s, flat(cache_b_k), flat(cache_b_v), _expand_bias(b_rpb[0]))

    xp1, hu_p, ss_p = _matmul_residual_prenorm([(og_p, 0), (og_p, 1)], w_out0, xp, mod0, True,
                                               norm_g[1], mod1)
    xs1, hu_s, ss_s = _matmul_residual_prenorm([(oa_s, 0), (ob_s, 0)], w_out0, xs, mod0, False,
                                               norm_g[1], mod1)

    shw = _shift_projection(mod[1], w_in1).reshape(MOD_ROWS, 1, 2 * d)
    ug_p = _matmul_postnorm(hu_p, ss_p, w_in1, shw, True, BF16)
    ug_s = _matmul_postnorm(hu_s, ss_s, w_in1, shw, False, BF16)
    y_p = _pool_mixer(ug_p, w_grp, pool_scale[0], SEQ)
    y_s = _pool_mixer(ug_s, w_grp, pool_scale[0], DEC_SEQ)
    y_prompt = _matmul_residual_norm([(y_p, 0), (y_p, 1)], w_out1, xp1, mod1, True,
                                     final_g).reshape(BATCH, SEQ, d)
    y_sample = _matmul_residual_norm([(y_s, 0), (y_s, 1)], w_out1, xs1, mod1, False,
                                     final_g).reshape(DEC_BATCH, DEC_SEQ, d)

    return (y_prompt, y_sample, new_a_k, new_a_v, new_b_k, new_b_v)
```

```python
import functools

import jax
import jax.numpy as jnp
from jax import lax
from jax.experimental import pallas as pl
from jax.experimental.pallas import tpu as pltpu

F32 = jnp.float32
BF16 = jnp.bfloat16

D_MODEL = 4096
BATCH = 32
SEQ = 256
DEC_BATCH = 8
DEC_SEQ = 1024
PAST_LEN = 512
GRID_W = 64
HEAD_DIM = 128
A_Q_HEADS = 16
A_KV_HEADS = 4
A_GROUPS = 4
A_HALF_WIN = 128
A_BLOCK = 128
B_HEADS = 16
B_WIN_ROWS = 8
B_WIN_COLS = 16
A_WIDTH = A_Q_HEADS * HEAD_DIM
A_KV_WIDTH = A_KV_HEADS * HEAD_DIM
B_WIDTH = B_HEADS * HEAD_DIM
MIX_WIDTH = A_WIDTH + B_WIDTH
POOL_WINDOWS = (2, 4, 8, 16)
POOL_GROUP = 1024
POOL_BAND_BLOCK = 256
CAST_CHUNK_ROWS = 64
ROPE_BASE = 10000.0
NORM_EPS = 1e-6
NEG_INF = -1e30
ATTN_SCALE = HEAD_DIM ** -0.5
LOG2E = 1.4426950408889634
LOGIT_SCALE = ATTN_SCALE * LOG2E

N_PROMPT = BATCH * SEQ
N_SAMPLE = DEC_BATCH * DEC_SEQ
MOD_ROWS = 16
CTX_MOD_ROW = DEC_BATCH
GRID_ROWS = DEC_SEQ // GRID_W
N_BIAS_PAIRS = 2 * B_WIN_ROWS - 2
RPB_ROWS = 2 * B_WIN_ROWS - 1
RPB_COLS = 2 * B_WIN_COLS - 1

COL_QA = 0
COL_KA = COL_QA + A_WIDTH
COL_VA = COL_KA + A_KV_WIDTH
COL_QB = COL_VA + A_KV_WIDTH
COL_KB = COL_QB + B_WIDTH
COL_VB = COL_KB + B_WIDTH
COL_GATE = COL_VB + B_WIDTH
ATTN_IN_WIDTH = COL_GATE + MIX_WIDTH
PROJ_TN = 1024


def _col_tiles(col0, width):
    assert col0 % PROJ_TN == 0 and width % PROJ_TN == 0
    return tuple(range(col0 // PROJ_TN, (col0 + width) // PROJ_TN))


def _tile_lookup(tiles):
    runs = []
    for jj, t in enumerate(tiles):
        if not runs or runs[-1][1] != t - jj:
            runs.append((jj, t - jj))

    def lookup(j):
        off = runs[0][1]
        for start, o in runs[1:]:
            off = jnp.where(j >= start, o, off)
        return j + off
    return lookup

VMEM_LIMIT = 48 * 1024 * 1024
VMEM_LIMIT_ROWS = 56 * 1024 * 1024


def _params(n_grid_dims):
    return pltpu.CompilerParams(
        dimension_semantics=("arbitrary",) * n_grid_dims,
        vmem_limit_bytes=VMEM_LIMIT,
    )


def _silu(x):
    return 0.5 * x * (1.0 + jnp.tanh(0.5 * x))


def _dot_nt(a, b):
    return lax.dot_general(a, b, (((1,), (1,)), ((), ())), preferred_element_type=F32)


def _dot(a, b):
    return jnp.dot(a, b, preferred_element_type=F32)


def _ada_kernel(cond_ref, w_ref, b_ref, o_ref):
    a = _silu(cond_ref[...]).astype(BF16)
    o_ref[...] = _dot(a, w_ref[...].astype(BF16)) + b_ref[...]


def _ada(cond, w_ada, b_ada, tn=512):
    depth, d, n = w_ada.shape
    return pl.pallas_call(
        _ada_kernel,
        grid=(depth, n // tn),
        in_specs=[
            pl.BlockSpec((MOD_ROWS, d), lambda l, j: (0, 0)),
            pl.BlockSpec((None, d, tn), lambda l, j: (l, 0, j)),
            pl.BlockSpec((None, 1, tn), lambda l, j: (l, 0, j)),
        ],
        out_specs=pl.BlockSpec((None, MOD_ROWS, tn), lambda l, j: (l, 0, j)),
        out_shape=jax.ShapeDtypeStruct((depth, MOD_ROWS, n), F32),
        compiler_params=_params(2),
        name="ada",
    )(cond, w_ada, b_ada.reshape(depth, 1, n))


def _mod_row_fn(is_prompt, tm):
    if is_prompt:
        return lambda i: CTX_MOD_ROW
    return lambda i: (i * tm) // DEC_SEQ


def _norm_mod_kernel(xp_ref, xs_ref, g_ref, sh_ref, sc_ref, hp_ref, hs_ref, *, n_prompt_tiles):
    def norm_mod(x_ref, o_ref):
        x = x_ref[...]
        ms = jnp.mean(x * x, axis=-1, keepdims=True)
        y = x * lax.rsqrt(ms + NORM_EPS) * g_ref[...]
        o_ref[...] = (y * (1.0 + sc_ref[...]) + sh_ref[...]).astype(o_ref.dtype)

    i = pl.program_id(0)

    @pl.when(i < n_prompt_tiles)
    def _():
        norm_mod(xp_ref, hp_ref)

    @pl.when(i >= n_prompt_tiles)
    def _():
        norm_mod(xs_ref, hs_ref)


def _norm_mod(xp, xs, gain, mod, tm=256):
    d = xp.shape[1]
    n_p, n_s = xp.shape[0] // tm, xs.shape[0] // tm
    p_blk = lambda i: (jnp.minimum(i, n_p - 1), 0)
    s_blk = lambda i: (jnp.maximum(i - n_p, 0), 0)
    row = lambda i: jnp.where(i < n_p, CTX_MOD_ROW, (jnp.maximum(i - n_p, 0) * tm) // DEC_SEQ)
    return pl.pallas_call(
        functools.partial(_norm_mod_kernel, n_prompt_tiles=n_p),
        grid=(n_p + n_s,),
        in_specs=[
            pl.BlockSpec((tm, d), p_blk),
            pl.BlockSpec((tm, d), s_blk),
            pl.BlockSpec((1, d), lambda i: (0, 0)),
            pl.BlockSpec((None, 1, d), lambda i: (row(i), 0, 0)),
            pl.BlockSpec((None, 1, d), lambda i: (row(i), 0, 1)),
        ],
        out_specs=[pl.BlockSpec((tm, d), p_blk), pl.BlockSpec((tm, d), s_blk)],
        out_shape=[jax.ShapeDtypeStruct(xp.shape, BF16), jax.ShapeDtypeStruct(xs.shape, BF16)],
        compiler_params=_params(1),
        name="norm_mod",
    )(xp, xs, gain.reshape(1, d), mod, mod)


def _mm_kernel(a_ref, w_ref, *refs, n_casts):
    cast_in, o_ref, cast_out = refs[:n_casts], refs[n_casts], refs[n_casts + 1:]
    o_ref[...] = _dot(a_ref[...], w_ref[...]).astype(o_ref.dtype)
    for src, dst in zip(cast_in, cast_out):
        dst[...] = src[...].astype(dst.dtype)


def _matmul(a, w, w_tiles, out_dtype, cast_weights=()):
    m, k = a.shape
    tn = PROJ_TN
    tm = 1024 if out_dtype == BF16 else 512
    assert m % tm == 0
    w_tile = _tile_lookup(w_tiles)
    ncols = len(w_tiles) * tn
    n_i = m // tm
    n_steps = len(w_tiles) * n_i
    in_specs = [pl.BlockSpec((tm, k), lambda j, i: (i, 0)),
                pl.BlockSpec((k, tn), lambda j, i: (0, w_tile(j)))]
    out_specs = [pl.BlockSpec((tm, tn), lambda j, i: (i, j))]
    out_shape = [jax.ShapeDtypeStruct((m, ncols), out_dtype)]
    for cw in cast_weights:
        rows, cols = cw.shape
        chunk = CAST_CHUNK_ROWS
        n_chunks = rows // chunk
        assert rows % chunk == 0 and n_chunks <= n_steps
        spec = pl.BlockSpec((chunk, cols),
                            lambda j, i, n_chunks=n_chunks: (jnp.minimum(j * n_i + i, n_chunks - 1), 0))
        in_specs.append(spec)
        out_specs.append(spec)
        out_shape.append(jax.ShapeDtypeStruct((rows, cols), BF16))
    outs = pl.pallas_call(
        functools.partial(_mm_kernel, n_casts=len(cast_weights)),
        grid=(ncols // tn, n_i),
        in_specs=in_specs,
        out_specs=out_specs,
        out_shape=out_shape,
        compiler_params=pltpu.CompilerParams(
            dimension_semantics=("arbitrary", "arbitrary"),
            vmem_limit_bytes=VMEM_LIMIT_ROWS if cast_weights else VMEM_LIMIT),
        name="proj",
    )(a, w, *cast_weights)
    return outs if cast_weights else outs[0]


def _kv_proj_kernel(a_ref, w_ref, ka_ref, va_ref, kb_ref, vb_ref, ka5_ref, va5_ref, kb5_ref, vb5_ref,
                    *, kb_tile0, vb_tile0):
    j = pl.program_id(0)
    acc = _dot(a_ref[...], w_ref[...])

    def emit(cols, out_ref, out5_ref):
        part = acc[:, cols]
        out_ref[...] = part.astype(out_ref.dtype)
        out5_ref[...] = part.reshape(out5_ref.shape)

    @pl.when(j < kb_tile0)
    def _():
        emit(slice(0, A_KV_WIDTH), ka_ref, ka5_ref)
        emit(slice(A_KV_WIDTH, 2 * A_KV_WIDTH), va_ref, va5_ref)

    @pl.when((j >= kb_tile0) & (j < vb_tile0))
    def _():
        emit(slice(None), kb_ref, kb5_ref)

    @pl.when(j >= vb_tile0)
    def _():
        emit(slice(None), vb_ref, vb5_ref)


def _kv_projection(a, w, tm=512):
    m, k = a.shape
    tn = PROJ_TN
    assert COL_VA == COL_KA + A_KV_WIDTH and 2 * A_KV_WIDTH == tn and tm % SEQ == 0
    w_tiles = _col_tiles(COL_KA, tn) + _col_tiles(COL_KB, B_WIDTH) + _col_tiles(COL_VB, B_WIDTH)
    w_tile = _tile_lookup(w_tiles)
    kb_tile0 = 1
    vb_tile0 = kb_tile0 + B_WIDTH // tn
    end_tile = len(w_tiles)
    n_i = m // tm
    last = n_i - 1

    def parked(j, i, t0, t1):
        row = jnp.where(j < t0, 0, jnp.where(j < t1, i, last))
        return row, jnp.clip(j - t0, 0, t1 - t0 - 1)

    def parked5(j, i, t0, t1):
        row, col = parked(j, i, t0, t1)
        return row, 0, 0, col, 0

    req = tm // SEQ
    a_spec = pl.BlockSpec((tm, A_KV_WIDTH), lambda j, i: parked(j, i, 0, kb_tile0))
    a_spec5 = pl.BlockSpec((req, None, SEQ, A_KV_HEADS, HEAD_DIM),
                           lambda j, i: parked5(j, i, 0, kb_tile0))
    b_block5 = (req, None, SEQ, tn // HEAD_DIM, HEAD_DIM)
    cache_shape_a = (m // SEQ, 1, SEQ, A_KV_HEADS, HEAD_DIM)
    cache_shape_b = (m // SEQ, 1, SEQ, B_HEADS, HEAD_DIM)
    return pl.pallas_call(
        functools.partial(_kv_proj_kernel, kb_tile0=kb_tile0, vb_tile0=vb_tile0),
        grid=(end_tile, n_i),
        in_specs=[pl.BlockSpec((tm, k), lambda j, i: (i, 0)),
                  pl.BlockSpec((k, tn), lambda j, i: (0, w_tile(j)))],
        out_specs=[a_spec, a_spec,
                   pl.BlockSpec((tm, tn), lambda j, i: parked(j, i, kb_tile0, vb_tile0)),
                   pl.BlockSpec((tm, tn), lambda j, i: parked(j, i, vb_tile0, end_tile)),
                   a_spec5, a_spec5,
                   pl.BlockSpec(b_block5, lambda j, i: parked5(j, i, kb_tile0, vb_tile0)),
                   pl.BlockSpec(b_block5, lambda j, i: parked5(j, i, vb_tile0, end_tile))],
        out_shape=[jax.ShapeDtypeStruct((m, A_KV_WIDTH), BF16),
                   jax.ShapeDtypeStruct((m, A_KV_WIDTH), BF16),
                   jax.ShapeDtypeStruct((m, B_WIDTH), BF16),
                   jax.ShapeDtypeStruct((m, B_WIDTH), BF16),
                   jax.ShapeDtypeStruct(cache_shape_a, F32),
                   jax.ShapeDtypeStruct(cache_shape_a, F32),
                   jax.ShapeDtypeStruct(cache_shape_b, F32),
                   jax.ShapeDtypeStruct(cache_shape_b, F32)],
        compiler_params=pltpu.CompilerParams(dimension_semantics=("arbitrary", "arbitrary"),
                                             vmem_limit_bytes=VMEM_LIMIT_ROWS),
        name="kv_proj",
    )(a, w)


def _dot_halves(a1_ref, a2_ref, w_ref):
    kh = a1_ref.shape[1]
    return _dot(a1_ref[...], w_ref[:kh, :]) + _dot(a2_ref[...], w_ref[kh:, :])


def _mm_res_prenorm_kernel(a1_ref, a2_ref, w_ref, x_ref, g_ref, gain_ref, sc_ref,
                           x1_ref, hu_ref, ss_ref):
    x1 = x_ref[...] + g_ref[...] * _dot_halves(a1_ref, a2_ref, w_ref)
    x1_ref[...] = x1
    hu_ref[...] = (x1 * (gain_ref[...] * (1.0 + sc_ref[...]))).astype(hu_ref.dtype)
    ss_ref[...] = jnp.sum(x1 * x1, axis=-1, keepdims=True)


def _matmul_residual_prenorm(a_halves, w, x, mod, is_prompt, gain, next_mod, tm=512):
    (a1, c1), (a2, c2) = a_halves
    m = a1.shape[0]
    k, n = w.shape
    tn = PROJ_TN
    n_tiles = n // tn
    row = _mod_row_fn(is_prompt, tm)
    scale_block0, gate_block0 = n_tiles, 2 * n_tiles
    tile = pl.BlockSpec((tm, tn), lambda j, i: (i, j))
    return pl.pallas_call(
        _mm_res_prenorm_kernel,
        grid=(n_tiles, m // tm),
        in_specs=[pl.BlockSpec((tm, k // 2), lambda j, i: (i, c1)),
                  pl.BlockSpec((tm, k // 2), lambda j, i: (i, c2)),
                  pl.BlockSpec((k, tn), lambda j, i: (0, j)),
                  tile,
                  pl.BlockSpec((None, 1, tn), lambda j, i: (row(i), 0, gate_block0 + j)),
                  pl.BlockSpec((1, tn), lambda j, i: (0, j)),
                  pl.BlockSpec((None, 1, tn), lambda j, i: (row(i), 0, scale_block0 + j))],
        out_specs=[tile, tile, pl.BlockSpec((None, tm, 1), lambda j, i: (j, i, 0))],
        out_shape=[jax.ShapeDtypeStruct((m, n), F32), jax.ShapeDtypeStruct((m, n), BF16),
                   jax.ShapeDtypeStruct((n_tiles, m, 1), F32)],
        compiler_params=_params(2),
        name="proj_residual_prenorm",
    )(a1, a2, w, x, mod, gain.reshape(1, n), next_mod)


def _shift_proj_kernel(sh_ref, w_ref, o_ref):
    o_ref[...] = _dot(sh_ref[...].astype(BF16), w_ref[...])


def _shift_projection(mod2d, w):
    k, n = w.shape
    tn = PROJ_TN
    return pl.pallas_call(
        _shift_proj_kernel,
        grid=(n // tn,),
        in_specs=[pl.BlockSpec((MOD_ROWS, k), lambda j: (0, 0)),
                  pl.BlockSpec((k, tn), lambda j: (0, j))],
        out_specs=pl.BlockSpec((MOD_ROWS, tn), lambda j: (0, j)),
        out_shape=jax.ShapeDtypeStruct((MOD_ROWS, n), F32),
        compiler_params=_params(1),
        name="shift_proj",
    )(mod2d, w)


def _mm_postnorm_kernel(a_ref, w_ref, ss_ref, shw_ref, o_ref, *, width):
    acc = _dot(a_ref[...], w_ref[...])
    ss = ss_ref[0]
    for t in range(1, ss_ref.shape[0]):
        ss = ss + ss_ref[t]
    r = lax.rsqrt(ss / width + NORM_EPS)
    o_ref[...] = (acc * r + shw_ref[...]).astype(o_ref.dtype)


def _matmul_postnorm(hu, ss, w, shw, is_prompt, out_dtype, tm=1024):
    m, k = hu.shape
    n = w.shape[1]
    tn = PROJ_TN
    row = _mod_row_fn(is_prompt, tm)
    return pl.pallas_call(
        functools.partial(_mm_postnorm_kernel, width=k),
        grid=(n // tn, m // tm),
        in_specs=[pl.BlockSpec((tm, k), lambda j, i: (i, 0)),
                  pl.BlockSpec((k, tn), lambda j, i: (0, j)),
                  pl.BlockSpec((ss.shape[0], tm, 1), lambda j, i: (0, i, 0)),
                  pl.BlockSpec((None, 1, tn), lambda j, i: (row(i), 0, j))],
        out_specs=pl.BlockSpec((tm, tn), lambda j, i: (i, j)),
        out_shape=jax.ShapeDtypeStruct((m, n), out_dtype),
        compiler_params=pltpu.CompilerParams(dimension_semantics=("arbitrary", "arbitrary"),
                                             vmem_limit_bytes=VMEM_LIMIT_ROWS),
        name="proj_postnorm",
    )(hu, w, ss, shw)


def _mm_res_norm_kernel(a1_ref, a2_ref, w_ref, x_ref, g_ref, gain_ref, y_ref, *, n_tiles, tn):
    j = pl.program_id(1)
    x2 = x_ref[...] + g_ref[...] * _dot_halves(a1_ref, a2_ref, w_ref)
    for jj in range(n_tiles):
        @pl.when(j == jj)
        def _(jj=jj):
            y_ref[:, jj * tn:(jj + 1) * tn] = x2

    @pl.when(j == n_tiles - 1)
    def _():
        ss = None
        for jj in range(n_tiles):
            t = y_ref[:, jj * tn:(jj + 1) * tn]
            part = jnp.sum(t * t, axis=-1, keepdims=True)
            ss = part if ss is None else ss + part
        r = lax.rsqrt(ss / (n_tiles * tn) + NORM_EPS)
        for jj in range(n_tiles):
            cols = slice(jj * tn, (jj + 1) * tn)
            y_ref[:, cols] = y_ref[:, cols] * r * gain_ref[:, cols]


def _matmul_residual_norm(a_halves, w, x, mod, is_prompt, gain, tm=512):
    (a1, c1), (a2, c2) = a_halves
    m = a1.shape[0]
    k, n = w.shape
    tn = PROJ_TN
    n_tiles = n // tn
    row = _mod_row_fn(is_prompt, tm)
    gate_block0 = 2 * n_tiles
    return pl.pallas_call(
        functools.partial(_mm_res_norm_kernel, n_tiles=n_tiles, tn=tn),
        grid=(m // tm, n_tiles),
        in_specs=[pl.BlockSpec((tm, k // 2), lambda i, j: (i, c1)),
                  pl.BlockSpec((tm, k // 2), lambda i, j: (i, c2)),
                  pl.BlockSpec((k, tn), lambda i, j: (0, j)),
                  pl.BlockSpec((tm, tn), lambda i, j: (i, j)),
                  pl.BlockSpec((None, 1, tn), lambda i, j: (row(i), 0, gate_block0 + j)),
                  pl.BlockSpec((1, n), lambda i, j: (0, 0))],
        out_specs=pl.BlockSpec((tm, n), lambda i, j: (i, 0)),
        out_shape=jax.ShapeDtypeStruct((m, n), F32),
        compiler_params=pltpu.CompilerParams(dimension_semantics=("arbitrary", "arbitrary"),
                                             vmem_limit_bytes=VMEM_LIMIT_ROWS),
        name="proj_residual_norm",
    )(a1, a2, w, x, mod, gain.reshape(1, n))


def _softmax_numerators(logits, sink2=None):
    m = functools.reduce(jnp.maximum, [jnp.max(t, axis=-1, keepdims=True) for t in logits])
    if sink2 is not None:
        m = jnp.maximum(m, sink2)
    es = [jnp.exp2(t - m) for t in logits]
    l = functools.reduce(jnp.add, [jnp.sum(e, axis=-1, keepdims=True) for e in es])
    if sink2 is not None:
        l = l + jnp.exp2(sink2 - m)
    return [e.astype(BF16) for e in es], 1.0 / l


def _ctx_attn_kernel(sink_ref, qg_ref, ka_ref, va_ref, kb_ref, vb_ref, o_ref, s_scr, e_scr):
    n = SEQ
    gate0 = MIX_WIDTH
    n_stack = A_GROUPS

    def softmax(sink2):
        (e,), inv = _softmax_numerators([s_scr[...] * LOGIT_SCALE], sink2)
        e_scr[...] = e
        return inv

    def emit(o, g, out_col):
        cols = slice(out_col, out_col + HEAD_DIM)
        gate = qg_ref[:, gate0 + out_col:gate0 + out_col + HEAD_DIM].astype(F32)
        o_ref[:, cols] = (o[g * n:(g + 1) * n] * _silu(gate)).astype(o_ref.dtype)

    for kv in range(A_KV_HEADS):
        cols = slice(kv * HEAD_DIM, (kv + 1) * HEAD_DIM)
        heads = [kv * A_GROUPS + g for g in range(A_GROUPS)]
        q = jnp.concatenate(
            [qg_ref[:, h * HEAD_DIM:(h + 1) * HEAD_DIM] for h in heads], axis=0)
        sink2 = jnp.concatenate(
            [jnp.full((n, 1), sink_ref[h] * LOG2E, F32) for h in heads], axis=0)
        s_scr[...] = _dot_nt(q, ka_ref[:, cols])
        inv = softmax(sink2)
        o = _dot(e_scr[...], va_ref[:, cols]) * inv
        for g, h in enumerate(heads):
            emit(o, g, h * HEAD_DIM)
    for h0 in range(0, B_HEADS, n_stack):
        heads = range(h0, h0 + n_stack)
        for g, h in enumerate(heads):
            cols = slice(h * HEAD_DIM, (h + 1) * HEAD_DIM)
            q = qg_ref[:, A_WIDTH + h * HEAD_DIM:A_WIDTH + (h + 1) * HEAD_DIM]
            s_scr[g * n:(g + 1) * n, :] = _dot_nt(q, kb_ref[:, cols])
        inv = softmax(None)
        o = jnp.concatenate(
            [_dot(e_scr[g * n:(g + 1) * n, :],
                  vb_ref[:, h * HEAD_DIM:(h + 1) * HEAD_DIM])
             for g, h in enumerate(heads)], axis=0) * inv
        for g, h in enumerate(heads):
            emit(o, g, A_WIDTH + h * HEAD_DIM)


def _ctx_attention(sink, qg, ka, va, kb, vb):
    stack_rows = A_GROUPS * SEQ
    return pl.pallas_call(
        _ctx_attn_kernel,
        grid=(BATCH,),
        in_specs=[
            pl.BlockSpec(memory_space=pltpu.SMEM),
            pl.BlockSpec((SEQ, 2 * MIX_WIDTH), lambda b: (b, 0)),
            pl.BlockSpec((SEQ, A_KV_WIDTH), lambda b: (b, 0)),
            pl.BlockSpec((SEQ, A_KV_WIDTH), lambda b: (b, 0)),
            pl.BlockSpec((SEQ, B_WIDTH), lambda b: (b, 0)),
            pl.BlockSpec((SEQ, B_WIDTH), lambda b: (b, 0)),
        ],
        out_specs=pl.BlockSpec((SEQ, MIX_WIDTH), lambda b: (b, 0)),
        out_shape=jax.ShapeDtypeStruct((N_PROMPT, MIX_WIDTH), BF16),
        scratch_shapes=[pltpu.VMEM((stack_rows, SEQ), F32),
                        pltpu.VMEM((stack_rows, SEQ), BF16)],
        compiler_params=_params(1),
        name="ctx_attention",
    )(sink, qg, ka, va, kb, vb)


def _split_cache_heads(c_ref, scr, n_heads):
    for h in range(n_heads):
        scr[h] = c_ref[pl.ds(h, PAST_LEN, stride=n_heads), :].astype(scr.dtype)


def _win_attn_kernel(sink_ref, q_ref, k_ref, v_ref, ck_ref, cv_ref, gate_ref,
                     cos_ref, sin_ref, o_ref, qs_ref, ks_ref, ck_scr, cv_scr,
                     band_scr, sw_scr, sc_scr, ew_scr, ec_scr):
    kv = pl.program_id(1)
    n = DEC_SEQ

    @pl.when(kv == 0)
    def _():
        _split_cache_heads(ck_ref, ck_scr, A_KV_HEADS)
        _split_cache_heads(cv_ref, cv_scr, A_KV_HEADS)

    cos = cos_ref[...]
    sin = sin_ref[...]
    lane = lax.broadcasted_iota(jnp.int32, (n, HEAD_DIM), 1)
    first_quarter = (lane % (HEAD_DIM // 2)) < (HEAD_DIM // 4)

    def rope(x):
        rot = jnp.where(first_quarter,
                        pltpu.roll(x, HEAD_DIM - HEAD_DIM // 4, 1),
                        pltpu.roll(x, HEAD_DIM // 4, 1))
        return x * cos + rot * sin

    ks_ref[...] = rope(k_ref[...].astype(F32)).astype(BF16)
    for g in range(A_GROUPS):
        qs_ref[g] = (rope(q_ref[:, g * HEAD_DIM:(g + 1) * HEAD_DIM].astype(F32))
                     * LOGIT_SCALE).astype(BF16)

    ck = ck_scr[kv]
    cv = cv_scr[kv]
    rows = A_GROUPS * A_BLOCK
    sink2 = jnp.concatenate(
        [jnp.full((A_BLOCK, 1), sink_ref[kv * A_GROUPS + g] * LOG2E, F32)
         for g in range(A_GROUPS)], axis=0)
    qi = lax.broadcasted_iota(jnp.int32, (rows, 3 * A_BLOCK), 0) % A_BLOCK
    rel = lax.broadcasted_iota(jnp.int32, (rows, 3 * A_BLOCK), 1) - A_BLOCK - qi
    band_scr[...] = jnp.where(jnp.abs(rel) <= A_HALF_WIN, 0.0, NEG_INF)
    for blk in range(n // A_BLOCK):
        r0 = blk * A_BLOCK
        lo = max(0, r0 - A_BLOCK)
        hi = min(n, r0 + 2 * A_BLOCK)
        span = hi - lo
        b0 = lo - (r0 - A_BLOCK)
        q = jnp.concatenate([qs_ref[g, r0:r0 + A_BLOCK, :] for g in range(A_GROUPS)], axis=0)
        sw_scr[:, :span] = _dot_nt(q, ks_ref[lo:hi, :])
        sc_scr[...] = _dot_nt(q, ck)

        (e_w, e_c), inv = _softmax_numerators(
            [sw_scr[:, :span] + band_scr[:, b0:b0 + span], sc_scr[...]], sink2)
        ew_scr[:, :span] = e_w
        ec_scr[...] = e_c
        o = (_dot(ew_scr[:, :span], v_ref[lo:hi, :]) + _dot(ec_scr[...], cv)) * inv
        for g in range(A_GROUPS):
            hc = slice(g * HEAD_DIM, (g + 1) * HEAD_DIM)
            gate = gate_ref[r0:r0 + A_BLOCK, hc].astype(F32)
            o_ref[r0:r0 + A_BLOCK, hc] = (
                o[g * A_BLOCK:(g + 1) * A_BLOCK] * _silu(gate)).astype(o_ref.dtype)


def _win_attention(sink, proj, cache_k, cache_v, cos, sin_signed):
    gw = A_GROUPS * HEAD_DIM
    stack_rows = A_GROUPS * A_BLOCK
    n = DEC_SEQ
    cache_spec = pl.BlockSpec((None, PAST_LEN * A_KV_HEADS, HEAD_DIM), lambda b, k: (b, 0, 0))
    return pl.pallas_call(
        _win_attn_kernel,
        grid=(DEC_BATCH, A_KV_HEADS),
        in_specs=[
            pl.BlockSpec(memory_space=pltpu.SMEM),
            pl.BlockSpec((n, gw), lambda b, k: (b, COL_QA // gw + k)),
            pl.BlockSpec((n, HEAD_DIM), lambda b, k: (b, COL_KA // HEAD_DIM + k)),
            pl.BlockSpec((n, HEAD_DIM), lambda b, k: (b, COL_VA // HEAD_DIM + k)),
            cache_spec,
            cache_spec,
            pl.BlockSpec((n, gw), lambda b, k: (b, COL_GATE // gw + k)),
            pl.BlockSpec((n, HEAD_DIM), lambda b, k: (0, 0)),
            pl.BlockSpec((n, HEAD_DIM), lambda b, k: (0, 0)),
        ],
        out_specs=pl.BlockSpec((n, gw), lambda b, k: (b, k)),
        out_shape=jax.ShapeDtypeStruct((N_SAMPLE, A_WIDTH), BF16),
        scratch_shapes=[pltpu.VMEM((A_GROUPS, n, HEAD_DIM), BF16),
                        pltpu.VMEM((n, HEAD_DIM), BF16),
                        pltpu.VMEM((A_KV_HEADS, PAST_LEN, HEAD_DIM), BF16),
                        pltpu.VMEM((A_KV_HEADS, PAST_LEN, HEAD_DIM), BF16),
                        pltpu.VMEM((stack_rows, 3 * A_BLOCK), F32),
                        pltpu.VMEM((stack_rows, 3 * A_BLOCK), F32),
                        pltpu.VMEM((stack_rows, PAST_LEN), F32),
                        pltpu.VMEM((stack_rows, 3 * A_BLOCK), BF16),
                        pltpu.VMEM((stack_rows, PAST_LEN), BF16)],
        compiler_params=_params(2),
        name="win_attention",
    )(sink, proj, proj, proj, cache_k, cache_v, proj, cos, sin_signed)


def _bias_kernel(rpb_ref, o_ref):
    h = pl.program_id(0)
    shape = (GRID_W, 2 * GRID_W)
    c = lax.broadcasted_iota(jnp.int32, shape, 0)
    j2 = lax.broadcasted_iota(jnp.int32, shape, 1)
    kc = j2 % GRID_W
    second = j2 >= GRID_W
    col_start = jnp.clip(c - B_WIN_COLS // 2, 0, GRID_W - B_WIN_COLS)
    ok = (kc >= col_start) & (kc < col_start + B_WIN_COLS)
    dc = kc - c + B_WIN_COLS - 1
    base = h * (RPB_ROWS * RPB_COLS)
    pair_scr = []
    for i in range(N_BIAS_PAIRS):
        acc = jnp.full(shape, NEG_INF, F32)
        for d in range(RPB_COLS):
            val = jnp.where(second, rpb_ref[base + (i + 1) * RPB_COLS + d],
                            rpb_ref[base + i * RPB_COLS + d]) * LOG2E
            acc = jnp.where(ok & (dc == d), val, acc)
        pair_scr.append(acc)
    for r in range(GRID_ROWS):
        dr0 = _na_key_row0(r) - r + B_WIN_ROWS - 1
        for i in range(B_WIN_ROWS // 2):
            o_ref[r * GRID_W:(r + 1) * GRID_W, i * 2 * GRID_W:(i + 1) * 2 * GRID_W] = (
                pair_scr[dr0 + 2 * i])


def _expand_bias(rpb):
    kw = B_WIN_ROWS * GRID_W
    return pl.pallas_call(
        _bias_kernel,
        grid=(B_HEADS,),
        in_specs=[pl.BlockSpec(memory_space=pltpu.SMEM)],
        out_specs=pl.BlockSpec((None, DEC_SEQ, kw), lambda h: (h, 0, 0)),
        out_shape=jax.ShapeDtypeStruct((B_HEADS, DEC_SEQ, kw), F32),
        compiler_params=_params(1),
        name="expand_bias",
    )(rpb.reshape(-1))


def _na_key_row0(r):
    return min(max(r - B_WIN_ROWS // 2, 0), GRID_ROWS - B_WIN_ROWS)


def _na_row_groups():
    groups = []
    for r in range(GRID_ROWS):
        rs = _na_key_row0(r)
        if groups and groups[-1][2] == rs:
            groups[-1] = (groups[-1][0], r + 1, rs)
        else:
            groups.append((r, r + 1, rs))
    return groups


def _na_attn_kernel(q_ref, k_ref, v_ref, ck_ref, cv_ref, gate_ref, bias_ref, o_ref,
                    ck_scr, cv_scr, sn_scr, sc_scr, en_scr, ec_scr):
    kw = B_WIN_ROWS * GRID_W
    h = pl.program_id(1)

    @pl.when(h == 0)
    def _():
        _split_cache_heads(ck_ref, ck_scr, B_HEADS)
        _split_cache_heads(cv_ref, cv_scr, B_HEADS)

    groups = _na_row_groups()
    sc_scr[...] = _dot_nt(q_ref[...], ck_scr[h])
    for r0, r1, rs in groups:
        sn_scr[r0 * GRID_W:r1 * GRID_W, :] = _dot_nt(
            q_ref[r0 * GRID_W:r1 * GRID_W, :], k_ref[rs * GRID_W:rs * GRID_W + kw, :])

    (e_n, e_c), inv = _softmax_numerators(
        [sn_scr[...] * LOGIT_SCALE + bias_ref[...], sc_scr[...] * LOGIT_SCALE])
    en_scr[...] = e_n
    ec_scr[...] = e_c
    o_n = jnp.concatenate(
        [_dot(en_scr[r0 * GRID_W:r1 * GRID_W, :], v_ref[rs * GRID_W:rs * GRID_W + kw, :])
         for r0, r1, rs in groups], axis=0)
    o = (o_n + _dot(ec_scr[...], cv_scr[h])) * inv
    o_ref[...] = (o * _silu(gate_ref[...].astype(F32))).astype(o_ref.dtype)


def _na_attention(proj, cache_k, cache_v, bias):
    n = DEC_SEQ
    hd = HEAD_DIM
    cache_spec = pl.BlockSpec((None, PAST_LEN * B_HEADS, hd), lambda b, h: (b, 0, 0))
    return pl.pallas_call(
        _na_attn_kernel,
        grid=(DEC_BATCH, B_HEADS),
        in_specs=[
            pl.BlockSpec((n, hd), lambda b, h: (b, COL_QB // hd + h)),
            pl.BlockSpec((n, hd), lambda b, h: (b, COL_KB // hd + h)),
            pl.BlockSpec((n, hd), lambda b, h: (b, COL_VB // hd + h)),
            cache_spec,
            cache_spec,
            pl.BlockSpec((n, hd), lambda b, h: (b, (COL_GATE + A_WIDTH) // hd + h)),
            pl.BlockSpec((None, n, B_WIN_ROWS * GRID_W), lambda b, h: (h, 0, 0)),
        ],
        out_specs=pl.BlockSpec((n, hd), lambda b, h: (b, h)),
        out_shape=jax.ShapeDtypeStruct((N_SAMPLE, B_WIDTH), BF16),
        scratch_shapes=[pltpu.VMEM((B_HEADS, PAST_LEN, hd), BF16),
                        pltpu.VMEM((B_HEADS, PAST_LEN, hd), BF16),
                        pltpu.VMEM((n, B_WIN_ROWS * GRID_W), F32),
                        pltpu.VMEM((n, PAST_LEN), F32),
                        pltpu.VMEM((n, B_WIN_ROWS * GRID_W), BF16),
                        pltpu.VMEM((n, PAST_LEN), BF16)],
        compiler_params=_params(2),
        name="na_attention",
    )(proj, proj, proj, cache_k, cache_v, proj, bias)


def _pool_kernel(u_ref, gate_ref, band_ref, inv_count_ref, w_ref, scale_ref, o_ref, *, seq, rows):
    cb = POOL_BAND_BLOCK
    parts = []
    for i in range(rows // cb):
        lo = i * cb if seq <= cb else max(0, (i - 1) * cb)
        hi = (i + 1) * cb if seq <= cb else min(rows, (i + 2) * cb)
        parts.append(_dot(band_ref[i * cb:(i + 1) * cb, lo:hi], u_ref[lo:hi, :]))
    wsum = jnp.concatenate(parts, axis=0)
    pooled = wsum * inv_count_ref[...] - u_ref[...].astype(F32)
    y = _dot(pooled.astype(BF16), w_ref[...]) * scale_ref[...]
    o_ref[...] = (y * _silu(gate_ref[...].astype(F32))).astype(o_ref.dtype)


def _pool_operators(seq, rows):
    t = jnp.arange(rows)[:, None]
    j = jnp.arange(rows)[None, :]
    same_seq = (t // seq) == (j // seq)
    pos = t % seq
    bands, inv_counts = [], []
    for window in POOL_WINDOWS:
        half = window // 2
        assert half <= POOL_BAND_BLOCK
        bands.append(((j - t >= -half) & (j - t < half) & same_seq).astype(BF16))
        inv_counts.append(1.0 / (jnp.minimum(pos + half, seq) - jnp.maximum(pos - half, 0)).astype(F32))
    return jnp.stack(bands), jnp.stack(inv_counts)


def _pool_mixer(ug, w_grp, scale, seq, rows=1024):
    m = ug.shape[0]
    n_groups = len(POOL_WINDOWS)
    pg = POOL_GROUP
    band, inv_count = _pool_operators(seq, rows)
    return pl.pallas_call(
        functools.partial(_pool_kernel, seq=seq, rows=rows),
        grid=(n_groups, m // rows),
        in_specs=[
            pl.BlockSpec((rows, pg), lambda g, i: (i, g)),
            pl.BlockSpec((rows, pg), lambda g, i: (i, n_groups + g)),
            pl.BlockSpec((None, rows, rows), lambda g, i: (g, 0, 0)),
            pl.BlockSpec((None, rows, 1), lambda g, i: (g, 0, 0)),
            pl.BlockSpec((None, pg, pg), lambda g, i: (g, 0, 0)),
            pl.BlockSpec((1, pg), lambda g, i: (0, g)),
        ],
        out_specs=pl.BlockSpec((rows, pg), lambda g, i: (i, g)),
        out_shape=jax.ShapeDtypeStruct((m, n_groups * pg), BF16),
        compiler_params=_params(2),
        name="pool_mixer",
    )(ug, ug, band, inv_count, w_grp, scale.reshape(1, -1))


def _rope_tables():
    t = jnp.arange(DEC_SEQ)
    quarter = HEAD_DIM // 4
    inv_freq = ROPE_BASE ** (-jnp.arange(quarter, dtype=F32) / quarter)
    ang_r = (t // GRID_W).astype(F32)[:, None] * inv_freq
    ang_c = (t % GRID_W).astype(F32)[:, None] * inv_freq
    ang = jnp.concatenate([ang_r, ang_r, ang_c, ang_c], axis=-1)
    sign = jnp.tile(jnp.concatenate([-jnp.ones((quarter,), F32), jnp.ones((quarter,), F32)]), 2)
    return jnp.cos(ang), jnp.sin(ang) * sign


def kernel(x_prompt, x_sample, c, cache_a_k, cache_a_v, cache_b_k, cache_b_v, c_ctx,
           w_ada, b_ada, norm_g, w_in_attn, a_sink, b_rpb, w_out_attn,
           w_in_pool, w_grp_pool, pool_scale, w_out_pool, final_g):
    d = D_MODEL
    xp = x_prompt.reshape(N_PROMPT, d)
    xs = x_sample.reshape(N_SAMPLE, d)

    w_in0 = w_in_attn[0].astype(BF16)

    cond = jnp.zeros((MOD_ROWS, d), F32).at[:DEC_BATCH].set(c).at[CTX_MOD_ROW].set(c_ctx)
    mod = _ada(cond, w_ada, b_ada)
    mod0 = mod[0].reshape(MOD_ROWS, 1, 3 * d)
    mod1 = mod[1].reshape(MOD_ROWS, 1, 3 * d)

    hp, hs = _norm_mod(xp, xs, norm_g[0], mod0)

    ka, va, kb, vb, new_a_k, new_a_v, new_b_k, new_b_v = _kv_projection(hp, w_in0)
    qg_tiles = (_col_tiles(COL_QA, A_WIDTH) + _col_tiles(COL_QB, B_WIDTH)
                + _col_tiles(COL_GATE, MIX_WIDTH))
    qg_p, w_in1 = _matmul(hp, w_in0, qg_tiles, BF16, cast_weights=(w_in_pool[0],))
    n_groups = len(POOL_WINDOWS)
    proj_s, w_out0, w_out1, w_grp = _matmul(
        hs, w_in0, _col_tiles(0, ATTN_IN_WIDTH), BF16,
        cast_weights=(w_out_attn[0], w_out_pool[0],
                      w_grp_pool[0].reshape(n_groups * POOL_GROUP, POOL_GROUP)))
    w_grp = w_grp.reshape(n_groups, POOL_GROUP, POOL_GROUP)

    sink = a_sink[0]
    og_p = _ctx_attention(sink, qg_p, ka, va, kb, vb)

    cos, sin_signed = _rope_tables()
    flat = lambda cache: cache.reshape(DEC_BATCH, -1, HEAD_DIM)
    oa_s = _win_attention(sink, proj_s, flat(cache_a_k), flat(cache_a_v), cos, sin_signed)
    ob_s = _na_attention(proj_s, flat(cache_b_k), flat(cache_b_v), _expand_bias(b_rpb[0]))

    xp1, hu_p, ss_p = _matmul_residual_prenorm([(og_p, 0), (og_p, 1)], w_out0, xp, mod0, True,
                                               norm_g[1], mod1)
    xs1, hu_s, ss_s = _matmul_residual_prenorm([(oa_s, 0), (ob_s, 0)], w_out0, xs, mod0, False,
                                               norm_g[1], mod1)

    shw = _shift_projection(mod[1], w_in1).reshape(MOD_ROWS, 1, 2 * d)
    ug_p = _matmul_postnorm(hu_p, ss_p, w_in1, shw, True, BF16)
    ug_s = _matmul_postnorm(hu_s, ss_s, w_in1, shw, False, BF16)
    y_p = _pool_mixer(ug_p, w_grp, pool_scale[0], SEQ)
    y_s = _pool_mixer(ug_s, w_grp, pool_scale[0], DEC_SEQ)
    y_prompt = _matmul_residual_norm([(y_p, 0), (y_p, 1)], w_out1, xp1, mod1, True,
                                     final_g).reshape(BATCH, SEQ, d)
    y_sample = _matmul_residual_norm([(y_s, 0), (y_s, 1)], w_out1, xs1, mod1, False,
                                     final_g).reshape(DEC_BATCH, DEC_SEQ, d)

    return (y_prompt, y_sample, new_a_k, new_a_v, new_b_k, new_b_v)
```

```python
import functools

import jax
import jax.numpy as jnp
from jax import lax
from jax.experimental import pallas as pl
from jax.experimental.pallas import tpu as pltpu

F32 = jnp.float32
BF16 = jnp.bfloat16

D_MODEL = 4096
BATCH = 32
SEQ = 256
DEC_BATCH = 8
DEC_SEQ = 1024
PAST_LEN = 512
GRID_W = 64
HEAD_DIM = 128
A_Q_HEADS = 16
A_KV_HEADS = 4
A_GROUPS = 4
A_HALF_WIN = 128
A_BLOCK = 128
B_HEADS = 16
B_WIN_ROWS = 8
B_WIN_COLS = 16
A_WIDTH = A_Q_HEADS * HEAD_DIM
A_KV_WIDTH = A_KV_HEADS * HEAD_DIM
B_WIDTH = B_HEADS * HEAD_DIM
MIX_WIDTH = A_WIDTH + B_WIDTH
POOL_WINDOWS = (2, 4, 8, 16)
POOL_GROUP = 1024
POOL_BAND_BLOCK = 256
CAST_CHUNK_ROWS = 64
ROPE_BASE = 10000.0
NORM_EPS = 1e-6
NEG_INF = -1e30
ATTN_SCALE = HEAD_DIM ** -0.5
LOG2E = 1.4426950408889634
LOGIT_SCALE = ATTN_SCALE * LOG2E

N_PROMPT = BATCH * SEQ
N_SAMPLE = DEC_BATCH * DEC_SEQ
MOD_ROWS = 16
CTX_MOD_ROW = DEC_BATCH
GRID_ROWS = DEC_SEQ // GRID_W
N_BIAS_PAIRS = 2 * B_WIN_ROWS - 2
RPB_ROWS = 2 * B_WIN_ROWS - 1
RPB_COLS = 2 * B_WIN_COLS - 1

COL_QA = 0
COL_KA = COL_QA + A_WIDTH
COL_VA = COL_KA + A_KV_WIDTH
COL_QB = COL_VA + A_KV_WIDTH
COL_KB = COL_QB + B_WIDTH
COL_VB = COL_KB + B_WIDTH
COL_GATE = COL_VB + B_WIDTH
ATTN_IN_WIDTH = COL_GATE + MIX_WIDTH
PROJ_TN = 1024


def _col_tiles(col0, width):
    assert col0 % PROJ_TN == 0 and width % PROJ_TN == 0
    return tuple(range(col0 // PROJ_TN, (col0 + width) // PROJ_TN))


def _tile_lookup(tiles):
    runs = []
    for jj, t in enumerate(tiles):
        if not runs or runs[-1][1] != t - jj:
            runs.append((jj, t - jj))

    def lookup(j):
        off = runs[0][1]
        for start, o in runs[1:]:
            off = jnp.where(j >= start, o, off)
        return j + off
    return lookup

VMEM_LIMIT = 48 * 1024 * 1024
VMEM_LIMIT_ROWS = 56 * 1024 * 1024


def _params(n_grid_dims):
    return pltpu.CompilerParams(
        dimension_semantics=("arbitrary",) * n_grid_dims,
        vmem_limit_bytes=VMEM_LIMIT,
    )


def _silu(x):
    return 0.5 * x * (1.0 + jnp.tanh(0.5 * x))


def _dot_nt(a, b):
    return lax.dot_general(a, b, (((1,), (1,)), ((), ())), preferred_element_type=F32)


def _dot(a, b):
    return jnp.dot(a, b, preferred_element_type=F32)


def _ada_kernel(cond_ref, w_ref, b_ref, o_ref):
    a = _silu(cond_ref[...]).astype(BF16)
    o_ref[...] = _dot(a, w_ref[...].astype(BF16)) + b_ref[...]


def _ada(cond, w_ada, b_ada, tn=512):
    depth, d, n = w_ada.shape
    return pl.pallas_call(
        _ada_kernel,
        grid=(depth, n // tn),
        in_specs=[
            pl.BlockSpec((MOD_ROWS, d), lambda l, j: (0, 0)),
            pl.BlockSpec((None, d, tn), lambda l, j: (l, 0, j)),
            pl.BlockSpec((None, 1, tn), lambda l, j: (l, 0, j)),
        ],
        out_specs=pl.BlockSpec((None, MOD_ROWS, tn), lambda l, j: (l, 0, j)),
        out_shape=jax.ShapeDtypeStruct((depth, MOD_ROWS, n), F32),
        compiler_params=_params(2),
        name="ada",
    )(cond, w_ada, b_ada.reshape(depth, 1, n))


def _mod_row_fn(is_prompt, tm):
    if is_prompt:
        return lambda i: CTX_MOD_ROW
    return lambda i: (i * tm) // DEC_SEQ


def _norm_mod_kernel(xp_ref, xs_ref, g_ref, sh_ref, sc_ref, hp_ref, hs_ref, *, n_prompt_tiles):
    def norm_mod(x_ref, o_ref):
        x = x_ref[...]
        ms = jnp.mean(x * x, axis=-1, keepdims=True)
        y = x * lax.rsqrt(ms + NORM_EPS) * g_ref[...]
        o_ref[...] = (y * (1.0 + sc_ref[...]) + sh_ref[...]).astype(o_ref.dtype)

    i = pl.program_id(0)

    @pl.when(i < n_prompt_tiles)
    def _():
        norm_mod(xp_ref, hp_ref)

    @pl.when(i >= n_prompt_tiles)
    def _():
        norm_mod(xs_ref, hs_ref)


def _norm_mod(xp, xs, gain, mod, tm=256):
    d = xp.shape[1]
    n_p, n_s = xp.shape[0] // tm, xs.shape[0] // tm
    p_blk = lambda i: (jnp.minimum(i, n_p - 1), 0)
    s_blk = lambda i: (jnp.maximum(i - n_p, 0), 0)
    row = lambda i: jnp.where(i < n_p, CTX_MOD_ROW, (jnp.maximum(i - n_p, 0) * tm) // DEC_SEQ)
    return pl.pallas_call(
        functools.partial(_norm_mod_kernel, n_prompt_tiles=n_p),
        grid=(n_p + n_s,),
        in_specs=[
            pl.BlockSpec((tm, d), p_blk),
            pl.BlockSpec((tm, d), s_blk),
            pl.BlockSpec((1, d), lambda i: (0, 0)),
            pl.BlockSpec((None, 1, d), lambda i: (row(i), 0, 0)),
            pl.BlockSpec((None, 1, d), lambda i: (row(i), 0, 1)),
        ],
        out_specs=[pl.BlockSpec((tm, d), p_blk), pl.BlockSpec((tm, d), s_blk)],
        out_shape=[jax.ShapeDtypeStruct(xp.shape, BF16), jax.ShapeDtypeStruct(xs.shape, BF16)],
        compiler_params=_params(1),
        name="norm_mod",
    )(xp, xs, gain.reshape(1, d), mod, mod)


def _mm_kernel(a_ref, w_ref, *refs, n_casts):
    cast_in, o_ref, cast_out = refs[:n_casts], refs[n_casts], refs[n_casts + 1:]
    o_ref[...] = _dot(a_ref[...], w_ref[...]).astype(o_ref.dtype)
    for src, dst in zip(cast_in, cast_out):
        dst[...] = src[...].astype(dst.dtype)


def _matmul(a, w, w_tiles, out_dtype, cast_weights=()):
    m, k = a.shape
    tn = PROJ_TN
    tm = 1024 if out_dtype == BF16 else 512
    assert m % tm == 0
    w_tile = _tile_lookup(w_tiles)
    ncols = len(w_tiles) * tn
    n_i = m // tm
    n_steps = len(w_tiles) * n_i
    in_specs = [pl.BlockSpec((tm, k), lambda j, i: (i, 0)),
                pl.BlockSpec((k, tn), lambda j, i: (0, w_tile(j)))]
    out_specs = [pl.BlockSpec((tm, tn), lambda j, i: (i, j))]
    out_shape = [jax.ShapeDtypeStruct((m, ncols), out_dtype)]
    for cw in cast_weights:
        rows, cols = cw.shape
        chunk = CAST_CHUNK_ROWS
        n_chunks = rows // chunk
        assert rows % chunk == 0 and n_chunks <= n_steps
        spec = pl.BlockSpec((chunk, cols),
                            lambda j, i, n_chunks=n_chunks: (jnp.minimum(j * n_i + i, n_chunks - 1), 0))
        in_specs.append(spec)
        out_specs.append(spec)
        out_shape.append(jax.ShapeDtypeStruct((rows, cols), BF16))
    outs = pl.pallas_call(
        functools.partial(_mm_kernel, n_casts=len(cast_weights)),
        grid=(ncols // tn, n_i),
        in_specs=in_specs,
        out_specs=out_specs,
        out_shape=out_shape,
        compiler_params=pltpu.CompilerParams(
            dimension_semantics=("arbitrary", "arbitrary"),
            vmem_limit_bytes=VMEM_LIMIT_ROWS if cast_weights else VMEM_LIMIT),
        name="proj",
    )(a, w, *cast_weights)
    return outs if cast_weights else outs[0]


def _kv_proj_kernel(a_ref, w_ref, ka_ref, va_ref, kb_ref, vb_ref, ka5_ref, va5_ref, kb5_ref, vb5_ref,
                    *, kb_tile0, vb_tile0):
    j = pl.program_id(0)
    acc = _dot(a_ref[...], w_ref[...])

    def emit(cols, out_ref, out5_ref):
        part = acc[:, cols]
        out_ref[...] = part.astype(out_ref.dtype)
        out5_ref[...] = part.reshape(out5_ref.shape)

    @pl.when(j < kb_tile0)
    def _():
        emit(slice(0, A_KV_WIDTH), ka_ref, ka5_ref)
        emit(slice(A_KV_WIDTH, 2 * A_KV_WIDTH), va_ref, va5_ref)

    @pl.when((j >= kb_tile0) & (j < vb_tile0))
    def _():
        emit(slice(None), kb_ref, kb5_ref)

    @pl.when(j >= vb_tile0)
    def _():
        emit(slice(None), vb_ref, vb5_ref)


def _kv_projection(a, w, tm=512):
    m, k = a.shape
    tn = PROJ_TN
    assert COL_VA == COL_KA + A_KV_WIDTH and 2 * A_KV_WIDTH == tn and tm % SEQ == 0
    w_tiles = _col_tiles(COL_KA, tn) + _col_tiles(COL_KB, B_WIDTH) + _col_tiles(COL_VB, B_WIDTH)
    w_tile = _tile_lookup(w_tiles)
    kb_tile0 = 1
    vb_tile0 = kb_tile0 + B_WIDTH // tn
    end_tile = len(w_tiles)
    n_i = m // tm
    last = n_i - 1

    def parked(j, i, t0, t1):
        row = jnp.where(j < t0, 0, jnp.where(j < t1, i, last))
        return row, jnp.clip(j - t0, 0, t1 - t0 - 1)

    def parked5(j, i, t0, t1):
        row, col = parked(j, i, t0, t1)
        return row, 0, 0, col, 0

    req = tm // SEQ
    a_spec = pl.BlockSpec((tm, A_KV_WIDTH), lambda j, i: parked(j, i, 0, kb_tile0))
    a_spec5 = pl.BlockSpec((req, None, SEQ, A_KV_HEADS, HEAD_DIM),
                           lambda j, i: parked5(j, i, 0, kb_tile0))
    b_block5 = (req, None, SEQ, tn // HEAD_DIM, HEAD_DIM)
    cache_shape_a = (m // SEQ, 1, SEQ, A_KV_HEADS, HEAD_DIM)
    cache_shape_b = (m // SEQ, 1, SEQ, B_HEADS, HEAD_DIM)
    return pl.pallas_call(
        functools.partial(_kv_proj_kernel, kb_tile0=kb_tile0, vb_tile0=vb_tile0),
        grid=(end_tile, n_i),
        in_specs=[pl.BlockSpec((tm, k), lambda j, i: (i, 0)),
                  pl.BlockSpec((k, tn), lambda j, i: (0, w_tile(j)))],
        out_specs=[a_spec, a_spec,
                   pl.BlockSpec((tm, tn), lambda j, i: parked(j, i, kb_tile0, vb_tile0)),
                   pl.BlockSpec((tm, tn), lambda j, i: parked(j, i, vb_tile0, end_tile)),
                   a_spec5, a_spec5,
                   pl.BlockSpec(b_block5, lambda j, i: parked5(j, i, kb_tile0, vb_tile0)),
                   pl.BlockSpec(b_block5, lambda j, i: parked5(j, i, vb_tile0, end_tile))],
        out_shape=[jax.ShapeDtypeStruct((m, A_KV_WIDTH), BF16),
                   jax.ShapeDtypeStruct((m, A_KV_WIDTH), BF16),
                   jax.ShapeDtypeStruct((m, B_WIDTH), BF16),
                   jax.ShapeDtypeStruct((m, B_WIDTH), BF16),
                   jax.ShapeDtypeStruct(cache_shape_a, F32),
                   jax.ShapeDtypeStruct(cache_shape_a, F32),
                   jax.ShapeDtypeStruct(cache_shape_b, F32),
                   jax.ShapeDtypeStruct(cache_shape_b, F32)],
        compiler_params=pltpu.CompilerParams(dimension_semantics=("arbitrary", "arbitrary"),
                                             vmem_limit_bytes=VMEM_LIMIT_ROWS),
        name="kv_proj",
    )(a, w)


def _dot_halves(a1_ref, a2_ref, w_ref):
    kh = a1_ref.shape[1]
    return _dot(a1_ref[...], w_ref[:kh, :]) + _dot(a2_ref[...], w_ref[kh:, :])


def _mm_res_prenorm_kernel(a1_ref, a2_ref, w_ref, x_ref, g_ref, gain_ref, sc_ref,
                           x1_ref, hu_ref, ss_ref):
    x1 = x_ref[...] + g_ref[...] * _dot_halves(a1_ref, a2_ref, w_ref)
    x1_ref[...] = x1
    hu_ref[...] = (x1 * (gain_ref[...] * (1.0 + sc_ref[...]))).astype(hu_ref.dtype)
    ss_ref[...] = jnp.sum(x1 * x1, axis=-1, keepdims=True)


def _matmul_residual_prenorm(a_halves, w, x, mod, is_prompt, gain, next_mod, tm=512):
    (a1, c1), (a2, c2) = a_halves
    m = a1.shape[0]
    k, n = w.shape
    tn = PROJ_TN
    n_tiles = n // tn
    row = _mod_row_fn(is_prompt, tm)
    scale_block0, gate_block0 = n_tiles, 2 * n_tiles
    tile = pl.BlockSpec((tm, tn), lambda j, i: (i, j))
    return pl.pallas_call(
        _mm_res_prenorm_kernel,
        grid=(n_tiles, m // tm),
        in_specs=[pl.BlockSpec((tm, k // 2), lambda j, i: (i, c1)),
                  pl.BlockSpec((tm, k // 2), lambda j, i: (i, c2)),
                  pl.BlockSpec((k, tn), lambda j, i: (0, j)),
                  tile,
                  pl.BlockSpec((None, 1, tn), lambda j, i: (row(i), 0, gate_block0 + j)),
                  pl.BlockSpec((1, tn), lambda j, i: (0, j)),
                  pl.BlockSpec((None, 1, tn), lambda j, i: (row(i), 0, scale_block0 + j))],
        out_specs=[tile, tile, pl.BlockSpec((None, tm, 1), lambda j, i: (j, i, 0))],
        out_shape=[jax.ShapeDtypeStruct((m, n), F32), jax.ShapeDtypeStruct((m, n), BF16),
                   jax.ShapeDtypeStruct((n_tiles, m, 1), F32)],
        compiler_params=_params(2),
        name="proj_residual_prenorm",
    )(a1, a2, w, x, mod, gain.reshape(1, n), next_mod)


def _shift_proj_kernel(sh_ref, w_ref, o_ref):
    o_ref[...] = _dot(sh_ref[...].astype(BF16), w_ref[...])


def _shift_projection(mod2d, w):
    k, n = w.shape
    tn = PROJ_TN
    return pl.pallas_call(
        _shift_proj_kernel,
        grid=(n // tn,),
        in_specs=[pl.BlockSpec((MOD_ROWS, k), lambda j: (0, 0)),
                  pl.BlockSpec((k, tn), lambda j: (0, j))],
        out_specs=pl.BlockSpec((MOD_ROWS, tn), lambda j: (0, j)),
        out_shape=jax.ShapeDtypeStruct((MOD_ROWS, n), F32),
        compiler_params=_params(1),
        name="shift_proj",
    )(mod2d, w)


def _mm_postnorm_kernel(a_ref, w_ref, ss_ref, shw_ref, o_ref, *, width):
    acc = _dot(a_ref[...], w_ref[...])
    ss = ss_ref[0]
    for t in range(1, ss_ref.shape[0]):
        ss = ss + ss_ref[t]
    r = lax.rsqrt(ss / width + NORM_EPS)
    o_ref[...] = (acc * r + shw_ref[...]).astype(o_ref.dtype)


def _matmul_postnorm(hu, ss, w, shw, is_prompt, out_dtype, tm=1024):
    m, k = hu.shape
    n = w.shape[1]
    tn = PROJ_TN
    row = _mod_row_fn(is_prompt, tm)
    return pl.pallas_call(
        functools.partial(_mm_postnorm_kernel, width=k),
        grid=(n // tn, m // tm),
        in_specs=[pl.BlockSpec((tm, k), lambda j, i: (i, 0)),
                  pl.BlockSpec((k, tn), lambda j, i: (0, j)),
                  pl.BlockSpec((ss.shape[0], tm, 1), lambda j, i: (0, i, 0)),
                  pl.BlockSpec((None, 1, tn), lambda j, i: (row(i), 0, j))],
        out_specs=pl.BlockSpec((tm, tn), lambda j, i: (i, j)),
        out_shape=jax.ShapeDtypeStruct((m, n), out_dtype),
        compiler_params=pltpu.CompilerParams(dimension_semantics=("arbitrary", "arbitrary"),
                                             vmem_limit_bytes=VMEM_LIMIT_ROWS),
        name="proj_postnorm",
    )(hu, w, ss, shw)


def _mm_res_norm_kernel(a1_ref, a2_ref, w_ref, x_ref, g_ref, gain_ref, y_ref, *, n_tiles, tn):
    j = pl.program_id(1)
    x2 = x_ref[...] + g_ref[...] * _dot_halves(a1_ref, a2_ref, w_ref)
    for jj in range(n_tiles):
        @pl.when(j == jj)
        def _(jj=jj):
            y_ref[:, jj * tn:(jj + 1) * tn] = x2

    @pl.when(j == n_tiles - 1)
    def _():
        ss = None
        for jj in range(n_tiles):
            t = y_ref[:, jj * tn:(jj + 1) * tn]
            part = jnp.sum(t * t, axis=-1, keepdims=True)
            ss = part if ss is None else ss + part
        r = lax.rsqrt(ss / (n_tiles * tn) + NORM_EPS)
        for jj in range(n_tiles):
            cols = slice(jj * tn, (jj + 1) * tn)
            y_ref[:, cols] = y_ref[:, cols] * r * gain_ref[:, cols]


def _matmul_residual_norm(a_halves, w, x, mod, is_prompt, gain, tm=512):
    (a1, c1), (a2, c2) = a_halves
    m = a1.shape[0]
    k, n = w.shape
    tn = PROJ_TN
    n_tiles = n // tn
    row = _mod_row_fn(is_prompt, tm)
    gate_block0 = 2 * n_tiles
    return pl.pallas_call(
        functools.partial(_mm_res_norm_kernel, n_tiles=n_tiles, tn=tn),
        grid=(m // tm, n_tiles),
        in_specs=[pl.BlockSpec((tm, k // 2), lambda i, j: (i, c1)),
                  pl.BlockSpec((tm, k // 2), lambda i, j: (i, c2)),
                  pl.BlockSpec((k, tn), lambda i, j: (0, j)),
                  pl.BlockSpec((tm, tn), lambda i, j: (i, j)),
                  pl.BlockSpec((None, 1, tn), lambda i, j: (row(i), 0, gate_block0 + j)),
                  pl.BlockSpec((1, n), lambda i, j: (0, 0))],
        out_specs=pl.BlockSpec((tm, n), lambda i, j: (i, 0)),
        out_shape=jax.ShapeDtypeStruct((m, n), F32),
        compiler_params=pltpu.CompilerParams(dimension_semantics=("arbitrary", "arbitrary"),
                                             vmem_limit_bytes=VMEM_LIMIT_ROWS),
        name="proj_residual_norm",
    )(a1, a2, w, x, mod, gain.reshape(1, n))


def _softmax_numerators(logits, sink2=None):
    m = functools.reduce(jnp.maximum, [jnp.max(t, axis=-1, keepdims=True) for t in logits])
    if sink2 is not None:
        m = jnp.maximum(m, sink2)
    es = [jnp.exp2(t - m) for t in logits]
    l = functools.reduce(jnp.add, [jnp.sum(e, axis=-1, keepdims=True) for e in es])
    if sink2 is not None:
        l = l + jnp.exp2(sink2 - m)
    return [e.astype(BF16) for e in es], 1.0 / l


def _ctx_attn_kernel(sink_ref, qg_ref, ka_ref, va_ref, kb_ref, vb_ref, o_ref, s_scr, e_scr):
    n = SEQ
    gate0 = MIX_WIDTH
    n_stack = A_GROUPS

    def softmax(sink2):
        (e,), inv = _softmax_numerators([s_scr[...] * LOGIT_SCALE], sink2)
        e_scr[...] = e
        return inv

    def emit(o, g, out_col):
        cols = slice(out_col, out_col + HEAD_DIM)
        gate = qg_ref[:, gate0 + out_col:gate0 + out_col + HEAD_DIM].astype(F32)
        o_ref[:, cols] = (o[g * n:(g + 1) * n] * _silu(gate)).astype(o_ref.dtype)

    for kv in range(A_KV_HEADS):
        cols = slice(kv * HEAD_DIM, (kv + 1) * HEAD_DIM)
        heads = [kv * A_GROUPS + g for g in range(A_GROUPS)]
        q = jnp.concatenate(
            [qg_ref[:, h * HEAD_DIM:(h + 1) * HEAD_DIM] for h in heads], axis=0)
        sink2 = jnp.concatenate(
            [jnp.full((n, 1), sink_ref[h] * LOG2E, F32) for h in heads], axis=0)
        s_scr[...] = _dot_nt(q, ka_ref[:, cols])
        inv = softmax(sink2)
        o = _dot(e_scr[...], va_ref[:, cols]) * inv
        for g, h in enumerate(heads):
            emit(o, g, h * HEAD_DIM)
    for h0 in range(0, B_HEADS, n_stack):
        heads = range(h0, h0 + n_stack)
        for g, h in enumerate(heads):
            cols = slice(h * HEAD_DIM, (h + 1) * HEAD_DIM)
            q = qg_ref[:, A_WIDTH + h * HEAD_DIM:A_WIDTH + (h + 1) * HEAD_DIM]
            s_scr[g * n:(g + 1) * n, :] = _dot_nt(q, kb_ref[:, cols])
        inv = softmax(None)
        o = jnp.concatenate(
            [_dot(e_scr[g * n:(g + 1) * n, :],
                  vb_ref[:, h * HEAD_DIM:(h + 1) * HEAD_DIM])
             for g, h in enumerate(heads)], axis=0) * inv
        for g, h in enumerate(heads):
            emit(o, g, A_WIDTH + h * HEAD_DIM)


def _ctx_attention(sink, qg, ka, va, kb, vb):
    stack_rows = A_GROUPS * SEQ
    return pl.pallas_call(
        _ctx_attn_kernel,
        grid=(BATCH,),
        in_specs=[
            pl.BlockSpec(memory_space=pltpu.SMEM),
            pl.BlockSpec((SEQ, 2 * MIX_WIDTH), lambda b: (b, 0)),
            pl.BlockSpec((SEQ, A_KV_WIDTH), lambda b: (b, 0)),
            pl.BlockSpec((SEQ, A_KV_WIDTH), lambda b: (b, 0)),
            pl.BlockSpec((SEQ, B_WIDTH), lambda b: (b, 0)),
            pl.BlockSpec((SEQ, B_WIDTH), lambda b: (b, 0)),
        ],
        out_specs=pl.BlockSpec((SEQ, MIX_WIDTH), lambda b: (b, 0)),
        out_shape=jax.ShapeDtypeStruct((N_PROMPT, MIX_WIDTH), BF16),
        scratch_shapes=[pltpu.VMEM((stack_rows, SEQ), F32),
                        pltpu.VMEM((stack_rows, SEQ), BF16)],
        compiler_params=_params(1),
        name="ctx_attention",
    )(sink, qg, ka, va, kb, vb)


def _split_cache_heads(c_ref, scr, n_heads):
    x = c_ref[...].reshape(PAST_LEN, n_heads, HEAD_DIM)
    scr[...] = pltpu.einshape("thd->htd", x).astype(scr.dtype)


def _win_attn_kernel(sink_ref, q_ref, k_ref, v_ref, ck_ref, cv_ref, gate_ref,
                     cos_ref, sin_ref, o_ref, qs_ref, ks_ref, ck_scr, cv_scr,
                     band_scr, sw_scr, sc_scr, ew_scr, ec_scr):
    kv = pl.program_id(1)
    n = DEC_SEQ

    @pl.when(kv == 0)
    def _():
        _split_cache_heads(ck_ref, ck_scr, A_KV_HEADS)
        _split_cache_heads(cv_ref, cv_scr, A_KV_HEADS)

    cos = cos_ref[...]
    sin = sin_ref[...]
    lane = lax.broadcasted_iota(jnp.int32, (n, HEAD_DIM), 1)
    first_quarter = (lane % (HEAD_DIM // 2)) < (HEAD_DIM // 4)

    def rope(x):
        rot = jnp.where(first_quarter,
                        pltpu.roll(x, HEAD_DIM - HEAD_DIM // 4, 1),
                        pltpu.roll(x, HEAD_DIM // 4, 1))
        return x * cos + rot * sin

    ks_ref[...] = rope(k_ref[...].astype(F32)).astype(BF16)
    for g in range(A_GROUPS):
        qs_ref[g] = (rope(q_ref[:, g * HEAD_DIM:(g + 1) * HEAD_DIM].astype(F32))
                     * LOGIT_SCALE).astype(BF16)

    ck = ck_scr[kv]
    cv = cv_scr[kv]
    rows = A_GROUPS * A_BLOCK
    sink2 = jnp.concatenate(
        [jnp.full((A_BLOCK, 1), sink_ref[kv * A_GROUPS + g] * LOG2E, F32)
         for g in range(A_GROUPS)], axis=0)
    qi = lax.broadcasted_iota(jnp.int32, (rows, 3 * A_BLOCK), 0) % A_BLOCK
    rel = lax.broadcasted_iota(jnp.int32, (rows, 3 * A_BLOCK), 1) - A_BLOCK - qi
    band_scr[...] = jnp.where(jnp.abs(rel) <= A_HALF_WIN, 0.0, NEG_INF)
    for blk in range(n // A_BLOCK):
        r0 = blk * A_BLOCK
        lo = max(0, r0 - A_BLOCK)
        hi = min(n, r0 + 2 * A_BLOCK)
        span = hi - lo
        b0 = lo - (r0 - A_BLOCK)
        q = jnp.concatenate([qs_ref[g, r0:r0 + A_BLOCK, :] for g in range(A_GROUPS)], axis=0)
        sw_scr[:, :span] = _dot_nt(q, ks_ref[lo:hi, :])
        sc_scr[...] = _dot_nt(q, ck)

        (e_w, e_c), inv = _softmax_numerators(
            [sw_scr[:, :span] + band_scr[:, b0:b0 + span], sc_scr[...]], sink2)
        ew_scr[:, :span] = e_w
        ec_scr[...] = e_c
        o = (_dot(ew_scr[:, :span], v_ref[lo:hi, :]) + _dot(ec_scr[...], cv)) * inv
        for g in range(A_GROUPS):
            hc = slice(g * HEAD_DIM, (g + 1) * HEAD_DIM)
            gate = gate_ref[r0:r0 + A_BLOCK, hc].astype(F32)
            o_ref[r0:r0 + A_BLOCK, hc] = (
                o[g * A_BLOCK:(g + 1) * A_BLOCK] * _silu(gate)).astype(o_ref.dtype)


def _win_attention(sink, proj, cache_k, cache_v, cos, sin_signed):
    gw = A_GROUPS * HEAD_DIM
    stack_rows = A_GROUPS * A_BLOCK
    n = DEC_SEQ
    cache_spec = pl.BlockSpec((None, PAST_LEN * A_KV_HEADS, HEAD_DIM), lambda b, k: (b, 0, 0))
    return pl.pallas_call(
        _win_attn_kernel,
        grid=(DEC_BATCH, A_KV_HEADS),
        in_specs=[
            pl.BlockSpec(memory_space=pltpu.SMEM),
            pl.BlockSpec((n, gw), lambda b, k: (b, COL_QA // gw + k)),
            pl.BlockSpec((n, HEAD_DIM), lambda b, k: (b, COL_KA // HEAD_DIM + k)),
            pl.BlockSpec((n, HEAD_DIM), lambda b, k: (b, COL_VA // HEAD_DIM + k)),
            cache_spec,
            cache_spec,
            pl.BlockSpec((n, gw), lambda b, k: (b, COL_GATE // gw + k)),
            pl.BlockSpec((n, HEAD_DIM), lambda b, k: (0, 0)),
            pl.BlockSpec((n, HEAD_DIM), lambda b, k: (0, 0)),
        ],
        out_specs=pl.BlockSpec((n, gw), lambda b, k: (b, k)),
        out_shape=jax.ShapeDtypeStruct((N_SAMPLE, A_WIDTH), BF16),
        scratch_shapes=[pltpu.VMEM((A_GROUPS, n, HEAD_DIM), BF16),
                        pltpu.VMEM((n, HEAD_DIM), BF16),
                        pltpu.VMEM((A_KV_HEADS, PAST_LEN, HEAD_DIM), BF16),
                        pltpu.VMEM((A_KV_HEADS, PAST_LEN, HEAD_DIM), BF16),
                        pltpu.VMEM((stack_rows, 3 * A_BLOCK), F32),
                        pltpu.VMEM((stack_rows, 3 * A_BLOCK), F32),
                        pltpu.VMEM((stack_rows, PAST_LEN), F32),
                        pltpu.VMEM((stack_rows, 3 * A_BLOCK), BF16),
                        pltpu.VMEM((stack_rows, PAST_LEN), BF16)],
        compiler_params=_params(2),
        name="win_attention",
    )(sink, proj, proj, proj, cache_k, cache_v, proj, cos, sin_signed)


def _bias_kernel(rpb_ref, o_ref):
    h = pl.program_id(0)
    shape = (GRID_W, 2 * GRID_W)
    c = lax.broadcasted_iota(jnp.int32, shape, 0)
    j2 = lax.broadcasted_iota(jnp.int32, shape, 1)
    kc = j2 % GRID_W
    second = j2 >= GRID_W
    col_start = jnp.clip(c - B_WIN_COLS // 2, 0, GRID_W - B_WIN_COLS)
    ok = (kc >= col_start) & (kc < col_start + B_WIN_COLS)
    dc = kc - c + B_WIN_COLS - 1
    base = h * (RPB_ROWS * RPB_COLS)
    pair_scr = []
    for i in range(N_BIAS_PAIRS):
        acc = jnp.full(shape, NEG_INF, F32)
        for d in range(RPB_COLS):
            val = jnp.where(second, rpb_ref[base + (i + 1) * RPB_COLS + d],
                            rpb_ref[base + i * RPB_COLS + d]) * LOG2E
            acc = jnp.where(ok & (dc == d), val, acc)
        pair_scr.append(acc)
    for r in range(GRID_ROWS):
        dr0 = _na_key_row0(r) - r + B_WIN_ROWS - 1
        for i in range(B_WIN_ROWS // 2):
            o_ref[r * GRID_W:(r + 1) * GRID_W, i * 2 * GRID_W:(i + 1) * 2 * GRID_W] = (
                pair_scr[dr0 + 2 * i])


def _expand_bias(rpb):
    kw = B_WIN_ROWS * GRID_W
    return pl.pallas_call(
        _bias_kernel,
        grid=(B_HEADS,),
        in_specs=[pl.BlockSpec(memory_space=pltpu.SMEM)],
        out_specs=pl.BlockSpec((None, DEC_SEQ, kw), lambda h: (h, 0, 0)),
        out_shape=jax.ShapeDtypeStruct((B_HEADS, DEC_SEQ, kw), F32),
        compiler_params=_params(1),
        name="expand_bias",
    )(rpb.reshape(-1))


def _na_key_row0(r):
    return min(max(r - B_WIN_ROWS // 2, 0), GRID_ROWS - B_WIN_ROWS)


def _na_row_groups():
    groups = []
    for r in range(GRID_ROWS):
        rs = _na_key_row0(r)
        if groups and groups[-1][2] == rs:
            groups[-1] = (groups[-1][0], r + 1, rs)
        else:
            groups.append((r, r + 1, rs))
    return groups


def _na_attn_kernel(q_ref, k_ref, v_ref, ck_ref, cv_ref, gate_ref, bias_ref, o_ref,
                    ck_scr, cv_scr, sn_scr, sc_scr, en_scr, ec_scr):
    kw = B_WIN_ROWS * GRID_W
    h = pl.program_id(1)

    @pl.when(h == 0)
    def _():
        _split_cache_heads(ck_ref, ck_scr, B_HEADS)
        _split_cache_heads(cv_ref, cv_scr, B_HEADS)

    groups = _na_row_groups()
    sc_scr[...] = _dot_nt(q_ref[...], ck_scr[h])
    for r0, r1, rs in groups:
        sn_scr[r0 * GRID_W:r1 * GRID_W, :] = _dot_nt(
            q_ref[r0 * GRID_W:r1 * GRID_W, :], k_ref[rs * GRID_W:rs * GRID_W + kw, :])

    (e_n, e_c), inv = _softmax_numerators(
        [sn_scr[...] * LOGIT_SCALE + bias_ref[...], sc_scr[...] * LOGIT_SCALE])
    en_scr[...] = e_n
    ec_scr[...] = e_c
    o_n = jnp.concatenate(
        [_dot(en_scr[r0 * GRID_W:r1 * GRID_W, :], v_ref[rs * GRID_W:rs * GRID_W + kw, :])
         for r0, r1, rs in groups], axis=0)
    o = (o_n + _dot(ec_scr[...], cv_scr[h])) * inv
    o_ref[...] = (o * _silu(gate_ref[...].astype(F32))).astype(o_ref.dtype)


def _na_attention(proj, cache_k, cache_v, bias):
    n = DEC_SEQ
    hd = HEAD_DIM
    cache_spec = pl.BlockSpec((None, PAST_LEN * B_HEADS, hd), lambda b, h: (b, 0, 0))
    return pl.pallas_call(
        _na_attn_kernel,
        grid=(DEC_BATCH, B_HEADS),
        in_specs=[
            pl.BlockSpec((n, hd), lambda b, h: (b, COL_QB // hd + h)),
            pl.BlockSpec((n, hd), lambda b, h: (b, COL_KB // hd + h)),
            pl.BlockSpec((n, hd), lambda b, h: (b, COL_VB // hd + h)),
            cache_spec,
            cache_spec,
            pl.BlockSpec((n, hd), lambda b, h: (b, (COL_GATE + A_WIDTH) // hd + h)),
            pl.BlockSpec((None, n, B_WIN_ROWS * GRID_W), lambda b, h: (h, 0, 0)),
        ],
        out_specs=pl.BlockSpec((n, hd), lambda b, h: (b, h)),
        out_shape=jax.ShapeDtypeStruct((N_SAMPLE, B_WIDTH), BF16),
        scratch_shapes=[pltpu.VMEM((B_HEADS, PAST_LEN, hd), BF16),
                        pltpu.VMEM((B_HEADS, PAST_LEN, hd), BF16),
                        pltpu.VMEM((n, B_WIN_ROWS * GRID_W), F32),
                        pltpu.VMEM((n, PAST_LEN), F32),
                        pltpu.VMEM((n, B_WIN_ROWS * GRID_W), BF16),
                        pltpu.VMEM((n, PAST_LEN), BF16)],
        compiler_params=_params(2),
        name="na_attention",
    )(proj, proj, proj, cache_k, cache_v, proj, bias)


def _pool_kernel(u_ref, gate_ref, band_ref, inv_count_ref, w_ref, scale_ref, o_ref, *, seq, rows):
    cb = POOL_BAND_BLOCK
    parts = []
    for i in range(rows // cb):
        lo = i * cb if seq <= cb else max(0, (i - 1) * cb)
        hi = (i + 1) * cb if seq <= cb else min(rows, (i + 2) * cb)
        parts.append(_dot(band_ref[i * cb:(i + 1) * cb, lo:hi], u_ref[lo:hi, :]))
    wsum = jnp.concatenate(parts, axis=0)
    pooled = wsum * inv_count_ref[...] - u_ref[...].astype(F32)
    y = _dot(pooled.astype(BF16), w_ref[...]) * scale_ref[...]
    o_ref[...] = (y * _silu(gate_ref[...].astype(F32))).astype(o_ref.dtype)


def _pool_operators(seq, rows):
    t = jnp.arange(rows)[:, None]
    j = jnp.arange(rows)[None, :]
    same_seq = (t // seq) == (j // seq)
    pos = t % seq
    bands, inv_counts = [], []
    for window in POOL_WINDOWS:
        half = window // 2
        assert half <= POOL_BAND_BLOCK
        bands.append(((j - t >= -half) & (j - t < half) & same_seq).astype(BF16))
        inv_counts.append(1.0 / (jnp.minimum(pos + half, seq) - jnp.maximum(pos - half, 0)).astype(F32))
    return jnp.stack(bands), jnp.stack(inv_counts)


def _pool_mixer(ug, w_grp, scale, seq, rows=1024):
    m = ug.shape[0]
    n_groups = len(POOL_WINDOWS)
    pg = POOL_GROUP
    band, inv_count = _pool_operators(seq, rows)
    return pl.pallas_call(
        functools.partial(_pool_kernel, seq=seq, rows=rows),
        grid=(n_groups, m // rows),
        in_specs=[
            pl.BlockSpec((rows, pg), lambda g, i: (i, g)),
            pl.BlockSpec((rows, pg), lambda g, i: (i, n_groups + g)),
            pl.BlockSpec((None, rows, rows), lambda g, i: (g, 0, 0)),
            pl.BlockSpec((None, rows, 1), lambda g, i: (g, 0, 0)),
            pl.BlockSpec((None, pg, pg), lambda g, i: (g, 0, 0)),
            pl.BlockSpec((1, pg), lambda g, i: (0, g)),
        ],
        out_specs=pl.BlockSpec((rows, pg), lambda g, i: (i, g)),
        out_shape=jax.ShapeDtypeStruct((m, n_groups * pg), BF16),
        compiler_params=_params(2),
        name="pool_mixer",
    )(ug, ug, band, inv_count, w_grp, scale.reshape(1, -1))


def _rope_tables():
    t = jnp.arange(DEC_SEQ)
    quarter = HEAD_DIM // 4
    inv_freq = ROPE_BASE ** (-jnp.arange(quarter, dtype=F32) / quarter)
    ang_r = (t // GRID_W).astype(F32)[:, None] * inv_freq
    ang_c = (t % GRID_W).astype(F32)[:, None] * inv_freq
    ang = jnp.concatenate([ang_r, ang_r, ang_c, ang_c], axis=-1)
    sign = jnp.tile(jnp.concatenate([-jnp.ones((quarter,), F32), jnp.ones((quarter,), F32)]), 2)
    return jnp.cos(ang), jnp.sin(ang) * sign


def kernel(x_prompt, x_sample, c, cache_a_k, cache_a_v, cache_b_k, cache_b_v, c_ctx,
           w_ada, b_ada, norm_g, w_in_attn, a_sink, b_rpb, w_out_attn,
           w_in_pool, w_grp_pool, pool_scale, w_out_pool, final_g):
    d = D_MODEL
    xp = x_prompt.reshape(N_PROMPT, d)
    xs = x_sample.reshape(N_SAMPLE, d)

    w_in0 = w_in_attn[0].astype(BF16)

    cond = jnp.zeros((MOD_ROWS, d), F32).at[:DEC_BATCH].set(c).at[CTX_MOD_ROW].set(c_ctx)
    mod = _ada(cond, w_ada, b_ada)
    mod0 = mod[0].reshape(MOD_ROWS, 1, 3 * d)
    mod1 = mod[1].reshape(MOD_ROWS, 1, 3 * d)

    hp, hs = _norm_mod(xp, xs, norm_g[0], mod0)

    ka, va, kb, vb, new_a_k, new_a_v, new_b_k, new_b_v = _kv_projection(hp, w_in0)
    qg_tiles = (_col_tiles(COL_QA, A_WIDTH) + _col_tiles(COL_QB, B_WIDTH)
                + _col_tiles(COL_GATE, MIX_WIDTH))
    qg_p, w_in1 = _matmul(hp, w_in0, qg_tiles, BF16, cast_weights=(w_in_pool[0],))
    n_groups = len(POOL_WINDOWS)
    proj_s, w_out0, w_out1, w_grp = _matmul(
        hs, w_in0, _col_tiles(0, ATTN_IN_WIDTH), BF16,
        cast_weights=(w_out_attn[0], w_out_pool[0],
                      w_grp_pool[0].reshape(n_groups * POOL_GROUP, POOL_GROUP)))
    w_grp = w_grp.reshape(n_groups, POOL_GROUP, POOL_GROUP)

    sink = a_sink[0]
    og_p = _ctx_attention(sink, qg_p, ka, va, kb, vb)

    cos, sin_signed = _rope_tables()
    flat = lambda cache: cache.reshape(DEC_BATCH, -1, HEAD_DIM)
    oa_s = _win_attention(sink, proj_s, flat(cache_a_k), flat(cache_a_v), cos, sin_signed)
    ob_s = _na_attention(proj_s, flat(cache_b_k), flat(cache_b_v), _expand_bias(b_rpb[0]))

    xp1, hu_p, ss_p = _matmul_residual_prenorm([(og_p, 0), (og_p, 1)], w_out0, xp, mod0, True,
                                               norm_g[1], mod1)
    xs1, hu_s, ss_s = _matmul_residual_prenorm([(oa_s, 0), (ob_s, 0)], w_out0, xs, mod0, False,
                                               norm_g[1], mod1)

    shw = _shift_projection(mod[1], w_in1).reshape(MOD_ROWS, 1, 2 * d)
    ug_p = _matmul_postnorm(hu_p, ss_p, w_in1, shw, True, BF16)
    ug_s = _matmul_postnorm(hu_s, ss_s, w_in1, shw, False, BF16)
    y_p = _pool_mixer(ug_p, w_grp, pool_scale[0], SEQ)
    y_s = _pool_mixer(ug_s, w_grp, pool_scale[0], DEC_SEQ)
    y_prompt = _matmul_residual_norm([(y_p, 0), (y_p, 1)], w_out1, xp1, mod1, True,
                                     final_g).reshape(BATCH, SEQ, d)
    y_sample = _matmul_residual_norm([(y_s, 0), (y_s, 1)], w_out1, xs1, mod1, False,
                                     final_g).reshape(DEC_BATCH, DEC_SEQ, d)

    return (y_prompt, y_sample, new_a_k, new_a_v, new_b_k, new_b_v)
```

```python
import functools

import jax
import jax.numpy as jnp
from jax import lax
from jax.experimental import pallas as pl
from jax.experimental.pallas import tpu as pltpu

F32 = jnp.float32
BF16 = jnp.bfloat16

D_MODEL = 4096
BATCH = 32
SEQ = 256
DEC_BATCH = 8
DEC_SEQ = 1024
PAST_LEN = 512
GRID_W = 64
HEAD_DIM = 128
A_Q_HEADS = 16
A_KV_HEADS = 4
A_GROUPS = 4
A_HALF_WIN = 128
A_BLOCK = 128
WIN_HEADS_PER_SWEEP = 2
B_HEADS = 16
B_WIN_ROWS = 8
B_WIN_COLS = 16
A_WIDTH = A_Q_HEADS * HEAD_DIM
A_KV_WIDTH = A_KV_HEADS * HEAD_DIM
B_WIDTH = B_HEADS * HEAD_DIM
MIX_WIDTH = A_WIDTH + B_WIDTH
POOL_WINDOWS = (2, 4, 8, 16)
POOL_GROUP = 1024
POOL_BAND_BLOCK = 256
CAST_CHUNK_ROWS = 64
ROPE_BASE = 10000.0
NORM_EPS = 1e-6
NEG_INF = -1e30
ATTN_SCALE = HEAD_DIM ** -0.5
LOG2E = 1.4426950408889634
LOGIT_SCALE = ATTN_SCALE * LOG2E

N_PROMPT = BATCH * SEQ
N_SAMPLE = DEC_BATCH * DEC_SEQ
MOD_ROWS = 16
CTX_MOD_ROW = DEC_BATCH
GRID_ROWS = DEC_SEQ // GRID_W
N_BIAS_PAIRS = 2 * B_WIN_ROWS - 2
RPB_ROWS = 2 * B_WIN_ROWS - 1
RPB_COLS = 2 * B_WIN_COLS - 1

COL_QA = 0
COL_KA = COL_QA + A_WIDTH
COL_VA = COL_KA + A_KV_WIDTH
COL_QB = COL_VA + A_KV_WIDTH
COL_KB = COL_QB + B_WIDTH
COL_VB = COL_KB + B_WIDTH
COL_GATE = COL_VB + B_WIDTH
ATTN_IN_WIDTH = COL_GATE + MIX_WIDTH
PROJ_TN = 1024


def _col_tiles(col0, width):
    assert col0 % PROJ_TN == 0 and width % PROJ_TN == 0
    return tuple(range(col0 // PROJ_TN, (col0 + width) // PROJ_TN))


def _tile_lookup(tiles):
    runs = []
    for jj, t in enumerate(tiles):
        if not runs or runs[-1][1] != t - jj:
            runs.append((jj, t - jj))

    def lookup(j):
        off = runs[0][1]
        for start, o in runs[1:]:
            off = jnp.where(j >= start, o, off)
        return j + off
    return lookup

VMEM_LIMIT = 48 * 1024 * 1024
VMEM_LIMIT_ROWS = 56 * 1024 * 1024


def _params(n_grid_dims):
    return pltpu.CompilerParams(
        dimension_semantics=("arbitrary",) * n_grid_dims,
        vmem_limit_bytes=VMEM_LIMIT,
    )


def _silu(x):
    return 0.5 * x * (1.0 + jnp.tanh(0.5 * x))


def _dot_nt(a, b):
    return lax.dot_general(a, b, (((1,), (1,)), ((), ())), preferred_element_type=F32)


def _dot(a, b):
    return jnp.dot(a, b, preferred_element_type=F32)


def _ada_kernel(cond_ref, w_ref, b_ref, o_ref):
    a = _silu(cond_ref[...]).astype(BF16)
    o_ref[...] = _dot(a, w_ref[...].astype(BF16)) + b_ref[...]


def _ada(cond, w_ada, b_ada, tn=512):
    depth, d, n = w_ada.shape
    return pl.pallas_call(
        _ada_kernel,
        grid=(depth, n // tn),
        in_specs=[
            pl.BlockSpec((MOD_ROWS, d), lambda l, j: (0, 0)),
            pl.BlockSpec((None, d, tn), lambda l, j: (l, 0, j)),
            pl.BlockSpec((None, 1, tn), lambda l, j: (l, 0, j)),
        ],
        out_specs=pl.BlockSpec((None, MOD_ROWS, tn), lambda l, j: (l, 0, j)),
        out_shape=jax.ShapeDtypeStruct((depth, MOD_ROWS, n), F32),
        compiler_params=_params(2),
        name="ada",
    )(cond, w_ada, b_ada.reshape(depth, 1, n))


def _mod_row_fn(is_prompt, tm):
    if is_prompt:
        return lambda i: CTX_MOD_ROW
    return lambda i: (i * tm) // DEC_SEQ


def _norm_mod_kernel(xp_ref, xs_ref, g_ref, sh_ref, sc_ref, hp_ref, hs_ref, *, n_prompt_tiles):
    def norm_mod(x_ref, o_ref):
        x = x_ref[...]
        ms = jnp.mean(x * x, axis=-1, keepdims=True)
        y = x * lax.rsqrt(ms + NORM_EPS) * g_ref[...]
        o_ref[...] = (y * (1.0 + sc_ref[...]) + sh_ref[...]).astype(o_ref.dtype)

    i = pl.program_id(0)

    @pl.when(i < n_prompt_tiles)
    def _():
        norm_mod(xp_ref, hp_ref)

    @pl.when(i >= n_prompt_tiles)
    def _():
        norm_mod(xs_ref, hs_ref)


def _norm_mod(xp, xs, gain, mod, tm=256):
    d = xp.shape[1]
    n_p, n_s = xp.shape[0] // tm, xs.shape[0] // tm
    p_blk = lambda i: (jnp.minimum(i, n_p - 1), 0)
    s_blk = lambda i: (jnp.maximum(i - n_p, 0), 0)
    row = lambda i: jnp.where(i < n_p, CTX_MOD_ROW, (jnp.maximum(i - n_p, 0) * tm) // DEC_SEQ)
    return pl.pallas_call(
        functools.partial(_norm_mod_kernel, n_prompt_tiles=n_p),
        grid=(n_p + n_s,),
        in_specs=[
            pl.BlockSpec((tm, d), p_blk),
            pl.BlockSpec((tm, d), s_blk),
            pl.BlockSpec((1, d), lambda i: (0, 0)),
            pl.BlockSpec((None, 1, d), lambda i: (row(i), 0, 0)),
            pl.BlockSpec((None, 1, d), lambda i: (row(i), 0, 1)),
        ],
        out_specs=[pl.BlockSpec((tm, d), p_blk), pl.BlockSpec((tm, d), s_blk)],
        out_shape=[jax.ShapeDtypeStruct(xp.shape, BF16), jax.ShapeDtypeStruct(xs.shape, BF16)],
        compiler_params=_params(1),
        name="norm_mod",
    )(xp, xs, gain.reshape(1, d), mod, mod)


def _mm_kernel(a_ref, w_ref, *refs, n_casts):
    cast_in, o_ref, cast_out = refs[:n_casts], refs[n_casts], refs[n_casts + 1:]
    o_ref[...] = _dot(a_ref[...], w_ref[...]).astype(o_ref.dtype)
    for src, dst in zip(cast_in, cast_out):
        dst[...] = src[...].astype(dst.dtype)


def _matmul(a, w, w_tiles, out_dtype, cast_weights=()):
    m, k = a.shape
    tn = PROJ_TN
    tm = 1024 if out_dtype == BF16 else 512
    assert m % tm == 0
    w_tile = _tile_lookup(w_tiles)
    ncols = len(w_tiles) * tn
    n_i = m // tm
    n_steps = len(w_tiles) * n_i
    in_specs = [pl.BlockSpec((tm, k), lambda j, i: (i, 0)),
                pl.BlockSpec((k, tn), lambda j, i: (0, w_tile(j)))]
    out_specs = [pl.BlockSpec((tm, tn), lambda j, i: (i, j))]
    out_shape = [jax.ShapeDtypeStruct((m, ncols), out_dtype)]
    for cw in cast_weights:
        rows, cols = cw.shape
        chunk = CAST_CHUNK_ROWS
        n_chunks = rows // chunk
        assert rows % chunk == 0 and n_chunks <= n_steps
        spec = pl.BlockSpec((chunk, cols),
                            lambda j, i, n_chunks=n_chunks: (jnp.minimum(j * n_i + i, n_chunks - 1), 0))
        in_specs.append(spec)
        out_specs.append(spec)
        out_shape.append(jax.ShapeDtypeStruct((rows, cols), BF16))
    outs = pl.pallas_call(
        functools.partial(_mm_kernel, n_casts=len(cast_weights)),
        grid=(ncols // tn, n_i),
        in_specs=in_specs,
        out_specs=out_specs,
        out_shape=out_shape,
        compiler_params=pltpu.CompilerParams(
            dimension_semantics=("arbitrary", "arbitrary"),
            vmem_limit_bytes=VMEM_LIMIT_ROWS if cast_weights else VMEM_LIMIT),
        name="proj",
    )(a, w, *cast_weights)
    return outs if cast_weights else outs[0]


def _kv_proj_kernel(a_ref, w_ref, ka_ref, va_ref, kb_ref, vb_ref, ka5_ref, va5_ref, kb5_ref, vb5_ref,
                    *, kb_tile0, vb_tile0):
    j = pl.program_id(0)
    acc = _dot(a_ref[...], w_ref[...])

    def emit(cols, out_ref, out5_ref):
        part = acc[:, cols]
        out_ref[...] = part.astype(out_ref.dtype)
        out5_ref[...] = part.reshape(out5_ref.shape)

    @pl.when(j < kb_tile0)
    def _():
        emit(slice(0, A_KV_WIDTH), ka_ref, ka5_ref)
        emit(slice(A_KV_WIDTH, 2 * A_KV_WIDTH), va_ref, va5_ref)

    @pl.when((j >= kb_tile0) & (j < vb_tile0))
    def _():
        emit(slice(None), kb_ref, kb5_ref)

    @pl.when(j >= vb_tile0)
    def _():
        emit(slice(None), vb_ref, vb5_ref)


def _kv_projection(a, w, tm=512):
    m, k = a.shape
    tn = PROJ_TN
    assert COL_VA == COL_KA + A_KV_WIDTH and 2 * A_KV_WIDTH == tn and tm % SEQ == 0
    w_tiles = _col_tiles(COL_KA, tn) + _col_tiles(COL_KB, B_WIDTH) + _col_tiles(COL_VB, B_WIDTH)
    w_tile = _tile_lookup(w_tiles)
    kb_tile0 = 1
    vb_tile0 = kb_tile0 + B_WIDTH // tn
    end_tile = len(w_tiles)
    n_i = m // tm
    last = n_i - 1

    def parked(j, i, t0, t1):
        row = jnp.where(j < t0, 0, jnp.where(j < t1, i, last))
        return row, jnp.clip(j - t0, 0, t1 - t0 - 1)

    def parked5(j, i, t0, t1):
        row, col = parked(j, i, t0, t1)
        return row, 0, 0, col, 0

    req = tm // SEQ
    a_spec = pl.BlockSpec((tm, A_KV_WIDTH), lambda j, i: parked(j, i, 0, kb_tile0))
    a_spec5 = pl.BlockSpec((req, None, SEQ, A_KV_HEADS, HEAD_DIM),
                           lambda j, i: parked5(j, i, 0, kb_tile0))
    b_block5 = (req, None, SEQ, tn // HEAD_DIM, HEAD_DIM)
    cache_shape_a = (m // SEQ, 1, SEQ, A_KV_HEADS, HEAD_DIM)
    cache_shape_b = (m // SEQ, 1, SEQ, B_HEADS, HEAD_DIM)
    return pl.pallas_call(
        functools.partial(_kv_proj_kernel, kb_tile0=kb_tile0, vb_tile0=vb_tile0),
        grid=(end_tile, n_i),
        in_specs=[pl.BlockSpec((tm, k), lambda j, i: (i, 0)),
                  pl.BlockSpec((k, tn), lambda j, i: (0, w_tile(j)))],
        out_specs=[a_spec, a_spec,
                   pl.BlockSpec((tm, tn), lambda j, i: parked(j, i, kb_tile0, vb_tile0)),
                   pl.BlockSpec((tm, tn), lambda j, i: parked(j, i, vb_tile0, end_tile)),
                   a_spec5, a_spec5,
                   pl.BlockSpec(b_block5, lambda j, i: parked5(j, i, kb_tile0, vb_tile0)),
                   pl.BlockSpec(b_block5, lambda j, i: parked5(j, i, vb_tile0, end_tile))],
        out_shape=[jax.ShapeDtypeStruct((m, A_KV_WIDTH), BF16),
                   jax.ShapeDtypeStruct((m, A_KV_WIDTH), BF16),
                   jax.ShapeDtypeStruct((m, B_WIDTH), BF16),
                   jax.ShapeDtypeStruct((m, B_WIDTH), BF16),
                   jax.ShapeDtypeStruct(cache_shape_a, F32),
                   jax.ShapeDtypeStruct(cache_shape_a, F32),
                   jax.ShapeDtypeStruct(cache_shape_b, F32),
                   jax.ShapeDtypeStruct(cache_shape_b, F32)],
        compiler_params=pltpu.CompilerParams(dimension_semantics=("arbitrary", "arbitrary"),
                                             vmem_limit_bytes=VMEM_LIMIT_ROWS),
        name="kv_proj",
    )(a, w)


def _dot_halves(a1_ref, a2_ref, w_ref):
    kh = a1_ref.shape[1]
    return _dot(a1_ref[...], w_ref[:kh, :]) + _dot(a2_ref[...], w_ref[kh:, :])


def _mm_res_prenorm_kernel(a1_ref, a2_ref, w_ref, x_ref, g_ref, gain_ref, sc_ref,
                           x1_ref, hu_ref, ss_ref):
    x1 = x_ref[...] + g_ref[...] * _dot_halves(a1_ref, a2_ref, w_ref)
    x1_ref[...] = x1
    hu_ref[...] = (x1 * (gain_ref[...] * (1.0 + sc_ref[...]))).astype(hu_ref.dtype)
    ss_ref[...] = jnp.sum(x1 * x1, axis=-1, keepdims=True)


def _matmul_residual_prenorm(a_halves, w, x, mod, is_prompt, gain, next_mod, tm=512):
    (a1, c1), (a2, c2) = a_halves
    m = a1.shape[0]
    k, n = w.shape
    tn = PROJ_TN
    n_tiles = n // tn
    row = _mod_row_fn(is_prompt, tm)
    scale_block0, gate_block0 = n_tiles, 2 * n_tiles
    tile = pl.BlockSpec((tm, tn), lambda j, i: (i, j))
    return pl.pallas_call(
        _mm_res_prenorm_kernel,
        grid=(n_tiles, m // tm),
        in_specs=[pl.BlockSpec((tm, k // 2), lambda j, i: (i, c1)),
                  pl.BlockSpec((tm, k // 2), lambda j, i: (i, c2)),
                  pl.BlockSpec((k, tn), lambda j, i: (0, j)),
                  tile,
                  pl.BlockSpec((None, 1, tn), lambda j, i: (row(i), 0, gate_block0 + j)),
                  pl.BlockSpec((1, tn), lambda j, i: (0, j)),
                  pl.BlockSpec((None, 1, tn), lambda j, i: (row(i), 0, scale_block0 + j))],
        out_specs=[tile, tile, pl.BlockSpec((None, tm, 1), lambda j, i: (j, i, 0))],
        out_shape=[jax.ShapeDtypeStruct((m, n), F32), jax.ShapeDtypeStruct((m, n), BF16),
                   jax.ShapeDtypeStruct((n_tiles, m, 1), F32)],
        compiler_params=_params(2),
        name="proj_residual_prenorm",
    )(a1, a2, w, x, mod, gain.reshape(1, n), next_mod)


def _shift_proj_kernel(sh_ref, w_ref, o_ref):
    o_ref[...] = _dot(sh_ref[...].astype(BF16), w_ref[...])


def _shift_projection(mod2d, w):
    k, n = w.shape
    tn = PROJ_TN
    return pl.pallas_call(
        _shift_proj_kernel,
        grid=(n // tn,),
        in_specs=[pl.BlockSpec((MOD_ROWS, k), lambda j: (0, 0)),
                  pl.BlockSpec((k, tn), lambda j: (0, j))],
        out_specs=pl.BlockSpec((MOD_ROWS, tn), lambda j: (0, j)),
        out_shape=jax.ShapeDtypeStruct((MOD_ROWS, n), F32),
        compiler_params=_params(1),
        name="shift_proj",
    )(mod2d, w)


def _mm_postnorm_kernel(a_ref, w_ref, ss_ref, shw_ref, o_ref, *, width):
    acc = _dot(a_ref[...], w_ref[...])
    ss = ss_ref[0]
    for t in range(1, ss_ref.shape[0]):
        ss = ss + ss_ref[t]
    r = lax.rsqrt(ss / width + NORM_EPS)
    o_ref[...] = (acc * r + shw_ref[...]).astype(o_ref.dtype)


def _matmul_postnorm(hu, ss, w, shw, is_prompt, out_dtype, tm=1024):
    m, k = hu.shape
    n = w.shape[1]
    tn = PROJ_TN
    row = _mod_row_fn(is_prompt, tm)
    return pl.pallas_call(
        functools.partial(_mm_postnorm_kernel, width=k),
        grid=(n // tn, m // tm),
        in_specs=[pl.BlockSpec((tm, k), lambda j, i: (i, 0)),
                  pl.BlockSpec((k, tn), lambda j, i: (0, j)),
                  pl.BlockSpec((ss.shape[0], tm, 1), lambda j, i: (0, i, 0)),
                  pl.BlockSpec((None, 1, tn), lambda j, i: (row(i), 0, j))],
        out_specs=pl.BlockSpec((tm, tn), lambda j, i: (i, j)),
        out_shape=jax.ShapeDtypeStruct((m, n), out_dtype),
        compiler_params=pltpu.CompilerParams(dimension_semantics=("arbitrary", "arbitrary"),
                                             vmem_limit_bytes=VMEM_LIMIT_ROWS),
        name="proj_postnorm",
    )(hu, w, ss, shw)


def _mm_res_norm_kernel(a1_ref, a2_ref, w_ref, x_ref, g_ref, gain_ref, y_ref, *, n_tiles, tn):
    j = pl.program_id(1)
    x2 = x_ref[...] + g_ref[...] * _dot_halves(a1_ref, a2_ref, w_ref)
    for jj in range(n_tiles):
        @pl.when(j == jj)
        def _(jj=jj):
            y_ref[:, jj * tn:(jj + 1) * tn] = x2

    @pl.when(j == n_tiles - 1)
    def _():
        ss = None
        for jj in range(n_tiles):
            t = y_ref[:, jj * tn:(jj + 1) * tn]
            part = jnp.sum(t * t, axis=-1, keepdims=True)
            ss = part if ss is None else ss + part
        r = lax.rsqrt(ss / (n_tiles * tn) + NORM_EPS)
        for jj in range(n_tiles):
            cols = slice(jj * tn, (jj + 1) * tn)
            y_ref[:, cols] = y_ref[:, cols] * r * gain_ref[:, cols]


def _matmul_residual_norm(a_halves, w, x, mod, is_prompt, gain, tm=512):
    (a1, c1), (a2, c2) = a_halves
    m = a1.shape[0]
    k, n = w.shape
    tn = PROJ_TN
    n_tiles = n // tn
    row = _mod_row_fn(is_prompt, tm)
    gate_block0 = 2 * n_tiles
    return pl.pallas_call(
        functools.partial(_mm_res_norm_kernel, n_tiles=n_tiles, tn=tn),
        grid=(m // tm, n_tiles),
        in_specs=[pl.BlockSpec((tm, k // 2), lambda i, j: (i, c1)),
                  pl.BlockSpec((tm, k // 2), lambda i, j: (i, c2)),
                  pl.BlockSpec((k, tn), lambda i, j: (0, j)),
                  pl.BlockSpec((tm, tn), lambda i, j: (i, j)),
                  pl.BlockSpec((None, 1, tn), lambda i, j: (row(i), 0, gate_block0 + j)),
                  pl.BlockSpec((1, n), lambda i, j: (0, 0))],
        out_specs=pl.BlockSpec((tm, n), lambda i, j: (i, 0)),
        out_shape=jax.ShapeDtypeStruct((m, n), F32),
        compiler_params=pltpu.CompilerParams(dimension_semantics=("arbitrary", "arbitrary"),
                                             vmem_limit_bytes=VMEM_LIMIT_ROWS),
        name="proj_residual_norm",
    )(a1, a2, w, x, mod, gain.reshape(1, n))


def _softmax_numerators(logits, sink2=None):
    m = functools.reduce(jnp.maximum, [jnp.max(t, axis=-1, keepdims=True) for t in logits])
    if sink2 is not None:
        m = jnp.maximum(m, sink2)
    es = [jnp.exp2(t - m) for t in logits]
    l = functools.reduce(jnp.add, [jnp.sum(e, axis=-1, keepdims=True) for e in es])
    if sink2 is not None:
        l = l + jnp.exp2(sink2 - m)
    return [e.astype(BF16) for e in es], 1.0 / l


def _ctx_attn_kernel(sink_ref, qg_ref, ka_ref, va_ref, kb_ref, vb_ref, o_ref, s_scr, e_scr):
    n = SEQ
    gate0 = MIX_WIDTH
    n_stack = A_GROUPS

    def softmax(sink2):
        (e,), inv = _softmax_numerators([s_scr[...] * LOGIT_SCALE], sink2)
        e_scr[...] = e
        return inv

    def emit(o, g, out_col):
        cols = slice(out_col, out_col + HEAD_DIM)
        gate = qg_ref[:, gate0 + out_col:gate0 + out_col + HEAD_DIM].astype(F32)
        o_ref[:, cols] = (o[g * n:(g + 1) * n] * _silu(gate)).astype(o_ref.dtype)

    for kv in range(A_KV_HEADS):
        cols = slice(kv * HEAD_DIM, (kv + 1) * HEAD_DIM)
        heads = [kv * A_GROUPS + g for g in range(A_GROUPS)]
        q = jnp.concatenate(
            [qg_ref[:, h * HEAD_DIM:(h + 1) * HEAD_DIM] for h in heads], axis=0)
        sink2 = jnp.concatenate(
            [jnp.full((n, 1), sink_ref[h] * LOG2E, F32) for h in heads], axis=0)
        s_scr[...] = _dot_nt(q, ka_ref[:, cols])
        inv = softmax(sink2)
        o = _dot(e_scr[...], va_ref[:, cols]) * inv
        for g, h in enumerate(heads):
            emit(o, g, h * HEAD_DIM)
    for h0 in range(0, B_HEADS, n_stack):
        heads = range(h0, h0 + n_stack)
        for g, h in enumerate(heads):
            cols = slice(h * HEAD_DIM, (h + 1) * HEAD_DIM)
            q = qg_ref[:, A_WIDTH + h * HEAD_DIM:A_WIDTH + (h + 1) * HEAD_DIM]
            s_scr[g * n:(g + 1) * n, :] = _dot_nt(q, kb_ref[:, cols])
        inv = softmax(None)
        o = jnp.concatenate(
            [_dot(e_scr[g * n:(g + 1) * n, :],
                  vb_ref[:, h * HEAD_DIM:(h + 1) * HEAD_DIM])
             for g, h in enumerate(heads)], axis=0) * inv
        for g, h in enumerate(heads):
            emit(o, g, A_WIDTH + h * HEAD_DIM)


def _ctx_attention(sink, qg, ka, va, kb, vb):
    stack_rows = A_GROUPS * SEQ
    return pl.pallas_call(
        _ctx_attn_kernel,
        grid=(BATCH,),
        in_specs=[
            pl.BlockSpec(memory_space=pltpu.SMEM),
            pl.BlockSpec((SEQ, 2 * MIX_WIDTH), lambda b: (b, 0)),
            pl.BlockSpec((SEQ, A_KV_WIDTH), lambda b: (b, 0)),
            pl.BlockSpec((SEQ, A_KV_WIDTH), lambda b: (b, 0)),
            pl.BlockSpec((SEQ, B_WIDTH), lambda b: (b, 0)),
            pl.BlockSpec((SEQ, B_WIDTH), lambda b: (b, 0)),
        ],
        out_specs=pl.BlockSpec((SEQ, MIX_WIDTH), lambda b: (b, 0)),
        out_shape=jax.ShapeDtypeStruct((N_PROMPT, MIX_WIDTH), BF16),
        scratch_shapes=[pltpu.VMEM((stack_rows, SEQ), F32),
                        pltpu.VMEM((stack_rows, SEQ), BF16)],
        compiler_params=_params(1),
        name="ctx_attention",
    )(sink, qg, ka, va, kb, vb)


def _split_cache_heads(c_ref, scr, n_heads):
    for h in range(n_heads):
        scr[h] = c_ref[pl.ds(h, PAST_LEN, stride=n_heads), :].astype(scr.dtype)


def _win_attn_kernel(sink_ref, q_ref, k_ref, v_ref, ck_ref, cv_ref, gate_ref,
                     cos_ref, sin_ref, o_ref, qs_ref, ks_ref, ck_scr, cv_scr,
                     band_scr, sw_scr, sc_scr, ew_scr, ec_scr):
    kv = pl.program_id(1)
    n = DEC_SEQ

    @pl.when(kv == 0)
    def _():
        _split_cache_heads(ck_ref, ck_scr, A_KV_HEADS)
        _split_cache_heads(cv_ref, cv_scr, A_KV_HEADS)

    cos = cos_ref[...]
    sin = sin_ref[...]
    lane = lax.broadcasted_iota(jnp.int32, (n, HEAD_DIM), 1)
    first_quarter = (lane % (HEAD_DIM // 2)) < (HEAD_DIM // 4)

    def rope(x):
        rot = jnp.where(first_quarter,
                        pltpu.roll(x, HEAD_DIM - HEAD_DIM // 4, 1),
                        pltpu.roll(x, HEAD_DIM // 4, 1))
        return x * cos + rot * sin

    ks_ref[...] = rope(k_ref[...].astype(F32)).astype(BF16)
    for g in range(A_GROUPS):
        qs_ref[g] = (rope(q_ref[:, g * HEAD_DIM:(g + 1) * HEAD_DIM].astype(F32))
                     * LOGIT_SCALE).astype(BF16)

    ck = ck_scr[kv]
    cv = cv_scr[kv]
    rows = A_GROUPS * A_BLOCK
    sink2 = jnp.concatenate(
        [jnp.full((A_BLOCK, 1), sink_ref[kv * A_GROUPS + g] * LOG2E, F32)
         for g in range(A_GROUPS)], axis=0)
    qi = lax.broadcasted_iota(jnp.int32, (rows, 3 * A_BLOCK), 0) % A_BLOCK
    rel = lax.broadcasted_iota(jnp.int32, (rows, 3 * A_BLOCK), 1) - A_BLOCK - qi
    band_scr[...] = jnp.where(jnp.abs(rel) <= A_HALF_WIN, 0.0, NEG_INF)
    for blk in range(n // A_BLOCK):
        r0 = blk * A_BLOCK
        lo = max(0, r0 - A_BLOCK)
        hi = min(n, r0 + 2 * A_BLOCK)
        span = hi - lo
        b0 = lo - (r0 - A_BLOCK)
        for g0 in range(0, A_GROUPS, WIN_HEADS_PER_SWEEP):
            gs = range(g0, g0 + WIN_HEADS_PER_SWEEP)
            sr = slice(0, WIN_HEADS_PER_SWEEP * A_BLOCK)
            q = jnp.concatenate([qs_ref[g, r0:r0 + A_BLOCK, :] for g in gs], axis=0)
            sw_scr[sr, :span] = _dot_nt(q, ks_ref[lo:hi, :])
            sc_scr[sr, :] = _dot_nt(q, ck)
            (e_w, e_c), inv = _softmax_numerators(
                [sw_scr[sr, :span] + band_scr[sr, b0:b0 + span], sc_scr[sr, :]],
                sink2[g0 * A_BLOCK:(g0 + WIN_HEADS_PER_SWEEP) * A_BLOCK])
            ew_scr[sr, :span] = e_w
            ec_scr[sr, :] = e_c
            o = (_dot(ew_scr[sr, :span], v_ref[lo:hi, :]) + _dot(ec_scr[sr, :], cv)) * inv
            for i, g in enumerate(gs):
                hc = slice(g * HEAD_DIM, (g + 1) * HEAD_DIM)
                gate = gate_ref[r0:r0 + A_BLOCK, hc].astype(F32)
                o_ref[r0:r0 + A_BLOCK, hc] = (
                    o[i * A_BLOCK:(i + 1) * A_BLOCK] * _silu(gate)).astype(o_ref.dtype)


def _win_attention(sink, proj, cache_k, cache_v, cos, sin_signed):
    gw = A_GROUPS * HEAD_DIM
    stack_rows = A_GROUPS * A_BLOCK
    n = DEC_SEQ
    cache_spec = pl.BlockSpec((None, PAST_LEN * A_KV_HEADS, HEAD_DIM), lambda b, k: (b, 0, 0))
    return pl.pallas_call(
        _win_attn_kernel,
        grid=(DEC_BATCH, A_KV_HEADS),
        in_specs=[
            pl.BlockSpec(memory_space=pltpu.SMEM),
            pl.BlockSpec((n, gw), lambda b, k: (b, COL_QA // gw + k)),
            pl.BlockSpec((n, HEAD_DIM), lambda b, k: (b, COL_KA // HEAD_DIM + k)),
            pl.BlockSpec((n, HEAD_DIM), lambda b, k: (b, COL_VA // HEAD_DIM + k)),
            cache_spec,
            cache_spec,
            pl.BlockSpec((n, gw), lambda b, k: (b, COL_GATE // gw + k)),
            pl.BlockSpec((n, HEAD_DIM), lambda b, k: (0, 0)),
            pl.BlockSpec((n, HEAD_DIM), lambda b, k: (0, 0)),
        ],
        out_specs=pl.BlockSpec((n, gw), lambda b, k: (b, k)),
        out_shape=jax.ShapeDtypeStruct((N_SAMPLE, A_WIDTH), BF16),
        scratch_shapes=[pltpu.VMEM((A_GROUPS, n, HEAD_DIM), BF16),
                        pltpu.VMEM((n, HEAD_DIM), BF16),
                        pltpu.VMEM((A_KV_HEADS, PAST_LEN, HEAD_DIM), BF16),
                        pltpu.VMEM((A_KV_HEADS, PAST_LEN, HEAD_DIM), BF16),
                        pltpu.VMEM((stack_rows, 3 * A_BLOCK), F32),
                        pltpu.VMEM((stack_rows, 3 * A_BLOCK), F32),
                        pltpu.VMEM((stack_rows, PAST_LEN), F32),
                        pltpu.VMEM((stack_rows, 3 * A_BLOCK), BF16),
                        pltpu.VMEM((stack_rows, PAST_LEN), BF16)],
        compiler_params=_params(2),
        name="win_attention",
    )(sink, proj, proj, proj, cache_k, cache_v, proj, cos, sin_signed)


def _bias_kernel(rpb_ref, o_ref):
    h = pl.program_id(0)
    shape = (GRID_W, 2 * GRID_W)
    c = lax.broadcasted_iota(jnp.int32, shape, 0)
    j2 = lax.broadcasted_iota(jnp.int32, shape, 1)
    kc = j2 % GRID_W
    second = j2 >= GRID_W
    col_start = jnp.clip(c - B_WIN_COLS // 2, 0, GRID_W - B_WIN_COLS)
    ok = (kc >= col_start) & (kc < col_start + B_WIN_COLS)
    dc = kc - c + B_WIN_COLS - 1
    base = h * (RPB_ROWS * RPB_COLS)
    pair_scr = []
    for i in range(N_BIAS_PAIRS):
        acc = jnp.full(shape, NEG_INF, F32)
        for d in range(RPB_COLS):
            val = jnp.where(second, rpb_ref[base + (i + 1) * RPB_COLS + d],
                            rpb_ref[base + i * RPB_COLS + d]) * LOG2E
            acc = jnp.where(ok & (dc == d), val, acc)
        pair_scr.append(acc)
    for r in range(GRID_ROWS):
        dr0 = _na_key_row0(r) - r + B_WIN_ROWS - 1
        for i in range(B_WIN_ROWS // 2):
            o_ref[r * GRID_W:(r + 1) * GRID_W, i * 2 * GRID_W:(i + 1) * 2 * GRID_W] = (
                pair_scr[dr0 + 2 * i])


def _expand_bias(rpb):
    kw = B_WIN_ROWS * GRID_W
    return pl.pallas_call(
        _bias_kernel,
        grid=(B_HEADS,),
        in_specs=[pl.BlockSpec(memory_space=pltpu.SMEM)],
        out_specs=pl.BlockSpec((None, DEC_SEQ, kw), lambda h: (h, 0, 0)),
        out_shape=jax.ShapeDtypeStruct((B_HEADS, DEC_SEQ, kw), F32),
        compiler_params=_params(1),
        name="expand_bias",
    )(rpb.reshape(-1))


def _na_key_row0(r):
    return min(max(r - B_WIN_ROWS // 2, 0), GRID_ROWS - B_WIN_ROWS)


def _na_row_groups():
    groups = []
    for r in range(GRID_ROWS):
        rs = _na_key_row0(r)
        if groups and groups[-1][2] == rs:
            groups[-1] = (groups[-1][0], r + 1, rs)
        else:
            groups.append((r, r + 1, rs))
    return groups


def _na_attn_kernel(q_ref, k_ref, v_ref, ck_ref, cv_ref, gate_ref, bias_ref, o_ref,
                    ck_scr, cv_scr, sn_scr, sc_scr, en_scr, ec_scr):
    kw = B_WIN_ROWS * GRID_W
    h = pl.program_id(1)

    @pl.when(h == 0)
    def _():
        _split_cache_heads(ck_ref, ck_scr, B_HEADS)
        _split_cache_heads(cv_ref, cv_scr, B_HEADS)

    groups = _na_row_groups()
    sc_scr[...] = _dot_nt(q_ref[...], ck_scr[h])
    for r0, r1, rs in groups:
        sn_scr[r0 * GRID_W:r1 * GRID_W, :] = _dot_nt(
            q_ref[r0 * GRID_W:r1 * GRID_W, :], k_ref[rs * GRID_W:rs * GRID_W + kw, :])

    (e_n, e_c), inv = _softmax_numerators(
        [sn_scr[...] * LOGIT_SCALE + bias_ref[...], sc_scr[...] * LOGIT_SCALE])
    en_scr[...] = e_n
    ec_scr[...] = e_c
    o_n = jnp.concatenate(
        [_dot(en_scr[r0 * GRID_W:r1 * GRID_W, :], v_ref[rs * GRID_W:rs * GRID_W + kw, :])
         for r0, r1, rs in groups], axis=0)
    o = (o_n + _dot(ec_scr[...], cv_scr[h])) * inv
    o_ref[...] = (o * _silu(gate_ref[...].astype(F32))).astype(o_ref.dtype)


def _na_attention(proj, cache_k, cache_v, bias):
    n = DEC_SEQ
    hd = HEAD_DIM
    cache_spec = pl.BlockSpec((None, PAST_LEN * B_HEADS, hd), lambda b, h: (b, 0, 0))
    return pl.pallas_call(
        _na_attn_kernel,
        grid=(DEC_BATCH, B_HEADS),
        in_specs=[
            pl.BlockSpec((n, hd), lambda b, h: (b, COL_QB // hd + h)),
            pl.BlockSpec((n, hd), lambda b, h: (b, COL_KB // hd + h)),
            pl.BlockSpec((n, hd), lambda b, h: (b, COL_VB // hd + h)),
            cache_spec,
            cache_spec,
            pl.BlockSpec((n, hd), lambda b, h: (b, (COL_GATE + A_WIDTH) // hd + h)),
            pl.BlockSpec((None, n, B_WIN_ROWS * GRID_W), lambda b, h: (h, 0, 0)),
        ],
        out_specs=pl.BlockSpec((n, hd), lambda b, h: (b, h)),
        out_shape=jax.ShapeDtypeStruct((N_SAMPLE, B_WIDTH), BF16),
        scratch_shapes=[pltpu.VMEM((B_HEADS, PAST_LEN, hd), BF16),
                        pltpu.VMEM((B_HEADS, PAST_LEN, hd), BF16),
                        pltpu.VMEM((n, B_WIN_ROWS * GRID_W), F32),
                        pltpu.VMEM((n, PAST_LEN), F32),
                        pltpu.VMEM((n, B_WIN_ROWS * GRID_W), BF16),
                        pltpu.VMEM((n, PAST_LEN), BF16)],
        compiler_params=_params(2),
        name="na_attention",
    )(proj, proj, proj, cache_k, cache_v, proj, bias)


def _pool_kernel(u_ref, gate_ref, band_ref, inv_count_ref, w_ref, scale_ref, o_ref, *, seq, rows):
    cb = POOL_BAND_BLOCK
    parts = []
    for i in range(rows // cb):
        lo = i * cb if seq <= cb else max(0, (i - 1) * cb)
        hi = (i + 1) * cb if seq <= cb else min(rows, (i + 2) * cb)
        parts.append(_dot(band_ref[i * cb:(i + 1) * cb, lo:hi], u_ref[lo:hi, :]))
    wsum = jnp.concatenate(parts, axis=0)
    pooled = wsum * inv_count_ref[...] - u_ref[...].astype(F32)
    y = _dot(pooled.astype(BF16), w_ref[...]) * scale_ref[...]
    o_ref[...] = (y * _silu(gate_ref[...].astype(F32))).astype(o_ref.dtype)


def _pool_operators(seq, rows):
    t = jnp.arange(rows)[:, None]
    j = jnp.arange(rows)[None, :]
    same_seq = (t // seq) == (j // seq)
    pos = t % seq
    bands, inv_counts = [], []
    for window in POOL_WINDOWS:
        half = window // 2
        assert half <= POOL_BAND_BLOCK
        bands.append(((j - t >= -half) & (j - t < half) & same_seq).astype(BF16))
        inv_counts.append(1.0 / (jnp.minimum(pos + half, seq) - jnp.maximum(pos - half, 0)).astype(F32))
    return jnp.stack(bands), jnp.stack(inv_counts)


def _pool_mixer(ug, w_grp, scale, seq, rows=1024):
    m = ug.shape[0]
    n_groups = len(POOL_WINDOWS)
    pg = POOL_GROUP
    band, inv_count = _pool_operators(seq, rows)
    return pl.pallas_call(
        functools.partial(_pool_kernel, seq=seq, rows=rows),
        grid=(n_groups, m // rows),
        in_specs=[
            pl.BlockSpec((rows, pg), lambda g, i: (i, g)),
            pl.BlockSpec((rows, pg), lambda g, i: (i, n_groups + g)),
            pl.BlockSpec((None, rows, rows), lambda g, i: (g, 0, 0)),
            pl.BlockSpec((None, rows, 1), lambda g, i: (g, 0, 0)),
            pl.BlockSpec((None, pg, pg), lambda g, i: (g, 0, 0)),
            pl.BlockSpec((1, pg), lambda g, i: (0, g)),
        ],
        out_specs=pl.BlockSpec((rows, pg), lambda g, i: (i, g)),
        out_shape=jax.ShapeDtypeStruct((m, n_groups * pg), BF16),
        compiler_params=_params(2),
        name="pool_mixer",
    )(ug, ug, band, inv_count, w_grp, scale.reshape(1, -1))


def _rope_tables():
    t = jnp.arange(DEC_SEQ)
    quarter = HEAD_DIM // 4
    inv_freq = ROPE_BASE ** (-jnp.arange(quarter, dtype=F32) / quarter)
    ang_r = (t // GRID_W).astype(F32)[:, None] * inv_freq
    ang_c = (t % GRID_W).astype(F32)[:, None] * inv_freq
    ang = jnp.concatenate([ang_r, ang_r, ang_c, ang_c], axis=-1)
    sign = jnp.tile(jnp.concatenate([-jnp.ones((quarter,), F32), jnp.ones((quarter,), F32)]), 2)
    return jnp.cos(ang), jnp.sin(ang) * sign


def kernel(x_prompt, x_sample, c, cache_a_k, cache_a_v, cache_b_k, cache_b_v, c_ctx,
           w_ada, b_ada, norm_g, w_in_attn, a_sink, b_rpb, w_out_attn,
           w_in_pool, w_grp_pool, pool_scale, w_out_pool, final_g):
    d = D_MODEL
    xp = x_prompt.reshape(N_PROMPT, d)
    xs = x_sample.reshape(N_SAMPLE, d)

    w_in0 = w_in_attn[0].astype(BF16)

    cond = jnp.zeros((MOD_ROWS, d), F32).at[:DEC_BATCH].set(c).at[CTX_MOD_ROW].set(c_ctx)
    mod = _ada(cond, w_ada, b_ada)
    mod0 = mod[0].reshape(MOD_ROWS, 1, 3 * d)
    mod1 = mod[1].reshape(MOD_ROWS, 1, 3 * d)

    hp, hs = _norm_mod(xp, xs, norm_g[0], mod0)

    ka, va, kb, vb, new_a_k, new_a_v, new_b_k, new_b_v = _kv_projection(hp, w_in0)
    qg_tiles = (_col_tiles(COL_QA, A_WIDTH) + _col_tiles(COL_QB, B_WIDTH)
                + _col_tiles(COL_GATE, MIX_WIDTH))
    qg_p, w_in1 = _matmul(hp, w_in0, qg_tiles, BF16, cast_weights=(w_in_pool[0],))
    n_groups = len(POOL_WINDOWS)
    proj_s, w_out0, w_out1, w_grp = _matmul(
        hs, w_in0, _col_tiles(0, ATTN_IN_WIDTH), BF16,
        cast_weights=(w_out_attn[0], w_out_pool[0],
                      w_grp_pool[0].reshape(n_groups * POOL_GROUP, POOL_GROUP)))
    w_grp = w_grp.reshape(n_groups, POOL_GROUP, POOL_GROUP)

    sink = a_sink[0]
    og_p = _ctx_attention(sink, qg_p, ka, va, kb, vb)

    cos, sin_signed = _rope_tables()
    flat = lambda cache: cache.reshape(DEC_BATCH, -1, HEAD_DIM)
    oa_s = _win_attention(sink, proj_s, flat(cache_a_k), flat(cache_a_v), cos, sin_signed)
    ob_s = _na_attention(proj_s, flat(cache_b_k), flat(cache_b_v), _expand_bias(b_rpb[0]))

    xp1, hu_p, ss_p = _matmul_residual_prenorm([(og_p, 0), (og_p, 1)], w_out0, xp, mod0, True,
                                               norm_g[1], mod1)
    xs1, hu_s, ss_s = _matmul_residual_prenorm([(oa_s, 0), (ob_s, 0)], w_out0, xs, mod0, False,
                                               norm_g[1], mod1)

    shw = _shift_projection(mod[1], w_in1).reshape(MOD_ROWS, 1, 2 * d)
    ug_p = _matmul_postnorm(hu_p, ss_p, w_in1, shw, True, BF16)
    ug_s = _matmul_postnorm(hu_s, ss_s, w_in1, shw, False, BF16)
    y_p = _pool_mixer(ug_p, w_grp, pool_scale[0], SEQ)
    y_s = _pool_mixer(ug_s, w_grp, pool_scale[0], DEC_SEQ)
    y_prompt = _matmul_residual_norm([(y_p, 0), (y_p, 1)], w_out1, xp1, mod1, True,
                                     final_g).reshape(BATCH, SEQ, d)
    y_sample = _matmul_residual_norm([(y_s, 0), (y_s, 1)], w_out1, xs1, mod1, False,
                                     final_g).reshape(DEC_BATCH, DEC_SEQ, d)

    return (y_prompt, y_sample, new_a_k, new_a_v, new_b_k, new_b_v)
```
